```python
import math
import jax, jax.numpy as jnp
from jax import lax
import numpy as np

D_MODEL = 2048
BATCH = 4
SEQ = 2048
DEPTH = 1
DEC_BATCH = 128
DEC_SEQ = 4
PAST_LEN = 16384
PAGE_SIZE = 128

D_MIX = D_MODEL
D_POOL = D_MIX // 2
D_SSM = D_MIX - D_POOL
POOL_WINDOWS = (2, 4, 8, 16)
N_POOL_GROUPS = len(POOL_WINDOWS)
POOL_GROUP_DIM = D_POOL // N_POOL_GROUPS
POOL_BUF = max(POOL_WINDOWS) - 1
SSM_GROUP_CH = 16
N_SSM_GROUPS = D_SSM // SSM_GROUP_CH
SSM_STATE = 64
N_MEM = 256
N_XHEADS = 4
XHEAD_DIM = D_MODEL // N_XHEADS
D_FF = 5632
CONV_W = 3
CONV_BUF = CONV_W - 1
ALPHA = (2.0 * DEPTH) ** 0.25
BETA = (8.0 * DEPTH) ** -0.25
LN_EPS = 1e-5

kernel_name = "hymba_pool_s5_memxattn_convffn_step"


def layer_norm(x, g, b):
    xf = x.astype(jnp.float32)
    mu = jnp.mean(xf, axis=-1, keepdims=True)
    var = jnp.mean(jnp.square(xf - mu), axis=-1, keepdims=True)
    y = (xf - mu) * lax.rsqrt(var + LN_EPS) * g.astype(jnp.float32) + b.astype(jnp.float32)
    return y.astype(x.dtype)


def pool_mixer(u, buf, pos0, w_pool, pool_scale):
    nb, t, _ = u.shape
    ext = jnp.concatenate([buf.astype(u.dtype), u], axis=1)
    cs = jnp.cumsum(ext.astype(jnp.float32), axis=1)
    cs = jnp.pad(cs, ((0, 0), (1, 0), (0, 0)))
    end = cs[:, POOL_BUF + 1:POOL_BUF + 1 + t]
    pos = pos0 + jnp.arange(t, dtype=jnp.int32)
    outs = []
    for g, w in enumerate(POOL_WINDOWS):
        sl = slice(g * POOL_GROUP_DIM, (g + 1) * POOL_GROUP_DIM)
        start = cs[:, POOL_BUF + 1 - w:POOL_BUF + 1 - w + t, sl]
        cnt = jnp.minimum(pos + 1, w).astype(jnp.float32)[None, :, None]
        outs.append((end[..., sl] - start) / cnt)
    pooled = jnp.concatenate(outs, axis=-1) - u.astype(jnp.float32)
    pooled = pooled.astype(u.dtype).reshape(nb, t, N_POOL_GROUPS, POOL_GROUP_DIM)
    mixed = jnp.einsum('btgc,gcd->btgd', pooled, w_pool).reshape(nb, t, D_POOL)
    return mixed * pool_scale, ext[:, -POOL_BUF:]


def _ssm_combine(e1, e2):
    a1, b1 = e1
    a2, b2 = e2
    return a1 * a2, a2 * b1 + b2


def ssm_mixer(u, h0_re, h0_im, lambda_re, lambda_im, log_step, b_re, b_im, c_re, c_im, d_skip, w_glu, b_glu):
    f32 = jnp.float32
    nb, t, _ = u.shape
    uf = u.astype(f32).reshape(nb, t, N_SSM_GROUPS, SSM_GROUP_CH)
    lam = lax.complex(lambda_re.astype(f32), lambda_im.astype(f32))
    dt = jnp.exp(log_step.astype(f32))[:, None]
    lam_bar = jnp.exp(lam * dt)
    b_bar = ((lam_bar - 1.0) / lam)[:, :, None] * lax.complex(b_re.astype(f32), b_im.astype(f32))
    c = lax.complex(c_re.astype(f32), c_im.astype(f32))
    bu = jnp.einsum('gnc,btgc->btgn', b_bar, uf.astype(jnp.complex64))
    h0 = lax.complex(h0_re.astype(f32), h0_im.astype(f32))
    bu = bu.at[:, 0].add(lam_bar[None] * h0)
    a = jnp.broadcast_to(lam_bar, bu.shape)
    _, h = lax.associative_scan(_ssm_combine, (a, bu), axis=1)
    y = jnp.real(jnp.einsum('gcn,btgn->btgc', c, h)) + d_skip.astype(f32).reshape(N_SSM_GROUPS, SSM_GROUP_CH) * uf
    z = jax.nn.gelu(y.reshape(nb, t, D_SSM)).astype(u.dtype)
    out = z * jax.nn.sigmoid(z @ w_glu + b_glu)
    h_last = h[:, -1]
    return out, jnp.real(h_last).astype(h0_re.dtype), jnp.imag(h_last).astype(h0_re.dtype)


def mem_kv(mem, w_k, w_v):
    nb = mem.shape[0]
    k = (mem @ w_k).reshape(nb, N_MEM, N_XHEADS, XHEAD_DIM)
    v = (mem @ w_v).reshape(nb, N_MEM, N_XHEADS, XHEAD_DIM)
    return k, v


def cross_attn(h, k, v, w_q, w_o):
    nb, t, _ = h.shape
    q = (h @ w_q).reshape(nb, t, N_XHEADS, XHEAD_DIM)
    s = jnp.einsum('bthd,bmhd->bhtm', q, k).astype(jnp.float32) * (XHEAD_DIM ** -0.5)
    p = jax.nn.softmax(s, axis=-1).astype(v.dtype)
    o = jnp.einsum('bhtm,bmhd->bthd', p, v).reshape(nb, t, N_XHEADS * XHEAD_DIM)
    return o @ w_o


def conv_ffn(h, buf, w_gate, w_up, conv_w, conv_b, w_down):
    t = h.shape[1]
    g = h @ w_gate
    ext = jnp.concatenate([buf.astype(g.dtype), g], axis=1)
    gc = conv_b + ext[:, 0:t] * conv_w[0]
    for k in range(1, CONV_W):
        gc = gc + ext[:, k:k + t] * conv_w[k]
    out = (jax.nn.silu(gc) * (h @ w_up)) @ w_down
    return out, ext[:, -CONV_BUF:]


def layer(x, mk, mv, pool_buf, pos0, h0_re, h0_im, conv_buf,
          w_in, w_pool, pool_scale, lambda_re, lambda_im, log_step, b_re, b_im, c_re, c_im,
          d_skip, w_glu, b_glu, w_out, ln1_g, ln1_b, w_q, w_o, ln2_g, ln2_b,
          w_gate, w_up, conv_w, conv_b, w_down, ln3_g, ln3_b):
    u = x @ w_in
    a_out, new_pool = pool_mixer(u[..., :D_POOL], pool_buf, pos0, w_pool, pool_scale)
    b_out, new_re, new_im = ssm_mixer(u[..., D_POOL:], h0_re, h0_im, lambda_re, lambda_im, log_step,
                                      b_re, b_im, c_re, c_im, d_skip, w_glu, b_glu)
    mix = jnp.concatenate([a_out, b_out], axis=-1) @ w_out
    h = layer_norm(ALPHA * x + mix, ln1_g, ln1_b)
    h = layer_norm(ALPHA * h + cross_attn(h, mk, mv, w_q, w_o), ln2_g, ln2_b)
    f, new_conv = conv_ffn(h, conv_buf, w_gate, w_up, conv_w, conv_b, w_down)
    y = layer_norm(ALPHA * h + f, ln3_g, ln3_b)
    return y, new_pool, new_re, new_im, new_conv


def setup_inputs(seed: int = 0) -> dict:
    key = jax.random.key(seed)
    ks = iter(jax.random.split(key, 48))
    f32 = jnp.float32
    L, G, N, C = DEPTH, N_SSM_GROUPS, SSM_STATE, SSM_GROUP_CH

    def nrm(shape, scale):
        return jax.random.normal(next(ks), shape, f32) * scale

    def gain(shape):
        return 1.0 + nrm(shape, 0.05)

    x_prompt = nrm((BATCH, SEQ, D_MODEL), 1.0)
    x_sample = nrm((DEC_BATCH, DEC_SEQ, D_MODEL), 1.0)
    mem_prompt = nrm((BATCH, N_MEM, D_MODEL), 1.0)
    state_pool = nrm((L, DEC_BATCH, POOL_BUF, D_POOL), 1.0)
    state_ssm_re = nrm((L, DEC_BATCH, G, N), 0.2)
    state_ssm_im = nrm((L, DEC_BATCH, G, N), 0.2)
    state_conv = nrm((L, DEC_BATCH, CONV_BUF, D_FF), 1.0)
    cache_mem_k = nrm((L, DEC_BATCH, N_MEM, N_XHEADS, XHEAD_DIM), 1.0)
    cache_mem_v = nrm((L, DEC_BATCH, N_MEM, N_XHEADS, XHEAD_DIM), BETA)

    w_in = nrm((L, D_MODEL, D_MIX), D_MODEL ** -0.5)
    w_pool = nrm((L, N_POOL_GROUPS, POOL_GROUP_DIM, POOL_GROUP_DIM), POOL_GROUP_DIM ** -0.5)
    pool_scale = gain((L, D_POOL))
    lambda_re = -0.5 + nrm((L, G, N), 0.01)
    lambda_im = math.pi * jnp.broadcast_to(jnp.arange(N, dtype=f32), (L, G, N)) + nrm((L, G, N), 0.01)
    log_step = jax.random.uniform(next(ks), (L, G), f32, math.log(1e-3), math.log(1e-1))
    b_re = nrm((L, G, N, C), (2.0 * C) ** -0.5)
    b_im = nrm((L, G, N, C), (2.0 * C) ** -0.5)
    c_re = nrm((L, G, C, N), (2.0 * N) ** -0.5)
    c_im = nrm((L, G, C, N), (2.0 * N) ** -0.5)
    d_skip = nrm((L, D_SSM), 1.0)
    w_glu = nrm((L, D_SSM, D_SSM), D_SSM ** -0.5)
    b_glu = nrm((L, D_SSM), 0.01)
    w_out = nrm((L, D_MIX, D_MODEL), BETA * D_MIX ** -0.5)
    ln1_g = gain((L, D_MODEL))
    ln1_b = nrm((L, D_MODEL), 0.01)
    w_q = nrm((L, D_MODEL, N_XHEADS * XHEAD_DIM), D_MODEL ** -0.5)
    w_k = nrm((L, D_MODEL, N_XHEADS * XHEAD_DIM), D_MODEL ** -0.5)
    w_v = nrm((L, D_MODEL, N_XHEADS * XHEAD_DIM), BETA * D_MODEL ** -0.5)
    w_o = nrm((L, N_XHEADS * XHEAD_DIM, D_MODEL), BETA * D_MODEL ** -0.5)
    ln2_g = gain((L, D_MODEL))
    ln2_b = nrm((L, D_MODEL), 0.01)
    w_gate = nrm((L, D_MODEL, D_FF), D_MODEL ** -0.5)
    w_up = nrm((L, D_MODEL, D_FF), D_MODEL ** -0.5)
    conv_w = nrm((L, CONV_W, D_FF), CONV_W ** -0.5)
    conv_b = nrm((L, D_FF), 0.01)
    w_down = nrm((L, D_FF, D_MODEL), BETA * D_FF ** -0.5)
    ln3_g = gain((L, D_MODEL))
    ln3_b = nrm((L, D_MODEL), 0.01)
    return {
        "x_prompt": x_prompt, "x_sample": x_sample, "mem_prompt": mem_prompt,
        "state_pool": state_pool, "state_ssm_re": state_ssm_re, "state_ssm_im": state_ssm_im,
        "state_conv": state_conv, "cache_mem_k": cache_mem_k, "cache_mem_v": cache_mem_v,
        "w_in": w_in, "w_pool": w_pool, "pool_scale": pool_scale,
        "lambda_re": lambda_re, "lambda_im": lambda_im, "log_step": log_step,
        "b_re": b_re, "b_im": b_im, "c_re": c_re, "c_im": c_im, "d_skip": d_skip,
        "w_glu": w_glu, "b_glu": b_glu, "w_out": w_out, "ln1_g": ln1_g, "ln1_b": ln1_b,
        "w_q": w_q, "w_k": w_k, "w_v": w_v, "w_o": w_o, "ln2_g": ln2_g, "ln2_b": ln2_b,
        "w_gate": w_gate, "w_up": w_up, "conv_w": conv_w, "conv_b": conv_b, "w_down": w_down,
        "ln3_g": ln3_g, "ln3_b": ln3_b,
    }


def reference(x_prompt, x_sample, mem_prompt, state_pool, state_ssm_re, state_ssm_im, state_conv,
              cache_mem_k, cache_mem_v, w_in, w_pool, pool_scale, lambda_re, lambda_im, log_step,
              b_re, b_im, c_re, c_im, d_skip, w_glu, b_glu, w_out, ln1_g, ln1_b,
              w_q, w_k, w_v, w_o, ln2_g, ln2_b, w_gate, w_up, conv_w, conv_b, w_down, ln3_g, ln3_b):
    yp, ys = x_prompt, x_sample
    nbp = x_prompt.shape[0]
    p_pool, p_re, p_im, p_conv, p_mk, p_mv = [], [], [], [], [], []
    s_pool, s_re, s_im, s_conv = [], [], [], []
    for l in range(DEPTH):
        prm = dict(w_in=w_in[l], w_pool=w_pool[l], pool_scale=pool_scale[l],
                   lambda_re=lambda_re[l], lambda_im=lambda_im[l], log_step=log_step[l],
                   b_re=b_re[l], b_im=b_im[l], c_re=c_re[l], c_im=c_im[l], d_skip=d_skip[l],
                   w_glu=w_glu[l], b_glu=b_glu[l], w_out=w_out[l], ln1_g=ln1_g[l], ln1_b=ln1_b[l],
                   w_q=w_q[l], w_o=w_o[l], ln2_g=ln2_g[l], ln2_b=ln2_b[l],
                   w_gate=w_gate[l], w_up=w_up[l], conv_w=conv_w[l], conv_b=conv_b[l],
                   w_down=w_down[l], ln3_g=ln3_g[l], ln3_b=ln3_b[l])
        mk_p, mv_p = mem_kv(mem_prompt, w_k[l], w_v[l])
        zpool = jnp.zeros((nbp, POOL_BUF, D_POOL), x_prompt.dtype)
        zssm = jnp.zeros((nbp, N_SSM_GROUPS, SSM_STATE), state_ssm_re.dtype)
        zconv = jnp.zeros((nbp, CONV_BUF, D_FF), x_prompt.dtype)
        yp, pb, pre, pim, pc = layer(yp, mk_p, mv_p, zpool, 0, zssm, zssm, zconv, **prm)
        ys, sb, sre, sim, sc = layer(ys, cache_mem_k[l], cache_mem_v[l], state_pool[l], PAST_LEN,
                                     state_ssm_re[l], state_ssm_im[l], state_conv[l], **prm)
        p_pool.append(pb); p_re.append(pre); p_im.append(pim); p_conv.append(pc)
        p_mk.append(mk_p); p_mv.append(mv_p)
        s_pool.append(sb); s_re.append(sre); s_im.append(sim); s_conv.append(sc)
    return (yp, ys,
            jnp.stack(p_pool), jnp.stack(p_re), jnp.stack(p_im), jnp.stack(p_conv),
            jnp.stack(p_mk), jnp.stack(p_mv),
            jnp.stack(s_pool), jnp.stack(s_re), jnp.stack(s_im), jnp.stack(s_conv))
```

```python
import functools
import math

import jax
import jax.numpy as jnp
from jax import lax
from jax.experimental import pallas as pl
from jax.experimental.pallas import tpu as pltpu

F32 = jnp.float32
BF16 = jnp.bfloat16

D_MODEL = 2048
BATCH = 4
SEQ = 2048
DEC_BATCH = 128
DEC_SEQ = 4
PAST_LEN = 16384
D_POOL = 1024
D_SSM = 1024
POOL_WINDOWS = (2, 4, 8, 16)
POOL_GROUP_DIM = 256
POOL_BUF = 15
SSM_GROUP_CH = 16
N_SSM_GROUPS = 64
SSM_STATE = 64
N_MEM = 256
N_XHEADS = 4
XHEAD_DIM = 512
D_FF = 5632
CONV_W = 3
ALPHA = 2.0 ** 0.25
LN_EPS = 1e-5

RP = BATCH * SEQ
RS = DEC_BATCH * DEC_SEQ
R = RP + RS
TM = 512
NB = R // TM
NBP = RP // TM
BLOCKS_PER_SEQ = SEQ // TM

CHUNK = 16
N_PAIRS = N_SSM_GROUPS // 2
PAIR_W = 2 * CHUNK * SSM_GROUP_CH
CHUNKS_PER_SEQ = SEQ // CHUNK
P_CHUNK_ROWS = BATCH * CHUNKS_PER_SEQ
SSM_ROWS = P_CHUNK_ROWS + DEC_BATCH
SCAN_PAD = 64
N_SCAN_STEPS = 7

TF = 512
NF = D_FF // TF

VMEM_LIMIT = 56 * 1024 * 1024


def _cparams(sem):
    return pltpu.CompilerParams(dimension_semantics=sem, vmem_limit_bytes=VMEM_LIMIT)


def _const_spec(shape):
    n = len(shape)
    return pl.BlockSpec(shape, lambda *_: (0,) * n, pipeline_mode=pl.Buffered(1))


def _layer_norm(x, g, b):
    mu = jnp.mean(x, axis=-1, keepdims=True)
    xc = x - mu
    var = jnp.mean(xc * xc, axis=-1, keepdims=True)
    return xc * lax.rsqrt(var + LN_EPS) * g + b


def _dot(a, b):
    return jnp.dot(a, b, preferred_element_type=F32)


def _inproj_kernel(xp_ref, xs_ref, w_ref, o_ref):
    i = pl.program_id(0)

    @pl.when(i < NBP)
    def _():
        o_ref[...] = _dot(xp_ref[...].astype(BF16), w_ref[...])

    @pl.when(i >= NBP)
    def _():
        o_ref[...] = _dot(xs_ref[...].astype(BF16), w_ref[...])


def _inproj(xp, xs, w):
    return pl.pallas_call(
        _inproj_kernel,
        out_shape=jax.ShapeDtypeStruct((R, D_MODEL), F32),
        grid=(NB,),
        in_specs=[
            pl.BlockSpec((TM, D_MODEL), lambda i: (jnp.minimum(i, NBP - 1), 0)),
            pl.BlockSpec((TM, D_MODEL), lambda i: (0, 0)),
            _const_spec((D_MODEL, D_MODEL)),
        ],
        out_specs=pl.BlockSpec((TM, D_MODEL), lambda i: (i, 0)),
        compiler_params=_cparams(("arbitrary",)),
        name="inproj",
    )(xp, xs, w)


def _pool_prompt_kernel(u_ref, w_ref, scale_ref, o_ref, ext_ref):
    i = pl.program_id(0)
    first = (i % BLOCKS_PER_SEQ) == 0

    @pl.when(first)
    def _():
        ext_ref[0:16, :] = jnp.zeros((16, D_POOL), F32)

    @pl.when(jnp.logical_not(first))
    def _():
        ext_ref[0:16, :] = ext_ref[TM:TM + 16, :]

    ext_ref[16:16 + TM, :] = u_ref[...]
    pos = (i % BLOCKS_PER_SEQ) * TM + lax.broadcasted_iota(jnp.int32, (TM, 1), 0)
    for g, w in enumerate(POOL_WINDOWS):
        sl = slice(g * POOL_GROUP_DIM, (g + 1) * POOL_GROUP_DIM)
        acc = ext_ref[16:16 + TM, sl]
        for k in range(1, w):
            acc = acc + ext_ref[16 - k:16 - k + TM, sl]
        cnt = jnp.minimum(pos + 1, w).astype(F32)
        pooled = acc / cnt - u_ref[:, sl]
        mixed = _dot(pooled.astype(BF16), w_ref[g]) * scale_ref[:, sl]
        o_ref[:, sl] = mixed.astype(BF16)


def _pool_prompt(u, w_pool, pool_scale):
    return pl.pallas_call(
        _pool_prompt_kernel,
        out_shape=jax.ShapeDtypeStruct((R, D_POOL), BF16),
        grid=(NBP,),
        in_specs=[
            pl.BlockSpec((TM, D_POOL), lambda i: (i, 0)),
            _const_spec((4, POOL_GROUP_DIM, POOL_GROUP_DIM)),
            _const_spec((1, D_POOL)),
        ],
        out_specs=pl.BlockSpec((TM, D_POOL), lambda i: (i, 0)),
        scratch_shapes=[pltpu.VMEM((TM + 16, D_POOL), F32)],
        compiler_params=_cparams(("arbitrary",)),
        name="pool_prompt",
    )(u, w_pool, pool_scale)


def _pool_sample_kernel(st_ref, u_ref, w_ref, scale_ref, a_hbm_ref, o_ref):
    del a_hbm_ref
    for j in range(DEC_SEQ):
        rows = slice(j * DEC_BATCH, (j + 1) * DEC_BATCH)
        for g, w in enumerate(POOL_WINDOWS):
            sl = slice(g * POOL_GROUP_DIM, (g + 1) * POOL_GROUP_DIM)
            acc = u_ref[rows, sl]
            for k in range(1, w):
                e = POOL_BUF + j - k
                if e >= POOL_BUF:
                    t = e - POOL_BUF
                    acc = acc + u_ref[t * DEC_BATCH:(t + 1) * DEC_BATCH, sl]
                else:
                    acc = acc + st_ref[e, :, sl]
            cnt = float(min(PAST_LEN + j + 1, w))
            pooled = acc / cnt - u_ref[rows, sl]
            mixed = _dot(pooled.astype(BF16), w_ref[g]) * scale_ref[:, sl]
            o_ref[rows, sl] = mixed.astype(BF16)


def _pool_sample(state_t, u, w_pool, pool_scale, a_out):
    return pl.pallas_call(
        _pool_sample_kernel,
        out_shape=jax.ShapeDtypeStruct((R, D_POOL), BF16),
        grid=(1,),
        in_specs=[
            pl.BlockSpec((POOL_BUF, DEC_BATCH, D_POOL), lambda i: (0, 0, 0)),
            pl.BlockSpec((RS, D_POOL), lambda i: (NBP, 0)),
            pl.BlockSpec((4, POOL_GROUP_DIM, POOL_GROUP_DIM), lambda i: (0, 0, 0)),
            pl.BlockSpec((1, D_POOL), lambda i: (0, 0)),
            pl.BlockSpec(memory_space=pl.ANY),
        ],
        out_specs=pl.BlockSpec((RS, D_POOL), lambda i: (NBP, 0)),
        input_output_aliases={4: 0},
        compiler_params=_cparams(("arbitrary",)),
        name="pool_sample",
    )(state_t, u, w_pool, pool_scale, a_out)


def _ssm_kernel(u_ref, t_ref, pre_ref, pim_ref, p4re_ref, p4im_ref, qre_ref, qim_ref,
                apow_re_ref, apow_im_ref, lam4_ref, dskip_ref, h0re_ref, h0im_ref,
                z_ref, hp_re_ref, hp_im_ref, hs_re_ref, hs_im_ref,
                hre_scr, him_scr):
    u = u_ref[0]
    ub = u.astype(BF16)
    half = PAIR_W // 2
    y = jnp.concatenate([_dot(ub[:, :half], t_ref[0, 0]), _dot(ub[:, half:], t_ref[0, 1])], axis=1)

    ubp = ub[:P_CHUNK_ROWS]
    hre_scr[0:SCAN_PAD, :] = jnp.zeros((SCAN_PAD, 128), F32)
    him_scr[0:SCAN_PAD, :] = jnp.zeros((SCAN_PAD, 128), F32)
    hre_scr[SCAN_PAD:SCAN_PAD + P_CHUNK_ROWS, :] = _dot(ubp, pre_ref[0])
    him_scr[SCAN_PAD:SCAN_PAD + P_CHUNK_ROWS, :] = _dot(ubp, pim_ref[0])
    kk = lax.broadcasted_iota(jnp.int32, (P_CHUNK_ROWS, 1), 0) % CHUNKS_PER_SEQ
    for s in range(N_SCAN_STEPS):
        d = 1 << s
        ar = apow_re_ref[0, s:s + 1, :]
        ai = apow_im_ref[0, s:s + 1, :]
        hr = hre_scr[SCAN_PAD:SCAN_PAD + P_CHUNK_ROWS, :]
        hi = him_scr[SCAN_PAD:SCAN_PAD + P_CHUNK_ROWS, :]
        pr = hre_scr[SCAN_PAD - d:SCAN_PAD - d + P_CHUNK_ROWS, :]
        pi = him_scr[SCAN_PAD - d:SCAN_PAD - d + P_CHUNK_ROWS, :]
        keep = kk >= d
        hre_scr[SCAN_PAD:SCAN_PAD + P_CHUNK_ROWS, :] = hr + jnp.where(keep, ar * pr - ai * pi, 0.0)
        him_scr[SCAN_PAD:SCAN_PAD + P_CHUNK_ROWS, :] = hi + jnp.where(keep, ar * pi + ai * pr, 0.0)
    hp_re_ref[0] = jnp.zeros((8, 128), F32)
    hp_im_ref[0] = jnp.zeros((8, 128), F32)
    for b in range(BATCH):
        last = SCAN_PAD + (b + 1) * CHUNKS_PER_SEQ - 1
        hp_re_ref[0, b:b + 1, :] = hre_scr[last:last + 1, :]
        hp_im_ref[0, b:b + 1, :] = him_scr[last:last + 1, :]
    prev_ok = kk >= 1
    hprev_re = jnp.where(prev_ok, hre_scr[SCAN_PAD - 1:SCAN_PAD - 1 + P_CHUNK_ROWS, :], 0.0)
    hprev_im = jnp.where(prev_ok, him_scr[SCAN_PAD - 1:SCAN_PAD - 1 + P_CHUNK_ROWS, :], 0.0)
    carry_p = _dot(hprev_re.astype(BF16), qre_ref[0]) + _dot(hprev_im.astype(BF16), qim_ref[0])

    ubs = ub[P_CHUNK_ROWS:]
    h0r = h0re_ref[0]
    h0i = h0im_ref[0]
    l4r = lam4_ref[0, 0:1, :]
    l4i = lam4_ref[0, 1:2, :]
    hs_re_ref[0] = l4r * h0r - l4i * h0i + _dot(ubs, p4re_ref[0])
    hs_im_ref[0] = l4r * h0i + l4i * h0r + _dot(ubs, p4im_ref[0])
    carry_s = _dot(h0r.astype(BF16), qre_ref[0]) + _dot(h0i.astype(BF16), qim_ref[0])

    y = y + jnp.concatenate([carry_p, carry_s], axis=0) + dskip_ref[0] * u
    z_ref[0] = jax.nn.gelu(y).astype(BF16)


def _ssm(u_pairs, prep, h0re, h0im):
    pair3 = lambda a, b: pl.BlockSpec((1, a, b), lambda p: (p, 0, 0))
    return pl.pallas_call(
        _ssm_kernel,
        out_shape=(
            jax.ShapeDtypeStruct((N_PAIRS, SSM_ROWS, PAIR_W), BF16),
            jax.ShapeDtypeStruct((N_PAIRS, 8, 128), F32),
            jax.ShapeDtypeStruct((N_PAIRS, 8, 128), F32),
            jax.ShapeDtypeStruct((N_PAIRS, DEC_BATCH, 128), F32),
            jax.ShapeDtypeStruct((N_PAIRS, DEC_BATCH, 128), F32),
        ),
        grid=(N_PAIRS,),
        in_specs=[
            pair3(SSM_ROWS, PAIR_W),
            pl.BlockSpec((1, 2, 256, 256), lambda p: (p, 0, 0, 0)),
            pair3(PAIR_W, 128), pair3(PAIR_W, 128), pair3(PAIR_W, 128), pair3(PAIR_W, 128),
            pair3(128, PAIR_W), pair3(128, PAIR_W),
            pair3(8, 128), pair3(8, 128), pair3(8, 128),
            pair3(1, PAIR_W),
            pair3(DEC_BATCH, 128), pair3(DEC_BATCH, 128),
        ],
        out_specs=(
            pair3(SSM_ROWS, PAIR_W), pair3(8, 128), pair3(8, 128),
            pair3(DEC_BATCH, 128), pair3(DEC_BATCH, 128),
        ),
        scratch_shapes=[pltpu.VMEM((SCAN_PAD + P_CHUNK_ROWS, 128), F32),
                        pltpu.VMEM((SCAN_PAD + P_CHUNK_ROWS, 128), F32)],
        compiler_params=_cparams(("arbitrary",)),
        name="ssm",
    )(u_pairs, prep["t"], prep["pre"], prep["pim"], prep["p4re"], prep["p4im"],
      prep["qre"], prep["qim"], prep["apow_re"], prep["apow_im"], prep["lam4"],
      prep["dskip"], h0re, h0im)


def _ssm_prep(lambda_re, lambda_im, log_step, b_re, b_im, c_re, c_im, d_skip):
    G, N, C, L = N_SSM_GROUPS, SSM_STATE, SSM_GROUP_CH, CHUNK
    hi = lax.Precision.HIGHEST
    dt = jnp.exp(log_step)[:, None]
    mag = jnp.exp(lambda_re * dt)
    ang = lambda_im * dt
    lr, li = mag * jnp.cos(ang), mag * jnp.sin(ang)
    den = lambda_re * lambda_re + lambda_im * lambda_im
    fr = ((lr - 1.0) * lambda_re + li * lambda_im) / den
    fi = (li * lambda_re - (lr - 1.0) * lambda_im) / den
    bbr = fr[:, :, None] * b_re - fi[:, :, None] * b_im
    bbi = fr[:, :, None] * b_im + fi[:, :, None] * b_re
    pr, pi = [jnp.ones_like(lr)], [jnp.zeros_like(lr)]
    for _ in range(L):
        pr.append(pr[-1] * lr - pi[-1] * li)
        pi.append(pr[-2] * li + pi[-1] * lr)
    pwr, pwi = jnp.stack(pr, 1), jnp.stack(pi, 1)

    ckr = c_re[:, None] * pwr[:, :L, None, :] - c_im[:, None] * pwi[:, :L, None, :]
    cki = c_re[:, None] * pwi[:, :L, None, :] + c_im[:, None] * pwr[:, :L, None, :]
    kern = (jnp.einsum("gkxn,gnc->gkxc", ckr, bbr, precision=hi)
            - jnp.einsum("gkxn,gnc->gkxc", cki, bbi, precision=hi))
    kern = jnp.concatenate([kern, jnp.zeros((G, 1, C, C), F32)], axis=1)
    ii = jnp.arange(L)
    lag = ii[None, :] - ii[:, None]
    lag = jnp.where(lag >= 0, lag, L)
    t5 = kern[:, lag]
    t = t5.transpose(0, 1, 4, 2, 3).reshape(G, L * C, L * C)

    akr = pwr[:, :L, :, None] * bbr[:, None] - pwi[:, :L, :, None] * bbi[:, None]
    aki = pwr[:, :L, :, None] * bbi[:, None] + pwi[:, :L, :, None] * bbr[:, None]

    def p_mat(a, first_pow, n_steps):
        rows = a[:, first_pow - jnp.arange(n_steps)]
        rows = rows.transpose(0, 1, 3, 2).reshape(G, n_steps * C, N)
        return jnp.concatenate([rows, jnp.zeros((G, (L - n_steps) * C, N), F32)], axis=1)

    cqr = c_re[:, None] * pwr[:, 1:, None, :] - c_im[:, None] * pwi[:, 1:, None, :]
    cqi = c_re[:, None] * pwi[:, 1:, None, :] + c_im[:, None] * pwr[:, 1:, None, :]
    q_mat = lambda a: a.transpose(0, 3, 1, 2).reshape(G, N, L * C)

    def pair_rows(m):
        m = m.reshape(N_PAIRS, 2, m.shape[1], m.shape[2])
        z = jnp.zeros_like(m[:, 0])
        return jnp.concatenate([jnp.concatenate([m[:, 0], z], axis=2),
                                jnp.concatenate([z, m[:, 1]], axis=2)], axis=1)

    def pair_lanes(v):
        v = v.reshape(N_PAIRS, 2, v.shape[1], v.shape[2])
        return jnp.concatenate([v[:, 0], v[:, 1]], axis=2)

    ar, ai = [pwr[:, L]], [pwi[:, L]]
    for _ in range(N_SCAN_STEPS - 1):
        ar.append(ar[-1] * ar[-1] - ai[-1] * ai[-1])
        ai.append(2.0 * ar[-2] * ai[-1])
    pad8 = lambda v: jnp.concatenate([v, jnp.zeros((N_PAIRS, 8 - v.shape[1], 128), F32)], axis=1)
    dsk = jnp.broadcast_to(d_skip.reshape(N_PAIRS, 2, 1, C), (N_PAIRS, 2, L, C)).reshape(N_PAIRS, 1, PAIR_W)
    return dict(
        t=t.reshape(N_PAIRS, 2, L * C, L * C).astype(BF16),
        pre=pair_rows(p_mat(akr, L - 1, L)).astype(BF16),
        pim=pair_rows(p_mat(aki, L - 1, L)).astype(BF16),
        p4re=pair_rows(p_mat(akr, DEC_SEQ - 1, DEC_SEQ)).astype(BF16),
        p4im=pair_rows(p_mat(aki, DEC_SEQ - 1, DEC_SEQ)).astype(BF16),
        qre=pair_rows(q_mat(cqr)).astype(BF16),
        qim=pair_rows(q_mat(-cqi)).astype(BF16),
        apow_re=pad8(pair_lanes(jnp.stack(ar, 1))),
        apow_im=pad8(pair_lanes(jnp.stack(ai, 1))),
        lam4=pad8(pair_lanes(jnp.stack([pwr[:, DEC_SEQ], pwi[:, DEC_SEQ]], 1))),
        dskip=dsk,
    )


def _mix_kernel(a_ref, z_ref, xp_ref, xs_ref, wglu_ref, bglu_ref, wout_ref, g_ref, b_ref, o_ref):
    i = pl.program_id(0)
    z = z_ref[...]
    gate = _dot(z, wglu_ref[...]) + bglu_ref[...]
    bmix = (z.astype(F32) * jax.nn.sigmoid(gate)).astype(BF16)
    mix = _dot(a_ref[...], wout_ref[0:D_POOL, :]) + _dot(bmix, wout_ref[D_POOL:, :])

    @pl.when(i < NBP)
    def _():
        o_ref[...] = _layer_norm(ALPHA * xp_ref[...] + mix, g_ref[...], b_ref[...])

    @pl.when(i >= NBP)
    def _():
        o_ref[...] = _layer_norm(ALPHA * xs_ref[...] + mix, g_ref[...], b_ref[...])


def _mix(a, z, xp, xs, w_glu, b_glu, w_out, g, b):
    row = lambda w: pl.BlockSpec((TM, w), lambda i: (i, 0))
    return pl.pallas_call(
        _mix_kernel,
        out_shape=jax.ShapeDtypeStruct((R, D_MODEL), F32),
        grid=(NB,),
        in_specs=[
            row(D_POOL), row(D_SSM),
            pl.BlockSpec((TM, D_MODEL), lambda i: (jnp.minimum(i, NBP - 1), 0)),
            pl.BlockSpec((TM, D_MODEL), lambda i: (0, 0)),
            _const_spec((D_SSM, D_SSM)), _const_spec((1, D_SSM)),
            _const_spec((D_MODEL, D_MODEL)), _const_spec((1, D_MODEL)), _const_spec((1, D_MODEL)),
        ],
        out_specs=row(D_MODEL),
        compiler_params=_cparams(("arbitrary",)),
        name="mix_ln1",
    )(a, z, xp, xs, w_glu, b_glu, w_out, g, b)


def _proj_kernel(x_ref, w_ref, o_ref):
    o_ref[...] = _dot(x_ref[...].astype(BF16), w_ref[...]).astype(o_ref.dtype)


def _proj(x, w, out_dtype, name):
    rows = x.shape[0]
    return pl.pallas_call(
        _proj_kernel,
        out_shape=jax.ShapeDtypeStruct((rows, w.shape[1]), out_dtype),
        grid=(rows // TM,),
        in_specs=[pl.BlockSpec((TM, x.shape[1]), lambda i: (i, 0)), _const_spec(w.shape)],
        out_specs=pl.BlockSpec((TM, w.shape[1]), lambda i: (i, 0)),
        compiler_params=_cparams(("arbitrary",)),
        name=name,
    )(x, w)


def _proj_ln_kernel(o_in_ref, h_ref, w_ref, g_ref, b_ref, out_ref):
    y = _dot(o_in_ref[...], w_ref[...])
    out_ref[...] = _layer_norm(ALPHA * h_ref[...] + y, g_ref[...], b_ref[...])


def _proj_ln(o_in, h, w, g, b):
    row = pl.BlockSpec((TM, D_MODEL), lambda i: (i, 0))
    return pl.pallas_call(
        _proj_ln_kernel,
        out_shape=jax.ShapeDtypeStruct((R, D_MODEL), F32),
        grid=(NB,),
        in_specs=[row, row, _const_spec((D_MODEL, D_MODEL)),
                  _const_spec((1, D_MODEL)), _const_spec((1, D_MODEL))],
        out_specs=row,
        compiler_params=_cparams(("arbitrary",)),
        name="oproj_ln2",
    )(o_in, h, w, g, b)


def _memkv_kernel(m_ref, wk_ref, wv_ref, k_ref, v_ref, kb_ref, vb_ref):
    mb = m_ref[...].astype(BF16)
    k = _dot(mb, wk_ref[...])
    v = _dot(mb, wv_ref[...])
    k_ref[...] = k
    v_ref[...] = v
    kb_ref[...] = k.astype(BF16)
    vb_ref[...] = v.astype(BF16)


def _memkv(mem, wk, wv):
    rows = BATCH * N_MEM
    tn = 512
    col = pl.BlockSpec((rows, tn), lambda j: (0, j))
    wcol = pl.BlockSpec((D_MODEL, tn), lambda j: (0, j))
    return pl.pallas_call(
        _memkv_kernel,
        out_shape=(jax.ShapeDtypeStruct((rows, D_MODEL), F32), jax.ShapeDtypeStruct((rows, D_MODEL), F32),
                   jax.ShapeDtypeStruct((rows, D_MODEL), BF16), jax.ShapeDtypeStruct((rows, D_MODEL), BF16)),
        grid=(D_MODEL // tn,),
        in_specs=[_const_spec((rows, D_MODEL)), wcol, wcol],
        out_specs=(col, col, col, col),
        compiler_params=_cparams(("arbitrary",)),
        name="memkv",
    )(mem, wk, wv)


def _attend(q, k, v):
    outs = []
    for h in range(N_XHEADS):
        sl = slice(h * XHEAD_DIM, (h + 1) * XHEAD_DIM)
        s = lax.dot_general(q[:, sl], k[:, sl], (((1,), (1,)), ((), ())),
                            preferred_element_type=F32) * (XHEAD_DIM ** -0.5)
        s = s - jnp.max(s, axis=-1, keepdims=True)
        e = jnp.exp(s)
        p = e / jnp.sum(e, axis=-1, keepdims=True)
        outs.append(_dot(p.astype(BF16), v[:, sl]))
    return jnp.concatenate(outs, axis=1)


def _attn_prompt_kernel(q_ref, k_ref, v_ref, o_ref):
    o_ref[...] = _attend(q_ref[...], k_ref[...], v_ref[...]).astype(BF16)


def _attn_prompt(q, kb, vb):
    kv = pl.BlockSpec((N_MEM, D_MODEL), lambda i: (i // BLOCKS_PER_SEQ, 0))
    row = pl.BlockSpec((TM, D_MODEL), lambda i: (i, 0))
    return pl.pallas_call(
        _attn_prompt_kernel,
        out_shape=jax.ShapeDtypeStruct((R, D_MODEL), BF16),
        grid=(NBP,),
        in_specs=[row, kv, kv],
        out_specs=row,
        compiler_params=_cparams(("arbitrary",)),
        name="attn_prompt",
    )(q, kb, vb)


ATT_BB = 4
Q_PAD = 8


def _attn_sample_kernel(q_ref, k_ref, v_ref, o_ref):
    for b in range(ATT_BB):
        o_ref[b] = _attend(q_ref[b].astype(BF16), k_ref[b].astype(BF16), v_ref[b].astype(BF16))


def _attn_sample(q_pad, k, v):
    kv = pl.BlockSpec((ATT_BB, N_MEM, D_MODEL), lambda i: (i, 0, 0))
    qo = pl.BlockSpec((ATT_BB, Q_PAD, D_MODEL), lambda i: (i, 0, 0))
    return pl.pallas_call(
        _attn_sample_kernel,
        out_shape=jax.ShapeDtypeStruct((DEC_BATCH, Q_PAD, D_MODEL), F32),
        grid=(DEC_BATCH // ATT_BB,),
        in_specs=[qo, kv, kv],
        out_specs=qo,
        compiler_params=_cparams(("arbitrary",)),
        name="attn_sample",
    )(q_pad, k, v)


def _ffn_common(f, h_ref, hb_scr, y_ref, wd_ref, lng_ref, lnb_ref, act):
    contrib = _dot(act, wd_ref[...])

    @pl.when(f == 0)
    def _():
        y_ref[...] = ALPHA * h_ref[...] + contrib

    @pl.when(jnp.logical_and(f > 0, f < NF - 1))
    def _():
        y_ref[...] = y_ref[...] + contrib

    @pl.when(f == NF - 1)
    def _():
        y_ref[...] = _layer_norm(y_ref[...] + contrib, lng_ref[...], lnb_ref[...])


def _ffn_prompt_kernel(h_ref, wg_ref, wu_ref, wd_ref, cw_ref, cb_ref, lng_ref, lnb_ref,
                       y_ref, gtail_ref, hb_scr, g_scr, carry_scr):
    i = pl.program_id(0)
    f = pl.program_id(1)

    @pl.when(f == 0)
    def _():
        hb_scr[...] = h_ref[...].astype(BF16)

    hb = hb_scr[...]
    g = _dot(hb, wg_ref[...])
    up = _dot(hb, wu_ref[...])
    first = (i % BLOCKS_PER_SEQ) == 0

    @pl.when(first)
    def _():
        g_scr[0:8, :] = jnp.zeros((8, TF), F32)

    @pl.when(jnp.logical_not(first))
    def _():
        g_scr[0:8, :] = carry_scr[f]

    g_scr[8:8 + TM, :] = g
    tail = g_scr[TM:TM + 8, :]
    carry_scr[f] = tail
    gtail_ref[0] = tail
    gc = (cb_ref[...] + cw_ref[0:1, :] * g_scr[6:6 + TM, :] + cw_ref[1:2, :] * g_scr[7:7 + TM, :]
          + cw_ref[2:3, :] * g)
    act = (jax.nn.silu(gc) * up).astype(BF16)
    _ffn_common(f, h_ref, hb_scr, y_ref, wd_ref, lng_ref, lnb_ref, act)


def _ffn_sample_kernel(h_ref, wg_ref, wu_ref, wd_ref, cw_ref, cb_ref, lng_ref, lnb_ref, st_ref,
                       y_ref, gnew_ref, hb_scr, g_scr):
    f = pl.program_id(1)

    @pl.when(f == 0)
    def _():
        hb_scr[...] = h_ref[...].astype(BF16)

    hb = hb_scr[...]
    g = _dot(hb, wg_ref[...])
    up = _dot(hb, wu_ref[...])
    n_st = 2 * DEC_BATCH
    g_scr[0:n_st, :] = st_ref[...]
    g_scr[n_st:n_st + RS, :] = g
    gnew_ref[...] = g_scr[RS:RS + n_st, :]
    gc = (cb_ref[...] + cw_ref[0:1, :] * g_scr[0:RS, :]
          + cw_ref[1:2, :] * g_scr[DEC_BATCH:DEC_BATCH + RS, :] + cw_ref[2:3, :] * g)
    act = (jax.nn.silu(gc) * up).astype(BF16)
    _ffn_common(f, h_ref, hb_scr, y_ref, wd_ref, lng_ref, lnb_ref, act)


def _ffn_specs(row_block_of):
    return [
        pl.BlockSpec((TM, D_MODEL), lambda i, f: (row_block_of(i), 0)),
        pl.BlockSpec((D_MODEL, TF), lambda i, f: (0, f)),
        pl.BlockSpec((D_MODEL, TF), lambda i, f: (0, f)),
        pl.BlockSpec((TF, D_MODEL), lambda i, f: (f, 0)),
        pl.BlockSpec((CONV_W, TF), lambda i, f: (0, f)),
        pl.BlockSpec((1, TF), lambda i, f: (0, f)),
        pl.BlockSpec((1, D_MODEL), lambda i, f: (0, 0)),
        pl.BlockSpec((1, D_MODEL), lambda i, f: (0, 0)),
    ]


def _ffn_prompt(h, wg, wu, wd, cw, cb, lng, lnb):
    return pl.pallas_call(
        _ffn_prompt_kernel,
        out_shape=(jax.ShapeDtypeStruct((RP, D_MODEL), F32),
                   jax.ShapeDtypeStruct((NBP, 8, D_FF), F32)),
        grid=(NBP, NF),
        in_specs=_ffn_specs(lambda i: i),
        out_specs=(pl.BlockSpec((TM, D_MODEL), lambda i, f: (i, 0)),
                   pl.BlockSpec((1, 8, TF), lambda i, f: (i, 0, f))),
        scratch_shapes=[pltpu.VMEM((TM, D_MODEL), BF16),
                        pltpu.VMEM((TM + 8, TF), F32),
                        pltpu.VMEM((NF, 8, TF), F32)],
        compiler_params=_cparams(("arbitrary", "arbitrary")),
        name="ffn_prompt",
    )(h, wg, wu, wd, cw, cb, lng, lnb)


def _ffn_sample(h, wg, wu, wd, cw, cb, lng, lnb, conv_state):
    n_st = 2 * DEC_BATCH
    return pl.pallas_call(
        _ffn_sample_kernel,
        out_shape=(jax.ShapeDtypeStruct((RS, D_MODEL), F32),
                   jax.ShapeDtypeStruct((n_st, D_FF), F32)),
        grid=(1, NF),
        in_specs=_ffn_specs(lambda i: NBP) + [pl.BlockSpec((n_st, TF), lambda i, f: (0, f))],
        out_specs=(pl.BlockSpec((RS, D_MODEL), lambda i, f: (0, 0)),
                   pl.BlockSpec((n_st, TF), lambda i, f: (0, f))),
        scratch_shapes=[pltpu.VMEM((RS, D_MODEL), BF16),
                        pltpu.VMEM((n_st + RS, TF), F32)],
        compiler_params=_cparams(("arbitrary", "arbitrary")),
        name="ffn_sample",
    )(h, wg, wu, wd, cw, cb, lng, lnb, conv_state)


def _to_pairs(u_ssm):
    c, l = SSM_GROUP_CH, CHUNK
    up = u_ssm[:RP].reshape(P_CHUNK_ROWS, l, N_PAIRS, 2, c).transpose(2, 0, 3, 1, 4)
    up = up.reshape(N_PAIRS, P_CHUNK_ROWS, PAIR_W)
    us = u_ssm[RP:].reshape(DEC_SEQ, DEC_BATCH, N_PAIRS, 2, c).transpose(2, 1, 3, 0, 4)
    us = jnp.pad(us, ((0, 0), (0, 0), (0, 0), (0, l - DEC_SEQ), (0, 0))).reshape(N_PAIRS, DEC_BATCH, PAIR_W)
    return jnp.concatenate([up, us], axis=1)


def _from_pairs(z_pairs):
    c, l = SSM_GROUP_CH, CHUNK
    zp = z_pairs[:, :P_CHUNK_ROWS].reshape(N_PAIRS, P_CHUNK_ROWS, 2, l, c).transpose(1, 3, 0, 2, 4)
    zp = zp.reshape(RP, D_SSM)
    zs = z_pairs[:, P_CHUNK_ROWS:].reshape(N_PAIRS, DEC_BATCH, 2, l, c)[:, :, :, :DEC_SEQ]
    zs = zs.transpose(3, 1, 0, 2, 4).reshape(RS, D_SSM)
    return jnp.concatenate([zp, zs], axis=0)


def _state_to_pairs(s):
    return s.reshape(s.shape[0], N_PAIRS, 128).transpose(1, 0, 2)


def _state_from_pairs(s):
    return s.transpose(1, 0, 2).reshape(1, s.shape[1], N_SSM_GROUPS, SSM_STATE)


def kernel(x_prompt, x_sample, mem_prompt, state_pool, state_ssm_re, state_ssm_im, state_conv, cache_mem_k, cache_mem_v, w_in, w_pool, pool_scale, lambda_re, lambda_im, log_step, b_re, b_im, c_re, c_im, d_skip, w_glu, b_glu, w_out, ln1_g, ln1_b, w_q, w_k, w_v, w_o, ln2_g, ln2_b, w_gate, w_up, conv_w, conv_b, w_down, ln3_g, ln3_b):
    bf = lambda w: w[0].astype(BF16)
    row = lambda v: v[0].reshape(1, -1)
    xp = x_prompt.reshape(RP, D_MODEL)
    xs = x_sample.transpose(1, 0, 2).reshape(RS, D_MODEL)

    u = _inproj(xp, xs, bf(w_in))
    u_pool, u_ssm = u[:, :D_POOL], u[:, D_POOL:]
    wp = bf(w_pool)
    a = _pool_prompt(u, wp, row(pool_scale))
    a = _pool_sample(state_pool[0].transpose(1, 0, 2), u, wp, row(pool_scale), a)
    prep = _ssm_prep(lambda_re[0], lambda_im[0], log_step[0], b_re[0], b_im[0], c_re[0], c_im[0], d_skip[0])
    z_pairs, hp_re, hp_im, hs_re, hs_im = _ssm(
        _to_pairs(u_ssm), prep, _state_to_pairs(state_ssm_re[0]), _state_to_pairs(state_ssm_im[0]))
    z = _from_pairs(z_pairs)
    h1 = _mix(a, z, xp, xs, bf(w_glu), row(b_glu), bf(w_out), row(ln1_g), row(ln1_b))

    mk, mv, mkb, mvb = _memkv(mem_prompt.reshape(BATCH * N_MEM, D_MODEL), bf(w_k), bf(w_v))
    q = _proj(h1, bf(w_q), BF16, "qproj")
    o_p = _attn_prompt(q, mkb, mvb)
    q_s = q[RP:].reshape(DEC_SEQ, DEC_BATCH, D_MODEL).transpose(1, 0, 2)
    q_s = jnp.pad(q_s, ((0, 0), (0, Q_PAD - DEC_SEQ), (0, 0)))
    o_s = _attn_sample(q_s, cache_mem_k[0].reshape(DEC_BATCH, N_MEM, D_MODEL),
                       cache_mem_v[0].reshape(DEC_BATCH, N_MEM, D_MODEL))
    o_s = o_s[:, :DEC_SEQ].transpose(1, 0, 2).reshape(RS, D_MODEL).astype(BF16)
    o_all = lax.dynamic_update_slice(o_p, o_s, (RP, 0))
    h2 = _proj_ln(o_all, h1, bf(w_o), row(ln2_g), row(ln2_b))

    wg, wu, wd = bf(w_gate), bf(w_up), bf(w_down)
    cw, cb = conv_w[0], row(conv_b)
    y_p, gtail = _ffn_prompt(h2, wg, wu, wd, cw, cb, row(ln3_g), row(ln3_b))
    conv_st = state_conv[0].transpose(1, 0, 2).reshape(2 * DEC_BATCH, D_FF)
    y_s, g_new = _ffn_sample(h2, wg, wu, wd, cw, cb, row(ln3_g), row(ln3_b), conv_st)

    y_prompt = y_p.reshape(BATCH, SEQ, D_MODEL)
    y_sample = y_s.reshape(DEC_SEQ, DEC_BATCH, D_MODEL).transpose(1, 0, 2)
    p_pool = u_pool[:RP].reshape(BATCH, SEQ, D_POOL)[None, :, SEQ - POOL_BUF:]
    s_ext = jnp.concatenate([state_pool[0], u_pool[RP:].reshape(DEC_SEQ, DEC_BATCH, D_POOL).transpose(1, 0, 2)], axis=1)
    s_pool = s_ext[None, :, DEC_SEQ:]
    p_conv = gtail[BLOCKS_PER_SEQ - 1::BLOCKS_PER_SEQ, 6:8][None]
    s_conv = g_new.reshape(2, DEC_BATCH, D_FF).transpose(1, 0, 2)[None]
    shape_kv = (1, BATCH, N_MEM, N_XHEADS, XHEAD_DIM)
    return (y_prompt, y_sample,
            p_pool, _state_from_pairs(hp_re[:, :BATCH]), _state_from_pairs(hp_im[:, :BATCH]), p_conv,
            mk.reshape(shape_kv), mv.reshape(shape_kv),
            s_pool, _state_from_pairs(hs_re), _state_from_pairs(hs_im), s_conv)
```

```python
import functools
import math

import jax
import jax.numpy as jnp
from jax import lax
from jax.experimental import pallas as pl
from jax.experimental.pallas import tpu as pltpu

F32 = jnp.float32
BF16 = jnp.bfloat16

D_MODEL = 2048
BATCH = 4
SEQ = 2048
DEC_BATCH = 128
DEC_SEQ = 4
PAST_LEN = 16384
D_POOL = 1024
D_SSM = 1024
POOL_WINDOWS = (2, 4, 8, 16)
POOL_GROUP_DIM = 256
POOL_BUF = 15
SSM_GROUP_CH = 16
N_SSM_GROUPS = 64
SSM_STATE = 64
N_MEM = 256
N_XHEADS = 4
XHEAD_DIM = 512
D_FF = 5632
CONV_W = 3
ALPHA = 2.0 ** 0.25
LN_EPS = 1e-5

RP = BATCH * SEQ
RS = DEC_BATCH * DEC_SEQ
R = RP + RS
TM = 512
NB = R // TM
NBP = RP // TM
BLOCKS_PER_SEQ = SEQ // TM

CHUNK = 16
N_PAIRS = N_SSM_GROUPS // 2
PAIR_W = 2 * CHUNK * SSM_GROUP_CH
CHUNKS_PER_SEQ = SEQ // CHUNK
P_CHUNK_ROWS = BATCH * CHUNKS_PER_SEQ
SSM_ROWS = P_CHUNK_ROWS + DEC_BATCH
SCAN_PAD = 64
N_SCAN_STEPS = 7

TF = 512
NF = D_FF // TF

VMEM_LIMIT = 56 * 1024 * 1024


def _cparams(sem):
    return pltpu.CompilerParams(dimension_semantics=sem, vmem_limit_bytes=VMEM_LIMIT)


def _const_spec(shape):
    n = len(shape)
    return pl.BlockSpec(shape, lambda *_: (0,) * n, pipeline_mode=pl.Buffered(1))


def _layer_norm(x, g, b):
    mu = jnp.mean(x, axis=-1, keepdims=True)
    xc = x - mu
    var = jnp.mean(xc * xc, axis=-1, keepdims=True)
    return xc * lax.rsqrt(var + LN_EPS) * g + b


def _dot(a, b):
    return jnp.dot(a, b, preferred_element_type=F32)


def _inproj_kernel(xp_ref, xs_ref, w_ref, o_ref):
    i = pl.program_id(0)

    @pl.when(i < NBP)
    def _():
        o_ref[...] = _dot(xp_ref[...].astype(BF16), w_ref[...])

    @pl.when(i >= NBP)
    def _():
        o_ref[...] = _dot(xs_ref[...].astype(BF16), w_ref[...])


def _inproj(xp, xs, w):
    return pl.pallas_call(
        _inproj_kernel,
        out_shape=jax.ShapeDtypeStruct((R, D_MODEL), F32),
        grid=(NB,),
        in_specs=[
            pl.BlockSpec((TM, D_MODEL), lambda i: (jnp.minimum(i, NBP - 1), 0)),
            pl.BlockSpec((TM, D_MODEL), lambda i: (0, 0)),
            _const_spec((D_MODEL, D_MODEL)),
        ],
        out_specs=pl.BlockSpec((TM, D_MODEL), lambda i: (i, 0)),
        compiler_params=_cparams(("arbitrary",)),
        name="inproj",
    )(xp, xs, w)


def _pool_prompt_kernel(u_ref, w_ref, scale_ref, o_ref, ext_ref):
    i = pl.program_id(0)
    first = (i % BLOCKS_PER_SEQ) == 0

    @pl.when(first)
    def _():
        ext_ref[0:16, :] = jnp.zeros((16, D_POOL), F32)

    @pl.when(jnp.logical_not(first))
    def _():
        ext_ref[0:16, :] = ext_ref[TM:TM + 16, :]

    ext_ref[16:16 + TM, :] = u_ref[...]
    pos = (i % BLOCKS_PER_SEQ) * TM + lax.broadcasted_iota(jnp.int32, (TM, 1), 0)
    for g, w in enumerate(POOL_WINDOWS):
        sl = slice(g * POOL_GROUP_DIM, (g + 1) * POOL_GROUP_DIM)
        acc = ext_ref[16:16 + TM, sl]
        for k in range(1, w):
            acc = acc + ext_ref[16 - k:16 - k + TM, sl]
        cnt = jnp.minimum(pos + 1, w).astype(F32)
        pooled = acc / cnt - u_ref[:, sl]
        mixed = _dot(pooled.astype(BF16), w_ref[g]) * scale_ref[:, sl]
        o_ref[:, sl] = mixed.astype(BF16)


def _pool_prompt(u, w_pool, pool_scale):
    return pl.pallas_call(
        _pool_prompt_kernel,
        out_shape=jax.ShapeDtypeStruct((R, D_POOL), BF16),
        grid=(NBP,),
        in_specs=[
            pl.BlockSpec((TM, D_POOL), lambda i: (i, 0)),
            _const_spec((4, POOL_GROUP_DIM, POOL_GROUP_DIM)),
            _const_spec((1, D_POOL)),
        ],
        out_specs=pl.BlockSpec((TM, D_POOL), lambda i: (i, 0)),
        scratch_shapes=[pltpu.VMEM((TM + 16, D_POOL), F32)],
        compiler_params=_cparams(("arbitrary",)),
        name="pool_prompt",
    )(u, w_pool, pool_scale)


def _pool_sample_kernel(st_ref, u_ref, w_ref, scale_ref, a_hbm_ref, o_ref):
    del a_hbm_ref
    for j in range(DEC_SEQ):
        rows = slice(j * DEC_BATCH, (j + 1) * DEC_BATCH)
        for g, w in enumerate(POOL_WINDOWS):
            sl = slice(g * POOL_GROUP_DIM, (g + 1) * POOL_GROUP_DIM)
            acc = u_ref[rows, sl]
            for k in range(1, w):
                e = POOL_BUF + j - k
                if e >= POOL_BUF:
                    t = e - POOL_BUF
                    acc = acc + u_ref[t * DEC_BATCH:(t + 1) * DEC_BATCH, sl]
                else:
                    acc = acc + st_ref[e, :, sl]
            cnt = float(min(PAST_LEN + j + 1, w))
            pooled = acc / cnt - u_ref[rows, sl]
            mixed = _dot(pooled.astype(BF16), w_ref[g]) * scale_ref[:, sl]
            o_ref[rows, sl] = mixed.astype(BF16)


def _pool_sample(state_t, u, w_pool, pool_scale, a_out):
    return pl.pallas_call(
        _pool_sample_kernel,
        out_shape=jax.ShapeDtypeStruct((R, D_POOL), BF16),
        grid=(1,),
        in_specs=[
            pl.BlockSpec((POOL_BUF, DEC_BATCH, D_POOL), lambda i: (0, 0, 0)),
            pl.BlockSpec((RS, D_POOL), lambda i: (NBP, 0)),
            pl.BlockSpec((4, POOL_GROUP_DIM, POOL_GROUP_DIM), lambda i: (0, 0, 0)),
            pl.BlockSpec((1, D_POOL), lambda i: (0, 0)),
            pl.BlockSpec(memory_space=pl.ANY),
        ],
        out_specs=pl.BlockSpec((RS, D_POOL), lambda i: (NBP, 0)),
        input_output_aliases={4: 0},
        compiler_params=_cparams(("arbitrary",)),
        name="pool_sample",
    )(state_t, u, w_pool, pool_scale, a_out)


SLAB_GROUPS = 128 // SSM_GROUP_CH
SLAB_PAIRS = SLAB_GROUPS // 2
N_SLABS = N_SSM_GROUPS // SLAB_GROUPS


def _ssm_pair(q, u, t_ref, pre_ref, pim_ref, p4re_ref, p4im_ref, qre_ref, qim_ref,
              apow_re_ref, apow_im_ref, lam4_ref, dskip_ref, h0re_ref, h0im_ref,
              hp_re_ref, hp_im_ref, hs_re_ref, hs_im_ref, hre_scr, him_scr):
    ub = u.astype(BF16)
    half = PAIR_W // 2
    y = jnp.concatenate([_dot(ub[:, :half], t_ref[q, 0]), _dot(ub[:, half:], t_ref[q, 1])], axis=1)

    ubp = ub[:P_CHUNK_ROWS]
    hre_scr[SCAN_PAD:SCAN_PAD + P_CHUNK_ROWS, :] = _dot(ubp, pre_ref[q])
    him_scr[SCAN_PAD:SCAN_PAD + P_CHUNK_ROWS, :] = _dot(ubp, pim_ref[q])
    kk = lax.broadcasted_iota(jnp.int32, (P_CHUNK_ROWS, 1), 0) % CHUNKS_PER_SEQ
    for s in range(N_SCAN_STEPS):
        d = 1 << s
        ar = apow_re_ref[q, s:s + 1, :]
        ai = apow_im_ref[q, s:s + 1, :]
        hr = hre_scr[SCAN_PAD:SCAN_PAD + P_CHUNK_ROWS, :]
        hi = him_scr[SCAN_PAD:SCAN_PAD + P_CHUNK_ROWS, :]
        pr = hre_scr[SCAN_PAD - d:SCAN_PAD - d + P_CHUNK_ROWS, :]
        pi = him_scr[SCAN_PAD - d:SCAN_PAD - d + P_CHUNK_ROWS, :]
        keep = kk >= d
        hre_scr[SCAN_PAD:SCAN_PAD + P_CHUNK_ROWS, :] = hr + jnp.where(keep, ar * pr - ai * pi, 0.0)
        him_scr[SCAN_PAD:SCAN_PAD + P_CHUNK_ROWS, :] = hi + jnp.where(keep, ar * pi + ai * pr, 0.0)
    hp_re_ref[q] = jnp.zeros((8, 128), F32)
    hp_im_ref[q] = jnp.zeros((8, 128), F32)
    for b in range(BATCH):
        last = SCAN_PAD + (b + 1) * CHUNKS_PER_SEQ - 1
        hp_re_ref[q, b:b + 1, :] = hre_scr[last:last + 1, :]
        hp_im_ref[q, b:b + 1, :] = him_scr[last:last + 1, :]
    prev_ok = kk >= 1
    hprev_re = jnp.where(prev_ok, hre_scr[SCAN_PAD - 1:SCAN_PAD - 1 + P_CHUNK_ROWS, :], 0.0)
    hprev_im = jnp.where(prev_ok, him_scr[SCAN_PAD - 1:SCAN_PAD - 1 + P_CHUNK_ROWS, :], 0.0)
    carry_p = _dot(hprev_re.astype(BF16), qre_ref[q]) + _dot(hprev_im.astype(BF16), qim_ref[q])

    ubs = ub[P_CHUNK_ROWS:]
    h0r = h0re_ref[q]
    h0i = h0im_ref[q]
    l4r = lam4_ref[q, 0:1, :]
    l4i = lam4_ref[q, 1:2, :]
    hs_re_ref[q] = l4r * h0r - l4i * h0i + _dot(ubs, p4re_ref[q])
    hs_im_ref[q] = l4r * h0i + l4i * h0r + _dot(ubs, p4im_ref[q])
    carry_s = _dot(h0r.astype(BF16), qre_ref[q]) + _dot(h0i.astype(BF16), qim_ref[q])

    y = y + jnp.concatenate([carry_p, carry_s], axis=0) + dskip_ref[q] * u
    return jax.nn.gelu(y)


def _ssm_kernel(u_ref, *refs):
    z_ref = refs[13]
    hre_scr, him_scr = refs[18], refs[19]
    c = SSM_GROUP_CH
    hre_scr[0:SCAN_PAD, :] = jnp.zeros((SCAN_PAD, 128), F32)
    him_scr[0:SCAN_PAD, :] = jnp.zeros((SCAN_PAD, 128), F32)
    xt = []
    for i in range(CHUNK):
        xp = u_ref[pl.ds(i, P_CHUNK_ROWS, stride=CHUNK), :]
        if i < DEC_SEQ:
            xs = u_ref[RP + i * DEC_BATCH:RP + (i + 1) * DEC_BATCH, :]
        else:
            xs = jnp.zeros((DEC_BATCH, 128), F32)
        xt.append(jnp.concatenate([xp, xs], axis=0).T)
    zt = []
    for q in range(SLAB_PAIRS):
        halves = []
        for e in range(2):
            g = 2 * q + e
            bt = jnp.concatenate([xt[i][g * c:(g + 1) * c, :] for i in range(CHUNK)], axis=0)
            halves.append(bt.T)
        z = _ssm_pair(q, jnp.concatenate(halves, axis=1), *refs[:13], *refs[14:])
        zt.append(z[:, :PAIR_W // 2].T)
        zt.append(z[:, PAIR_W // 2:].T)
    for i in range(CHUNK):
        zi = jnp.concatenate([zt[g][i * c:(i + 1) * c, :] for g in range(SLAB_GROUPS)], axis=0).T
        z_ref[pl.ds(i, P_CHUNK_ROWS, stride=CHUNK), :] = zi[:P_CHUNK_ROWS]
        if i < DEC_SEQ:
            z_ref[RP + i * DEC_BATCH:RP + (i + 1) * DEC_BATCH, :] = zi[P_CHUNK_ROWS:]


def _ssm(u, prep, h0re, h0im):
    sp = SLAB_PAIRS
    slab3 = lambda a, b: pl.BlockSpec((sp, a, b), lambda s: (s, 0, 0))
    return pl.pallas_call(
        _ssm_kernel,
        out_shape=(
            jax.ShapeDtypeStruct((R, D_SSM), F32),
            jax.ShapeDtypeStruct((N_PAIRS, 8, 128), F32),
            jax.ShapeDtypeStruct((N_PAIRS, 8, 128), F32),
            jax.ShapeDtypeStruct((N_PAIRS, DEC_BATCH, 128), F32),
            jax.ShapeDtypeStruct((N_PAIRS, DEC_BATCH, 128), F32),
        ),
        grid=(N_SLABS,),
        in_specs=[
            pl.BlockSpec((R, 128), lambda s: (0, D_POOL // 128 + s)),
            pl.BlockSpec((sp, 2, 256, 256), lambda s: (s, 0, 0, 0)),
            slab3(PAIR_W, 128), slab3(PAIR_W, 128), slab3(PAIR_W, 128), slab3(PAIR_W, 128),
            slab3(128, PAIR_W), slab3(128, PAIR_W),
            slab3(8, 128), slab3(8, 128), slab3(8, 128),
            slab3(1, PAIR_W),
            slab3(DEC_BATCH, 128), slab3(DEC_BATCH, 128),
        ],
        out_specs=(
            pl.BlockSpec((R, 128), lambda s: (0, s)),
            slab3(8, 128), slab3(8, 128), slab3(DEC_BATCH, 128), slab3(DEC_BATCH, 128),
        ),
        scratch_shapes=[pltpu.VMEM((SCAN_PAD + P_CHUNK_ROWS, 128), F32),
                        pltpu.VMEM((SCAN_PAD + P_CHUNK_ROWS, 128), F32)],
        compiler_params=_cparams(("arbitrary",)),
        name="ssm",
    )(u, prep["t"], prep["pre"], prep["pim"], prep["p4re"], prep["p4im"],
      prep["qre"], prep["qim"], prep["apow_re"], prep["apow_im"], prep["lam4"],
      prep["dskip"], h0re, h0im)


def _ssm_prep(lambda_re, lambda_im, log_step, b_re, b_im, c_re, c_im, d_skip):
    G, N, C, L = N_SSM_GROUPS, SSM_STATE, SSM_GROUP_CH, CHUNK
    hi = lax.Precision.HIGHEST
    dt = jnp.exp(log_step)[:, None]
    mag = jnp.exp(lambda_re * dt)
    ang = lambda_im * dt
    lr, li = mag * jnp.cos(ang), mag * jnp.sin(ang)
    den = lambda_re * lambda_re + lambda_im * lambda_im
    fr = ((lr - 1.0) * lambda_re + li * lambda_im) / den
    fi = (li * lambda_re - (lr - 1.0) * lambda_im) / den
    bbr = fr[:, :, None] * b_re - fi[:, :, None] * b_im
    bbi = fr[:, :, None] * b_im + fi[:, :, None] * b_re
    pr, pi = [jnp.ones_like(lr)], [jnp.zeros_like(lr)]
    for _ in range(L):
        pr.append(pr[-1] * lr - pi[-1] * li)
        pi.append(pr[-2] * li + pi[-1] * lr)
    pwr, pwi = jnp.stack(pr, 1), jnp.stack(pi, 1)

    ckr = c_re[:, None] * pwr[:, :L, None, :] - c_im[:, None] * pwi[:, :L, None, :]
    cki = c_re[:, None] * pwi[:, :L, None, :] + c_im[:, None] * pwr[:, :L, None, :]
    kern = (jnp.einsum("gkxn,gnc->gkxc", ckr, bbr, precision=hi)
            - jnp.einsum("gkxn,gnc->gkxc", cki, bbi, precision=hi))
    kern = jnp.concatenate([kern, jnp.zeros((G, 1, C, C), F32)], axis=1)
    ii = jnp.arange(L)
    lag = ii[None, :] - ii[:, None]
    lag = jnp.where(lag >= 0, lag, L)
    t5 = kern[:, lag]
    t = t5.transpose(0, 1, 4, 2, 3).reshape(G, L * C, L * C)

    akr = pwr[:, :L, :, None] * bbr[:, None] - pwi[:, :L, :, None] * bbi[:, None]
    aki = pwr[:, :L, :, None] * bbi[:, None] + pwi[:, :L, :, None] * bbr[:, None]

    def p_mat(a, first_pow, n_steps):
        rows = a[:, first_pow - jnp.arange(n_steps)]
        rows = rows.transpose(0, 1, 3, 2).reshape(G, n_steps * C, N)
        return jnp.concatenate([rows, jnp.zeros((G, (L - n_steps) * C, N), F32)], axis=1)

    cqr = c_re[:, None] * pwr[:, 1:, None, :] - c_im[:, None] * pwi[:, 1:, None, :]
    cqi = c_re[:, None] * pwi[:, 1:, None, :] + c_im[:, None] * pwr[:, 1:, None, :]
    q_mat = lambda a: a.transpose(0, 3, 1, 2).reshape(G, N, L * C)

    def pair_rows(m):
        m = m.reshape(N_PAIRS, 2, m.shape[1], m.shape[2])
        z = jnp.zeros_like(m[:, 0])
        return jnp.concatenate([jnp.concatenate([m[:, 0], z], axis=2),
                                jnp.concatenate([z, m[:, 1]], axis=2)], axis=1)

    def pair_lanes(v):
        v = v.reshape(N_PAIRS, 2, v.shape[1], v.shape[2])
        return jnp.concatenate([v[:, 0], v[:, 1]], axis=2)

    ar, ai = [pwr[:, L]], [pwi[:, L]]
    for _ in range(N_SCAN_STEPS - 1):
        ar.append(ar[-1] * ar[-1] - ai[-1] * ai[-1])
        ai.append(2.0 * ar[-2] * ai[-1])
    pad8 = lambda v: jnp.concatenate([v, jnp.zeros((N_PAIRS, 8 - v.shape[1], 128), F32)], axis=1)
    dsk = jnp.broadcast_to(d_skip.reshape(N_PAIRS, 2, 1, C), (N_PAIRS, 2, L, C)).reshape(N_PAIRS, 1, PAIR_W)
    return dict(
        t=t.reshape(N_PAIRS, 2, L * C, L * C).astype(BF16),
        pre=pair_rows(p_mat(akr, L - 1, L)).astype(BF16),
        pim=pair_rows(p_mat(aki, L - 1, L)).astype(BF16),
        p4re=pair_rows(p_mat(akr, DEC_SEQ - 1, DEC_SEQ)).astype(BF16),
        p4im=pair_rows(p_mat(aki, DEC_SEQ - 1, DEC_SEQ)).astype(BF16),
        qre=pair_rows(q_mat(cqr)).astype(BF16),
        qim=pair_rows(q_mat(-cqi)).astype(BF16),
        apow_re=pad8(pair_lanes(jnp.stack(ar, 1))),
        apow_im=pad8(pair_lanes(jnp.stack(ai, 1))),
        lam4=pad8(pair_lanes(jnp.stack([pwr[:, DEC_SEQ], pwi[:, DEC_SEQ]], 1))),
        dskip=dsk,
    )


def _mix_kernel(a_ref, z_ref, xp_ref, xs_ref, wglu_ref, bglu_ref, wout_ref, g_ref, b_ref, o_ref):
    i = pl.program_id(0)
    z = z_ref[...]
    gate = _dot(z.astype(BF16), wglu_ref[...]) + bglu_ref[...]
    bmix = (z * jax.nn.sigmoid(gate)).astype(BF16)
    mix = _dot(a_ref[...], wout_ref[0:D_POOL, :]) + _dot(bmix, wout_ref[D_POOL:, :])

    @pl.when(i < NBP)
    def _():
        o_ref[...] = _layer_norm(ALPHA * xp_ref[...] + mix, g_ref[...], b_ref[...])

    @pl.when(i >= NBP)
    def _():
        o_ref[...] = _layer_norm(ALPHA * xs_ref[...] + mix, g_ref[...], b_ref[...])


def _mix(a, z, xp, xs, w_glu, b_glu, w_out, g, b):
    row = lambda w: pl.BlockSpec((TM, w), lambda i: (i, 0))
    return pl.pallas_call(
        _mix_kernel,
        out_shape=jax.ShapeDtypeStruct((R, D_MODEL), F32),
        grid=(NB,),
        in_specs=[
            row(D_POOL), row(D_SSM),
            pl.BlockSpec((TM, D_MODEL), lambda i: (jnp.minimum(i, NBP - 1), 0)),
            pl.BlockSpec((TM, D_MODEL), lambda i: (0, 0)),
            _const_spec((D_SSM, D_SSM)), _const_spec((1, D_SSM)),
            _const_spec((D_MODEL, D_MODEL)), _const_spec((1, D_MODEL)), _const_spec((1, D_MODEL)),
        ],
        out_specs=row(D_MODEL),
        compiler_params=_cparams(("arbitrary",)),
        name="mix_ln1",
    )(a, z, xp, xs, w_glu, b_glu, w_out, g, b)


def _proj_kernel(x_ref, w_ref, o_ref):
    o_ref[...] = _dot(x_ref[...].astype(BF16), w_ref[...]).astype(o_ref.dtype)


def _proj(x, w, out_dtype, name):
    rows = x.shape[0]
    return pl.pallas_call(
        _proj_kernel,
        out_shape=jax.ShapeDtypeStruct((rows, w.shape[1]), out_dtype),
        grid=(rows // TM,),
        in_specs=[pl.BlockSpec((TM, x.shape[1]), lambda i: (i, 0)), _const_spec(w.shape)],
        out_specs=pl.BlockSpec((TM, w.shape[1]), lambda i: (i, 0)),
        compiler_params=_cparams(("arbitrary",)),
        name=name,
    )(x, w)


def _proj_ln_kernel(o_in_ref, h_ref, w_ref, g_ref, b_ref, out_ref):
    y = _dot(o_in_ref[...], w_ref[...])
    out_ref[...] = _layer_norm(ALPHA * h_ref[...] + y, g_ref[...], b_ref[...])


def _proj_ln(o_in, h, w, g, b):
    row = pl.BlockSpec((TM, D_MODEL), lambda i: (i, 0))
    return pl.pallas_call(
        _proj_ln_kernel,
        out_shape=jax.ShapeDtypeStruct((R, D_MODEL), F32),
        grid=(NB,),
        in_specs=[row, row, _const_spec((D_MODEL, D_MODEL)),
                  _const_spec((1, D_MODEL)), _const_spec((1, D_MODEL))],
        out_specs=row,
        compiler_params=_cparams(("arbitrary",)),
        name="oproj_ln2",
    )(o_in, h, w, g, b)


def _memkv_kernel(m_ref, wk_ref, wv_ref, k_ref, v_ref, kb_ref, vb_ref):
    mb = m_ref[...].astype(BF16)
    k = _dot(mb, wk_ref[...])
    v = _dot(mb, wv_ref[...])
    k_ref[...] = k
    v_ref[...] = v
    kb_ref[...] = k.astype(BF16)
    vb_ref[...] = v.astype(BF16)


def _memkv(mem, wk, wv):
    rows = BATCH * N_MEM
    tn = 512
    col = pl.BlockSpec((rows, tn), lambda j: (0, j))
    wcol = pl.BlockSpec((D_MODEL, tn), lambda j: (0, j))
    return pl.pallas_call(
        _memkv_kernel,
        out_shape=(jax.ShapeDtypeStruct((rows, D_MODEL), F32), jax.ShapeDtypeStruct((rows, D_MODEL), F32),
                   jax.ShapeDtypeStruct((rows, D_MODEL), BF16), jax.ShapeDtypeStruct((rows, D_MODEL), BF16)),
        grid=(D_MODEL // tn,),
        in_specs=[_const_spec((rows, D_MODEL)), wcol, wcol],
        out_specs=(col, col, col, col),
        compiler_params=_cparams(("arbitrary",)),
        name="memkv",
    )(mem, wk, wv)


def _attend(q, k, v):
    outs = []
    for h in range(N_XHEADS):
        sl = slice(h * XHEAD_DIM, (h + 1) * XHEAD_DIM)
        s = lax.dot_general(q[:, sl], k[:, sl], (((1,), (1,)), ((), ())),
                            preferred_element_type=F32) * (XHEAD_DIM ** -0.5)
        s = s - jnp.max(s, axis=-1, keepdims=True)
        e = jnp.exp(s)
        p = e / jnp.sum(e, axis=-1, keepdims=True)
        outs.append(_dot(p.astype(BF16), v[:, sl]))
    return jnp.concatenate(outs, axis=1)


def _attn_prompt_kernel(q_ref, k_ref, v_ref, o_ref):
    o_ref[...] = _attend(q_ref[...], k_ref[...], v_ref[...]).astype(BF16)


def _attn_prompt(q, kb, vb):
    kv = pl.BlockSpec((N_MEM, D_MODEL), lambda i: (i // BLOCKS_PER_SEQ, 0))
    row = pl.BlockSpec((TM, D_MODEL), lambda i: (i, 0))
    return pl.pallas_call(
        _attn_prompt_kernel,
        out_shape=jax.ShapeDtypeStruct((R, D_MODEL), BF16),
        grid=(NBP,),
        in_specs=[row, kv, kv],
        out_specs=row,
        compiler_params=_cparams(("arbitrary",)),
        name="attn_prompt",
    )(q, kb, vb)


ATT_BB = 4
Q_PAD = 8
KV_ROWS = N_MEM * N_XHEADS


def _attn_sample_kernel(q_ref, k_ref, v_ref, o_ref):
    shape = (N_XHEADS * Q_PAD, KV_ROWS)
    same_head = (lax.broadcasted_iota(jnp.int32, shape, 0) // Q_PAD
                 == lax.broadcasted_iota(jnp.int32, shape, 1) % N_XHEADS)
    for b in range(ATT_BB):
        q = q_ref[b].astype(F32)
        qs = jnp.concatenate([q[:, h * XHEAD_DIM:(h + 1) * XHEAD_DIM] for h in range(N_XHEADS)], axis=0)
        k = k_ref[0, b].reshape(KV_ROWS, XHEAD_DIM).astype(BF16)
        v = v_ref[0, b].reshape(KV_ROWS, XHEAD_DIM).astype(BF16)
        s = lax.dot_general(qs.astype(BF16), k, (((1,), (1,)), ((), ())),
                            preferred_element_type=F32) * (XHEAD_DIM ** -0.5)
        s = jnp.where(same_head, s, -1e30)
        s = s - jnp.max(s, axis=-1, keepdims=True)
        e = jnp.exp(s)
        p = e / jnp.sum(e, axis=-1, keepdims=True)
        o = _dot(p.astype(BF16), v)
        for h in range(N_XHEADS):
            o_ref[b, :, h * XHEAD_DIM:(h + 1) * XHEAD_DIM] = o[h * Q_PAD:(h + 1) * Q_PAD]


def _attn_sample(q_pad, k, v):
    kv = pl.BlockSpec((1, ATT_BB, N_MEM, N_XHEADS, XHEAD_DIM), lambda i: (0, i, 0, 0, 0))
    qo = pl.BlockSpec((ATT_BB, Q_PAD, D_MODEL), lambda i: (i, 0, 0))
    return pl.pallas_call(
        _attn_sample_kernel,
        out_shape=jax.ShapeDtypeStruct((DEC_BATCH, Q_PAD, D_MODEL), F32),
        grid=(DEC_BATCH // ATT_BB,),
        in_specs=[qo, kv, kv],
        out_specs=qo,
        compiler_params=_cparams(("arbitrary",)),
        name="attn_sample",
    )(q_pad, k, v)


def _ffn_common(f, h_ref, hb_scr, y_ref, wd_ref, lng_ref, lnb_ref, act):
    contrib = _dot(act, wd_ref[...])

    @pl.when(f == 0)
    def _():
        y_ref[...] = ALPHA * h_ref[...] + contrib

    @pl.when(jnp.logical_and(f > 0, f < NF - 1))
    def _():
        y_ref[...] = y_ref[...] + contrib

    @pl.when(f == NF - 1)
    def _():
        y_ref[...] = _layer_norm(y_ref[...] + contrib, lng_ref[...], lnb_ref[...])


def _ffn_prompt_kernel(h_ref, wg_ref, wu_ref, wd_ref, cw_ref, cb_ref, lng_ref, lnb_ref,
                       y_ref, gtail_ref, hb_scr, g_scr, carry_scr):
    i = pl.program_id(0)
    f = pl.program_id(1)

    @pl.when(f == 0)
    def _():
        hb_scr[...] = h_ref[...].astype(BF16)

    hb = hb_scr[...]
    g = _dot(hb, wg_ref[...])
    up = _dot(hb, wu_ref[...])
    first = (i % BLOCKS_PER_SEQ) == 0

    @pl.when(first)
    def _():
        g_scr[0:8, :] = jnp.zeros((8, TF), F32)

    @pl.when(jnp.logical_not(first))
    def _():
        g_scr[0:8, :] = carry_scr[f]

    g_scr[8:8 + TM, :] = g
    tail = g_scr[TM:TM + 8, :]
    carry_scr[f] = tail
    gtail_ref[0] = tail
    gc = (cb_ref[...] + cw_ref[0:1, :] * g_scr[6:6 + TM, :] + cw_ref[1:2, :] * g_scr[7:7 + TM, :]
          + cw_ref[2:3, :] * g)
    act = (jax.nn.silu(gc) * up).astype(BF16)
    _ffn_common(f, h_ref, hb_scr, y_ref, wd_ref, lng_ref, lnb_ref, act)


def _ffn_sample_kernel(h_ref, wg_ref, wu_ref, wd_ref, cw_ref, cb_ref, lng_ref, lnb_ref, st_ref,
                       y_ref, gnew_ref, hb_scr, g_scr):
    f = pl.program_id(1)

    @pl.when(f == 0)
    def _():
        hb_scr[...] = h_ref[...].astype(BF16)

    hb = hb_scr[...]
    g = _dot(hb, wg_ref[...])
    up = _dot(hb, wu_ref[...])
    n_st = 2 * DEC_BATCH
    g_scr[0:n_st, :] = st_ref[...]
    g_scr[n_st:n_st + RS, :] = g
    gnew_ref[...] = g_scr[RS:RS + n_st, :]
    gc = (cb_ref[...] + cw_ref[0:1, :] * g_scr[0:RS, :]
          + cw_ref[1:2, :] * g_scr[DEC_BATCH:DEC_BATCH + RS, :] + cw_ref[2:3, :] * g)
    act = (jax.nn.silu(gc) * up).astype(BF16)
    _ffn_common(f, h_ref, hb_scr, y_ref, wd_ref, lng_ref, lnb_ref, act)


def _ffn_specs(row_block_of):
    return [
        pl.BlockSpec((TM, D_MODEL), lambda i, f: (row_block_of(i), 0)),
        pl.BlockSpec((D_MODEL, TF), lambda i, f: (0, f)),
        pl.BlockSpec((D_MODEL, TF), lambda i, f: (0, f)),
        pl.BlockSpec((TF, D_MODEL), lambda i, f: (f, 0)),
        pl.BlockSpec((CONV_W, TF), lambda i, f: (0, f)),
        pl.BlockSpec((1, TF), lambda i, f: (0, f)),
        pl.BlockSpec((1, D_MODEL), lambda i, f: (0, 0)),
        pl.BlockSpec((1, D_MODEL), lambda i, f: (0, 0)),
    ]


def _ffn_prompt(h, wg, wu, wd, cw, cb, lng, lnb):
    return pl.pallas_call(
        _ffn_prompt_kernel,
        out_shape=(jax.ShapeDtypeStruct((RP, D_MODEL), F32),
                   jax.ShapeDtypeStruct((NBP, 8, D_FF), F32)),
        grid=(NBP, NF),
        in_specs=_ffn_specs(lambda i: i),
        out_specs=(pl.BlockSpec((TM, D_MODEL), lambda i, f: (i, 0)),
                   pl.BlockSpec((1, 8, TF), lambda i, f: (i, 0, f))),
        scratch_shapes=[pltpu.VMEM((TM, D_MODEL), BF16),
                        pltpu.VMEM((TM + 8, TF), F32),
                        pltpu.VMEM((NF, 8, TF), F32)],
        compiler_params=_cparams(("arbitrary", "arbitrary")),
        name="ffn_prompt",
    )(h, wg, wu, wd, cw, cb, lng, lnb)


def _ffn_sample(h, wg, wu, wd, cw, cb, lng, lnb, conv_state):
    n_st = 2 * DEC_BATCH
    return pl.pallas_call(
        _ffn_sample_kernel,
        out_shape=(jax.ShapeDtypeStruct((RS, D_MODEL), F32),
                   jax.ShapeDtypeStruct((n_st, D_FF), F32)),
        grid=(1, NF),
        in_specs=_ffn_specs(lambda i: NBP) + [pl.BlockSpec((n_st, TF), lambda i, f: (0, f))],
        out_specs=(pl.BlockSpec((RS, D_MODEL), lambda i, f: (0, 0)),
                   pl.BlockSpec((n_st, TF), lambda i, f: (0, f))),
        scratch_shapes=[pltpu.VMEM((RS, D_MODEL), BF16),
                        pltpu.VMEM((n_st + RS, TF), F32)],
        compiler_params=_cparams(("arbitrary", "arbitrary")),
        name="ffn_sample",
    )(h, wg, wu, wd, cw, cb, lng, lnb, conv_state)


def _state_to_pairs(s):
    return s.reshape(s.shape[0], N_PAIRS, 128).transpose(1, 0, 2)


def _state_from_pairs(s):
    return s.transpose(1, 0, 2).reshape(1, s.shape[1], N_SSM_GROUPS, SSM_STATE)


def kernel(x_prompt, x_sample, mem_prompt, state_pool, state_ssm_re, state_ssm_im, state_conv, cache_mem_k, cache_mem_v, w_in, w_pool, pool_scale, lambda_re, lambda_im, log_step, b_re, b_im, c_re, c_im, d_skip, w_glu, b_glu, w_out, ln1_g, ln1_b, w_q, w_k, w_v, w_o, ln2_g, ln2_b, w_gate, w_up, conv_w, conv_b, w_down, ln3_g, ln3_b):
    bf = lambda w: w[0].astype(BF16)
    row = lambda v: v[0].reshape(1, -1)
    xp = x_prompt.reshape(RP, D_MODEL)
    xs = x_sample.transpose(1, 0, 2).reshape(RS, D_MODEL)

    u = _inproj(xp, xs, bf(w_in))
    u_pool = u[:, :D_POOL]
    wp = bf(w_pool)
    a = _pool_prompt(u, wp, row(pool_scale))
    a = _pool_sample(state_pool[0].transpose(1, 0, 2), u, wp, row(pool_scale), a)
    prep = _ssm_prep(lambda_re[0], lambda_im[0], log_step[0], b_re[0], b_im[0], c_re[0], c_im[0], d_skip[0])
    z, hp_re, hp_im, hs_re, hs_im = _ssm(
        u, prep, _state_to_pairs(state_ssm_re[0]), _state_to_pairs(state_ssm_im[0]))
    h1 = _mix(a, z, xp, xs, bf(w_glu), row(b_glu), bf(w_out), row(ln1_g), row(ln1_b))

    mk, mv, mkb, mvb = _memkv(mem_prompt.reshape(BATCH * N_MEM, D_MODEL), bf(w_k), bf(w_v))
    q = _proj(h1, bf(w_q), BF16, "qproj")
    o_p = _attn_prompt(q, mkb, mvb)
    q_s = q[RP:].reshape(DEC_SEQ, DEC_BATCH, D_MODEL).transpose(1, 0, 2)
    q_s = jnp.pad(q_s, ((0, 0), (0, Q_PAD - DEC_SEQ), (0, 0)))
    o_s = _attn_sample(q_s, cache_mem_k, cache_mem_v)
    o_s = o_s[:, :DEC_SEQ].transpose(1, 0, 2).reshape(RS, D_MODEL).astype(BF16)
    o_all = lax.dynamic_update_slice(o_p, o_s, (RP, 0))
    h2 = _proj_ln(o_all, h1, bf(w_o), row(ln2_g), row(ln2_b))

    wg, wu, wd = bf(w_gate), bf(w_up), bf(w_down)
    cw, cb = conv_w[0], row(conv_b)
    y_p, gtail = _ffn_prompt(h2, wg, wu, wd, cw, cb, row(ln3_g), row(ln3_b))
    conv_st = state_conv[0].transpose(1, 0, 2).reshape(2 * DEC_BATCH, D_FF)
    y_s, g_new = _ffn_sample(h2, wg, wu, wd, cw, cb, row(ln3_g), row(ln3_b), conv_st)

    y_prompt = y_p.reshape(BATCH, SEQ, D_MODEL)
    y_sample = y_s.reshape(DEC_SEQ, DEC_BATCH, D_MODEL).transpose(1, 0, 2)
    p_pool = u_pool[:RP].reshape(BATCH, SEQ, D_POOL)[None, :, SEQ - POOL_BUF:]
    s_ext = jnp.concatenate([state_pool[0], u_pool[RP:].reshape(DEC_SEQ, DEC_BATCH, D_POOL).transpose(1, 0, 2)], axis=1)
    s_pool = s_ext[None, :, DEC_SEQ:]
    p_conv = gtail[BLOCKS_PER_SEQ - 1::BLOCKS_PER_SEQ, 6:8][None]
    s_conv = g_new.reshape(2, DEC_BATCH, D_FF).transpose(1, 0, 2)[None]
    shape_kv = (1, BATCH, N_MEM, N_XHEADS, XHEAD_DIM)
    return (y_prompt, y_sample,
            p_pool, _state_from_pairs(hp_re[:, :BATCH]), _state_from_pairs(hp_im[:, :BATCH]), p_conv,
            mk.reshape(shape_kv), mv.reshape(shape_kv),
            s_pool, _state_from_pairs(hs_re), _state_from_pairs(hs_im), s_conv)
```

```python
import functools
import math

import jax
import jax.numpy as jnp
from jax import lax
from jax.experimental import pallas as pl
from jax.experimental.pallas import tpu as pltpu

F32 = jnp.float32
BF16 = jnp.bfloat16

D_MODEL = 2048
BATCH = 4
SEQ = 2048
DEC_BATCH = 128
DEC_SEQ = 4
PAST_LEN = 16384
D_POOL = 1024
D_SSM = 1024
POOL_WINDOWS = (2, 4, 8, 16)
POOL_GROUP_DIM = 256
POOL_BUF = 15
SSM_GROUP_CH = 16
N_SSM_GROUPS = 64
SSM_STATE = 64
N_MEM = 256
N_XHEADS = 4
XHEAD_DIM = 512
D_FF = 5632
CONV_W = 3
ALPHA = 2.0 ** 0.25
LN_EPS = 1e-5

RP = BATCH * SEQ
RS = DEC_BATCH * DEC_SEQ
R = RP + RS
TM = 512
NB = R // TM
NBP = RP // TM
BLOCKS_PER_SEQ = SEQ // TM

CHUNK = 16
N_PAIRS = N_SSM_GROUPS // 2
PAIR_W = 2 * CHUNK * SSM_GROUP_CH
CHUNKS_PER_SEQ = SEQ // CHUNK
P_CHUNK_ROWS = BATCH * CHUNKS_PER_SEQ
SSM_ROWS = P_CHUNK_ROWS + DEC_BATCH
SCAN_PAD = 64
N_SCAN_STEPS = 7

SUB = 256
TF = 512
NF = D_FF // TF

VMEM_LIMIT = 56 * 1024 * 1024
VMEM_LIMIT_FFN = 60 * 1024 * 1024


def _cparams(sem, vmem_limit=VMEM_LIMIT):
    return pltpu.CompilerParams(dimension_semantics=sem, vmem_limit_bytes=vmem_limit)


def _const_spec(shape):
    n = len(shape)
    return pl.BlockSpec(shape, lambda *_: (0,) * n, pipeline_mode=pl.Buffered(1))


def _layer_norm(x, g, b):
    mu = jnp.mean(x, axis=-1, keepdims=True)
    xc = x - mu
    var = jnp.mean(xc * xc, axis=-1, keepdims=True)
    return xc * lax.rsqrt(var + LN_EPS) * g + b


def _dot(a, b):
    return jnp.dot(a, b, preferred_element_type=F32)


def _inproj_kernel(xp_ref, xs_ref, w_ref, o_ref):
    x = jnp.where(pl.program_id(0) < NBP, xp_ref[...], xs_ref[...])
    o_ref[...] = _dot(x.astype(BF16), w_ref[...])


def _inproj(xp, xs, w):
    return pl.pallas_call(
        _inproj_kernel,
        out_shape=jax.ShapeDtypeStruct((R, D_MODEL), F32),
        grid=(NB,),
        in_specs=[
            pl.BlockSpec((TM, D_MODEL), lambda i: (jnp.minimum(i, NBP - 1), 0)),
            pl.BlockSpec((TM, D_MODEL), lambda i: (0, 0)),
            _const_spec((D_MODEL, D_MODEL)),
        ],
        out_specs=pl.BlockSpec((TM, D_MODEL), lambda i: (i, 0)),
        compiler_params=_cparams(("arbitrary",)),
        name="inproj",
    )(xp, xs, w)


def _pool_mix(g, pooled, w_ref, scale_ref):
    sl = slice(g * POOL_GROUP_DIM, (g + 1) * POOL_GROUP_DIM)
    return (_dot(pooled.astype(BF16), w_ref[g]) * scale_ref[:, sl]).astype(BF16)


def _pool_kernel(u_ref, st_ref, w_ref, scale_ref, o_ref, ext_ref):
    i = pl.program_id(0)

    @pl.when(i < NBP)
    def _():
        first = (i % BLOCKS_PER_SEQ) == 0

        @pl.when(first)
        def _():
            ext_ref[0:16, :] = jnp.zeros((16, D_POOL), F32)

        @pl.when(jnp.logical_not(first))
        def _():
            ext_ref[0:16, :] = ext_ref[TM:TM + 16, :]

        ext_ref[16:16 + TM, :] = u_ref[...]
        pos = (i % BLOCKS_PER_SEQ) * TM + lax.broadcasted_iota(jnp.int32, (TM, 1), 0)
        for g, w in enumerate(POOL_WINDOWS):
            sl = slice(g * POOL_GROUP_DIM, (g + 1) * POOL_GROUP_DIM)
            acc = ext_ref[16:16 + TM, sl]
            for k in range(1, w):
                acc = acc + ext_ref[16 - k:16 - k + TM, sl]
            cnt = jnp.minimum(pos + 1, w).astype(F32)
            o_ref[:, sl] = _pool_mix(g, acc / cnt - u_ref[:, sl], w_ref, scale_ref)

    @pl.when(i == NBP)
    def _():
        for j in range(DEC_SEQ):
            rows = slice(j * DEC_BATCH, (j + 1) * DEC_BATCH)
            for g, w in enumerate(POOL_WINDOWS):
                sl = slice(g * POOL_GROUP_DIM, (g + 1) * POOL_GROUP_DIM)
                acc = u_ref[rows, sl]
                for k in range(1, w):
                    e = POOL_BUF + j - k
                    if e >= POOL_BUF:
                        t = e - POOL_BUF
                        acc = acc + u_ref[t * DEC_BATCH:(t + 1) * DEC_BATCH, sl]
                    else:
                        acc = acc + st_ref[e, :, sl]
                cnt = float(min(PAST_LEN + j + 1, w))
                o_ref[rows, sl] = _pool_mix(g, acc / cnt - u_ref[rows, sl], w_ref, scale_ref)


def _pool(u, state_t, w_pool, pool_scale):
    return pl.pallas_call(
        _pool_kernel,
        out_shape=jax.ShapeDtypeStruct((R, D_POOL), BF16),
        grid=(NB,),
        in_specs=[
            pl.BlockSpec((TM, D_POOL), lambda i: (i, 0)),
            _const_spec((POOL_BUF, DEC_BATCH, D_POOL)),
            _const_spec((4, POOL_GROUP_DIM, POOL_GROUP_DIM)),
            _const_spec((1, D_POOL)),
        ],
        out_specs=pl.BlockSpec((TM, D_POOL), lambda i: (i, 0)),
        scratch_shapes=[pltpu.VMEM((TM + 16, D_POOL), F32)],
        compiler_params=_cparams(("arbitrary",)),
        name="pool",
    )(u, state_t, w_pool, pool_scale)


SLAB_GROUPS = 128 // SSM_GROUP_CH
SLAB_PAIRS = SLAB_GROUPS // 2
N_SLABS = N_SSM_GROUPS // SLAB_GROUPS


def _ssm_pair(q, u, t_ref, pre_ref, pim_ref, p4re_ref, p4im_ref, qre_ref, qim_ref,
              apow_re_ref, apow_im_ref, lam4_ref, dskip_ref, h0re_ref, h0im_ref,
              hp_re_ref, hp_im_ref, hs_re_ref, hs_im_ref, hre_scr, him_scr):
    ub = u.astype(BF16)
    half = PAIR_W // 2
    y = jnp.concatenate([_dot(ub[:, :half], t_ref[q, 0]), _dot(ub[:, half:], t_ref[q, 1])], axis=1)

    ubp = ub[:P_CHUNK_ROWS]
    hre_scr[SCAN_PAD:SCAN_PAD + P_CHUNK_ROWS, :] = _dot(ubp, pre_ref[q])
    him_scr[SCAN_PAD:SCAN_PAD + P_CHUNK_ROWS, :] = _dot(ubp, pim_ref[q])
    kk = lax.broadcasted_iota(jnp.int32, (P_CHUNK_ROWS, 1), 0) % CHUNKS_PER_SEQ
    for s in range(N_SCAN_STEPS):
        d = 1 << s
        ar = apow_re_ref[q, s:s + 1, :]
        ai = apow_im_ref[q, s:s + 1, :]
        hr = hre_scr[SCAN_PAD:SCAN_PAD + P_CHUNK_ROWS, :]
        hi = him_scr[SCAN_PAD:SCAN_PAD + P_CHUNK_ROWS, :]
        pr = hre_scr[SCAN_PAD - d:SCAN_PAD - d + P_CHUNK_ROWS, :]
        pi = him_scr[SCAN_PAD - d:SCAN_PAD - d + P_CHUNK_ROWS, :]
        keep = kk >= d
        hre_scr[SCAN_PAD:SCAN_PAD + P_CHUNK_ROWS, :] = hr + jnp.where(keep, ar * pr - ai * pi, 0.0)
        him_scr[SCAN_PAD:SCAN_PAD + P_CHUNK_ROWS, :] = hi + jnp.where(keep, ar * pi + ai * pr, 0.0)
    hp_re_ref[q] = jnp.zeros((8, 128), F32)
    hp_im_ref[q] = jnp.zeros((8, 128), F32)
    for b in range(BATCH):
        last = SCAN_PAD + (b + 1) * CHUNKS_PER_SEQ - 1
        hp_re_ref[q, b:b + 1, :] = hre_scr[last:last + 1, :]
        hp_im_ref[q, b:b + 1, :] = him_scr[last:last + 1, :]
    prev_ok = kk >= 1
    hprev_re = jnp.where(prev_ok, hre_scr[SCAN_PAD - 1:SCAN_PAD - 1 + P_CHUNK_ROWS, :], 0.0)
    hprev_im = jnp.where(prev_ok, him_scr[SCAN_PAD - 1:SCAN_PAD - 1 + P_CHUNK_ROWS, :], 0.0)
    carry_p = _dot(hprev_re.astype(BF16), qre_ref[q]) + _dot(hprev_im.astype(BF16), qim_ref[q])

    ubs = ub[P_CHUNK_ROWS:]
    h0r = h0re_ref[q]
    h0i = h0im_ref[q]
    l4r = lam4_ref[q, 0:1, :]
    l4i = lam4_ref[q, 1:2, :]
    hs_re_ref[q] = l4r * h0r - l4i * h0i + _dot(ubs, p4re_ref[q])
    hs_im_ref[q] = l4r * h0i + l4i * h0r + _dot(ubs, p4im_ref[q])
    carry_s = _dot(h0r.astype(BF16), qre_ref[q]) + _dot(h0i.astype(BF16), qim_ref[q])

    y = y + jnp.concatenate([carry_p, carry_s], axis=0) + dskip_ref[q] * u
    return jax.nn.gelu(y)


def _ssm_kernel(u_ref, *refs):
    z_ref = refs[13]
    hre_scr, him_scr = refs[18], refs[19]
    c = SSM_GROUP_CH
    hre_scr[0:SCAN_PAD, :] = jnp.zeros((SCAN_PAD, 128), F32)
    him_scr[0:SCAN_PAD, :] = jnp.zeros((SCAN_PAD, 128), F32)
    xt = []
    for i in range(CHUNK):
        xp = u_ref[pl.ds(i, P_CHUNK_ROWS, stride=CHUNK), :]
        if i < DEC_SEQ:
            xs = u_ref[RP + i * DEC_BATCH:RP + (i + 1) * DEC_BATCH, :]
        else:
            xs = jnp.zeros((DEC_BATCH, 128), F32)
        xt.append(jnp.concatenate([xp, xs], axis=0).T)
    zt = []
    for q in range(SLAB_PAIRS):
        halves = []
        for e in range(2):
            g = 2 * q + e
            bt = jnp.concatenate([xt[i][g * c:(g + 1) * c, :] for i in range(CHUNK)], axis=0)
            halves.append(bt.T)
        z = _ssm_pair(q, jnp.concatenate(halves, axis=1), *refs[:13], *refs[14:])
        zt.append(z[:, :PAIR_W // 2].T)
        zt.append(z[:, PAIR_W // 2:].T)
    for i in range(CHUNK):
        zi = jnp.concatenate([zt[g][i * c:(i + 1) * c, :] for g in range(SLAB_GROUPS)], axis=0).T
        z_ref[pl.ds(i, P_CHUNK_ROWS, stride=CHUNK), :] = zi[:P_CHUNK_ROWS]
        if i < DEC_SEQ:
            z_ref[RP + i * DEC_BATCH:RP + (i + 1) * DEC_BATCH, :] = zi[P_CHUNK_ROWS:]


def _ssm(u, prep, h0re, h0im):
    sp = SLAB_PAIRS
    slab3 = lambda a, b: pl.BlockSpec((sp, a, b), lambda s: (s, 0, 0))
    return pl.pallas_call(
        _ssm_kernel,
        out_shape=(
            jax.ShapeDtypeStruct((R, D_SSM), F32),
            jax.ShapeDtypeStruct((N_PAIRS, 8, 128), F32),
            jax.ShapeDtypeStruct((N_PAIRS, 8, 128), F32),
            jax.ShapeDtypeStruct((N_PAIRS, DEC_BATCH, 128), F32),
            jax.ShapeDtypeStruct((N_PAIRS, DEC_BATCH, 128), F32),
        ),
        grid=(N_SLABS,),
        in_specs=[
            pl.BlockSpec((R, 128), lambda s: (0, D_POOL // 128 + s)),
            pl.BlockSpec((sp, 2, 256, 256), lambda s: (s, 0, 0, 0)),
            slab3(PAIR_W, 128), slab3(PAIR_W, 128), slab3(PAIR_W, 128), slab3(PAIR_W, 128),
            slab3(128, PAIR_W), slab3(128, PAIR_W),
            slab3(8, 128), slab3(8, 128), slab3(8, 128),
            slab3(1, PAIR_W),
            slab3(DEC_BATCH, 128), slab3(DEC_BATCH, 128),
        ],
        out_specs=(
            pl.BlockSpec((R, 128), lambda s: (0, s)),
            slab3(8, 128), slab3(8, 128), slab3(DEC_BATCH, 128), slab3(DEC_BATCH, 128),
        ),
        scratch_shapes=[pltpu.VMEM((SCAN_PAD + P_CHUNK_ROWS, 128), F32),
                        pltpu.VMEM((SCAN_PAD + P_CHUNK_ROWS, 128), F32)],
        compiler_params=_cparams(("arbitrary",)),
        name="ssm",
    )(u, prep["t"], prep["pre"], prep["pim"], prep["p4re"], prep["p4im"],
      prep["qre"], prep["qim"], prep["apow_re"], prep["apow_im"], prep["lam4"],
      prep["dskip"], h0re, h0im)


def _ssm_prep(lambda_re, lambda_im, log_step, b_re, b_im, c_re, c_im, d_skip):
    G, N, C, L = N_SSM_GROUPS, SSM_STATE, SSM_GROUP_CH, CHUNK
    hi = lax.Precision.HIGHEST
    dt = jnp.exp(log_step)[:, None]
    mag = jnp.exp(lambda_re * dt)
    ang = lambda_im * dt
    lr, li = mag * jnp.cos(ang), mag * jnp.sin(ang)
    den = lambda_re * lambda_re + lambda_im * lambda_im
    fr = ((lr - 1.0) * lambda_re + li * lambda_im) / den
    fi = (li * lambda_re - (lr - 1.0) * lambda_im) / den
    bbr = fr[:, :, None] * b_re - fi[:, :, None] * b_im
    bbi = fr[:, :, None] * b_im + fi[:, :, None] * b_re
    pr, pi = [jnp.ones_like(lr)], [jnp.zeros_like(lr)]
    for _ in range(L):
        pr.append(pr[-1] * lr - pi[-1] * li)
        pi.append(pr[-2] * li + pi[-1] * lr)
    pwr, pwi = jnp.stack(pr, 1), jnp.stack(pi, 1)

    ckr = c_re[:, None] * pwr[:, :L, None, :] - c_im[:, None] * pwi[:, :L, None, :]
    cki = c_re[:, None] * pwi[:, :L, None, :] + c_im[:, None] * pwr[:, :L, None, :]
    kern = (jnp.einsum("gkxn,gnc->gkxc", ckr, bbr, precision=hi)
            - jnp.einsum("gkxn,gnc->gkxc", cki, bbi, precision=hi))
    kern = jnp.concatenate([kern, jnp.zeros((G, 1, C, C), F32)], axis=1)
    ii = jnp.arange(L)
    lag = ii[None, :] - ii[:, None]
    lag = jnp.where(lag >= 0, lag, L)
    t5 = kern[:, lag]
    t = t5.transpose(0, 1, 4, 2, 3).reshape(G, L * C, L * C)

    akr = pwr[:, :L, :, None] * bbr[:, None] - pwi[:, :L, :, None] * bbi[:, None]
    aki = pwr[:, :L, :, None] * bbi[:, None] + pwi[:, :L, :, None] * bbr[:, None]

    def p_mat(a, first_pow, n_steps):
        rows = a[:, first_pow - jnp.arange(n_steps)]
        rows = rows.transpose(0, 1, 3, 2).reshape(G, n_steps * C, N)
        return jnp.concatenate([rows, jnp.zeros((G, (L - n_steps) * C, N), F32)], axis=1)

    cqr = c_re[:, None] * pwr[:, 1:, None, :] - c_im[:, None] * pwi[:, 1:, None, :]
    cqi = c_re[:, None] * pwi[:, 1:, None, :] + c_im[:, None] * pwr[:, 1:, None, :]
    q_mat = lambda a: a.transpose(0, 3, 1, 2).reshape(G, N, L * C)

    def pair_rows(m):
        m = m.reshape(N_PAIRS, 2, m.shape[1], m.shape[2])
        z = jnp.zeros_like(m[:, 0])
        return jnp.concatenate([jnp.concatenate([m[:, 0], z], axis=2),
                                jnp.concatenate([z, m[:, 1]], axis=2)], axis=1)

    def pair_lanes(v):
        v = v.reshape(N_PAIRS, 2, v.shape[1], v.shape[2])
        return jnp.concatenate([v[:, 0], v[:, 1]], axis=2)

    ar, ai = [pwr[:, L]], [pwi[:, L]]
    for _ in range(N_SCAN_STEPS - 1):
        ar.append(ar[-1] * ar[-1] - ai[-1] * ai[-1])
        ai.append(2.0 * ar[-2] * ai[-1])
    pad8 = lambda v: jnp.concatenate([v, jnp.zeros((N_PAIRS, 8 - v.shape[1], 128), F32)], axis=1)
    dsk = jnp.broadcast_to(d_skip.reshape(N_PAIRS, 2, 1, C), (N_PAIRS, 2, L, C)).reshape(N_PAIRS, 1, PAIR_W)
    return dict(
        t=t.reshape(N_PAIRS, 2, L * C, L * C).astype(BF16),
        pre=pair_rows(p_mat(akr, L - 1, L)).astype(BF16),
        pim=pair_rows(p_mat(aki, L - 1, L)).astype(BF16),
        p4re=pair_rows(p_mat(akr, DEC_SEQ - 1, DEC_SEQ)).astype(BF16),
        p4im=pair_rows(p_mat(aki, DEC_SEQ - 1, DEC_SEQ)).astype(BF16),
        qre=pair_rows(q_mat(cqr)).astype(BF16),
        qim=pair_rows(q_mat(-cqi)).astype(BF16),
        apow_re=pad8(pair_lanes(jnp.stack(ar, 1))),
        apow_im=pad8(pair_lanes(jnp.stack(ai, 1))),
        lam4=pad8(pair_lanes(jnp.stack([pwr[:, DEC_SEQ], pwi[:, DEC_SEQ]], 1))),
        dskip=dsk,
    )


def _mix_kernel(a_ref, z_ref, xp_ref, xs_ref, wglu_ref, bglu_ref, wout_ref, g_ref, b_ref, o_ref):
    is_prompt = pl.program_id(0) < NBP
    for r in range(TM // SUB):
        rows = slice(r * SUB, (r + 1) * SUB)
        z = z_ref[rows, :]
        gate = _dot(z.astype(BF16), wglu_ref[...]) + bglu_ref[...]
        bmix = (z * jax.nn.sigmoid(gate)).astype(BF16)
        mix = _dot(a_ref[rows, :], wout_ref[0:D_POOL, :]) + _dot(bmix, wout_ref[D_POOL:, :])
        x = jnp.where(is_prompt, xp_ref[rows, :], xs_ref[rows, :])
        o_ref[rows, :] = _layer_norm(ALPHA * x + mix, g_ref[...], b_ref[...])


def _mix(a, z, xp, xs, w_glu, b_glu, w_out, g, b):
    row = lambda w: pl.BlockSpec((TM, w), lambda i: (i, 0))
    return pl.pallas_call(
        _mix_kernel,
        out_shape=jax.ShapeDtypeStruct((R, D_MODEL), F32),
        grid=(NB,),
        in_specs=[
            row(D_POOL), row(D_SSM),
            pl.BlockSpec((TM, D_MODEL), lambda i: (jnp.minimum(i, NBP - 1), 0)),
            pl.BlockSpec((TM, D_MODEL), lambda i: (0, 0)),
            _const_spec((D_SSM, D_SSM)), _const_spec((1, D_SSM)),
            _const_spec((D_MODEL, D_MODEL)), _const_spec((1, D_MODEL)), _const_spec((1, D_MODEL)),
        ],
        out_specs=row(D_MODEL),
        compiler_params=_cparams(("arbitrary",)),
        name="mix_ln1",
    )(a, z, xp, xs, w_glu, b_glu, w_out, g, b)


def _proj_kernel(x_ref, w_ref, o_ref):
    o_ref[...] = _dot(x_ref[...].astype(BF16), w_ref[...]).astype(o_ref.dtype)


def _proj(x, w, out_dtype, name):
    rows = x.shape[0]
    return pl.pallas_call(
        _proj_kernel,
        out_shape=jax.ShapeDtypeStruct((rows, w.shape[1]), out_dtype),
        grid=(rows // TM,),
        in_specs=[pl.BlockSpec((TM, x.shape[1]), lambda i: (i, 0)), _const_spec(w.shape)],
        out_specs=pl.BlockSpec((TM, w.shape[1]), lambda i: (i, 0)),
        compiler_params=_cparams(("arbitrary",)),
        name=name,
    )(x, w)


def _proj_ln_kernel(o_in_ref, h_ref, w_ref, g_ref, b_ref, out_ref):
    for r in range(TM // SUB):
        rows = slice(r * SUB, (r + 1) * SUB)
        y = _dot(o_in_ref[rows, :], w_ref[...])
        out_ref[rows, :] = _layer_norm(ALPHA * h_ref[rows, :] + y, g_ref[...], b_ref[...])


def _proj_ln(o_in, h, w, g, b):
    row = pl.BlockSpec((TM, D_MODEL), lambda i: (i, 0))
    return pl.pallas_call(
        _proj_ln_kernel,
        out_shape=jax.ShapeDtypeStruct((R, D_MODEL), F32),
        grid=(NB,),
        in_specs=[row, row, _const_spec((D_MODEL, D_MODEL)),
                  _const_spec((1, D_MODEL)), _const_spec((1, D_MODEL))],
        out_specs=row,
        compiler_params=_cparams(("arbitrary",)),
        name="oproj_ln2",
    )(o_in, h, w, g, b)


def _memkv_kernel(m_ref, wk_ref, wv_ref, k_ref, v_ref, kb_ref, vb_ref):
    mb = m_ref[...].astype(BF16)
    k = _dot(mb, wk_ref[...])
    v = _dot(mb, wv_ref[...])
    k_ref[...] = k
    v_ref[...] = v
    kb_ref[...] = k.astype(BF16)
    vb_ref[...] = v.astype(BF16)


def _memkv(mem, wk, wv):
    rows = BATCH * N_MEM
    tn = 512
    col = pl.BlockSpec((rows, tn), lambda j: (0, j))
    wcol = pl.BlockSpec((D_MODEL, tn), lambda j: (0, j))
    return pl.pallas_call(
        _memkv_kernel,
        out_shape=(jax.ShapeDtypeStruct((rows, D_MODEL), F32), jax.ShapeDtypeStruct((rows, D_MODEL), F32),
                   jax.ShapeDtypeStruct((rows, D_MODEL), BF16), jax.ShapeDtypeStruct((rows, D_MODEL), BF16)),
        grid=(D_MODEL // tn,),
        in_specs=[_const_spec((rows, D_MODEL)), wcol, wcol],
        out_specs=(col, col, col, col),
        compiler_params=_cparams(("arbitrary",)),
        name="memkv",
    )(mem, wk, wv)


def _attend(q, k, v):
    outs = []
    for h in range(N_XHEADS):
        sl = slice(h * XHEAD_DIM, (h + 1) * XHEAD_DIM)
        s = lax.dot_general(q[:, sl], k[:, sl], (((1,), (1,)), ((), ())),
                            preferred_element_type=F32) * (XHEAD_DIM ** -0.5)
        s = s - jnp.max(s, axis=-1, keepdims=True)
        e = jnp.exp(s)
        p = e / jnp.sum(e, axis=-1, keepdims=True)
        outs.append(_dot(p.astype(BF16), v[:, sl]))
    return jnp.concatenate(outs, axis=1)


def _attn_prompt_kernel(q_ref, k_ref, v_ref, os_ref, o_ref):
    i = pl.program_id(0)

    @pl.when(i < NBP)
    def _():
        o_ref[...] = _attend(q_ref[...], k_ref[...], v_ref[...]).astype(BF16)

    @pl.when(i == NBP)
    def _():
        o_ref[...] = os_ref[...]


def _attn_prompt(q, kb, vb, o_sample):
    kv = pl.BlockSpec((N_MEM, D_MODEL), lambda i: (jnp.minimum(i // BLOCKS_PER_SEQ, BATCH - 1), 0))
    row = pl.BlockSpec((TM, D_MODEL), lambda i: (i, 0))
    return pl.pallas_call(
        _attn_prompt_kernel,
        out_shape=jax.ShapeDtypeStruct((R, D_MODEL), BF16),
        grid=(NB,),
        in_specs=[row, kv, kv, pl.BlockSpec((RS, D_MODEL), lambda i: (0, 0))],
        out_specs=row,
        compiler_params=_cparams(("arbitrary",)),
        name="attn_prompt",
    )(q, kb, vb, o_sample)


ATT_BB = 4
Q_PAD = 8
KV_ROWS = N_MEM * N_XHEADS


def _attn_sample_kernel(q_ref, k_ref, v_ref, o_ref):
    shape = (N_XHEADS * Q_PAD, KV_ROWS)
    same_head = (lax.broadcasted_iota(jnp.int32, shape, 0) // Q_PAD
                 == lax.broadcasted_iota(jnp.int32, shape, 1) % N_XHEADS)
    for b in range(ATT_BB):
        q = q_ref[b].astype(F32)
        qs = jnp.concatenate([q[:, h * XHEAD_DIM:(h + 1) * XHEAD_DIM] for h in range(N_XHEADS)], axis=0)
        k = k_ref[0, b].reshape(KV_ROWS, XHEAD_DIM).astype(BF16)
        v = v_ref[0, b].reshape(KV_ROWS, XHEAD_DIM).astype(BF16)
        s = lax.dot_general(qs.astype(BF16), k, (((1,), (1,)), ((), ())),
                            preferred_element_type=F32) * (XHEAD_DIM ** -0.5)
        s = jnp.where(same_head, s, -1e30)
        s = s - jnp.max(s, axis=-1, keepdims=True)
        e = jnp.exp(s)
        p = e / jnp.sum(e, axis=-1, keepdims=True)
        o = _dot(p.astype(BF16), v)
        for h in range(N_XHEADS):
            o_ref[b, :, h * XHEAD_DIM:(h + 1) * XHEAD_DIM] = o[h * Q_PAD:(h + 1) * Q_PAD]


def _attn_sample(q_pad, k, v):
    kv = pl.BlockSpec((1, ATT_BB, N_MEM, N_XHEADS, XHEAD_DIM), lambda i: (0, i, 0, 0, 0))
    qo = pl.BlockSpec((ATT_BB, Q_PAD, D_MODEL), lambda i: (i, 0, 0))
    return pl.pallas_call(
        _attn_sample_kernel,
        out_shape=jax.ShapeDtypeStruct((DEC_BATCH, Q_PAD, D_MODEL), F32),
        grid=(DEC_BATCH // ATT_BB,),
        in_specs=[qo, kv, kv],
        out_specs=qo,
        compiler_params=_cparams(("arbitrary",)),
        name="attn_sample",
    )(q_pad, k, v)


FTM = 1024
FNB = RP // FTM
FBLOCKS_PER_SEQ = SEQ // FTM
FSUB = 256


def _ffn_begin(f, h_ref, hb_scr, y_ref):
    @pl.when(f == 0)
    def _():
        h = h_ref[...]
        hb_scr[...] = h.astype(BF16)
        y_ref[...] = ALPHA * h


def _ffn_end(f, y_ref, lng_ref, lnb_ref):
    @pl.when(f == NF - 1)
    def _():
        y_ref[...] = _layer_norm(y_ref[...], lng_ref[...], lnb_ref[...])


def _ffn_prompt_kernel(h_ref, wg_ref, wu_ref, wd_ref, cw_ref, cb_ref, lng_ref, lnb_ref,
                       y_ref, gtail_ref, hb_scr, g_scr, carry_scr):
    i = pl.program_id(0)
    f = pl.program_id(1)
    _ffn_begin(f, h_ref, hb_scr, y_ref)
    first = (i % FBLOCKS_PER_SEQ) == 0

    @pl.when(first)
    def _():
        g_scr[0:8, :] = jnp.zeros((8, TF), F32)

    @pl.when(jnp.logical_not(first))
    def _():
        g_scr[0:8, :] = carry_scr[f]

    for r in range(FTM // FSUB):
        lo = r * FSUB
        hb = hb_scr[lo:lo + FSUB, :]
        g = _dot(hb, wg_ref[...])
        up = _dot(hb, wu_ref[...])
        g_scr[8 + lo:8 + lo + FSUB, :] = g
        gc = (cb_ref[...] + cw_ref[0:1, :] * g_scr[6 + lo:6 + lo + FSUB, :]
              + cw_ref[1:2, :] * g_scr[7 + lo:7 + lo + FSUB, :] + cw_ref[2:3, :] * g)
        act = (jax.nn.silu(gc) * up).astype(BF16)
        y_ref[lo:lo + FSUB, :] += _dot(act, wd_ref[...])
    tail = g_scr[FTM:FTM + 8, :]
    carry_scr[f] = tail
    gtail_ref[0] = tail
    _ffn_end(f, y_ref, lng_ref, lnb_ref)


def _ffn_sample_kernel(h_ref, wg_ref, wu_ref, wd_ref, cw_ref, cb_ref, lng_ref, lnb_ref, st_ref,
                       y_ref, gnew_ref, hb_scr, g_scr):
    f = pl.program_id(1)
    _ffn_begin(f, h_ref, hb_scr, y_ref)
    n_st = 2 * DEC_BATCH
    g_scr[0:n_st, :] = st_ref[...]
    for r in range(RS // FSUB):
        lo = r * FSUB
        hb = hb_scr[lo:lo + FSUB, :]
        g = _dot(hb, wg_ref[...])
        up = _dot(hb, wu_ref[...])
        g_scr[n_st + lo:n_st + lo + FSUB, :] = g
        gc = (cb_ref[...] + cw_ref[0:1, :] * g_scr[lo:lo + FSUB, :]
              + cw_ref[1:2, :] * g_scr[DEC_BATCH + lo:DEC_BATCH + lo + FSUB, :] + cw_ref[2:3, :] * g)
        act = (jax.nn.silu(gc) * up).astype(BF16)
        y_ref[lo:lo + FSUB, :] += _dot(act, wd_ref[...])
    gnew_ref[...] = g_scr[RS:RS + n_st, :]
    _ffn_end(f, y_ref, lng_ref, lnb_ref)


def _ffn_specs(tm, row_block_of):
    return [
        pl.BlockSpec((tm, D_MODEL), lambda i, f: (row_block_of(i), 0)),
        pl.BlockSpec((D_MODEL, TF), lambda i, f: (0, f)),
        pl.BlockSpec((D_MODEL, TF), lambda i, f: (0, f)),
        pl.BlockSpec((TF, D_MODEL), lambda i, f: (f, 0)),
        pl.BlockSpec((CONV_W, TF), lambda i, f: (0, f)),
        pl.BlockSpec((1, TF), lambda i, f: (0, f)),
        pl.BlockSpec((1, D_MODEL), lambda i, f: (0, 0)),
        pl.BlockSpec((1, D_MODEL), lambda i, f: (0, 0)),
    ]


def _ffn_prompt(h, wg, wu, wd, cw, cb, lng, lnb):
    return pl.pallas_call(
        _ffn_prompt_kernel,
        out_shape=(jax.ShapeDtypeStruct((RP, D_MODEL), F32),
                   jax.ShapeDtypeStruct((FNB, 8, D_FF), F32)),
        grid=(FNB, NF),
        in_specs=_ffn_specs(FTM, lambda i: i),
        out_specs=(pl.BlockSpec((FTM, D_MODEL), lambda i, f: (i, 0)),
                   pl.BlockSpec((1, 8, TF), lambda i, f: (i, 0, f))),
        scratch_shapes=[pltpu.VMEM((FTM, D_MODEL), BF16),
                        pltpu.VMEM((FTM + 8, TF), F32),
                        pltpu.VMEM((NF, 8, TF), F32)],
        compiler_params=_cparams(("arbitrary", "arbitrary"), VMEM_LIMIT_FFN),
        name="ffn_prompt",
    )(h, wg, wu, wd, cw, cb, lng, lnb)


def _ffn_sample(h, wg, wu, wd, cw, cb, lng, lnb, conv_state):
    n_st = 2 * DEC_BATCH
    return pl.pallas_call(
        _ffn_sample_kernel,
        out_shape=(jax.ShapeDtypeStruct((RS, D_MODEL), F32),
                   jax.ShapeDtypeStruct((n_st, D_FF), F32)),
        grid=(1, NF),
        in_specs=_ffn_specs(RS, lambda i: RP // RS) + [pl.BlockSpec((n_st, TF), lambda i, f: (0, f))],
        out_specs=(pl.BlockSpec((RS, D_MODEL), lambda i, f: (0, 0)),
                   pl.BlockSpec((n_st, TF), lambda i, f: (0, f))),
        scratch_shapes=[pltpu.VMEM((RS, D_MODEL), BF16),
                        pltpu.VMEM((n_st + RS, TF), F32)],
        compiler_params=_cparams(("arbitrary", "arbitrary")),
        name="ffn_sample",
    )(h, wg, wu, wd, cw, cb, lng, lnb, conv_state)


def _state_to_pairs(s):
    return s.reshape(s.shape[0], N_PAIRS, 128).transpose(1, 0, 2)


def _state_from_pairs(s):
    return s.transpose(1, 0, 2).reshape(1, s.shape[1], N_SSM_GROUPS, SSM_STATE)


def kernel(x_prompt, x_sample, mem_prompt, state_pool, state_ssm_re, state_ssm_im, state_conv, cache_mem_k, cache_mem_v, w_in, w_pool, pool_scale, lambda_re, lambda_im, log_step, b_re, b_im, c_re, c_im, d_skip, w_glu, b_glu, w_out, ln1_g, ln1_b, w_q, w_k, w_v, w_o, ln2_g, ln2_b, w_gate, w_up, conv_w, conv_b, w_down, ln3_g, ln3_b):
    bf = lambda w: w[0].astype(BF16)
    row = lambda v: v[0].reshape(1, -1)
    xp = x_prompt.reshape(RP, D_MODEL)
    xs = x_sample.transpose(1, 0, 2).reshape(RS, D_MODEL)

    u = _inproj(xp, xs, bf(w_in))
    u_pool = u[:, :D_POOL]
    wp = bf(w_pool)
    a = _pool(u, state_pool[0].transpose(1, 0, 2), wp, row(pool_scale))
    prep = _ssm_prep(lambda_re[0], lambda_im[0], log_step[0], b_re[0], b_im[0], c_re[0], c_im[0], d_skip[0])
    z, hp_re, hp_im, hs_re, hs_im = _ssm(
        u, prep, _state_to_pairs(state_ssm_re[0]), _state_to_pairs(state_ssm_im[0]))
    h1 = _mix(a, z, xp, xs, bf(w_glu), row(b_glu), bf(w_out), row(ln1_g), row(ln1_b))

    mk, mv, mkb, mvb = _memkv(mem_prompt.reshape(BATCH * N_MEM, D_MODEL), bf(w_k), bf(w_v))
    q = _proj(h1, bf(w_q), BF16, "qproj")
    q_s = q[RP:].reshape(DEC_SEQ, DEC_BATCH, D_MODEL).transpose(1, 0, 2)
    q_s = jnp.pad(q_s, ((0, 0), (0, Q_PAD - DEC_SEQ), (0, 0)))
    o_s = _attn_sample(q_s, cache_mem_k, cache_mem_v)
    o_s = o_s[:, :DEC_SEQ].transpose(1, 0, 2).reshape(RS, D_MODEL).astype(BF16)
    o_all = _attn_prompt(q, mkb, mvb, o_s)
    h2 = _proj_ln(o_all, h1, bf(w_o), row(ln2_g), row(ln2_b))

    wg, wu, wd = bf(w_gate), bf(w_up), bf(w_down)
    cw, cb = conv_w[0], row(conv_b)
    y_p, gtail = _ffn_prompt(h2, wg, wu, wd, cw, cb, row(ln3_g), row(ln3_b))
    conv_st = state_conv[0].transpose(1, 0, 2).reshape(2 * DEC_BATCH, D_FF)
    y_s, g_new = _ffn_sample(h2, wg, wu, wd, cw, cb, row(ln3_g), row(ln3_b), conv_st)

    y_prompt = y_p.reshape(BATCH, SEQ, D_MODEL)
    y_sample = y_s.reshape(DEC_SEQ, DEC_BATCH, D_MODEL).transpose(1, 0, 2)
    p_pool = u_pool[:RP].reshape(BATCH, SEQ, D_POOL)[None, :, SEQ - POOL_BUF:]
    s_ext = jnp.concatenate([state_pool[0], u_pool[RP:].reshape(DEC_SEQ, DEC_BATCH, D_POOL).transpose(1, 0, 2)], axis=1)
    s_pool = s_ext[None, :, DEC_SEQ:]
    p_conv = gtail[FBLOCKS_PER_SEQ - 1::FBLOCKS_PER_SEQ, 6:8][None]
    s_conv = g_new.reshape(2, DEC_BATCH, D_FF).transpose(1, 0, 2)[None]
    shape_kv = (1, BATCH, N_MEM, N_XHEADS, XHEAD_DIM)
    return (y_prompt, y_sample,
            p_pool, _state_from_pairs(hp_re[:, :BATCH]), _state_from_pairs(hp_im[:, :BATCH]), p_conv,
            mk.reshape(shape_kv), mv.reshape(shape_kv),
            s_pool, _state_from_pairs(hs_re), _state_from_pairs(hs_im), s_conv)
```

```python
import functools
import math

import jax
import jax.numpy as jnp
from jax import lax
from jax.experimental import pallas as pl
from jax.experimental.pallas import tpu as pltpu

F32 = jnp.float32
BF16 = jnp.bfloat16

D_MODEL = 2048
BATCH = 4
SEQ = 2048
DEC_BATCH = 128
DEC_SEQ = 4
PAST_LEN = 16384
D_POOL = 1024
D_SSM = 1024
POOL_WINDOWS = (2, 4, 8, 16)
POOL_GROUP_DIM = 256
POOL_BUF = 15
SSM_GROUP_CH = 16
N_SSM_GROUPS = 64
SSM_STATE = 64
N_MEM = 256
N_XHEADS = 4
XHEAD_DIM = 512
D_FF = 5632
CONV_W = 3
ALPHA = 2.0 ** 0.25
LN_EPS = 1e-5

RP = BATCH * SEQ
RS = DEC_BATCH * DEC_SEQ
R = RP + RS
TM = 512
NB = R // TM
NBP = RP // TM
BLOCKS_PER_SEQ = SEQ // TM

CHUNK = 16
N_PAIRS = N_SSM_GROUPS // 2
PAIR_W = 2 * CHUNK * SSM_GROUP_CH
CHUNKS_PER_SEQ = SEQ // CHUNK
P_CHUNK_ROWS = BATCH * CHUNKS_PER_SEQ
SSM_ROWS = P_CHUNK_ROWS + DEC_BATCH
SCAN_PAD = 64
N_SCAN_STEPS = 7

SUB = 256
TF = 512
NF = D_FF // TF

VMEM_LIMIT = 56 * 1024 * 1024
VMEM_LIMIT_FFN = 60 * 1024 * 1024


def _cparams(sem, vmem_limit=VMEM_LIMIT):
    return pltpu.CompilerParams(dimension_semantics=sem, vmem_limit_bytes=vmem_limit)


def _const_spec(shape):
    n = len(shape)
    return pl.BlockSpec(shape, lambda *_: (0,) * n, pipeline_mode=pl.Buffered(1))


def _layer_norm(x, g, b):
    mu = jnp.mean(x, axis=-1, keepdims=True)
    xc = x - mu
    var = jnp.mean(xc * xc, axis=-1, keepdims=True)
    return xc * lax.rsqrt(var + LN_EPS) * g + b


def _dot(a, b):
    return jnp.dot(a, b, preferred_element_type=F32)


def _inproj_kernel(xp_ref, xs_ref, w_ref, o_ref):
    x = jnp.where(pl.program_id(0) < NBP, xp_ref[...], xs_ref[...])
    o_ref[...] = _dot(x.astype(BF16), w_ref[...])


def _inproj(xp, xs, w):
    return pl.pallas_call(
        _inproj_kernel,
        out_shape=jax.ShapeDtypeStruct((R, D_MODEL), F32),
        grid=(NB,),
        in_specs=[
            pl.BlockSpec((TM, D_MODEL), lambda i: (jnp.minimum(i, NBP - 1), 0)),
            pl.BlockSpec((TM, D_MODEL), lambda i: (0, 0)),
            _const_spec((D_MODEL, D_MODEL)),
        ],
        out_specs=pl.BlockSpec((TM, D_MODEL), lambda i: (i, 0)),
        compiler_params=_cparams(("arbitrary",)),
        name="inproj",
    )(xp, xs, w)


def _pool_mix(g, pooled, w_ref, scale_ref):
    sl = slice(g * POOL_GROUP_DIM, (g + 1) * POOL_GROUP_DIM)
    return (_dot(pooled.astype(BF16), w_ref[g]) * scale_ref[:, sl]).astype(BF16)


def _pool_kernel(u_ref, st_ref, w_ref, scale_ref, o_ref, ext_ref):
    i = pl.program_id(0)

    @pl.when(i < NBP)
    def _():
        first = (i % BLOCKS_PER_SEQ) == 0

        @pl.when(first)
        def _():
            ext_ref[0:16, :] = jnp.zeros((16, D_POOL), F32)

        @pl.when(jnp.logical_not(first))
        def _():
            ext_ref[0:16, :] = ext_ref[TM:TM + 16, :]

        ext_ref[16:16 + TM, :] = u_ref[...]
        pos = (i % BLOCKS_PER_SEQ) * TM + lax.broadcasted_iota(jnp.int32, (TM, 1), 0)
        for g, w in enumerate(POOL_WINDOWS):
            sl = slice(g * POOL_GROUP_DIM, (g + 1) * POOL_GROUP_DIM)
            acc = ext_ref[16:16 + TM, sl]
            for k in range(1, w):
                acc = acc + ext_ref[16 - k:16 - k + TM, sl]
            cnt = jnp.minimum(pos + 1, w).astype(F32)
            o_ref[:, sl] = _pool_mix(g, acc / cnt - u_ref[:, sl], w_ref, scale_ref)

    @pl.when(i == NBP)
    def _():
        for j in range(DEC_SEQ):
            rows = slice(j * DEC_BATCH, (j + 1) * DEC_BATCH)
            for g, w in enumerate(POOL_WINDOWS):
                sl = slice(g * POOL_GROUP_DIM, (g + 1) * POOL_GROUP_DIM)
                acc = u_ref[rows, sl]
                for k in range(1, w):
                    e = POOL_BUF + j - k
                    if e >= POOL_BUF:
                        t = e - POOL_BUF
                        acc = acc + u_ref[t * DEC_BATCH:(t + 1) * DEC_BATCH, sl]
                    else:
                        acc = acc + st_ref[e, :, sl]
                cnt = float(min(PAST_LEN + j + 1, w))
                o_ref[rows, sl] = _pool_mix(g, acc / cnt - u_ref[rows, sl], w_ref, scale_ref)


def _pool(u, state_t, w_pool, pool_scale):
    return pl.pallas_call(
        _pool_kernel,
        out_shape=jax.ShapeDtypeStruct((R, D_POOL), BF16),
        grid=(NB,),
        in_specs=[
            pl.BlockSpec((TM, D_POOL), lambda i: (i, 0)),
            _const_spec((POOL_BUF, DEC_BATCH, D_POOL)),
            _const_spec((4, POOL_GROUP_DIM, POOL_GROUP_DIM)),
            _const_spec((1, D_POOL)),
        ],
        out_specs=pl.BlockSpec((TM, D_POOL), lambda i: (i, 0)),
        scratch_shapes=[pltpu.VMEM((TM + 16, D_POOL), F32)],
        compiler_params=_cparams(("arbitrary",)),
        name="pool",
    )(u, state_t, w_pool, pool_scale)


SLAB_GROUPS = 128 // SSM_GROUP_CH
SLAB_PAIRS = SLAB_GROUPS // 2
N_SLABS = N_SSM_GROUPS // SLAB_GROUPS


def _ssm_pair(q, u, t_ref, pre_ref, pim_ref, p4re_ref, p4im_ref, qre_ref, qim_ref,
              apow_re_ref, apow_im_ref, lam4_ref, dskip_ref, h0re_ref, h0im_ref,
              hp_re_ref, hp_im_ref, hs_re_ref, hs_im_ref, hre_scr, him_scr):
    ub = u.astype(BF16)
    half = PAIR_W // 2
    y = jnp.concatenate([_dot(ub[:, :half], t_ref[q, 0]), _dot(ub[:, half:], t_ref[q, 1])], axis=1)

    ubp = ub[:P_CHUNK_ROWS]
    hre_scr[SCAN_PAD:SCAN_PAD + P_CHUNK_ROWS, :] = _dot(ubp, pre_ref[q])
    him_scr[SCAN_PAD:SCAN_PAD + P_CHUNK_ROWS, :] = _dot(ubp, pim_ref[q])
    kk = lax.broadcasted_iota(jnp.int32, (P_CHUNK_ROWS, 1), 0) % CHUNKS_PER_SEQ
    for s in range(N_SCAN_STEPS):
        d = 1 << s
        ar = apow_re_ref[q, s:s + 1, :]
        ai = apow_im_ref[q, s:s + 1, :]
        hr = hre_scr[SCAN_PAD:SCAN_PAD + P_CHUNK_ROWS, :]
        hi = him_scr[SCAN_PAD:SCAN_PAD + P_CHUNK_ROWS, :]
        pr = hre_scr[SCAN_PAD - d:SCAN_PAD - d + P_CHUNK_ROWS, :]
        pi = him_scr[SCAN_PAD - d:SCAN_PAD - d + P_CHUNK_ROWS, :]
        keep = kk >= d
        hre_scr[SCAN_PAD:SCAN_PAD + P_CHUNK_ROWS, :] = hr + jnp.where(keep, ar * pr - ai * pi, 0.0)
        him_scr[SCAN_PAD:SCAN_PAD + P_CHUNK_ROWS, :] = hi + jnp.where(keep, ar * pi + ai * pr, 0.0)
    hp_re_ref[q] = jnp.zeros((8, 128), F32)
    hp_im_ref[q] = jnp.zeros((8, 128), F32)
    for b in range(BATCH):
        last = SCAN_PAD + (b + 1) * CHUNKS_PER_SEQ - 1
        hp_re_ref[q, b:b + 1, :] = hre_scr[last:last + 1, :]
        hp_im_ref[q, b:b + 1, :] = him_scr[last:last + 1, :]
    prev_ok = kk >= 1
    hprev_re = jnp.where(prev_ok, hre_scr[SCAN_PAD - 1:SCAN_PAD - 1 + P_CHUNK_ROWS, :], 0.0)
    hprev_im = jnp.where(prev_ok, him_scr[SCAN_PAD - 1:SCAN_PAD - 1 + P_CHUNK_ROWS, :], 0.0)
    carry_p = _dot(hprev_re.astype(BF16), qre_ref[q]) + _dot(hprev_im.astype(BF16), qim_ref[q])

    ubs = ub[P_CHUNK_ROWS:]
    h0r = h0re_ref[q]
    h0i = h0im_ref[q]
    l4r = lam4_ref[q, 0:1, :]
    l4i = lam4_ref[q, 1:2, :]
    hs_re_ref[q] = l4r * h0r - l4i * h0i + _dot(ubs, p4re_ref[q])
    hs_im_ref[q] = l4r * h0i + l4i * h0r + _dot(ubs, p4im_ref[q])
    carry_s = _dot(h0r.astype(BF16), qre_ref[q]) + _dot(h0i.astype(BF16), qim_ref[q])

    y = y + jnp.concatenate([carry_p, carry_s], axis=0) + dskip_ref[q] * u
    return jax.nn.gelu(y)


def _ssm_kernel(u_ref, *refs):
    z_ref = refs[13]
    hre_scr, him_scr = refs[18], refs[19]
    c = SSM_GROUP_CH
    hre_scr[0:SCAN_PAD, :] = jnp.zeros((SCAN_PAD, 128), F32)
    him_scr[0:SCAN_PAD, :] = jnp.zeros((SCAN_PAD, 128), F32)
    xt = []
    for i in range(CHUNK):
        xp = u_ref[pl.ds(i, P_CHUNK_ROWS, stride=CHUNK), :]
        if i < DEC_SEQ:
            xs = u_ref[RP + i * DEC_BATCH:RP + (i + 1) * DEC_BATCH, :]
        else:
            xs = jnp.zeros((DEC_BATCH, 128), F32)
        xt.append(jnp.concatenate([xp, xs], axis=0).T)
    zt = []
    for q in range(SLAB_PAIRS):
        halves = []
        for e in range(2):
            g = 2 * q + e
            bt = jnp.concatenate([xt[i][g * c:(g + 1) * c, :] for i in range(CHUNK)], axis=0)
            halves.append(bt.T)
        z = _ssm_pair(q, jnp.concatenate(halves, axis=1), *refs[:13], *refs[14:])
        zt.append(z[:, :PAIR_W // 2].T)
        zt.append(z[:, PAIR_W // 2:].T)
    for i in range(CHUNK):
        zi = jnp.concatenate([zt[g][i * c:(i + 1) * c, :] for g in range(SLAB_GROUPS)], axis=0).T
        z_ref[pl.ds(i, P_CHUNK_ROWS, stride=CHUNK), :] = zi[:P_CHUNK_ROWS]
        if i < DEC_SEQ:
            z_ref[RP + i * DEC_BATCH:RP + (i + 1) * DEC_BATCH, :] = zi[P_CHUNK_ROWS:]


def _ssm(u, prep, h0re, h0im):
    sp = SLAB_PAIRS
    slab3 = lambda a, b: pl.BlockSpec((sp, a, b), lambda s: (s, 0, 0))
    return pl.pallas_call(
        _ssm_kernel,
        out_shape=(
            jax.ShapeDtypeStruct((R, D_SSM), F32),
            jax.ShapeDtypeStruct((N_PAIRS, 8, 128), F32),
            jax.ShapeDtypeStruct((N_PAIRS, 8, 128), F32),
            jax.ShapeDtypeStruct((N_PAIRS, DEC_BATCH, 128), F32),
            jax.ShapeDtypeStruct((N_PAIRS, DEC_BATCH, 128), F32),
        ),
        grid=(N_SLABS,),
        in_specs=[
            pl.BlockSpec((R, 128), lambda s: (0, D_POOL // 128 + s)),
            pl.BlockSpec((sp, 2, 256, 256), lambda s: (s, 0, 0, 0)),
            slab3(PAIR_W, 128), slab3(PAIR_W, 128), slab3(PAIR_W, 128), slab3(PAIR_W, 128),
            slab3(128, PAIR_W), slab3(128, PAIR_W),
            slab3(8, 128), slab3(8, 128), slab3(8, 128),
            slab3(1, PAIR_W),
            slab3(DEC_BATCH, 128), slab3(DEC_BATCH, 128),
        ],
        out_specs=(
            pl.BlockSpec((R, 128), lambda s: (0, s)),
            slab3(8, 128), slab3(8, 128), slab3(DEC_BATCH, 128), slab3(DEC_BATCH, 128),
        ),
        scratch_shapes=[pltpu.VMEM((SCAN_PAD + P_CHUNK_ROWS, 128), F32),
                        pltpu.VMEM((SCAN_PAD + P_CHUNK_ROWS, 128), F32)],
        compiler_params=_cparams(("arbitrary",)),
        name="ssm",
    )(u, prep["t"], prep["pre"], prep["pim"], prep["p4re"], prep["p4im"],
      prep["qre"], prep["qim"], prep["apow_re"], prep["apow_im"], prep["lam4"],
      prep["dskip"], h0re, h0im)


def _cmul(ar, ai, br, bi):
    return ar * br - ai * bi, ar * bi + ai * br


def _dot3(a, b):
    ah = a.astype(BF16)
    bh = b.astype(BF16)
    al = (a - ah.astype(F32)).astype(BF16)
    bl = (b - bh.astype(F32)).astype(BF16)
    return _dot(ah, bh) + _dot(ah, bl) + _dot(al, bh)


def _ssm_prep_kernel(lrow_re_ref, lrow_im_ref, lcol_re_ref, lcol_im_ref, lstep_ref,
                     bt_re_ref, bt_im_ref, cx_re_ref, cx_im_ref,
                     t_ref, pre_ref, pim_ref, p4re_ref, p4im_ref, qre_ref, qim_ref,
                     apr_ref, api_ref, lam4_ref):
    L, C, N = CHUNK, SSM_GROUP_CH, SSM_STATE
    w = L * C
    p_re, p_im, p4_re, p4_im, q_re, q_im, a_re, a_im, l4 = [], [], [], [], [], [], [], [], []
    for e in range(2):
        dt = jnp.exp(lstep_ref[e])

        def lam_bar(lr, li):
            mag = jnp.exp(lr * dt)
            ang = li * dt
            return mag * jnp.cos(ang), mag * jnp.sin(ang)

        lr, li = lrow_re_ref[e], lrow_im_ref[e]
        zr, zi = lam_bar(lr, li)
        den = lr * lr + li * li
        fr = ((zr - 1.0) * lr + zi * li) / den
        fi = (zi * lr - (zr - 1.0) * li) / den
        bbr, bbi = _cmul(fr, fi, bt_re_ref[e], bt_im_ref[e])
        pr, pi = [jnp.ones((1, N), F32)], [jnp.zeros((1, N), F32)]
        for _ in range(L):
            nr, ni = _cmul(pr[-1], pi[-1], zr, zi)
            pr.append(nr)
            pi.append(ni)
        blocks = [_cmul(bbr, bbi, pr[L - 1 - i], pi[L - 1 - i]) for i in range(L)]
        p_re.append(jnp.concatenate([b[0] for b in blocks], axis=0))
        p_im.append(jnp.concatenate([b[1] for b in blocks], axis=0))
        blocks4 = [_cmul(bbr, bbi, pr[DEC_SEQ - 1 - i], pi[DEC_SEQ - 1 - i]) for i in range(DEC_SEQ)]
        pad = jnp.zeros(((L - DEC_SEQ) * C, N), F32)
        p4_re.append(jnp.concatenate([b[0] for b in blocks4] + [pad], axis=0))
        p4_im.append(jnp.concatenate([b[1] for b in blocks4] + [pad], axis=0))
        sr, si = [pr[L]], [pi[L]]
        for _ in range(N_SCAN_STEPS - 1):
            nr, ni = _cmul(sr[-1], si[-1], sr[-1], si[-1])
            sr.append(nr)
            si.append(ni)
        a_re.append(jnp.concatenate(sr + [jnp.zeros((8 - N_SCAN_STEPS, N), F32)], axis=0))
        a_im.append(jnp.concatenate(si + [jnp.zeros((8 - N_SCAN_STEPS, N), F32)], axis=0))
        l4.append(jnp.concatenate([pr[DEC_SEQ], pi[DEC_SEQ], jnp.zeros((6, N), F32)], axis=0))

        cr, ci = lam_bar(lcol_re_ref[e], lcol_im_ref[e])
        kk = lax.broadcasted_iota(jnp.int32, (N, w), 1) // C
        er, ei = jnp.ones((N, w), F32), jnp.zeros((N, w), F32)
        sqr, sqi = cr, ci
        for bit in range(4):
            nr, ni = _cmul(er, ei, sqr, sqi)
            has = ((kk >> bit) & 1) == 1
            er, ei = jnp.where(has, nr, er), jnp.where(has, ni, ei)
            sqr, sqi = _cmul(sqr, sqi, sqr, sqi)
        ckr, cki = _cmul(cx_re_ref[e], cx_im_ref[e], er, ei)
        qr, qi = _cmul(ckr, cki, cr, ci)
        q_re.append(qr)
        q_im.append(-qi)
        v = _dot3(bbr, ckr) - _dot3(bbi, cki)
        lane = lax.broadcasted_iota(jnp.int32, (C, w), 1)
        rows = [v] + [jnp.where(lane >= C * i, pltpu.roll(v, C * i, axis=1), 0.0) for i in range(1, L)]
        t_ref[0, e] = jnp.concatenate(rows, axis=0).astype(BF16)

    def diag_rows(m):
        z = jnp.zeros_like(m[0])
        return jnp.concatenate([jnp.concatenate([m[0], z], axis=1),
                                jnp.concatenate([z, m[1]], axis=1)], axis=0)

    pre_ref[0] = diag_rows(p_re).astype(BF16)
    pim_ref[0] = diag_rows(p_im).astype(BF16)
    p4re_ref[0] = diag_rows(p4_re).astype(BF16)
    p4im_ref[0] = diag_rows(p4_im).astype(BF16)
    qre_ref[0] = diag_rows(q_re).astype(BF16)
    qim_ref[0] = diag_rows(q_im).astype(BF16)
    apr_ref[0] = jnp.concatenate(a_re, axis=1)
    api_ref[0] = jnp.concatenate(a_im, axis=1)
    lam4_ref[0] = jnp.concatenate(l4, axis=1)


def _ssm_prep(lambda_re, lambda_im, log_step, b_re, b_im, c_re, c_im, d_skip):
    G, N, C, L = N_SSM_GROUPS, SSM_STATE, SSM_GROUP_CH, CHUNK
    w = L * C
    cexp = lambda c: jnp.tile(c.transpose(0, 2, 1), (1, 1, L))
    g3 = lambda a, b: pl.BlockSpec((2, a, b), lambda p: (p, 0, 0))
    p3 = lambda a, b: pl.BlockSpec((1, a, b), lambda p: (p, 0, 0))
    sds = jax.ShapeDtypeStruct
    outs = pl.pallas_call(
        _ssm_prep_kernel,
        out_shape=(
            sds((N_PAIRS, 2, w, w), BF16),
            sds((N_PAIRS, PAIR_W, 128), BF16), sds((N_PAIRS, PAIR_W, 128), BF16),
            sds((N_PAIRS, PAIR_W, 128), BF16), sds((N_PAIRS, PAIR_W, 128), BF16),
            sds((N_PAIRS, 128, PAIR_W), BF16), sds((N_PAIRS, 128, PAIR_W), BF16),
            sds((N_PAIRS, 8, 128), F32), sds((N_PAIRS, 8, 128), F32), sds((N_PAIRS, 8, 128), F32),
        ),
        grid=(N_PAIRS,),
        in_specs=[g3(1, N), g3(1, N), g3(N, 1), g3(N, 1), g3(1, 1),
                  g3(C, N), g3(C, N), g3(N, w), g3(N, w)],
        out_specs=(
            pl.BlockSpec((1, 2, w, w), lambda p: (p, 0, 0, 0)),
            p3(PAIR_W, 128), p3(PAIR_W, 128), p3(PAIR_W, 128), p3(PAIR_W, 128),
            p3(128, PAIR_W), p3(128, PAIR_W),
            p3(8, 128), p3(8, 128), p3(8, 128),
        ),
        compiler_params=_cparams(("arbitrary",)),
        name="ssm_prep",
    )(lambda_re[:, None, :], lambda_im[:, None, :], lambda_re[:, :, None], lambda_im[:, :, None],
      log_step[:, None, None], b_re.transpose(0, 2, 1), b_im.transpose(0, 2, 1), cexp(c_re), cexp(c_im))
    names = ("t", "pre", "pim", "p4re", "p4im", "qre", "qim", "apow_re", "apow_im", "lam4")
    prep = dict(zip(names, outs))
    prep["dskip"] = jnp.broadcast_to(d_skip.reshape(N_PAIRS, 2, 1, C), (N_PAIRS, 2, L, C)).reshape(N_PAIRS, 1, PAIR_W)
    return prep


def _mix_kernel(a_ref, z_ref, xp_ref, xs_ref, wglu_ref, bglu_ref, wout_ref, g_ref, b_ref, o_ref):
    is_prompt = pl.program_id(0) < NBP
    for r in range(TM // SUB):
        rows = slice(r * SUB, (r + 1) * SUB)
        z = z_ref[rows, :]
        gate = _dot(z.astype(BF16), wglu_ref[...]) + bglu_ref[...]
        bmix = (z * jax.nn.sigmoid(gate)).astype(BF16)
        mix = _dot(a_ref[rows, :], wout_ref[0:D_POOL, :]) + _dot(bmix, wout_ref[D_POOL:, :])
        x = jnp.where(is_prompt, xp_ref[rows, :], xs_ref[rows, :])
        o_ref[rows, :] = _layer_norm(ALPHA * x + mix, g_ref[...], b_ref[...])


def _mix(a, z, xp, xs, w_glu, b_glu, w_out, g, b):
    row = lambda w: pl.BlockSpec((TM, w), lambda i: (i, 0))
    return pl.pallas_call(
        _mix_kernel,
        out_shape=jax.ShapeDtypeStruct((R, D_MODEL), F32),
        grid=(NB,),
        in_specs=[
            row(D_POOL), row(D_SSM),
            pl.BlockSpec((TM, D_MODEL), lambda i: (jnp.minimum(i, NBP - 1), 0)),
            pl.BlockSpec((TM, D_MODEL), lambda i: (0, 0)),
            _const_spec((D_SSM, D_SSM)), _const_spec((1, D_SSM)),
            _const_spec((D_MODEL, D_MODEL)), _const_spec((1, D_MODEL)), _const_spec((1, D_MODEL)),
        ],
        out_specs=row(D_MODEL),
        compiler_params=_cparams(("arbitrary",)),
        name="mix_ln1",
    )(a, z, xp, xs, w_glu, b_glu, w_out, g, b)


def _proj_kernel(x_ref, w_ref, o_ref):
    o_ref[...] = _dot(x_ref[...].astype(BF16), w_ref[...]).astype(o_ref.dtype)


def _proj(x, w, out_dtype, name):
    rows = x.shape[0]
    return pl.pallas_call(
        _proj_kernel,
        out_shape=jax.ShapeDtypeStruct((rows, w.shape[1]), out_dtype),
        grid=(rows // TM,),
        in_specs=[pl.BlockSpec((TM, x.shape[1]), lambda i: (i, 0)), _const_spec(w.shape)],
        out_specs=pl.BlockSpec((TM, w.shape[1]), lambda i: (i, 0)),
        compiler_params=_cparams(("arbitrary",)),
        name=name,
    )(x, w)


def _proj_ln_kernel(o_in_ref, h_ref, w_ref, g_ref, b_ref, out_ref):
    for r in range(TM // SUB):
        rows = slice(r * SUB, (r + 1) * SUB)
        y = _dot(o_in_ref[rows, :], w_ref[...])
        out_ref[rows, :] = _layer_norm(ALPHA * h_ref[rows, :] + y, g_ref[...], b_ref[...])


def _proj_ln(o_in, h, w, g, b):
    row = pl.BlockSpec((TM, D_MODEL), lambda i: (i, 0))
    return pl.pallas_call(
        _proj_ln_kernel,
        out_shape=jax.ShapeDtypeStruct((R, D_MODEL), F32),
        grid=(NB,),
        in_specs=[row, row, _const_spec((D_MODEL, D_MODEL)),
                  _const_spec((1, D_MODEL)), _const_spec((1, D_MODEL))],
        out_specs=row,
        compiler_params=_cparams(("arbitrary",)),
        name="oproj_ln2",
    )(o_in, h, w, g, b)


def _memkv_kernel(m_ref, wk_ref, wv_ref, k_ref, v_ref, kb_ref, vb_ref):
    mb = m_ref[...].astype(BF16)
    k = _dot(mb, wk_ref[...])
    v = _dot(mb, wv_ref[...])
    k_ref[...] = k
    v_ref[...] = v
    kb_ref[...] = k.astype(BF16)
    vb_ref[...] = v.astype(BF16)


def _memkv(mem, wk, wv):
    rows = BATCH * N_MEM
    tn = 512
    col = pl.BlockSpec((rows, tn), lambda j: (0, j))
    wcol = pl.BlockSpec((D_MODEL, tn), lambda j: (0, j))
    return pl.pallas_call(
        _memkv_kernel,
        out_shape=(jax.ShapeDtypeStruct((rows, D_MODEL), F32), jax.ShapeDtypeStruct((rows, D_MODEL), F32),
                   jax.ShapeDtypeStruct((rows, D_MODEL), BF16), jax.ShapeDtypeStruct((rows, D_MODEL), BF16)),
        grid=(D_MODEL // tn,),
        in_specs=[_const_spec((rows, D_MODEL)), wcol, wcol],
        out_specs=(col, col, col, col),
        compiler_params=_cparams(("arbitrary",)),
        name="memkv",
    )(mem, wk, wv)


def _attend(q, k, v):
    outs = []
    for h in range(N_XHEADS):
        sl = slice(h * XHEAD_DIM, (h + 1) * XHEAD_DIM)
        s = lax.dot_general(q[:, sl], k[:, sl], (((1,), (1,)), ((), ())),
                            preferred_element_type=F32) * (XHEAD_DIM ** -0.5)
        s = s - jnp.max(s, axis=-1, keepdims=True)
        e = jnp.exp(s)
        p = e / jnp.sum(e, axis=-1, keepdims=True)
        outs.append(_dot(p.astype(BF16), v[:, sl]))
    return jnp.concatenate(outs, axis=1)


def _attn_prompt_kernel(q_ref, k_ref, v_ref, os_ref, o_ref):
    i = pl.program_id(0)

    @pl.when(i < NBP)
    def _():
        o_ref[...] = _attend(q_ref[...], k_ref[...], v_ref[...]).astype(BF16)

    @pl.when(i == NBP)
    def _():
        o_ref[...] = os_ref[...]


def _attn_prompt(q, kb, vb, o_sample):
    kv = pl.BlockSpec((N_MEM, D_MODEL), lambda i: (jnp.minimum(i // BLOCKS_PER_SEQ, BATCH - 1), 0))
    row = pl.BlockSpec((TM, D_MODEL), lambda i: (i, 0))
    return pl.pallas_call(
        _attn_prompt_kernel,
        out_shape=jax.ShapeDtypeStruct((R, D_MODEL), BF16),
        grid=(NB,),
        in_specs=[row, kv, kv, pl.BlockSpec((RS, D_MODEL), lambda i: (0, 0))],
        out_specs=row,
        compiler_params=_cparams(("arbitrary",)),
        name="attn_prompt",
    )(q, kb, vb, o_sample)


ATT_BB = 4
Q_PAD = 8
KV_ROWS = N_MEM * N_XHEADS


def _attn_sample_kernel(q_ref, k_ref, v_ref, o_ref):
    shape = (N_XHEADS * Q_PAD, KV_ROWS)
    same_head = (lax.broadcasted_iota(jnp.int32, shape, 0) // Q_PAD
                 == lax.broadcasted_iota(jnp.int32, shape, 1) % N_XHEADS)
    for b in range(ATT_BB):
        q = q_ref[b].astype(F32)
        qs = jnp.concatenate([q[:, h * XHEAD_DIM:(h + 1) * XHEAD_DIM] for h in range(N_XHEADS)], axis=0)
        k = k_ref[0, b].reshape(KV_ROWS, XHEAD_DIM).astype(BF16)
        v = v_ref[0, b].reshape(KV_ROWS, XHEAD_DIM).astype(BF16)
        s = lax.dot_general(qs.astype(BF16), k, (((1,), (1,)), ((), ())),
                            preferred_element_type=F32) * (XHEAD_DIM ** -0.5)
        s = jnp.where(same_head, s, -1e30)
        s = s - jnp.max(s, axis=-1, keepdims=True)
        e = jnp.exp(s)
        p = e / jnp.sum(e, axis=-1, keepdims=True)
        o = _dot(p.astype(BF16), v)
        for h in range(N_XHEADS):
            o_ref[b, :, h * XHEAD_DIM:(h + 1) * XHEAD_DIM] = o[h * Q_PAD:(h + 1) * Q_PAD]


def _attn_sample(q_pad, k, v):
    kv = pl.BlockSpec((1, ATT_BB, N_MEM, N_XHEADS, XHEAD_DIM), lambda i: (0, i, 0, 0, 0))
    qo = pl.BlockSpec((ATT_BB, Q_PAD, D_MODEL), lambda i: (i, 0, 0))
    return pl.pallas_call(
        _attn_sample_kernel,
        out_shape=jax.ShapeDtypeStruct((DEC_BATCH, Q_PAD, D_MODEL), F32),
        grid=(DEC_BATCH // ATT_BB,),
        in_specs=[qo, kv, kv],
        out_specs=qo,
        compiler_params=_cparams(("arbitrary",)),
        name="attn_sample",
    )(q_pad, k, v)


FTM = 1024
FNB = RP // FTM
FBLOCKS_PER_SEQ = SEQ // FTM
FSUB = 256


def _ffn_begin(f, h_ref, hb_scr, y_ref):
    @pl.when(f == 0)
    def _():
        h = h_ref[...]
        hb_scr[...] = h.astype(BF16)
        y_ref[...] = ALPHA * h


def _ffn_end(f, y_ref, lng_ref, lnb_ref):
    @pl.when(f == NF - 1)
    def _():
        y_ref[...] = _layer_norm(y_ref[...], lng_ref[...], lnb_ref[...])


def _ffn_prompt_kernel(h_ref, wg_ref, wu_ref, wd_ref, cw_ref, cb_ref, lng_ref, lnb_ref,
                       y_ref, gtail_ref, hb_scr, g_scr, carry_scr):
    i = pl.program_id(0)
    f = pl.program_id(1)
    _ffn_begin(f, h_ref, hb_scr, y_ref)
    first = (i % FBLOCKS_PER_SEQ) == 0

    @pl.when(first)
    def _():
        g_scr[0:8, :] = jnp.zeros((8, TF), F32)

    @pl.when(jnp.logical_not(first))
    def _():
        g_scr[0:8, :] = carry_scr[f]

    for r in range(FTM // FSUB):
        lo = r * FSUB
        hb = hb_scr[lo:lo + FSUB, :]
        g = _dot(hb, wg_ref[...])
        up = _dot(hb, wu_ref[...])
        g_scr[8 + lo:8 + lo + FSUB, :] = g
        gc = (cb_ref[...] + cw_ref[0:1, :] * g_scr[6 + lo:6 + lo + FSUB, :]
              + cw_ref[1:2, :] * g_scr[7 + lo:7 + lo + FSUB, :] + cw_ref[2:3, :] * g)
        act = (jax.nn.silu(gc) * up).astype(BF16)
        y_ref[lo:lo + FSUB, :] += _dot(act, wd_ref[...])
    tail = g_scr[FTM:FTM + 8, :]
    carry_scr[f] = tail
    gtail_ref[0] = tail
    _ffn_end(f, y_ref, lng_ref, lnb_ref)


def _ffn_sample_kernel(h_ref, wg_ref, wu_ref, wd_ref, cw_ref, cb_ref, lng_ref, lnb_ref, st_ref,
                       y_ref, gnew_ref, hb_scr, g_scr):
    f = pl.program_id(1)
    _ffn_begin(f, h_ref, hb_scr, y_ref)
    n_st = 2 * DEC_BATCH
    g_scr[0:n_st, :] = st_ref[...]
    for r in range(RS // FSUB):
        lo = r * FSUB
        hb = hb_scr[lo:lo + FSUB, :]
        g = _dot(hb, wg_ref[...])
        up = _dot(hb, wu_ref[...])
        g_scr[n_st + lo:n_st + lo + FSUB, :] = g
        gc = (cb_ref[...] + cw_ref[0:1, :] * g_scr[lo:lo + FSUB, :]
              + cw_ref[1:2, :] * g_scr[DEC_BATCH + lo:DEC_BATCH + lo + FSUB, :] + cw_ref[2:3, :] * g)
        act = (jax.nn.silu(gc) * up).astype(BF16)
        y_ref[lo:lo + FSUB, :] += _dot(act, wd_ref[...])
    gnew_ref[...] = g_scr[RS:RS + n_st, :]
    _ffn_end(f, y_ref, lng_ref, lnb_ref)


def _ffn_specs(tm, row_block_of):
    return [
        pl.BlockSpec((tm, D_MODEL), lambda i, f: (row_block_of(i), 0)),
        pl.BlockSpec((D_MODEL, TF), lambda i, f: (0, f)),
        pl.BlockSpec((D_MODEL, TF), lambda i, f: (0, f)),
        pl.BlockSpec((TF, D_MODEL), lambda i, f: (f, 0)),
        pl.BlockSpec((CONV_W, TF), lambda i, f: (0, f)),
        pl.BlockSpec((1, TF), lambda i, f: (0, f)),
        pl.BlockSpec((1, D_MODEL), lambda i, f: (0, 0)),
        pl.BlockSpec((1, D_MODEL), lambda i, f: (0, 0)),
    ]


def _ffn_prompt(h, wg, wu, wd, cw, cb, lng, lnb):
    return pl.pallas_call(
        _ffn_prompt_kernel,
        out_shape=(jax.ShapeDtypeStruct((RP, D_MODEL), F32),
                   jax.ShapeDtypeStruct((FNB, 8, D_FF), F32)),
        grid=(FNB, NF),
        in_specs=_ffn_specs(FTM, lambda i: i),
        out_specs=(pl.BlockSpec((FTM, D_MODEL), lambda i, f: (i, 0)),
                   pl.BlockSpec((1, 8, TF), lambda i, f: (i, 0, f))),
        scratch_shapes=[pltpu.VMEM((FTM, D_MODEL), BF16),
                        pltpu.VMEM((FTM + 8, TF), F32),
                        pltpu.VMEM((NF, 8, TF), F32)],
        compiler_params=_cparams(("arbitrary", "arbitrary"), VMEM_LIMIT_FFN),
        name="ffn_prompt",
    )(h, wg, wu, wd, cw, cb, lng, lnb)


def _ffn_sample(h, wg, wu, wd, cw, cb, lng, lnb, conv_state):
    n_st = 2 * DEC_BATCH
    return pl.pallas_call(
        _ffn_sample_kernel,
        out_shape=(jax.ShapeDtypeStruct((RS, D_MODEL), F32),
                   jax.ShapeDtypeStruct((n_st, D_FF), F32)),
        grid=(1, NF),
        in_specs=_ffn_specs(RS, lambda i: RP // RS) + [pl.BlockSpec((n_st, TF), lambda i, f: (0, f))],
        out_specs=(pl.BlockSpec((RS, D_MODEL), lambda i, f: (0, 0)),
                   pl.BlockSpec((n_st, TF), lambda i, f: (0, f))),
        scratch_shapes=[pltpu.VMEM((RS, D_MODEL), BF16),
                        pltpu.VMEM((n_st + RS, TF), F32)],
        compiler_params=_cparams(("arbitrary", "arbitrary")),
        name="ffn_sample",
    )(h, wg, wu, wd, cw, cb, lng, lnb, conv_state)


def _state_to_pairs(s):
    return s.reshape(s.shape[0], N_PAIRS, 128).transpose(1, 0, 2)


def _state_from_pairs(s):
    return s.transpose(1, 0, 2).reshape(1, s.shape[1], N_SSM_GROUPS, SSM_STATE)


def kernel(x_prompt, x_sample, mem_prompt, state_pool, state_ssm_re, state_ssm_im, state_conv, cache_mem_k, cache_mem_v, w_in, w_pool, pool_scale, lambda_re, lambda_im, log_step, b_re, b_im, c_re, c_im, d_skip, w_glu, b_glu, w_out, ln1_g, ln1_b, w_q, w_k, w_v, w_o, ln2_g, ln2_b, w_gate, w_up, conv_w, conv_b, w_down, ln3_g, ln3_b):
    bf = lambda w: w[0].astype(BF16)
    row = lambda v: v[0].reshape(1, -1)
    xp = x_prompt.reshape(RP, D_MODEL)
    xs = x_sample.transpose(1, 0, 2).reshape(RS, D_MODEL)

    u = _inproj(xp, xs, bf(w_in))
    wp = bf(w_pool)
    a = _pool(u, state_pool[0].transpose(1, 0, 2), wp, row(pool_scale))
    prep = _ssm_prep(lambda_re[0], lambda_im[0], log_step[0], b_re[0], b_im[0], c_re[0], c_im[0], d_skip[0])
    z, hp_re, hp_im, hs_re, hs_im = _ssm(
        u, prep, _state_to_pairs(state_ssm_re[0]), _state_to_pairs(state_ssm_im[0]))
    h1 = _mix(a, z, xp, xs, bf(w_glu), row(b_glu), bf(w_out), row(ln1_g), row(ln1_b))

    mk, mv, mkb, mvb = _memkv(mem_prompt.reshape(BATCH * N_MEM, D_MODEL), bf(w_k), bf(w_v))
    q = _proj(h1, bf(w_q), BF16, "qproj")
    q_s = q[RP:].reshape(DEC_SEQ, DEC_BATCH, D_MODEL).transpose(1, 0, 2)
    q_s = jnp.pad(q_s, ((0, 0), (0, Q_PAD - DEC_SEQ), (0, 0)))
    o_s = _attn_sample(q_s, cache_mem_k, cache_mem_v)
    o_s = o_s[:, :DEC_SEQ].transpose(1, 0, 2).reshape(RS, D_MODEL).astype(BF16)
    o_all = _attn_prompt(q, mkb, mvb, o_s)
    h2 = _proj_ln(o_all, h1, bf(w_o), row(ln2_g), row(ln2_b))

    wg, wu, wd = bf(w_gate), bf(w_up), bf(w_down)
    cw, cb = conv_w[0], row(conv_b)
    y_p, gtail = _ffn_prompt(h2, wg, wu, wd, cw, cb, row(ln3_g), row(ln3_b))
    conv_st = state_conv[0].transpose(1, 0, 2).reshape(2 * DEC_BATCH, D_FF)
    y_s, g_new = _ffn_sample(h2, wg, wu, wd, cw, cb, row(ln3_g), row(ln3_b), conv_st)

    y_prompt = y_p.reshape(BATCH, SEQ, D_MODEL)
    y_sample = y_s.reshape(DEC_SEQ, DEC_BATCH, D_MODEL).transpose(1, 0, 2)
    p_pool = u[:RP].reshape(BATCH, SEQ, D_MODEL)[None, :, SEQ - POOL_BUF:, :D_POOL]
    s_ext = jnp.concatenate([state_pool[0], u[RP:, :D_POOL].reshape(DEC_SEQ, DEC_BATCH, D_POOL).transpose(1, 0, 2)], axis=1)
    s_pool = s_ext[None, :, DEC_SEQ:]
    p_conv = gtail[FBLOCKS_PER_SEQ - 1::FBLOCKS_PER_SEQ, 6:8][None]
    s_conv = g_new.reshape(2, DEC_BATCH, D_FF).transpose(1, 0, 2)[None]
    shape_kv = (1, BATCH, N_MEM, N_XHEADS, XHEAD_DIM)
    return (y_prompt, y_sample,
            p_pool, _state_from_pairs(hp_re[:, :BATCH]), _state_from_pairs(hp_im[:, :BATCH]), p_conv,
            mk.reshape(shape_kv), mv.reshape(shape_kv),
            s_pool, _state_from_pairs(hs_re), _state_from_pairs(hs_im), s_conv)
```

```python
import functools
import math

import jax
import jax.numpy as jnp
from jax import lax
from jax.experimental import pallas as pl
from jax.experimental.pallas import tpu as pltpu

F32 = jnp.float32
BF16 = jnp.bfloat16

D_MODEL = 2048
BATCH = 4
SEQ = 2048
DEC_BATCH = 128
DEC_SEQ = 4
PAST_LEN = 16384
D_POOL = 1024
D_SSM = 1024
POOL_WINDOWS = (2, 4, 8, 16)
POOL_GROUP_DIM = 256
POOL_BUF = 15
SSM_GROUP_CH = 16
N_SSM_GROUPS = 64
SSM_STATE = 64
N_MEM = 256
N_XHEADS = 4
XHEAD_DIM = 512
D_FF = 5632
CONV_W = 3
ALPHA = 2.0 ** 0.25
LN_EPS = 1e-5

RP = BATCH * SEQ
RS = DEC_BATCH * DEC_SEQ
R = RP + RS
TM = 512
NB = R // TM
NBP = RP // TM
BLOCKS_PER_SEQ = SEQ // TM

CHUNK = 16
N_PAIRS = N_SSM_GROUPS // 2
PAIR_W = 2 * CHUNK * SSM_GROUP_CH
CHUNKS_PER_SEQ = SEQ // CHUNK
P_CHUNK_ROWS = BATCH * CHUNKS_PER_SEQ
SSM_ROWS = P_CHUNK_ROWS + DEC_BATCH
SCAN_PAD = 64
N_SCAN_STEPS = 7

SUB = 256
TF = 512
NF = D_FF // TF

VMEM_LIMIT = 56 * 1024 * 1024
VMEM_LIMIT_FFN = 60 * 1024 * 1024


def _cparams(sem, vmem_limit=VMEM_LIMIT):
    return pltpu.CompilerParams(dimension_semantics=sem, vmem_limit_bytes=vmem_limit)


def _const_spec(shape):
    n = len(shape)
    return pl.BlockSpec(shape, lambda *_: (0,) * n, pipeline_mode=pl.Buffered(1))


def _layer_norm(x, g, b):
    mu = jnp.mean(x, axis=-1, keepdims=True)
    xc = x - mu
    var = jnp.mean(xc * xc, axis=-1, keepdims=True)
    return xc * lax.rsqrt(var + LN_EPS) * g + b


def _dot(a, b):
    return jnp.dot(a, b, preferred_element_type=F32)


def _inproj_kernel(xp_ref, xs_ref, w_ref, o_ref):
    x = jnp.where(pl.program_id(0) < NBP, xp_ref[...], xs_ref[...])
    o_ref[...] = _dot(x.astype(BF16), w_ref[...])


def _inproj(xp, xs, w):
    return pl.pallas_call(
        _inproj_kernel,
        out_shape=jax.ShapeDtypeStruct((R, D_MODEL), F32),
        grid=(NB,),
        in_specs=[
            pl.BlockSpec((TM, D_MODEL), lambda i: (jnp.minimum(i, NBP - 1), 0)),
            pl.BlockSpec((TM, D_MODEL), lambda i: (0, 0)),
            _const_spec((D_MODEL, D_MODEL)),
        ],
        out_specs=pl.BlockSpec((TM, D_MODEL), lambda i: (i, 0)),
        compiler_params=_cparams(("arbitrary",)),
        name="inproj",
    )(xp, xs, w)


def _pool_mix(g, pooled, w_ref, scale_ref):
    sl = slice(g * POOL_GROUP_DIM, (g + 1) * POOL_GROUP_DIM)
    return (_dot(pooled.astype(BF16), w_ref[g]) * scale_ref[:, sl]).astype(BF16)


def _pool_kernel(u_ref, st_ref, w_ref, scale_ref, o_ref, utail_ref, ext_ref):
    i = pl.program_id(0)
    utail_ref[0] = u_ref[TM - 16:TM, :]

    @pl.when(i < NBP)
    def _():
        first = (i % BLOCKS_PER_SEQ) == 0

        @pl.when(first)
        def _():
            ext_ref[0:16, :] = jnp.zeros((16, D_POOL), F32)

        @pl.when(jnp.logical_not(first))
        def _():
            ext_ref[0:16, :] = ext_ref[TM:TM + 16, :]

        ext_ref[16:16 + TM, :] = u_ref[...]
        pos = (i % BLOCKS_PER_SEQ) * TM + lax.broadcasted_iota(jnp.int32, (TM, 1), 0)
        for g, w in enumerate(POOL_WINDOWS):
            sl = slice(g * POOL_GROUP_DIM, (g + 1) * POOL_GROUP_DIM)
            acc = ext_ref[16:16 + TM, sl]
            for k in range(1, w):
                acc = acc + ext_ref[16 - k:16 - k + TM, sl]
            cnt = jnp.minimum(pos + 1, w).astype(F32)
            o_ref[:, sl] = _pool_mix(g, acc / cnt - u_ref[:, sl], w_ref, scale_ref)

    @pl.when(i == NBP)
    def _():
        for j in range(DEC_SEQ):
            rows = slice(j * DEC_BATCH, (j + 1) * DEC_BATCH)
            for g, w in enumerate(POOL_WINDOWS):
                sl = slice(g * POOL_GROUP_DIM, (g + 1) * POOL_GROUP_DIM)
                acc = u_ref[rows, sl]
                for k in range(1, w):
                    e = POOL_BUF + j - k
                    if e >= POOL_BUF:
                        t = e - POOL_BUF
                        acc = acc + u_ref[t * DEC_BATCH:(t + 1) * DEC_BATCH, sl]
                    else:
                        acc = acc + st_ref[e, :, sl]
                cnt = float(min(PAST_LEN + j + 1, w))
                o_ref[rows, sl] = _pool_mix(g, acc / cnt - u_ref[rows, sl], w_ref, scale_ref)


def _pool(u, state_t, w_pool, pool_scale):
    return pl.pallas_call(
        _pool_kernel,
        out_shape=(jax.ShapeDtypeStruct((R, D_POOL), BF16),
                   jax.ShapeDtypeStruct((NB, 16, D_POOL), F32)),
        grid=(NB,),
        in_specs=[
            pl.BlockSpec((TM, D_POOL), lambda i: (i, 0)),
            _const_spec((POOL_BUF, DEC_BATCH, D_POOL)),
            _const_spec((4, POOL_GROUP_DIM, POOL_GROUP_DIM)),
            _const_spec((1, D_POOL)),
        ],
        out_specs=(pl.BlockSpec((TM, D_POOL), lambda i: (i, 0)),
                   pl.BlockSpec((1, 16, D_POOL), lambda i: (i, 0, 0))),
        scratch_shapes=[pltpu.VMEM((TM + 16, D_POOL), F32)],
        compiler_params=_cparams(("arbitrary",)),
        name="pool",
    )(u, state_t, w_pool, pool_scale)


SLAB_GROUPS = 128 // SSM_GROUP_CH
SLAB_PAIRS = SLAB_GROUPS // 2
N_SLABS = N_SSM_GROUPS // SLAB_GROUPS


def _ssm_pair(q, u, t_ref, pre_ref, pim_ref, p4re_ref, p4im_ref, qre_ref, qim_ref,
              apow_re_ref, apow_im_ref, lam4_ref, dskip_ref, h0re_ref, h0im_ref,
              hp_re_ref, hp_im_ref, hs_re_ref, hs_im_ref, hre_scr, him_scr):
    ub = u.astype(BF16)
    half = PAIR_W // 2
    y = jnp.concatenate([_dot(ub[:, :half], t_ref[q, 0]), _dot(ub[:, half:], t_ref[q, 1])], axis=1)

    ubp = ub[:P_CHUNK_ROWS]
    hre_scr[SCAN_PAD:SCAN_PAD + P_CHUNK_ROWS, :] = _dot(ubp, pre_ref[q])
    him_scr[SCAN_PAD:SCAN_PAD + P_CHUNK_ROWS, :] = _dot(ubp, pim_ref[q])
    kk = lax.broadcasted_iota(jnp.int32, (P_CHUNK_ROWS, 1), 0) % CHUNKS_PER_SEQ
    for s in range(N_SCAN_STEPS):
        d = 1 << s
        ar = apow_re_ref[q, s:s + 1, :]
        ai = apow_im_ref[q, s:s + 1, :]
        hr = hre_scr[SCAN_PAD:SCAN_PAD + P_CHUNK_ROWS, :]
        hi = him_scr[SCAN_PAD:SCAN_PAD + P_CHUNK_ROWS, :]
        pr = hre_scr[SCAN_PAD - d:SCAN_PAD - d + P_CHUNK_ROWS, :]
        pi = him_scr[SCAN_PAD - d:SCAN_PAD - d + P_CHUNK_ROWS, :]
        keep = kk >= d
        hre_scr[SCAN_PAD:SCAN_PAD + P_CHUNK_ROWS, :] = hr + jnp.where(keep, ar * pr - ai * pi, 0.0)
        him_scr[SCAN_PAD:SCAN_PAD + P_CHUNK_ROWS, :] = hi + jnp.where(keep, ar * pi + ai * pr, 0.0)
    hp_re_ref[q] = jnp.zeros((8, 128), F32)
    hp_im_ref[q] = jnp.zeros((8, 128), F32)
    for b in range(BATCH):
        last = SCAN_PAD + (b + 1) * CHUNKS_PER_SEQ - 1
        hp_re_ref[q, b:b + 1, :] = hre_scr[last:last + 1, :]
        hp_im_ref[q, b:b + 1, :] = him_scr[last:last + 1, :]
    prev_ok = kk >= 1
    hprev_re = jnp.where(prev_ok, hre_scr[SCAN_PAD - 1:SCAN_PAD - 1 + P_CHUNK_ROWS, :], 0.0)
    hprev_im = jnp.where(prev_ok, him_scr[SCAN_PAD - 1:SCAN_PAD - 1 + P_CHUNK_ROWS, :], 0.0)
    carry_p = _dot(hprev_re.astype(BF16), qre_ref[q]) + _dot(hprev_im.astype(BF16), qim_ref[q])

    ubs = ub[P_CHUNK_ROWS:]
    h0r = h0re_ref[q]
    h0i = h0im_ref[q]
    l4r = lam4_ref[q, 0:1, :]
    l4i = lam4_ref[q, 1:2, :]
    hs_re_ref[q] = l4r * h0r - l4i * h0i + _dot(ubs, p4re_ref[q])
    hs_im_ref[q] = l4r * h0i + l4i * h0r + _dot(ubs, p4im_ref[q])
    carry_s = _dot(h0r.astype(BF16), qre_ref[q]) + _dot(h0i.astype(BF16), qim_ref[q])

    y = y + jnp.concatenate([carry_p, carry_s], axis=0) + dskip_ref[q] * u
    return jax.nn.gelu(y)


def _ssm_kernel(u_ref, *refs):
    z_ref = refs[13]
    hre_scr, him_scr = refs[18], refs[19]
    c = SSM_GROUP_CH
    hre_scr[0:SCAN_PAD, :] = jnp.zeros((SCAN_PAD, 128), F32)
    him_scr[0:SCAN_PAD, :] = jnp.zeros((SCAN_PAD, 128), F32)
    xt = []
    for i in range(CHUNK):
        xp = u_ref[pl.ds(i, P_CHUNK_ROWS, stride=CHUNK), :]
        if i < DEC_SEQ:
            xs = u_ref[RP + i * DEC_BATCH:RP + (i + 1) * DEC_BATCH, :]
        else:
            xs = jnp.zeros((DEC_BATCH, 128), F32)
        xt.append(jnp.concatenate([xp, xs], axis=0).T)
    zt = []
    for q in range(SLAB_PAIRS):
        halves = []
        for e in range(2):
            g = 2 * q + e
            bt = jnp.concatenate([xt[i][g * c:(g + 1) * c, :] for i in range(CHUNK)], axis=0)
            halves.append(bt.T)
        z = _ssm_pair(q, jnp.concatenate(halves, axis=1), *refs[:13], *refs[14:])
        zt.append(z[:, :PAIR_W // 2].T)
        zt.append(z[:, PAIR_W // 2:].T)
    for i in range(CHUNK):
        zi = jnp.concatenate([zt[g][i * c:(i + 1) * c, :] for g in range(SLAB_GROUPS)], axis=0).T
        z_ref[pl.ds(i, P_CHUNK_ROWS, stride=CHUNK), :] = zi[:P_CHUNK_ROWS]
        if i < DEC_SEQ:
            z_ref[RP + i * DEC_BATCH:RP + (i + 1) * DEC_BATCH, :] = zi[P_CHUNK_ROWS:]


def _ssm(u, prep, h0re, h0im):
    sp = SLAB_PAIRS
    slab3 = lambda a, b: pl.BlockSpec((sp, a, b), lambda s: (s, 0, 0))
    return pl.pallas_call(
        _ssm_kernel,
        out_shape=(
            jax.ShapeDtypeStruct((R, D_SSM), F32),
            jax.ShapeDtypeStruct((N_PAIRS, 8, 128), F32),
            jax.ShapeDtypeStruct((N_PAIRS, 8, 128), F32),
            jax.ShapeDtypeStruct((N_PAIRS, DEC_BATCH, 128), F32),
            jax.ShapeDtypeStruct((N_PAIRS, DEC_BATCH, 128), F32),
        ),
        grid=(N_SLABS,),
        in_specs=[
            pl.BlockSpec((R, 128), lambda s: (0, D_POOL // 128 + s)),
            pl.BlockSpec((sp, 2, 256, 256), lambda s: (s, 0, 0, 0)),
            slab3(PAIR_W, 128), slab3(PAIR_W, 128), slab3(PAIR_W, 128), slab3(PAIR_W, 128),
            slab3(128, PAIR_W), slab3(128, PAIR_W),
            slab3(8, 128), slab3(8, 128), slab3(8, 128),
            slab3(1, PAIR_W),
            slab3(DEC_BATCH, 128), slab3(DEC_BATCH, 128),
        ],
        out_specs=(
            pl.BlockSpec((R, 128), lambda s: (0, s)),
            slab3(8, 128), slab3(8, 128), slab3(DEC_BATCH, 128), slab3(DEC_BATCH, 128),
        ),
        scratch_shapes=[pltpu.VMEM((SCAN_PAD + P_CHUNK_ROWS, 128), F32),
                        pltpu.VMEM((SCAN_PAD + P_CHUNK_ROWS, 128), F32)],
        compiler_params=_cparams(("arbitrary",)),
        name="ssm",
    )(u, prep["t"], prep["pre"], prep["pim"], prep["p4re"], prep["p4im"],
      prep["qre"], prep["qim"], prep["apow_re"], prep["apow_im"], prep["lam4"],
      prep["dskip"], h0re, h0im)


def _cmul(ar, ai, br, bi):
    return ar * br - ai * bi, ar * bi + ai * br


def _dot3(a, b):
    ah = a.astype(BF16)
    bh = b.astype(BF16)
    al = (a - ah.astype(F32)).astype(BF16)
    bl = (b - bh.astype(F32)).astype(BF16)
    return _dot(ah, bh) + _dot(ah, bl) + _dot(al, bh)


def _ssm_prep_kernel(lrow_re_ref, lrow_im_ref, lcol_re_ref, lcol_im_ref, lstep_ref,
                     bt_re_ref, bt_im_ref, cx_re_ref, cx_im_ref,
                     t_ref, pre_ref, pim_ref, p4re_ref, p4im_ref, qre_ref, qim_ref,
                     apr_ref, api_ref, lam4_ref):
    L, C, N = CHUNK, SSM_GROUP_CH, SSM_STATE
    w = L * C
    p_re, p_im, p4_re, p4_im, q_re, q_im, a_re, a_im, l4 = [], [], [], [], [], [], [], [], []
    for e in range(2):
        dt = jnp.exp(lstep_ref[e])

        def lam_bar(lr, li):
            mag = jnp.exp(lr * dt)
            ang = li * dt
            return mag * jnp.cos(ang), mag * jnp.sin(ang)

        lr, li = lrow_re_ref[e], lrow_im_ref[e]
        zr, zi = lam_bar(lr, li)
        den = lr * lr + li * li
        fr = ((zr - 1.0) * lr + zi * li) / den
        fi = (zi * lr - (zr - 1.0) * li) / den
        bbr, bbi = _cmul(fr, fi, bt_re_ref[e], bt_im_ref[e])
        pr, pi = [jnp.ones((1, N), F32)], [jnp.zeros((1, N), F32)]
        for _ in range(L):
            nr, ni = _cmul(pr[-1], pi[-1], zr, zi)
            pr.append(nr)
            pi.append(ni)
        blocks = [_cmul(bbr, bbi, pr[L - 1 - i], pi[L - 1 - i]) for i in range(L)]
        p_re.append(jnp.concatenate([b[0] for b in blocks], axis=0))
        p_im.append(jnp.concatenate([b[1] for b in blocks], axis=0))
        blocks4 = [_cmul(bbr, bbi, pr[DEC_SEQ - 1 - i], pi[DEC_SEQ - 1 - i]) for i in range(DEC_SEQ)]
        pad = jnp.zeros(((L - DEC_SEQ) * C, N), F32)
        p4_re.append(jnp.concatenate([b[0] for b in blocks4] + [pad], axis=0))
        p4_im.append(jnp.concatenate([b[1] for b in blocks4] + [pad], axis=0))
        sr, si = [pr[L]], [pi[L]]
        for _ in range(N_SCAN_STEPS - 1):
            nr, ni = _cmul(sr[-1], si[-1], sr[-1], si[-1])
            sr.append(nr)
            si.append(ni)
        a_re.append(jnp.concatenate(sr + [jnp.zeros((8 - N_SCAN_STEPS, N), F32)], axis=0))
        a_im.append(jnp.concatenate(si + [jnp.zeros((8 - N_SCAN_STEPS, N), F32)], axis=0))
        l4.append(jnp.concatenate([pr[DEC_SEQ], pi[DEC_SEQ], jnp.zeros((6, N), F32)], axis=0))

        cr, ci = lam_bar(lcol_re_ref[e], lcol_im_ref[e])
        kk = lax.broadcasted_iota(jnp.int32, (N, w), 1) // C
        er, ei = jnp.ones((N, w), F32), jnp.zeros((N, w), F32)
        sqr, sqi = cr, ci
        for bit in range(4):
            nr, ni = _cmul(er, ei, sqr, sqi)
            has = ((kk >> bit) & 1) == 1
            er, ei = jnp.where(has, nr, er), jnp.where(has, ni, ei)
            sqr, sqi = _cmul(sqr, sqi, sqr, sqi)
        ckr, cki = _cmul(cx_re_ref[e], cx_im_ref[e], er, ei)
        qr, qi = _cmul(ckr, cki, cr, ci)
        q_re.append(qr)
        q_im.append(-qi)
        v = _dot3(bbr, ckr) - _dot3(bbi, cki)
        lane = lax.broadcasted_iota(jnp.int32, (C, w), 1)
        rows = [v] + [jnp.where(lane >= C * i, pltpu.roll(v, C * i, axis=1), 0.0) for i in range(1, L)]
        t_ref[0, e] = jnp.concatenate(rows, axis=0).astype(BF16)

    def diag_rows(m):
        z = jnp.zeros_like(m[0])
        return jnp.concatenate([jnp.concatenate([m[0], z], axis=1),
                                jnp.concatenate([z, m[1]], axis=1)], axis=0)

    pre_ref[0] = diag_rows(p_re).astype(BF16)
    pim_ref[0] = diag_rows(p_im).astype(BF16)
    p4re_ref[0] = diag_rows(p4_re).astype(BF16)
    p4im_ref[0] = diag_rows(p4_im).astype(BF16)
    qre_ref[0] = diag_rows(q_re).astype(BF16)
    qim_ref[0] = diag_rows(q_im).astype(BF16)
    apr_ref[0] = jnp.concatenate(a_re, axis=1)
    api_ref[0] = jnp.concatenate(a_im, axis=1)
    lam4_ref[0] = jnp.concatenate(l4, axis=1)


def _ssm_prep(lambda_re, lambda_im, log_step, b_re, b_im, c_re, c_im, d_skip):
    G, N, C, L = N_SSM_GROUPS, SSM_STATE, SSM_GROUP_CH, CHUNK
    w = L * C
    cexp = lambda c: jnp.tile(c.transpose(0, 2, 1), (1, 1, L))
    g3 = lambda a, b: pl.BlockSpec((2, a, b), lambda p: (p, 0, 0))
    p3 = lambda a, b: pl.BlockSpec((1, a, b), lambda p: (p, 0, 0))
    sds = jax.ShapeDtypeStruct
    outs = pl.pallas_call(
        _ssm_prep_kernel,
        out_shape=(
            sds((N_PAIRS, 2, w, w), BF16),
            sds((N_PAIRS, PAIR_W, 128), BF16), sds((N_PAIRS, PAIR_W, 128), BF16),
            sds((N_PAIRS, PAIR_W, 128), BF16), sds((N_PAIRS, PAIR_W, 128), BF16),
            sds((N_PAIRS, 128, PAIR_W), BF16), sds((N_PAIRS, 128, PAIR_W), BF16),
            sds((N_PAIRS, 8, 128), F32), sds((N_PAIRS, 8, 128), F32), sds((N_PAIRS, 8, 128), F32),
        ),
        grid=(N_PAIRS,),
        in_specs=[g3(1, N), g3(1, N), g3(N, 1), g3(N, 1), g3(1, 1),
                  g3(C, N), g3(C, N), g3(N, w), g3(N, w)],
        out_specs=(
            pl.BlockSpec((1, 2, w, w), lambda p: (p, 0, 0, 0)),
            p3(PAIR_W, 128), p3(PAIR_W, 128), p3(PAIR_W, 128), p3(PAIR_W, 128),
            p3(128, PAIR_W), p3(128, PAIR_W),
            p3(8, 128), p3(8, 128), p3(8, 128),
        ),
        compiler_params=_cparams(("arbitrary",)),
        name="ssm_prep",
    )(lambda_re[:, None, :], lambda_im[:, None, :], lambda_re[:, :, None], lambda_im[:, :, None],
      log_step[:, None, None], b_re.transpose(0, 2, 1), b_im.transpose(0, 2, 1), cexp(c_re), cexp(c_im))
    names = ("t", "pre", "pim", "p4re", "p4im", "qre", "qim", "apow_re", "apow_im", "lam4")
    prep = dict(zip(names, outs))
    prep["dskip"] = jnp.broadcast_to(d_skip.reshape(N_PAIRS, 2, 1, C), (N_PAIRS, 2, L, C)).reshape(N_PAIRS, 1, PAIR_W)
    return prep


def _mix_kernel(a_ref, z_ref, xp_ref, xs_ref, wglu_ref, bglu_ref, wout_ref, g_ref, b_ref, o_ref):
    is_prompt = pl.program_id(0) < NBP
    for r in range(TM // SUB):
        rows = slice(r * SUB, (r + 1) * SUB)
        z = z_ref[rows, :]
        gate = _dot(z.astype(BF16), wglu_ref[...]) + bglu_ref[...]
        bmix = (z * jax.nn.sigmoid(gate)).astype(BF16)
        mix = _dot(a_ref[rows, :], wout_ref[0:D_POOL, :]) + _dot(bmix, wout_ref[D_POOL:, :])
        x = jnp.where(is_prompt, xp_ref[rows, :], xs_ref[rows, :])
        o_ref[rows, :] = _layer_norm(ALPHA * x + mix, g_ref[...], b_ref[...])


def _mix(a, z, xp, xs, w_glu, b_glu, w_out, g, b):
    row = lambda w: pl.BlockSpec((TM, w), lambda i: (i, 0))
    return pl.pallas_call(
        _mix_kernel,
        out_shape=jax.ShapeDtypeStruct((R, D_MODEL), F32),
        grid=(NB,),
        in_specs=[
            row(D_POOL), row(D_SSM),
            pl.BlockSpec((TM, D_MODEL), lambda i: (jnp.minimum(i, NBP - 1), 0)),
            pl.BlockSpec((TM, D_MODEL), lambda i: (0, 0)),
            _const_spec((D_SSM, D_SSM)), _const_spec((1, D_SSM)),
            _const_spec((D_MODEL, D_MODEL)), _const_spec((1, D_MODEL)), _const_spec((1, D_MODEL)),
        ],
        out_specs=row(D_MODEL),
        compiler_params=_cparams(("arbitrary",)),
        name="mix_ln1",
    )(a, z, xp, xs, w_glu, b_glu, w_out, g, b)


def _proj_kernel(x_ref, w_ref, o_ref):
    o_ref[...] = _dot(x_ref[...].astype(BF16), w_ref[...]).astype(o_ref.dtype)


def _proj(x, w, out_dtype, name):
    rows = x.shape[0]
    return pl.pallas_call(
        _proj_kernel,
        out_shape=jax.ShapeDtypeStruct((rows, w.shape[1]), out_dtype),
        grid=(rows // TM,),
        in_specs=[pl.BlockSpec((TM, x.shape[1]), lambda i: (i, 0)), _const_spec(w.shape)],
        out_specs=pl.BlockSpec((TM, w.shape[1]), lambda i: (i, 0)),
        compiler_params=_cparams(("arbitrary",)),
        name=name,
    )(x, w)


def _proj_ln_kernel(o_in_ref, h_ref, w_ref, g_ref, b_ref, out_ref):
    for r in range(TM // SUB):
        rows = slice(r * SUB, (r + 1) * SUB)
        y = _dot(o_in_ref[rows, :], w_ref[...])
        out_ref[rows, :] = _layer_norm(ALPHA * h_ref[rows, :] + y, g_ref[...], b_ref[...])


def _proj_ln(o_in, h, w, g, b):
    row = pl.BlockSpec((TM, D_MODEL), lambda i: (i, 0))
    return pl.pallas_call(
        _proj_ln_kernel,
        out_shape=jax.ShapeDtypeStruct((R, D_MODEL), F32),
        grid=(NB,),
        in_specs=[row, row, _const_spec((D_MODEL, D_MODEL)),
                  _const_spec((1, D_MODEL)), _const_spec((1, D_MODEL))],
        out_specs=row,
        compiler_params=_cparams(("arbitrary",)),
        name="oproj_ln2",
    )(o_in, h, w, g, b)


def _memkv_kernel(m_ref, wk_ref, wv_ref, k_ref, v_ref, kb_ref, vb_ref):
    mb = m_ref[...].astype(BF16)
    k = _dot(mb, wk_ref[...].astype(BF16))
    v = _dot(mb, wv_ref[...].astype(BF16))
    k_ref[...] = k
    v_ref[...] = v
    kb_ref[...] = k.astype(BF16)
    vb_ref[...] = v.astype(BF16)


def _memkv(mem, wk, wv):
    rows = BATCH * N_MEM
    tn = 512
    col = pl.BlockSpec((rows, tn), lambda j: (0, j))
    wcol = pl.BlockSpec((D_MODEL, tn), lambda j: (0, j))
    return pl.pallas_call(
        _memkv_kernel,
        out_shape=(jax.ShapeDtypeStruct((rows, D_MODEL), F32), jax.ShapeDtypeStruct((rows, D_MODEL), F32),
                   jax.ShapeDtypeStruct((rows, D_MODEL), BF16), jax.ShapeDtypeStruct((rows, D_MODEL), BF16)),
        grid=(D_MODEL // tn,),
        in_specs=[_const_spec((rows, D_MODEL)), wcol, wcol],
        out_specs=(col, col, col, col),
        compiler_params=_cparams(("arbitrary",)),
        name="memkv",
    )(mem, wk, wv)


def _attend(q, k, v):
    outs = []
    for h in range(N_XHEADS):
        sl = slice(h * XHEAD_DIM, (h + 1) * XHEAD_DIM)
        s = lax.dot_general(q[:, sl], k[:, sl], (((1,), (1,)), ((), ())),
                            preferred_element_type=F32) * (XHEAD_DIM ** -0.5)
        s = s - jnp.max(s, axis=-1, keepdims=True)
        e = jnp.exp(s)
        p = e / jnp.sum(e, axis=-1, keepdims=True)
        outs.append(_dot(p.astype(BF16), v[:, sl]))
    return jnp.concatenate(outs, axis=1)


def _attn_prompt_kernel(q_ref, k_ref, v_ref, os_ref, o_ref):
    i = pl.program_id(0)

    @pl.when(i < NBP)
    def _():
        o_ref[...] = _attend(q_ref[...], k_ref[...], v_ref[...]).astype(BF16)

    @pl.when(i == NBP)
    def _():
        o_ref[...] = os_ref[...]


def _attn_prompt(q, kb, vb, o_sample):
    kv = pl.BlockSpec((N_MEM, D_MODEL), lambda i: (jnp.minimum(i // BLOCKS_PER_SEQ, BATCH - 1), 0))
    row = pl.BlockSpec((TM, D_MODEL), lambda i: (i, 0))
    return pl.pallas_call(
        _attn_prompt_kernel,
        out_shape=jax.ShapeDtypeStruct((R, D_MODEL), BF16),
        grid=(NB,),
        in_specs=[row, kv, kv, pl.BlockSpec((RS, D_MODEL), lambda i: (0, 0))],
        out_specs=row,
        compiler_params=_cparams(("arbitrary",)),
        name="attn_prompt",
    )(q, kb, vb, o_sample)


ATT_BB = 4
Q_PAD = 8
KV_ROWS = N_MEM * N_XHEADS


def _attn_sample_kernel(q_ref, k_ref, v_ref, o_ref):
    shape = (N_XHEADS * Q_PAD, KV_ROWS)
    same_head = (lax.broadcasted_iota(jnp.int32, shape, 0) // Q_PAD
                 == lax.broadcasted_iota(jnp.int32, shape, 1) % N_XHEADS)
    for b in range(ATT_BB):
        q = q_ref[b].astype(F32)
        qs = jnp.concatenate([q[:, h * XHEAD_DIM:(h + 1) * XHEAD_DIM] for h in range(N_XHEADS)], axis=0)
        k = k_ref[0, b].reshape(KV_ROWS, XHEAD_DIM).astype(BF16)
        v = v_ref[0, b].reshape(KV_ROWS, XHEAD_DIM).astype(BF16)
        s = lax.dot_general(qs.astype(BF16), k, (((1,), (1,)), ((), ())),
                            preferred_element_type=F32) * (XHEAD_DIM ** -0.5)
        s = jnp.where(same_head, s, -1e30)
        s = s - jnp.max(s, axis=-1, keepdims=True)
        e = jnp.exp(s)
        p = e / jnp.sum(e, axis=-1, keepdims=True)
        o = _dot(p.astype(BF16), v)
        for h in range(N_XHEADS):
            o_ref[b, :, h * XHEAD_DIM:(h + 1) * XHEAD_DIM] = o[h * Q_PAD:(h + 1) * Q_PAD]


def _attn_sample(q_pad, k, v):
    kv = pl.BlockSpec((1, ATT_BB, N_MEM, N_XHEADS, XHEAD_DIM), lambda i: (0, i, 0, 0, 0))
    qo = pl.BlockSpec((ATT_BB, Q_PAD, D_MODEL), lambda i: (i, 0, 0))
    return pl.pallas_call(
        _attn_sample_kernel,
        out_shape=jax.ShapeDtypeStruct((DEC_BATCH, Q_PAD, D_MODEL), F32),
        grid=(DEC_BATCH // ATT_BB,),
        in_specs=[qo, kv, kv],
        out_specs=qo,
        compiler_params=_cparams(("arbitrary",)),
        name="attn_sample",
    )(q_pad, k, v)


FTM = 1024
FNB = RP // FTM
FBLOCKS_PER_SEQ = SEQ // FTM
FSUB = 256


def _ffn_rows(n_rows, conv_taps, h_ref, hb_scr, g_scr, g_base, wg, wu, wd, cw_ref, cb_ref,
              lng_ref, lnb_ref, y_ref, first, last):
    for r in range(n_rows // FSUB):
        lo = r * FSUB
        if first:
            h = h_ref[lo:lo + FSUB, :]
            hb = h.astype(BF16)
            hb_scr[lo:lo + FSUB, :] = hb
        else:
            hb = hb_scr[lo:lo + FSUB, :]
        g = _dot(hb, wg)
        up = _dot(hb, wu)
        g_scr[g_base + lo:g_base + lo + FSUB, :] = g
        tap0, tap1 = conv_taps(lo)
        gc = cb_ref[...] + cw_ref[0:1, :] * tap0 + cw_ref[1:2, :] * tap1 + cw_ref[2:3, :] * g
        act = (jax.nn.silu(gc) * up).astype(BF16)
        y = _dot(act, wd) + (ALPHA * h if first else y_ref[lo:lo + FSUB, :])
        y_ref[lo:lo + FSUB, :] = _layer_norm(y, lng_ref[...], lnb_ref[...]) if last else y


def _ffn_chunk(f, run):
    @pl.when(f == 0)
    def _():
        run(True, False)

    @pl.when(jnp.logical_and(f > 0, f < NF - 1))
    def _():
        run(False, False)

    @pl.when(f == NF - 1)
    def _():
        run(False, True)


def _ffn_prompt_kernel(h_ref, wg_ref, wu_ref, wd_ref, cw_ref, cb_ref, lng_ref, lnb_ref,
                       y_ref, gtail_ref, hb_scr, g_scr, carry_scr):
    i = pl.program_id(0)
    f = pl.program_id(1)
    first_block = (i % FBLOCKS_PER_SEQ) == 0

    @pl.when(first_block)
    def _():
        g_scr[0:8, :] = jnp.zeros((8, TF), F32)

    @pl.when(jnp.logical_not(first_block))
    def _():
        g_scr[0:8, :] = carry_scr[f]

    taps = lambda lo: (g_scr[6 + lo:6 + lo + FSUB, :], g_scr[7 + lo:7 + lo + FSUB, :])
    _ffn_chunk(f, lambda first, last: _ffn_rows(
        FTM, taps, h_ref, hb_scr, g_scr, 8, wg_ref[...], wu_ref[...], wd_ref[...], cw_ref, cb_ref,
        lng_ref, lnb_ref, y_ref, first, last))
    tail = g_scr[FTM:FTM + 8, :]
    carry_scr[f] = tail
    gtail_ref[0] = tail


def _ffn_sample_kernel(h_ref, wg_ref, wu_ref, wd_ref, cw_ref, cb_ref, lng_ref, lnb_ref, st_ref,
                       y_ref, gnew_ref, wgb_ref, wub_ref, wdb_ref, hb_scr, g_scr):
    f = pl.program_id(1)
    n_st = 2 * DEC_BATCH
    wgb_ref[...] = wg_ref[...].astype(BF16)
    wub_ref[...] = wu_ref[...].astype(BF16)
    wdb_ref[...] = wd_ref[...].astype(BF16)
    g_scr[0:n_st, :] = st_ref[...]
    taps = lambda lo: (g_scr[lo:lo + FSUB, :], g_scr[DEC_BATCH + lo:DEC_BATCH + lo + FSUB, :])
    _ffn_chunk(f, lambda first, last: _ffn_rows(
        RS, taps, h_ref, hb_scr, g_scr, n_st, wgb_ref[...], wub_ref[...], wdb_ref[...], cw_ref, cb_ref,
        lng_ref, lnb_ref, y_ref, first, last))
    gnew_ref[...] = g_scr[RS:RS + n_st, :]


def _ffn_specs(tm, row_block_of):
    return [
        pl.BlockSpec((tm, D_MODEL), lambda i, f: (row_block_of(i), 0)),
        pl.BlockSpec((D_MODEL, TF), lambda i, f: (0, f)),
        pl.BlockSpec((D_MODEL, TF), lambda i, f: (0, f)),
        pl.BlockSpec((TF, D_MODEL), lambda i, f: (f, 0)),
        pl.BlockSpec((CONV_W, TF), lambda i, f: (0, f)),
        pl.BlockSpec((1, TF), lambda i, f: (0, f)),
        pl.BlockSpec((1, D_MODEL), lambda i, f: (0, 0)),
        pl.BlockSpec((1, D_MODEL), lambda i, f: (0, 0)),
    ]


def _ffn_prompt(h, wg, wu, wd, cw, cb, lng, lnb):
    return pl.pallas_call(
        _ffn_prompt_kernel,
        out_shape=(jax.ShapeDtypeStruct((RP, D_MODEL), F32),
                   jax.ShapeDtypeStruct((FNB, 8, D_FF), F32)),
        grid=(FNB, NF),
        in_specs=_ffn_specs(FTM, lambda i: i),
        out_specs=(pl.BlockSpec((FTM, D_MODEL), lambda i, f: (i, 0)),
                   pl.BlockSpec((1, 8, TF), lambda i, f: (i, 0, f))),
        scratch_shapes=[pltpu.VMEM((FTM, D_MODEL), BF16),
                        pltpu.VMEM((FTM + 8, TF), F32),
                        pltpu.VMEM((NF, 8, TF), F32)],
        compiler_params=_cparams(("arbitrary", "arbitrary"), VMEM_LIMIT_FFN),
        name="ffn_prompt",
    )(h, wg, wu, wd, cw, cb, lng, lnb)


def _ffn_sample(h, wg, wu, wd, cw, cb, lng, lnb, conv_state):
    n_st = 2 * DEC_BATCH
    specs = _ffn_specs(RS, lambda i: RP // RS)
    return pl.pallas_call(
        _ffn_sample_kernel,
        out_shape=(jax.ShapeDtypeStruct((RS, D_MODEL), F32),
                   jax.ShapeDtypeStruct((n_st, D_FF), F32),
                   jax.ShapeDtypeStruct((D_MODEL, D_FF), BF16),
                   jax.ShapeDtypeStruct((D_MODEL, D_FF), BF16),
                   jax.ShapeDtypeStruct((D_FF, D_MODEL), BF16)),
        grid=(1, NF),
        in_specs=specs + [pl.BlockSpec((n_st, TF), lambda i, f: (0, f))],
        out_specs=(pl.BlockSpec((RS, D_MODEL), lambda i, f: (0, 0)),
                   pl.BlockSpec((n_st, TF), lambda i, f: (0, f)),
                   specs[1], specs[2], specs[3]),
        scratch_shapes=[pltpu.VMEM((RS, D_MODEL), BF16),
                        pltpu.VMEM((n_st + RS, TF), F32)],
        compiler_params=_cparams(("arbitrary", "arbitrary")),
        name="ffn_sample",
    )(h, wg, wu, wd, cw, cb, lng, lnb, conv_state)


def _state_to_pairs(s):
    return s.reshape(s.shape[0], N_PAIRS, 128).transpose(1, 0, 2)


def _state_from_pairs(s):
    return s.transpose(1, 0, 2).reshape(1, s.shape[1], N_SSM_GROUPS, SSM_STATE)


def kernel(x_prompt, x_sample, mem_prompt, state_pool, state_ssm_re, state_ssm_im, state_conv, cache_mem_k, cache_mem_v, w_in, w_pool, pool_scale, lambda_re, lambda_im, log_step, b_re, b_im, c_re, c_im, d_skip, w_glu, b_glu, w_out, ln1_g, ln1_b, w_q, w_k, w_v, w_o, ln2_g, ln2_b, w_gate, w_up, conv_w, conv_b, w_down, ln3_g, ln3_b):
    bf = lambda w: w[0].astype(BF16)
    row = lambda v: v[0].reshape(1, -1)
    xp = x_prompt.reshape(RP, D_MODEL)
    xs = x_sample.transpose(1, 0, 2).reshape(RS, D_MODEL)

    u = _inproj(xp, xs, bf(w_in))
    wp = bf(w_pool)
    a, utail = _pool(u, state_pool[0].transpose(1, 0, 2), wp, row(pool_scale))
    prep = _ssm_prep(lambda_re[0], lambda_im[0], log_step[0], b_re[0], b_im[0], c_re[0], c_im[0], d_skip[0])
    z, hp_re, hp_im, hs_re, hs_im = _ssm(
        u, prep, _state_to_pairs(state_ssm_re[0]), _state_to_pairs(state_ssm_im[0]))
    h1 = _mix(a, z, xp, xs, bf(w_glu), row(b_glu), bf(w_out), row(ln1_g), row(ln1_b))

    mk, mv, mkb, mvb = _memkv(mem_prompt.reshape(BATCH * N_MEM, D_MODEL), w_k[0], w_v[0])
    q = _proj(h1, bf(w_q), BF16, "qproj")
    q_s = q[RP:].reshape(DEC_SEQ, DEC_BATCH, D_MODEL).transpose(1, 0, 2)
    q_s = jnp.pad(q_s, ((0, 0), (0, Q_PAD - DEC_SEQ), (0, 0)))
    o_s = _attn_sample(q_s, cache_mem_k, cache_mem_v)
    o_s = o_s[:, :DEC_SEQ].transpose(1, 0, 2).reshape(RS, D_MODEL).astype(BF16)
    o_all = _attn_prompt(q, mkb, mvb, o_s)
    h2 = _proj_ln(o_all, h1, bf(w_o), row(ln2_g), row(ln2_b))

    cw, cb = conv_w[0], row(conv_b)
    conv_st = state_conv[0].transpose(1, 0, 2).reshape(2 * DEC_BATCH, D_FF)
    y_s, g_new, wg, wu, wd = _ffn_sample(h2, w_gate[0], w_up[0], w_down[0], cw, cb, row(ln3_g), row(ln3_b), conv_st)
    y_p, gtail = _ffn_prompt(h2, wg, wu, wd, cw, cb, row(ln3_g), row(ln3_b))

    y_prompt = y_p.reshape(BATCH, SEQ, D_MODEL)
    y_sample = y_s.reshape(DEC_SEQ, DEC_BATCH, D_MODEL).transpose(1, 0, 2)
    p_pool = utail[BLOCKS_PER_SEQ - 1:NBP:BLOCKS_PER_SEQ, 16 - POOL_BUF:][None]
    s_ext = jnp.concatenate([state_pool[0], u[RP:, :D_POOL].reshape(DEC_SEQ, DEC_BATCH, D_POOL).transpose(1, 0, 2)], axis=1)
    s_pool = s_ext[None, :, DEC_SEQ:]
    p_conv = gtail[FBLOCKS_PER_SEQ - 1::FBLOCKS_PER_SEQ, 6:8][None]
    s_conv = g_new.reshape(2, DEC_BATCH, D_FF).transpose(1, 0, 2)[None]
    shape_kv = (1, BATCH, N_MEM, N_XHEADS, XHEAD_DIM)
    return (y_prompt, y_sample,
            p_pool, _state_from_pairs(hp_re[:, :BATCH]), _state_from_pairs(hp_im[:, :BATCH]), p_conv,
            mk.reshape(shape_kv), mv.reshape(shape_kv),
            s_pool, _state_from_pairs(hs_re), _state_from_pairs(hs_im), s_conv)
```

```python
import functools
import math

import jax
import jax.numpy as jnp
from jax import lax
from jax.experimental import pallas as pl
from jax.experimental.pallas import tpu as pltpu

F32 = jnp.float32
BF16 = jnp.bfloat16

D_MODEL = 2048
BATCH = 4
SEQ = 2048
DEC_BATCH = 128
DEC_SEQ = 4
PAST_LEN = 16384
D_POOL = 1024
D_SSM = 1024
POOL_WINDOWS = (2, 4, 8, 16)
POOL_GROUP_DIM = 256
POOL_BUF = 15
SSM_GROUP_CH = 16
N_SSM_GROUPS = 64
SSM_STATE = 64
N_MEM = 256
N_XHEADS = 4
XHEAD_DIM = 512
D_FF = 5632
CONV_W = 3
ALPHA = 2.0 ** 0.25
LN_EPS = 1e-5

RP = BATCH * SEQ
RS = DEC_BATCH * DEC_SEQ
R = RP + RS
TM = 512
NB = R // TM
NBP = RP // TM
BLOCKS_PER_SEQ = SEQ // TM

CHUNK = 16
N_PAIRS = N_SSM_GROUPS // 2
PAIR_W = 2 * CHUNK * SSM_GROUP_CH
CHUNKS_PER_SEQ = SEQ // CHUNK
P_CHUNK_ROWS = BATCH * CHUNKS_PER_SEQ
SSM_ROWS = P_CHUNK_ROWS + DEC_BATCH
SCAN_PAD = 64
N_SCAN_STEPS = 7

SUB = 256
TF = 512
NF = D_FF // TF

VMEM_LIMIT = 56 * 1024 * 1024
VMEM_LIMIT_FFN = 60 * 1024 * 1024


def _cparams(sem, vmem_limit=VMEM_LIMIT):
    return pltpu.CompilerParams(dimension_semantics=sem, vmem_limit_bytes=vmem_limit)


def _const_spec(shape):
    n = len(shape)
    return pl.BlockSpec(shape, lambda *_: (0,) * n, pipeline_mode=pl.Buffered(1))


def _layer_norm(x, g, b):
    mu = jnp.mean(x, axis=-1, keepdims=True)
    xc = x - mu
    var = jnp.mean(xc * xc, axis=-1, keepdims=True)
    return xc * lax.rsqrt(var + LN_EPS) * g + b


def _dot(a, b):
    return jnp.dot(a, b, preferred_element_type=F32)


N_CAST = 4
CAST_STEPS = NBP


def _inproj_kernel(xp_ref, xs_ref, w_ref, *refs):
    cast_in, o_ref, cast_out = refs[:N_CAST], refs[N_CAST], refs[N_CAST + 1:]
    x = jnp.where(pl.program_id(0) < NBP, xp_ref[...], xs_ref[...])
    o_ref[...] = _dot(x.astype(BF16), w_ref[...])
    for src_ref, dst_ref in zip(cast_in, cast_out):
        dst_ref[...] = src_ref[...].astype(BF16)


def _inproj(xp, xs, w, to_cast):
    assert len(to_cast) == N_CAST
    slab = lambda a: pl.BlockSpec((a.shape[0] // CAST_STEPS, a.shape[1]),
                                  lambda i: (jnp.minimum(i, CAST_STEPS - 1), 0))
    outs = pl.pallas_call(
        _inproj_kernel,
        out_shape=(jax.ShapeDtypeStruct((R, D_MODEL), F32),
                   *[jax.ShapeDtypeStruct(a.shape, BF16) for a in to_cast]),
        grid=(NB,),
        in_specs=[
            pl.BlockSpec((TM, D_MODEL), lambda i: (jnp.minimum(i, NBP - 1), 0)),
            pl.BlockSpec((TM, D_MODEL), lambda i: (0, 0)),
            _const_spec((D_MODEL, D_MODEL)),
            *[slab(a) for a in to_cast],
        ],
        out_specs=(pl.BlockSpec((TM, D_MODEL), lambda i: (i, 0)), *[slab(a) for a in to_cast]),
        compiler_params=_cparams(("arbitrary",)),
        name="inproj",
    )(xp, xs, w, *to_cast)
    return outs[0], outs[1:]


def _pool_mix(g, pooled, w_ref, scale_ref):
    sl = slice(g * POOL_GROUP_DIM, (g + 1) * POOL_GROUP_DIM)
    return (_dot(pooled.astype(BF16), w_ref[g]) * scale_ref[:, sl]).astype(BF16)


def _pool_kernel(u_ref, st_ref, w_ref, scale_ref, o_ref, utail_ref, ext_ref):
    i = pl.program_id(0)
    utail_ref[0] = u_ref[TM - 16:TM, :]

    @pl.when(i < NBP)
    def _():
        first = (i % BLOCKS_PER_SEQ) == 0

        @pl.when(first)
        def _():
            ext_ref[0:16, :] = jnp.zeros((16, D_POOL), F32)

        @pl.when(jnp.logical_not(first))
        def _():
            ext_ref[0:16, :] = ext_ref[TM:TM + 16, :]

        ext_ref[16:16 + TM, :] = u_ref[...]
        pos = (i % BLOCKS_PER_SEQ) * TM + lax.broadcasted_iota(jnp.int32, (TM, 1), 0)
        for g, w in enumerate(POOL_WINDOWS):
            sl = slice(g * POOL_GROUP_DIM, (g + 1) * POOL_GROUP_DIM)
            acc = ext_ref[16:16 + TM, sl]
            for k in range(1, w):
                acc = acc + ext_ref[16 - k:16 - k + TM, sl]
            cnt = jnp.minimum(pos + 1, w).astype(F32)
            o_ref[:, sl] = _pool_mix(g, acc / cnt - u_ref[:, sl], w_ref, scale_ref)

    @pl.when(i == NBP)
    def _():
        for j in range(DEC_SEQ):
            rows = slice(j * DEC_BATCH, (j + 1) * DEC_BATCH)
            for g, w in enumerate(POOL_WINDOWS):
                sl = slice(g * POOL_GROUP_DIM, (g + 1) * POOL_GROUP_DIM)
                acc = u_ref[rows, sl]
                for k in range(1, w):
                    e = POOL_BUF + j - k
                    if e >= POOL_BUF:
                        t = e - POOL_BUF
                        acc = acc + u_ref[t * DEC_BATCH:(t + 1) * DEC_BATCH, sl]
                    else:
                        acc = acc + st_ref[e, :, sl]
                cnt = float(min(PAST_LEN + j + 1, w))
                o_ref[rows, sl] = _pool_mix(g, acc / cnt - u_ref[rows, sl], w_ref, scale_ref)


def _pool(u, state_t, w_pool, pool_scale):
    return pl.pallas_call(
        _pool_kernel,
        out_shape=(jax.ShapeDtypeStruct((R, D_POOL), BF16),
                   jax.ShapeDtypeStruct((NB, 16, D_POOL), F32)),
        grid=(NB,),
        in_specs=[
            pl.BlockSpec((TM, D_POOL), lambda i: (i, 0)),
            _const_spec((POOL_BUF, DEC_BATCH, D_POOL)),
            _const_spec((4, POOL_GROUP_DIM, POOL_GROUP_DIM)),
            _const_spec((1, D_POOL)),
        ],
        out_specs=(pl.BlockSpec((TM, D_POOL), lambda i: (i, 0)),
                   pl.BlockSpec((1, 16, D_POOL), lambda i: (i, 0, 0))),
        scratch_shapes=[pltpu.VMEM((TM + 16, D_POOL), F32)],
        compiler_params=_cparams(("arbitrary",)),
        name="pool",
    )(u, state_t, w_pool, pool_scale)


SLAB_GROUPS = 128 // SSM_GROUP_CH
SLAB_PAIRS = SLAB_GROUPS // 2
N_SLABS = N_SSM_GROUPS // SLAB_GROUPS


def _ssm_pair(q, u, t_ref, pre_ref, pim_ref, p4re_ref, p4im_ref, qre_ref, qim_ref,
              apow_re_ref, apow_im_ref, lam4_ref, dskip_ref, h0re_ref, h0im_ref,
              hp_re_ref, hp_im_ref, hs_re_ref, hs_im_ref, hre_scr, him_scr):
    ub = u.astype(BF16)
    half = PAIR_W // 2
    y = jnp.concatenate([_dot(ub[:, :half], t_ref[q, 0]), _dot(ub[:, half:], t_ref[q, 1])], axis=1)

    ubp = ub[:P_CHUNK_ROWS]
    hre_scr[SCAN_PAD:SCAN_PAD + P_CHUNK_ROWS, :] = _dot(ubp, pre_ref[q])
    him_scr[SCAN_PAD:SCAN_PAD + P_CHUNK_ROWS, :] = _dot(ubp, pim_ref[q])
    kk = lax.broadcasted_iota(jnp.int32, (P_CHUNK_ROWS, 1), 0) % CHUNKS_PER_SEQ
    for s in range(N_SCAN_STEPS):
        d = 1 << s
        ar = apow_re_ref[q, s:s + 1, :]
        ai = apow_im_ref[q, s:s + 1, :]
        hr = hre_scr[SCAN_PAD:SCAN_PAD + P_CHUNK_ROWS, :]
        hi = him_scr[SCAN_PAD:SCAN_PAD + P_CHUNK_ROWS, :]
        pr = hre_scr[SCAN_PAD - d:SCAN_PAD - d + P_CHUNK_ROWS, :]
        pi = him_scr[SCAN_PAD - d:SCAN_PAD - d + P_CHUNK_ROWS, :]
        keep = kk >= d
        hre_scr[SCAN_PAD:SCAN_PAD + P_CHUNK_ROWS, :] = hr + jnp.where(keep, ar * pr - ai * pi, 0.0)
        him_scr[SCAN_PAD:SCAN_PAD + P_CHUNK_ROWS, :] = hi + jnp.where(keep, ar * pi + ai * pr, 0.0)
    hp_re_ref[q] = jnp.zeros((8, 128), F32)
    hp_im_ref[q] = jnp.zeros((8, 128), F32)
    for b in range(BATCH):
        last = SCAN_PAD + (b + 1) * CHUNKS_PER_SEQ - 1
        hp_re_ref[q, b:b + 1, :] = hre_scr[last:last + 1, :]
        hp_im_ref[q, b:b + 1, :] = him_scr[last:last + 1, :]
    prev_ok = kk >= 1
    hprev_re = jnp.where(prev_ok, hre_scr[SCAN_PAD - 1:SCAN_PAD - 1 + P_CHUNK_ROWS, :], 0.0)
    hprev_im = jnp.where(prev_ok, him_scr[SCAN_PAD - 1:SCAN_PAD - 1 + P_CHUNK_ROWS, :], 0.0)
    carry_p = _dot(hprev_re.astype(BF16), qre_ref[q]) + _dot(hprev_im.astype(BF16), qim_ref[q])

    ubs = ub[P_CHUNK_ROWS:]
    h0r = h0re_ref[q]
    h0i = h0im_ref[q]
    l4r = lam4_ref[q, 0:1, :]
    l4i = lam4_ref[q, 1:2, :]
    hs_re_ref[q] = l4r * h0r - l4i * h0i + _dot(ubs, p4re_ref[q])
    hs_im_ref[q] = l4r * h0i + l4i * h0r + _dot(ubs, p4im_ref[q])
    carry_s = _dot(h0r.astype(BF16), qre_ref[q]) + _dot(h0i.astype(BF16), qim_ref[q])

    y = y + jnp.concatenate([carry_p, carry_s], axis=0) + dskip_ref[q] * u
    return jax.nn.gelu(y)


def _ssm_kernel(u_ref, *refs):
    z_ref = refs[13]
    hre_scr, him_scr = refs[18], refs[19]
    c = SSM_GROUP_CH
    hre_scr[0:SCAN_PAD, :] = jnp.zeros((SCAN_PAD, 128), F32)
    him_scr[0:SCAN_PAD, :] = jnp.zeros((SCAN_PAD, 128), F32)
    xt = []
    for i in range(CHUNK):
        xp = u_ref[pl.ds(i, P_CHUNK_ROWS, stride=CHUNK), :]
        if i < DEC_SEQ:
            xs = u_ref[RP + i * DEC_BATCH:RP + (i + 1) * DEC_BATCH, :]
        else:
            xs = jnp.zeros((DEC_BATCH, 128), F32)
        xt.append(jnp.concatenate([xp, xs], axis=0).T)
    zt = []
    for q in range(SLAB_PAIRS):
        halves = []
        for e in range(2):
            g = 2 * q + e
            bt = jnp.concatenate([xt[i][g * c:(g + 1) * c, :] for i in range(CHUNK)], axis=0)
            halves.append(bt.T)
        z = _ssm_pair(q, jnp.concatenate(halves, axis=1), *refs[:13], *refs[14:])
        zt.append(z[:, :PAIR_W // 2].T)
        zt.append(z[:, PAIR_W // 2:].T)
    for i in range(CHUNK):
        zi = jnp.concatenate([zt[g][i * c:(i + 1) * c, :] for g in range(SLAB_GROUPS)], axis=0).T
        z_ref[pl.ds(i, P_CHUNK_ROWS, stride=CHUNK), :] = zi[:P_CHUNK_ROWS]
        if i < DEC_SEQ:
            z_ref[RP + i * DEC_BATCH:RP + (i + 1) * DEC_BATCH, :] = zi[P_CHUNK_ROWS:]


def _ssm(u, prep, h0re, h0im):
    sp = SLAB_PAIRS
    slab3 = lambda a, b: pl.BlockSpec((sp, a, b), lambda s: (s, 0, 0))
    return pl.pallas_call(
        _ssm_kernel,
        out_shape=(
            jax.ShapeDtypeStruct((R, D_SSM), F32),
            jax.ShapeDtypeStruct((N_PAIRS, 8, 128), F32),
            jax.ShapeDtypeStruct((N_PAIRS, 8, 128), F32),
            jax.ShapeDtypeStruct((N_PAIRS, DEC_BATCH, 128), F32),
            jax.ShapeDtypeStruct((N_PAIRS, DEC_BATCH, 128), F32),
        ),
        grid=(N_SLABS,),
        in_specs=[
            pl.BlockSpec((R, 128), lambda s: (0, D_POOL // 128 + s)),
            pl.BlockSpec((sp, 2, 256, 256), lambda s: (s, 0, 0, 0)),
            slab3(PAIR_W, 128), slab3(PAIR_W, 128), slab3(PAIR_W, 128), slab3(PAIR_W, 128),
            slab3(128, PAIR_W), slab3(128, PAIR_W),
            slab3(8, 128), slab3(8, 128), slab3(8, 128),
            slab3(1, PAIR_W),
            slab3(DEC_BATCH, 128), slab3(DEC_BATCH, 128),
        ],
        out_specs=(
            pl.BlockSpec((R, 128), lambda s: (0, s)),
            slab3(8, 128), slab3(8, 128), slab3(DEC_BATCH, 128), slab3(DEC_BATCH, 128),
        ),
        scratch_shapes=[pltpu.VMEM((SCAN_PAD + P_CHUNK_ROWS, 128), F32),
                        pltpu.VMEM((SCAN_PAD + P_CHUNK_ROWS, 128), F32)],
        compiler_params=_cparams(("arbitrary",)),
        name="ssm",
    )(u, prep["t"], prep["pre"], prep["pim"], prep["p4re"], prep["p4im"],
      prep["qre"], prep["qim"], prep["apow_re"], prep["apow_im"], prep["lam4"],
      prep["dskip"], h0re, h0im)


def _cmul(ar, ai, br, bi):
    return ar * br - ai * bi, ar * bi + ai * br


def _dot3(a, b):
    ah = a.astype(BF16)
    bh = b.astype(BF16)
    al = (a - ah.astype(F32)).astype(BF16)
    bl = (b - bh.astype(F32)).astype(BF16)
    return _dot(ah, bh) + _dot(ah, bl) + _dot(al, bh)


def _ssm_prep_kernel(lrow_re_ref, lrow_im_ref, lcol_re_ref, lcol_im_ref, lstep_ref,
                     bt_re_ref, bt_im_ref, cx_re_ref, cx_im_ref,
                     t_ref, pre_ref, pim_ref, p4re_ref, p4im_ref, qre_ref, qim_ref,
                     apr_ref, api_ref, lam4_ref):
    L, C, N = CHUNK, SSM_GROUP_CH, SSM_STATE
    w = L * C
    p_re, p_im, p4_re, p4_im, q_re, q_im, a_re, a_im, l4 = [], [], [], [], [], [], [], [], []
    for e in range(2):
        dt = jnp.exp(lstep_ref[e])

        def lam_bar(lr, li):
            mag = jnp.exp(lr * dt)
            ang = li * dt
            return mag * jnp.cos(ang), mag * jnp.sin(ang)

        lr, li = lrow_re_ref[e], lrow_im_ref[e]
        zr, zi = lam_bar(lr, li)
        den = lr * lr + li * li
        fr = ((zr - 1.0) * lr + zi * li) / den
        fi = (zi * lr - (zr - 1.0) * li) / den
        bbr, bbi = _cmul(fr, fi, bt_re_ref[e], bt_im_ref[e])
        pr, pi = [jnp.ones((1, N), F32)], [jnp.zeros((1, N), F32)]
        for _ in range(L):
            nr, ni = _cmul(pr[-1], pi[-1], zr, zi)
            pr.append(nr)
            pi.append(ni)
        blocks = [_cmul(bbr, bbi, pr[L - 1 - i], pi[L - 1 - i]) for i in range(L)]
        p_re.append(jnp.concatenate([b[0] for b in blocks], axis=0))
        p_im.append(jnp.concatenate([b[1] for b in blocks], axis=0))
        blocks4 = [_cmul(bbr, bbi, pr[DEC_SEQ - 1 - i], pi[DEC_SEQ - 1 - i]) for i in range(DEC_SEQ)]
        pad = jnp.zeros(((L - DEC_SEQ) * C, N), F32)
        p4_re.append(jnp.concatenate([b[0] for b in blocks4] + [pad], axis=0))
        p4_im.append(jnp.concatenate([b[1] for b in blocks4] + [pad], axis=0))
        sr, si = [pr[L]], [pi[L]]
        for _ in range(N_SCAN_STEPS - 1):
            nr, ni = _cmul(sr[-1], si[-1], sr[-1], si[-1])
            sr.append(nr)
            si.append(ni)
        a_re.append(jnp.concatenate(sr + [jnp.zeros((8 - N_SCAN_STEPS, N), F32)], axis=0))
        a_im.append(jnp.concatenate(si + [jnp.zeros((8 - N_SCAN_STEPS, N), F32)], axis=0))
        l4.append(jnp.concatenate([pr[DEC_SEQ], pi[DEC_SEQ], jnp.zeros((6, N), F32)], axis=0))

        cr, ci = lam_bar(lcol_re_ref[e], lcol_im_ref[e])
        kk = lax.broadcasted_iota(jnp.int32, (N, w), 1) // C
        er, ei = jnp.ones((N, w), F32), jnp.zeros((N, w), F32)
        sqr, sqi = cr, ci
        for bit in range(4):
            nr, ni = _cmul(er, ei, sqr, sqi)
            has = ((kk >> bit) & 1) == 1
            er, ei = jnp.where(has, nr, er), jnp.where(has, ni, ei)
            sqr, sqi = _cmul(sqr, sqi, sqr, sqi)
        ckr, cki = _cmul(cx_re_ref[e], cx_im_ref[e], er, ei)
        qr, qi = _cmul(ckr, cki, cr, ci)
        q_re.append(qr)
        q_im.append(-qi)
        v = _dot3(bbr, ckr) - _dot3(bbi, cki)
        lane = lax.broadcasted_iota(jnp.int32, (C, w), 1)
        rows = [v] + [jnp.where(lane >= C * i, pltpu.roll(v, C * i, axis=1), 0.0) for i in range(1, L)]
        t_ref[0, e] = jnp.concatenate(rows, axis=0).astype(BF16)

    def diag_rows(m):
        z = jnp.zeros_like(m[0])
        return jnp.concatenate([jnp.concatenate([m[0], z], axis=1),
                                jnp.concatenate([z, m[1]], axis=1)], axis=0)

    pre_ref[0] = diag_rows(p_re).astype(BF16)
    pim_ref[0] = diag_rows(p_im).astype(BF16)
    p4re_ref[0] = diag_rows(p4_re).astype(BF16)
    p4im_ref[0] = diag_rows(p4_im).astype(BF16)
    qre_ref[0] = diag_rows(q_re).astype(BF16)
    qim_ref[0] = diag_rows(q_im).astype(BF16)
    apr_ref[0] = jnp.concatenate(a_re, axis=1)
    api_ref[0] = jnp.concatenate(a_im, axis=1)
    lam4_ref[0] = jnp.concatenate(l4, axis=1)


def _ssm_prep(lambda_re, lambda_im, log_step, b_re, b_im, c_re, c_im, d_skip):
    G, N, C, L = N_SSM_GROUPS, SSM_STATE, SSM_GROUP_CH, CHUNK
    w = L * C
    cexp = lambda c: jnp.tile(c.transpose(0, 2, 1), (1, 1, L))
    g3 = lambda a, b: pl.BlockSpec((2, a, b), lambda p: (p, 0, 0))
    p3 = lambda a, b: pl.BlockSpec((1, a, b), lambda p: (p, 0, 0))
    sds = jax.ShapeDtypeStruct
    outs = pl.pallas_call(
        _ssm_prep_kernel,
        out_shape=(
            sds((N_PAIRS, 2, w, w), BF16),
            sds((N_PAIRS, PAIR_W, 128), BF16), sds((N_PAIRS, PAIR_W, 128), BF16),
            sds((N_PAIRS, PAIR_W, 128), BF16), sds((N_PAIRS, PAIR_W, 128), BF16),
            sds((N_PAIRS, 128, PAIR_W), BF16), sds((N_PAIRS, 128, PAIR_W), BF16),
            sds((N_PAIRS, 8, 128), F32), sds((N_PAIRS, 8, 128), F32), sds((N_PAIRS, 8, 128), F32),
        ),
        grid=(N_PAIRS,),
        in_specs=[g3(1, N), g3(1, N), g3(N, 1), g3(N, 1), g3(1, 1),
                  g3(C, N), g3(C, N), g3(N, w), g3(N, w)],
        out_specs=(
            pl.BlockSpec((1, 2, w, w), lambda p: (p, 0, 0, 0)),
            p3(PAIR_W, 128), p3(PAIR_W, 128), p3(PAIR_W, 128), p3(PAIR_W, 128),
            p3(128, PAIR_W), p3(128, PAIR_W),
            p3(8, 128), p3(8, 128), p3(8, 128),
        ),
        compiler_params=_cparams(("arbitrary",)),
        name="ssm_prep",
    )(lambda_re[:, None, :], lambda_im[:, None, :], lambda_re[:, :, None], lambda_im[:, :, None],
      log_step[:, None, None], b_re.transpose(0, 2, 1), b_im.transpose(0, 2, 1), cexp(c_re), cexp(c_im))
    names = ("t", "pre", "pim", "p4re", "p4im", "qre", "qim", "apow_re", "apow_im", "lam4")
    prep = dict(zip(names, outs))
    prep["dskip"] = jnp.broadcast_to(d_skip.reshape(N_PAIRS, 2, 1, C), (N_PAIRS, 2, L, C)).reshape(N_PAIRS, 1, PAIR_W)
    return prep


def _mix_kernel(a_ref, z_ref, xp_ref, xs_ref, wglu_ref, bglu_ref, wout_ref, g_ref, b_ref, o_ref):
    is_prompt = pl.program_id(0) < NBP
    for r in range(TM // SUB):
        rows = slice(r * SUB, (r + 1) * SUB)
        z = z_ref[rows, :]
        gate = _dot(z.astype(BF16), wglu_ref[...]) + bglu_ref[...]
        bmix = (z * jax.nn.sigmoid(gate)).astype(BF16)
        mix = _dot(a_ref[rows, :], wout_ref[0:D_POOL, :]) + _dot(bmix, wout_ref[D_POOL:, :])
        x = jnp.where(is_prompt, xp_ref[rows, :], xs_ref[rows, :])
        o_ref[rows, :] = _layer_norm(ALPHA * x + mix, g_ref[...], b_ref[...])


def _mix(a, z, xp, xs, w_glu, b_glu, w_out, g, b):
    row = lambda w: pl.BlockSpec((TM, w), lambda i: (i, 0))
    return pl.pallas_call(
        _mix_kernel,
        out_shape=jax.ShapeDtypeStruct((R, D_MODEL), F32),
        grid=(NB,),
        in_specs=[
            row(D_POOL), row(D_SSM),
            pl.BlockSpec((TM, D_MODEL), lambda i: (jnp.minimum(i, NBP - 1), 0)),
            pl.BlockSpec((TM, D_MODEL), lambda i: (0, 0)),
            _const_spec((D_SSM, D_SSM)), _const_spec((1, D_SSM)),
            _const_spec((D_MODEL, D_MODEL)), _const_spec((1, D_MODEL)), _const_spec((1, D_MODEL)),
        ],
        out_specs=row(D_MODEL),
        compiler_params=_cparams(("arbitrary",)),
        name="mix_ln1",
    )(a, z, xp, xs, w_glu, b_glu, w_out, g, b)


def _proj_kernel(x_ref, w_ref, o_ref):
    o_ref[...] = _dot(x_ref[...].astype(BF16), w_ref[...]).astype(o_ref.dtype)


def _qproj_sample(h, w):
    return pl.pallas_call(
        _proj_kernel,
        out_shape=jax.ShapeDtypeStruct((RS, D_MODEL), BF16),
        grid=(1,),
        in_specs=[pl.BlockSpec((RS, D_MODEL), lambda i: (RP // RS, 0)),
                  pl.BlockSpec((D_MODEL, D_MODEL), lambda i: (0, 0))],
        out_specs=pl.BlockSpec((RS, D_MODEL), lambda i: (0, 0)),
        compiler_params=_cparams(("arbitrary",)),
        name="qproj_sample",
    )(h, w)


def _memkv_kernel(m_ref, wk_ref, wv_ref, k_ref, v_ref, kb_ref, vb_ref):
    mb = m_ref[...].astype(BF16)
    k = _dot(mb, wk_ref[...].astype(BF16))
    v = _dot(mb, wv_ref[...].astype(BF16))
    k_ref[...] = k
    v_ref[...] = v
    kb_ref[...] = k.astype(BF16)
    vb_ref[...] = v.astype(BF16)


def _memkv(mem, wk, wv):
    rows = BATCH * N_MEM
    tn = 512
    col = pl.BlockSpec((rows, tn), lambda j: (0, j))
    wcol = pl.BlockSpec((D_MODEL, tn), lambda j: (0, j))
    return pl.pallas_call(
        _memkv_kernel,
        out_shape=(jax.ShapeDtypeStruct((rows, D_MODEL), F32), jax.ShapeDtypeStruct((rows, D_MODEL), F32),
                   jax.ShapeDtypeStruct((rows, D_MODEL), BF16), jax.ShapeDtypeStruct((rows, D_MODEL), BF16)),
        grid=(D_MODEL // tn,),
        in_specs=[_const_spec((rows, D_MODEL)), wcol, wcol],
        out_specs=(col, col, col, col),
        compiler_params=_cparams(("arbitrary",)),
        name="memkv",
    )(mem, wk, wv)


def _attend(q, k, v):
    outs = []
    for h in range(N_XHEADS):
        sl = slice(h * XHEAD_DIM, (h + 1) * XHEAD_DIM)
        s = lax.dot_general(q[:, sl], k[:, sl], (((1,), (1,)), ((), ())),
                            preferred_element_type=F32) * (XHEAD_DIM ** -0.5)
        s = s - jnp.max(s, axis=-1, keepdims=True)
        e = jnp.exp(s)
        p = e / jnp.sum(e, axis=-1, keepdims=True)
        outs.append(_dot(p.astype(BF16), v[:, sl]))
    return jnp.concatenate(outs, axis=1)


def _attn_block_kernel(h_ref, k_ref, v_ref, os_ref, wq_ref, wo_ref, g_ref, b_ref, o_ref):
    i = pl.program_id(0)

    def finish(rows, h, o):
        y = _dot(o, wo_ref[...])
        o_ref[rows, :] = _layer_norm(ALPHA * h + y, g_ref[...], b_ref[...])

    @pl.when(i < NBP)
    def _():
        for r in range(TM // SUB):
            rows = slice(r * SUB, (r + 1) * SUB)
            h = h_ref[rows, :]
            q = _dot(h.astype(BF16), wq_ref[...]).astype(BF16)
            finish(rows, h, _attend(q, k_ref[...], v_ref[...]).astype(BF16))

    @pl.when(i == NBP)
    def _():
        for r in range(TM // SUB):
            rows = slice(r * SUB, (r + 1) * SUB)
            finish(rows, h_ref[rows, :], os_ref[rows, :])


def _attn_block(h, kb, vb, o_sample, wq, wo, g, b):
    kv = pl.BlockSpec((N_MEM, D_MODEL), lambda i: (jnp.minimum(i // BLOCKS_PER_SEQ, BATCH - 1), 0))
    row = pl.BlockSpec((TM, D_MODEL), lambda i: (i, 0))
    return pl.pallas_call(
        _attn_block_kernel,
        out_shape=jax.ShapeDtypeStruct((R, D_MODEL), F32),
        grid=(NB,),
        in_specs=[row, kv, kv, _const_spec((RS, D_MODEL)),
                  _const_spec((D_MODEL, D_MODEL)), _const_spec((D_MODEL, D_MODEL)),
                  _const_spec((1, D_MODEL)), _const_spec((1, D_MODEL))],
        out_specs=row,
        compiler_params=_cparams(("arbitrary",)),
        name="attn_block",
    )(h, kb, vb, o_sample, wq, wo, g, b)


ATT_BB = 4
Q_PAD = 8
KV_ROWS = N_MEM * N_XHEADS


def _attn_sample_kernel(q_ref, k_ref, v_ref, o_ref):
    shape = (N_XHEADS * Q_PAD, KV_ROWS)
    same_head = (lax.broadcasted_iota(jnp.int32, shape, 0) // Q_PAD
                 == lax.broadcasted_iota(jnp.int32, shape, 1) % N_XHEADS)
    for b in range(ATT_BB):
        q = q_ref[b].astype(F32)
        qs = jnp.concatenate([q[:, h * XHEAD_DIM:(h + 1) * XHEAD_DIM] for h in range(N_XHEADS)], axis=0)
        k = k_ref[0, b].reshape(KV_ROWS, XHEAD_DIM).astype(BF16)
        v = v_ref[0, b].reshape(KV_ROWS, XHEAD_DIM).astype(BF16)
        s = lax.dot_general(qs.astype(BF16), k, (((1,), (1,)), ((), ())),
                            preferred_element_type=F32) * (XHEAD_DIM ** -0.5)
        s = jnp.where(same_head, s, -1e30)
        s = s - jnp.max(s, axis=-1, keepdims=True)
        e = jnp.exp(s)
        p = e / jnp.sum(e, axis=-1, keepdims=True)
        o = _dot(p.astype(BF16), v)
        for h in range(N_XHEADS):
            o_ref[b, :, h * XHEAD_DIM:(h + 1) * XHEAD_DIM] = o[h * Q_PAD:(h + 1) * Q_PAD]


def _attn_sample(q_pad, k, v):
    kv = pl.BlockSpec((1, ATT_BB, N_MEM, N_XHEADS, XHEAD_DIM), lambda i: (0, i, 0, 0, 0))
    qo = pl.BlockSpec((ATT_BB, Q_PAD, D_MODEL), lambda i: (i, 0, 0))
    return pl.pallas_call(
        _attn_sample_kernel,
        out_shape=jax.ShapeDtypeStruct((DEC_BATCH, Q_PAD, D_MODEL), F32),
        grid=(DEC_BATCH // ATT_BB,),
        in_specs=[qo, kv, kv],
        out_specs=qo,
        compiler_params=_cparams(("arbitrary",)),
        name="attn_sample",
    )(q_pad, k, v)


FTM = 1024
FNB = RP // FTM
FBLOCKS_PER_SEQ = SEQ // FTM
FSUB = 256


def _ffn_rows(n_rows, conv_taps, h_ref, hb_scr, g_scr, g_base, wg, wu, wd, cw_ref, cb_ref,
              lng_ref, lnb_ref, y_ref, first, last):
    for r in range(n_rows // FSUB):
        lo = r * FSUB
        if first:
            h = h_ref[lo:lo + FSUB, :]
            hb = h.astype(BF16)
            hb_scr[lo:lo + FSUB, :] = hb
        else:
            hb = hb_scr[lo:lo + FSUB, :]
        g = _dot(hb, wg)
        up = _dot(hb, wu)
        g_scr[g_base + lo:g_base + lo + FSUB, :] = g
        tap0, tap1 = conv_taps(lo)
        gc = cb_ref[...] + cw_ref[0:1, :] * tap0 + cw_ref[1:2, :] * tap1 + cw_ref[2:3, :] * g
        act = (jax.nn.silu(gc) * up).astype(BF16)
        y = _dot(act, wd) + (ALPHA * h if first else y_ref[lo:lo + FSUB, :])
        y_ref[lo:lo + FSUB, :] = _layer_norm(y, lng_ref[...], lnb_ref[...]) if last else y


def _ffn_chunk(f, run):
    @pl.when(f == 0)
    def _():
        run(True, False)

    @pl.when(jnp.logical_and(f > 0, f < NF - 1))
    def _():
        run(False, False)

    @pl.when(f == NF - 1)
    def _():
        run(False, True)


def _ffn_prompt_kernel(h_ref, wg_ref, wu_ref, wd_ref, cw_ref, cb_ref, lng_ref, lnb_ref,
                       y_ref, gtail_ref, hb_scr, g_scr, carry_scr):
    i = pl.program_id(0)
    f = pl.program_id(1)
    first_block = (i % FBLOCKS_PER_SEQ) == 0

    @pl.when(first_block)
    def _():
        g_scr[0:8, :] = jnp.zeros((8, TF), F32)

    @pl.when(jnp.logical_not(first_block))
    def _():
        g_scr[0:8, :] = carry_scr[f]

    taps = lambda lo: (g_scr[6 + lo:6 + lo + FSUB, :], g_scr[7 + lo:7 + lo + FSUB, :])
    _ffn_chunk(f, lambda first, last: _ffn_rows(
        FTM, taps, h_ref, hb_scr, g_scr, 8, wg_ref[...], wu_ref[...], wd_ref[...], cw_ref, cb_ref,
        lng_ref, lnb_ref, y_ref, first, last))
    tail = g_scr[FTM:FTM + 8, :]
    carry_scr[f] = tail
    gtail_ref[0] = tail


def _ffn_sample_kernel(h_ref, wg_ref, wu_ref, wd_ref, cw_ref, cb_ref, lng_ref, lnb_ref, st_ref,
                       y_ref, gnew_ref, wgb_ref, wub_ref, wdb_ref, hb_scr, g_scr):
    f = pl.program_id(1)
    n_st = 2 * DEC_BATCH
    wgb_ref[...] = wg_ref[...].astype(BF16)
    wub_ref[...] = wu_ref[...].astype(BF16)
    wdb_ref[...] = wd_ref[...].astype(BF16)
    g_scr[0:n_st, :] = st_ref[...]
    taps = lambda lo: (g_scr[lo:lo + FSUB, :], g_scr[DEC_BATCH + lo:DEC_BATCH + lo + FSUB, :])
    _ffn_chunk(f, lambda first, last: _ffn_rows(
        RS, taps, h_ref, hb_scr, g_scr, n_st, wgb_ref[...], wub_ref[...], wdb_ref[...], cw_ref, cb_ref,
        lng_ref, lnb_ref, y_ref, first, last))
    gnew_ref[...] = g_scr[RS:RS + n_st, :]


def _ffn_specs(tm, row_block_of):
    return [
        pl.BlockSpec((tm, D_MODEL), lambda i, f: (row_block_of(i), 0)),
        pl.BlockSpec((D_MODEL, TF), lambda i, f: (0, f)),
        pl.BlockSpec((D_MODEL, TF), lambda i, f: (0, f)),
        pl.BlockSpec((TF, D_MODEL), lambda i, f: (f, 0)),
        pl.BlockSpec((CONV_W, TF), lambda i, f: (0, f)),
        pl.BlockSpec((1, TF), lambda i, f: (0, f)),
        pl.BlockSpec((1, D_MODEL), lambda i, f: (0, 0)),
        pl.BlockSpec((1, D_MODEL), lambda i, f: (0, 0)),
    ]


def _ffn_prompt(h, wg, wu, wd, cw, cb, lng, lnb):
    return pl.pallas_call(
        _ffn_prompt_kernel,
        out_shape=(jax.ShapeDtypeStruct((RP, D_MODEL), F32),
                   jax.ShapeDtypeStruct((FNB, 8, D_FF), F32)),
        grid=(FNB, NF),
        in_specs=_ffn_specs(FTM, lambda i: i),
        out_specs=(pl.BlockSpec((FTM, D_MODEL), lambda i, f: (i, 0)),
                   pl.BlockSpec((1, 8, TF), lambda i, f: (i, 0, f))),
        scratch_shapes=[pltpu.VMEM((FTM, D_MODEL), BF16),
                        pltpu.VMEM((FTM + 8, TF), F32),
                        pltpu.VMEM((NF, 8, TF), F32)],
        compiler_params=_cparams(("arbitrary", "arbitrary"), VMEM_LIMIT_FFN),
        name="ffn_prompt",
    )(h, wg, wu, wd, cw, cb, lng, lnb)


def _ffn_sample(h, wg, wu, wd, cw, cb, lng, lnb, conv_state):
    n_st = 2 * DEC_BATCH
    specs = _ffn_specs(RS, lambda i: RP // RS)
    return pl.pallas_call(
        _ffn_sample_kernel,
        out_shape=(jax.ShapeDtypeStruct((RS, D_MODEL), F32),
                   jax.ShapeDtypeStruct((n_st, D_FF), F32),
                   jax.ShapeDtypeStruct((D_MODEL, D_FF), BF16),
                   jax.ShapeDtypeStruct((D_MODEL, D_FF), BF16),
                   jax.ShapeDtypeStruct((D_FF, D_MODEL), BF16)),
        grid=(1, NF),
        in_specs=specs + [pl.BlockSpec((n_st, TF), lambda i, f: (0, f))],
        out_specs=(pl.BlockSpec((RS, D_MODEL), lambda i, f: (0, 0)),
                   pl.BlockSpec((n_st, TF), lambda i, f: (0, f)),
                   specs[1], specs[2], specs[3]),
        scratch_shapes=[pltpu.VMEM((RS, D_MODEL), BF16),
                        pltpu.VMEM((n_st + RS, TF), F32)],
        compiler_params=_cparams(("arbitrary", "arbitrary")),
        name="ffn_sample",
    )(h, wg, wu, wd, cw, cb, lng, lnb, conv_state)


def _state_to_pairs(s):
    return s.reshape(s.shape[0], N_PAIRS, 128).transpose(1, 0, 2)


def _state_from_pairs(s):
    return s.transpose(1, 0, 2).reshape(1, s.shape[1], N_SSM_GROUPS, SSM_STATE)


def kernel(x_prompt, x_sample, mem_prompt, state_pool, state_ssm_re, state_ssm_im, state_conv, cache_mem_k, cache_mem_v, w_in, w_pool, pool_scale, lambda_re, lambda_im, log_step, b_re, b_im, c_re, c_im, d_skip, w_glu, b_glu, w_out, ln1_g, ln1_b, w_q, w_k, w_v, w_o, ln2_g, ln2_b, w_gate, w_up, conv_w, conv_b, w_down, ln3_g, ln3_b):
    bf = lambda w: w[0].astype(BF16)
    row = lambda v: v[0].reshape(1, -1)
    xp = x_prompt.reshape(RP, D_MODEL)
    xs = x_sample.transpose(1, 0, 2).reshape(RS, D_MODEL)

    u, (w_glu_b, w_out_b, w_q_b, w_o_b) = _inproj(xp, xs, bf(w_in), (w_glu[0], w_out[0], w_q[0], w_o[0]))
    wp = bf(w_pool)
    a, utail = _pool(u, state_pool[0].transpose(1, 0, 2), wp, row(pool_scale))
    prep = _ssm_prep(lambda_re[0], lambda_im[0], log_step[0], b_re[0], b_im[0], c_re[0], c_im[0], d_skip[0])
    z, hp_re, hp_im, hs_re, hs_im = _ssm(
        u, prep, _state_to_pairs(state_ssm_re[0]), _state_to_pairs(state_ssm_im[0]))
    h1 = _mix(a, z, xp, xs, w_glu_b, row(b_glu), w_out_b, row(ln1_g), row(ln1_b))

    mk, mv, mkb, mvb = _memkv(mem_prompt.reshape(BATCH * N_MEM, D_MODEL), w_k[0], w_v[0])
    q_s = _qproj_sample(h1, w_q_b).reshape(DEC_SEQ, DEC_BATCH, D_MODEL).transpose(1, 0, 2)
    q_s = jnp.pad(q_s, ((0, 0), (0, Q_PAD - DEC_SEQ), (0, 0)))
    o_s = _attn_sample(q_s, cache_mem_k, cache_mem_v)
    o_s = o_s[:, :DEC_SEQ].transpose(1, 0, 2).reshape(RS, D_MODEL).astype(BF16)
    h2 = _attn_block(h1, mkb, mvb, o_s, w_q_b, w_o_b, row(ln2_g), row(ln2_b))

    cw, cb = conv_w[0], row(conv_b)
    conv_st = state_conv[0].transpose(1, 0, 2).reshape(2 * DEC_BATCH, D_FF)
    y_s, g_new, wg, wu, wd = _ffn_sample(h2, w_gate[0], w_up[0], w_down[0], cw, cb, row(ln3_g), row(ln3_b), conv_st)
    y_p, gtail = _ffn_prompt(h2, wg, wu, wd, cw, cb, row(ln3_g), row(ln3_b))

    y_prompt = y_p.reshape(BATCH, SEQ, D_MODEL)
    y_sample = y_s.reshape(DEC_SEQ, DEC_BATCH, D_MODEL).transpose(1, 0, 2)
    p_pool = utail[BLOCKS_PER_SEQ - 1:NBP:BLOCKS_PER_SEQ, 16 - POOL_BUF:][None]
    s_ext = jnp.concatenate([state_pool[0], u[RP:, :D_POOL].reshape(DEC_SEQ, DEC_BATCH, D_POOL).transpose(1, 0, 2)], axis=1)
    s_pool = s_ext[None, :, DEC_SEQ:]
    p_conv = gtail[FBLOCKS_PER_SEQ - 1::FBLOCKS_PER_SEQ, 6:8][None]
    s_conv = g_new.reshape(2, DEC_BATCH, D_FF).transpose(1, 0, 2)[None]
    shape_kv = (1, BATCH, N_MEM, N_XHEADS, XHEAD_DIM)
    return (y_prompt, y_sample,
            p_pool, _state_from_pairs(hp_re[:, :BATCH]), _state_from_pairs(hp_im[:, :BATCH]), p_conv,
            mk.reshape(shape_kv), mv.reshape(shape_kv),
            s_pool, _state_from_pairs(hs_re), _state_from_pairs(hs_im), s_conv)
```

```python
import functools
import math

import jax
import jax.numpy as jnp
from jax import lax
from jax.experimental import pallas as pl
from jax.experimental.pallas import tpu as pltpu

F32 = jnp.float32
BF16 = jnp.bfloat16

D_MODEL = 2048
BATCH = 4
SEQ = 2048
DEC_BATCH = 128
DEC_SEQ = 4
PAST_LEN = 16384
D_POOL = 1024
D_SSM = 1024
POOL_WINDOWS = (2, 4, 8, 16)
POOL_GROUP_DIM = 256
POOL_BUF = 15
SSM_GROUP_CH = 16
N_SSM_GROUPS = 64
SSM_STATE = 64
N_MEM = 256
N_XHEADS = 4
XHEAD_DIM = 512
D_FF = 5632
CONV_W = 3
ALPHA = 2.0 ** 0.25
LN_EPS = 1e-5

RP = BATCH * SEQ
RS = DEC_BATCH * DEC_SEQ
R = RP + RS
TM = 512
NB = R // TM
NBP = RP // TM
BLOCKS_PER_SEQ = SEQ // TM

CHUNK = 16
N_PAIRS = N_SSM_GROUPS // 2
PAIR_W = 2 * CHUNK * SSM_GROUP_CH
CHUNKS_PER_SEQ = SEQ // CHUNK
P_CHUNK_ROWS = BATCH * CHUNKS_PER_SEQ
SSM_ROWS = P_CHUNK_ROWS + DEC_BATCH
SCAN_PAD = 64
N_SCAN_STEPS = 7

SUB = 256
TF = 512
NF = D_FF // TF

VMEM_LIMIT = 56 * 1024 * 1024
VMEM_LIMIT_FFN = 60 * 1024 * 1024


def _cparams(sem, vmem_limit=VMEM_LIMIT):
    return pltpu.CompilerParams(dimension_semantics=sem, vmem_limit_bytes=vmem_limit)


def _const_spec(shape):
    n = len(shape)
    return pl.BlockSpec(shape, lambda *_: (0,) * n, pipeline_mode=pl.Buffered(1))


def _layer_norm(x, g, b):
    mu = jnp.mean(x, axis=-1, keepdims=True)
    xc = x - mu
    var = jnp.mean(xc * xc, axis=-1, keepdims=True)
    return xc * lax.rsqrt(var + LN_EPS) * g + b


def _dot(a, b):
    return jnp.dot(a, b, preferred_element_type=F32)


def _cast_spec(a, n_steps):
    return pl.BlockSpec((a.shape[0] // n_steps, a.shape[1]), lambda i: (jnp.minimum(i, n_steps - 1), 0))


def _cast_specs(arrays, n_steps):
    specs = [_cast_spec(a, n_steps) for a in arrays]
    return specs, [jax.ShapeDtypeStruct(a.shape, BF16) for a in arrays]


def _cast_slices(in_refs, out_refs):
    for src_ref, dst_ref in zip(in_refs, out_refs):
        dst_ref[...] = src_ref[...].astype(BF16)


N_PREP_IN, N_PREP_OUT = 9, 10
PREP_PAIRS = N_PAIRS // NBP


def _pool_mix(g, pooled, w_ref, scale_ref):
    sl = slice(g * POOL_GROUP_DIM, (g + 1) * POOL_GROUP_DIM)
    return (_dot(pooled.astype(BF16), w_ref[g]) * scale_ref[:, sl]).astype(BF16)


def _front_kernel(xp_ref, xs_ref, w_ref, st_ref, wp_ref, scale_ref, *refs):
    prep_in, refs = refs[:N_PREP_IN], refs[N_PREP_IN:]
    n_cast = (len(refs) - 4 - N_PREP_OUT - 1) // 2
    cast_in, refs = refs[:n_cast], refs[n_cast:]
    ussm_ref, a_ref, utail_ref, us_ref = refs[:4]
    prep_out, cast_out, ext_ref = refs[4:4 + N_PREP_OUT], refs[4 + N_PREP_OUT:-1], refs[-1]
    i = pl.program_id(0)

    x = jnp.where(i < NBP, xp_ref[...], xs_ref[...])
    u = _dot(x.astype(BF16), w_ref[...])
    ussm_ref[...] = u[:, D_POOL:]

    first = (i % BLOCKS_PER_SEQ) == 0
    ext_ref[0:16, :] = jnp.where(first, 0.0, ext_ref[TM:TM + 16, :])
    ext_ref[16:16 + TM, :] = u[:, :D_POOL]
    utail_ref[0] = ext_ref[TM:TM + 16, :]
    pos = (i % BLOCKS_PER_SEQ) * TM + lax.broadcasted_iota(jnp.int32, (TM, 1), 0)
    for g, w in enumerate(POOL_WINDOWS):
        sl = slice(g * POOL_GROUP_DIM, (g + 1) * POOL_GROUP_DIM)
        acc = ext_ref[16:16 + TM, sl]
        for k in range(1, w):
            acc = acc + ext_ref[16 - k:16 - k + TM, sl]
        cnt = jnp.minimum(pos + 1, w).astype(F32)
        a_ref[:, sl] = _pool_mix(g, acc / cnt - ext_ref[16:16 + TM, sl], wp_ref, scale_ref)

    for pp in range(PREP_PAIRS):
        _ssm_prep_pair(pp, *prep_in, *prep_out)
    _cast_slices(cast_in, cast_out)

    @pl.when(i == NBP)
    def _():
        us_ref[...] = ext_ref[16:16 + TM, :]
        tok = lambda t, sl: ext_ref[16 + t * DEC_BATCH:16 + (t + 1) * DEC_BATCH, sl]
        for j in range(DEC_SEQ):
            rows = slice(j * DEC_BATCH, (j + 1) * DEC_BATCH)
            for g, w in enumerate(POOL_WINDOWS):
                sl = slice(g * POOL_GROUP_DIM, (g + 1) * POOL_GROUP_DIM)
                acc = tok(j, sl)
                for k in range(1, w):
                    e = POOL_BUF + j - k
                    acc = acc + (tok(e - POOL_BUF, sl) if e >= POOL_BUF else st_ref[e, :, sl])
                cnt = float(min(PAST_LEN + j + 1, w))
                a_ref[rows, sl] = _pool_mix(g, acc / cnt - tok(j, sl), wp_ref, scale_ref)


def _front(xp, xs, w, state_t, w_pool, pool_scale, prep_args, to_cast):
    G, N, C, L = N_SSM_GROUPS, SSM_STATE, SSM_GROUP_CH, CHUNK
    wd = L * C
    step = lambda i: jnp.minimum(i, NBP - 1)
    g3 = lambda a, b: pl.BlockSpec((2 * PREP_PAIRS, a, b), lambda i: (step(i), 0, 0))
    p3 = lambda a, b: pl.BlockSpec((PREP_PAIRS, a, b), lambda i: (step(i), 0, 0))
    sds = jax.ShapeDtypeStruct
    cast_specs, cast_shapes = _cast_specs(to_cast, NBP)
    row = lambda wdt: pl.BlockSpec((TM, wdt), lambda i: (i, 0))
    outs = pl.pallas_call(
        _front_kernel,
        out_shape=(
            sds((R, D_SSM), F32), sds((R, D_POOL), BF16), sds((NB, 16, D_POOL), F32), sds((RS, D_POOL), F32),
            sds((N_PAIRS, 2, wd, wd), BF16),
            sds((N_PAIRS, PAIR_W, 128), BF16), sds((N_PAIRS, PAIR_W, 128), BF16),
            sds((N_PAIRS, PAIR_W, 128), BF16), sds((N_PAIRS, PAIR_W, 128), BF16),
            sds((N_PAIRS, 128, PAIR_W), BF16), sds((N_PAIRS, 128, PAIR_W), BF16),
            sds((N_PAIRS, 8, 128), F32), sds((N_PAIRS, 8, 128), F32), sds((N_PAIRS, 8, 128), F32),
            *cast_shapes,
        ),
        grid=(NB,),
        in_specs=[
            pl.BlockSpec((TM, D_MODEL), lambda i: (step(i), 0)),
            _const_spec((RS, D_MODEL)),
            _const_spec((D_MODEL, D_MODEL)),
            _const_spec((POOL_BUF, DEC_BATCH, D_POOL)),
            _const_spec((4, POOL_GROUP_DIM, POOL_GROUP_DIM)),
            _const_spec((1, D_POOL)),
            g3(1, N), g3(1, N), g3(N, 1), g3(N, 1), g3(1, 1), g3(C, N), g3(C, N), g3(N, wd), g3(N, wd),
            *cast_specs,
        ],
        out_specs=(
            row(D_SSM), row(D_POOL), pl.BlockSpec((1, 16, D_POOL), lambda i: (i, 0, 0)),
            pl.BlockSpec((RS, D_POOL), lambda i: (0, 0)),
            pl.BlockSpec((PREP_PAIRS, 2, wd, wd), lambda i: (step(i), 0, 0, 0)),
            p3(PAIR_W, 128), p3(PAIR_W, 128), p3(PAIR_W, 128), p3(PAIR_W, 128),
            p3(128, PAIR_W), p3(128, PAIR_W),
            p3(8, 128), p3(8, 128), p3(8, 128),
            *cast_specs,
        ),
        scratch_shapes=[pltpu.VMEM((TM + 16, D_POOL), F32)],
        compiler_params=_cparams(("arbitrary",)),
        name="front",
    )(xp, xs, w, state_t, w_pool, pool_scale, *prep_args, *to_cast)
    names = ("t", "pre", "pim", "p4re", "p4im", "qre", "qim", "apow_re", "apow_im", "lam4")
    return outs[:4], dict(zip(names, outs[4:4 + N_PREP_OUT])), outs[4 + N_PREP_OUT:]


SLAB_GROUPS = 128 // SSM_GROUP_CH
SLAB_PAIRS = SLAB_GROUPS // 2
N_SLABS = N_SSM_GROUPS // SLAB_GROUPS


def _ssm_pair(q, u, t_ref, pre_ref, pim_ref, p4re_ref, p4im_ref, qre_ref, qim_ref,
              apow_re_ref, apow_im_ref, lam4_ref, dskip_ref, h0re_ref, h0im_ref,
              hp_re_ref, hp_im_ref, hs_re_ref, hs_im_ref, hre_scr, him_scr):
    ub = u.astype(BF16)
    half = PAIR_W // 2
    y = jnp.concatenate([_dot(ub[:, :half], t_ref[q, 0]), _dot(ub[:, half:], t_ref[q, 1])], axis=1)

    ubp = ub[:P_CHUNK_ROWS]
    hre_scr[SCAN_PAD:SCAN_PAD + P_CHUNK_ROWS, :] = _dot(ubp, pre_ref[q])
    him_scr[SCAN_PAD:SCAN_PAD + P_CHUNK_ROWS, :] = _dot(ubp, pim_ref[q])
    kk = lax.broadcasted_iota(jnp.int32, (P_CHUNK_ROWS, 1), 0) % CHUNKS_PER_SEQ
    for s in range(N_SCAN_STEPS):
        d = 1 << s
        ar = apow_re_ref[q, s:s + 1, :]
        ai = apow_im_ref[q, s:s + 1, :]
        hr = hre_scr[SCAN_PAD:SCAN_PAD + P_CHUNK_ROWS, :]
        hi = him_scr[SCAN_PAD:SCAN_PAD + P_CHUNK_ROWS, :]
        pr = hre_scr[SCAN_PAD - d:SCAN_PAD - d + P_CHUNK_ROWS, :]
        pi = him_scr[SCAN_PAD - d:SCAN_PAD - d + P_CHUNK_ROWS, :]
        keep = kk >= d
        hre_scr[SCAN_PAD:SCAN_PAD + P_CHUNK_ROWS, :] = hr + jnp.where(keep, ar * pr - ai * pi, 0.0)
        him_scr[SCAN_PAD:SCAN_PAD + P_CHUNK_ROWS, :] = hi + jnp.where(keep, ar * pi + ai * pr, 0.0)
    hp_re_ref[q] = jnp.zeros((8, 128), F32)
    hp_im_ref[q] = jnp.zeros((8, 128), F32)
    for b in range(BATCH):
        last = SCAN_PAD + (b + 1) * CHUNKS_PER_SEQ - 1
        hp_re_ref[q, b:b + 1, :] = hre_scr[last:last + 1, :]
        hp_im_ref[q, b:b + 1, :] = him_scr[last:last + 1, :]
    prev_ok = kk >= 1
    hprev_re = jnp.where(prev_ok, hre_scr[SCAN_PAD - 1:SCAN_PAD - 1 + P_CHUNK_ROWS, :], 0.0)
    hprev_im = jnp.where(prev_ok, him_scr[SCAN_PAD - 1:SCAN_PAD - 1 + P_CHUNK_ROWS, :], 0.0)
    carry_p = _dot(hprev_re.astype(BF16), qre_ref[q]) + _dot(hprev_im.astype(BF16), qim_ref[q])

    ubs = ub[P_CHUNK_ROWS:]
    h0r = h0re_ref[q]
    h0i = h0im_ref[q]
    l4r = lam4_ref[q, 0:1, :]
    l4i = lam4_ref[q, 1:2, :]
    hs_re_ref[q] = l4r * h0r - l4i * h0i + _dot(ubs, p4re_ref[q])
    hs_im_ref[q] = l4r * h0i + l4i * h0r + _dot(ubs, p4im_ref[q])
    carry_s = _dot(h0r.astype(BF16), qre_ref[q]) + _dot(h0i.astype(BF16), qim_ref[q])

    y = y + jnp.concatenate([carry_p, carry_s], axis=0) + dskip_ref[q] * u
    return jax.nn.gelu(y)


N_SSM_IN, N_SSM_OUT = 13, 5


def _ssm_kernel(u_ref, *all_refs):
    n_cast = (len(all_refs) - N_SSM_IN - N_SSM_OUT - 2) // 2
    _cast_slices(all_refs[N_SSM_IN:N_SSM_IN + n_cast], all_refs[N_SSM_IN + n_cast + N_SSM_OUT:-2])
    refs = all_refs[:N_SSM_IN] + all_refs[N_SSM_IN + n_cast:N_SSM_IN + n_cast + N_SSM_OUT] + all_refs[-2:]
    z_ref = refs[13]
    hre_scr, him_scr = refs[18], refs[19]
    c = SSM_GROUP_CH
    hre_scr[0:SCAN_PAD, :] = jnp.zeros((SCAN_PAD, 128), F32)
    him_scr[0:SCAN_PAD, :] = jnp.zeros((SCAN_PAD, 128), F32)
    xt = []
    for i in range(CHUNK):
        xp = u_ref[pl.ds(i, P_CHUNK_ROWS, stride=CHUNK), :]
        if i < DEC_SEQ:
            xs = u_ref[RP + i * DEC_BATCH:RP + (i + 1) * DEC_BATCH, :]
        else:
            xs = jnp.zeros((DEC_BATCH, 128), F32)
        xt.append(jnp.concatenate([xp, xs], axis=0).T)
    zt = []
    for q in range(SLAB_PAIRS):
        halves = []
        for e in range(2):
            g = 2 * q + e
            bt = jnp.concatenate([xt[i][g * c:(g + 1) * c, :] for i in range(CHUNK)], axis=0)
            halves.append(bt.T)
        z = _ssm_pair(q, jnp.concatenate(halves, axis=1), *refs[:13], *refs[14:])
        zt.append(z[:, :PAIR_W // 2].T)
        zt.append(z[:, PAIR_W // 2:].T)
    for i in range(CHUNK):
        zi = jnp.concatenate([zt[g][i * c:(i + 1) * c, :] for g in range(SLAB_GROUPS)], axis=0).T
        z_ref[pl.ds(i, P_CHUNK_ROWS, stride=CHUNK), :] = zi[:P_CHUNK_ROWS]
        if i < DEC_SEQ:
            z_ref[RP + i * DEC_BATCH:RP + (i + 1) * DEC_BATCH, :] = zi[P_CHUNK_ROWS:]


def _ssm(u_ssm, prep, h0re, h0im, to_cast):
    sp = SLAB_PAIRS
    slab3 = lambda a, b: pl.BlockSpec((sp, a, b), lambda s: (s, 0, 0))
    cast_specs, cast_shapes = _cast_specs(to_cast, N_SLABS)
    outs = pl.pallas_call(
        _ssm_kernel,
        out_shape=(
            jax.ShapeDtypeStruct((R, D_SSM), F32),
            jax.ShapeDtypeStruct((N_PAIRS, 8, 128), F32),
            jax.ShapeDtypeStruct((N_PAIRS, 8, 128), F32),
            jax.ShapeDtypeStruct((N_PAIRS, DEC_BATCH, 128), F32),
            jax.ShapeDtypeStruct((N_PAIRS, DEC_BATCH, 128), F32),
            *cast_shapes,
        ),
        grid=(N_SLABS,),
        in_specs=[
            pl.BlockSpec((R, 128), lambda s: (0, s)),
            pl.BlockSpec((sp, 2, 256, 256), lambda s: (s, 0, 0, 0)),
            slab3(PAIR_W, 128), slab3(PAIR_W, 128), slab3(PAIR_W, 128), slab3(PAIR_W, 128),
            slab3(128, PAIR_W), slab3(128, PAIR_W),
            slab3(8, 128), slab3(8, 128), slab3(8, 128),
            slab3(1, PAIR_W),
            slab3(DEC_BATCH, 128), slab3(DEC_BATCH, 128),
            *cast_specs,
        ],
        out_specs=(
            pl.BlockSpec((R, 128), lambda s: (0, s)),
            slab3(8, 128), slab3(8, 128), slab3(DEC_BATCH, 128), slab3(DEC_BATCH, 128),
            *cast_specs,
        ),
        scratch_shapes=[pltpu.VMEM((SCAN_PAD + P_CHUNK_ROWS, 128), F32),
                        pltpu.VMEM((SCAN_PAD + P_CHUNK_ROWS, 128), F32)],
        compiler_params=_cparams(("arbitrary",)),
        name="ssm",
    )(u_ssm, prep["t"], prep["pre"], prep["pim"], prep["p4re"], prep["p4im"],
      prep["qre"], prep["qim"], prep["apow_re"], prep["apow_im"], prep["lam4"],
      prep["dskip"], h0re, h0im, *to_cast)
    return outs[:N_SSM_OUT], outs[N_SSM_OUT:]


def _cmul(ar, ai, br, bi):
    return ar * br - ai * bi, ar * bi + ai * br


def _dot3(a, b):
    ah = a.astype(BF16)
    bh = b.astype(BF16)
    al = (a - ah.astype(F32)).astype(BF16)
    bl = (b - bh.astype(F32)).astype(BF16)
    return _dot(ah, bh) + _dot(ah, bl) + _dot(al, bh)


def _ssm_prep_pair(pp, lrow_re_ref, lrow_im_ref, lcol_re_ref, lcol_im_ref, lstep_ref,
                   bt_re_ref, bt_im_ref, cx_re_ref, cx_im_ref,
                   t_ref, pre_ref, pim_ref, p4re_ref, p4im_ref, qre_ref, qim_ref,
                   apr_ref, api_ref, lam4_ref):
    L, C, N = CHUNK, SSM_GROUP_CH, SSM_STATE
    w = L * C
    p_re, p_im, p4_re, p4_im, q_re, q_im, a_re, a_im, l4 = [], [], [], [], [], [], [], [], []
    for e in range(2):
        ge = 2 * pp + e
        dt = jnp.exp(lstep_ref[ge])

        def lam_bar(lr, li):
            mag = jnp.exp(lr * dt)
            ang = li * dt
            return mag * jnp.cos(ang), mag * jnp.sin(ang)

        lr, li = lrow_re_ref[ge], lrow_im_ref[ge]
        zr, zi = lam_bar(lr, li)
        den = lr * lr + li * li
        fr = ((zr - 1.0) * lr + zi * li) / den
        fi = (zi * lr - (zr - 1.0) * li) / den
        bbr, bbi = _cmul(fr, fi, bt_re_ref[ge], bt_im_ref[ge])
        pr, pi = [jnp.ones((1, N), F32)], [jnp.zeros((1, N), F32)]
        for _ in range(L):
            nr, ni = _cmul(pr[-1], pi[-1], zr, zi)
            pr.append(nr)
            pi.append(ni)
        blocks = [_cmul(bbr, bbi, pr[L - 1 - i], pi[L - 1 - i]) for i in range(L)]
        p_re.append(jnp.concatenate([b[0] for b in blocks], axis=0))
        p_im.append(jnp.concatenate([b[1] for b in blocks], axis=0))
        blocks4 = [_cmul(bbr, bbi, pr[DEC_SEQ - 1 - i], pi[DEC_SEQ - 1 - i]) for i in range(DEC_SEQ)]
        pad = jnp.zeros(((L - DEC_SEQ) * C, N), F32)
        p4_re.append(jnp.concatenate([b[0] for b in blocks4] + [pad], axis=0))
        p4_im.append(jnp.concatenate([b[1] for b in blocks4] + [pad], axis=0))
        sr, si = [pr[L]], [pi[L]]
        for _ in range(N_SCAN_STEPS - 1):
            nr, ni = _cmul(sr[-1], si[-1], sr[-1], si[-1])
            sr.append(nr)
            si.append(ni)
        a_re.append(jnp.concatenate(sr + [jnp.zeros((8 - N_SCAN_STEPS, N), F32)], axis=0))
        a_im.append(jnp.concatenate(si + [jnp.zeros((8 - N_SCAN_STEPS, N), F32)], axis=0))
        l4.append(jnp.concatenate([pr[DEC_SEQ], pi[DEC_SEQ], jnp.zeros((6, N), F32)], axis=0))

        cr, ci = lam_bar(lcol_re_ref[ge], lcol_im_ref[ge])
        kk = lax.broadcasted_iota(jnp.int32, (N, w), 1) // C
        er, ei = jnp.ones((N, w), F32), jnp.zeros((N, w), F32)
        sqr, sqi = cr, ci
        for bit in range(4):
            nr, ni = _cmul(er, ei, sqr, sqi)
            has = ((kk >> bit) & 1) == 1
            er, ei = jnp.where(has, nr, er), jnp.where(has, ni, ei)
            sqr, sqi = _cmul(sqr, sqi, sqr, sqi)
        ckr, cki = _cmul(cx_re_ref[ge], cx_im_ref[ge], er, ei)
        qr, qi = _cmul(ckr, cki, cr, ci)
        q_re.append(qr)
        q_im.append(-qi)
        v = _dot3(bbr, ckr) - _dot3(bbi, cki)
        lane = lax.broadcasted_iota(jnp.int32, (C, w), 1)
        rows = [v] + [jnp.where(lane >= C * i, pltpu.roll(v, C * i, axis=1), 0.0) for i in range(1, L)]
        t_ref[pp, e] = jnp.concatenate(rows, axis=0).astype(BF16)

    def diag_rows(m):
        z = jnp.zeros_like(m[0])
        return jnp.concatenate([jnp.concatenate([m[0], z], axis=1),
                                jnp.concatenate([z, m[1]], axis=1)], axis=0)

    pre_ref[pp] = diag_rows(p_re).astype(BF16)
    pim_ref[pp] = diag_rows(p_im).astype(BF16)
    p4re_ref[pp] = diag_rows(p4_re).astype(BF16)
    p4im_ref[pp] = diag_rows(p4_im).astype(BF16)
    qre_ref[pp] = diag_rows(q_re).astype(BF16)
    qim_ref[pp] = diag_rows(q_im).astype(BF16)
    apr_ref[pp] = jnp.concatenate(a_re, axis=1)
    api_ref[pp] = jnp.concatenate(a_im, axis=1)
    lam4_ref[pp] = jnp.concatenate(l4, axis=1)


def _ssm_prep_args(lambda_re, lambda_im, log_step, b_re, b_im, c_re, c_im):
    cexp = lambda c: jnp.tile(c.transpose(0, 2, 1), (1, 1, CHUNK))
    return (lambda_re[:, None, :], lambda_im[:, None, :], lambda_re[:, :, None], lambda_im[:, :, None],
            log_step[:, None, None], b_re.transpose(0, 2, 1), b_im.transpose(0, 2, 1), cexp(c_re), cexp(c_im))


def _mix_kernel(a_ref, z_ref, xp_ref, xs_ref, wglu_ref, bglu_ref, wout_ref, g_ref, b_ref, *refs):
    n_cast = (len(refs) - 1) // 2
    o_ref = refs[n_cast]
    _cast_slices(refs[:n_cast], refs[n_cast + 1:])
    is_prompt = pl.program_id(0) < NBP
    for r in range(TM // SUB):
        rows = slice(r * SUB, (r + 1) * SUB)
        z = z_ref[rows, :]
        gate = _dot(z.astype(BF16), wglu_ref[...]) + bglu_ref[...]
        bmix = (z * jax.nn.sigmoid(gate)).astype(BF16)
        mix = _dot(a_ref[rows, :], wout_ref[0:D_POOL, :]) + _dot(bmix, wout_ref[D_POOL:, :])
        x = jnp.where(is_prompt, xp_ref[rows, :], xs_ref[rows, :])
        o_ref[rows, :] = _layer_norm(ALPHA * x + mix, g_ref[...], b_ref[...])


def _mix(a, z, xp, xs, w_glu, b_glu, w_out, g, b, to_cast):
    row = lambda w: pl.BlockSpec((TM, w), lambda i: (i, 0))
    cast_specs, cast_shapes = _cast_specs(to_cast, NBP)
    outs = pl.pallas_call(
        _mix_kernel,
        out_shape=(jax.ShapeDtypeStruct((R, D_MODEL), F32), *cast_shapes),
        grid=(NB,),
        in_specs=[
            row(D_POOL), row(D_SSM),
            pl.BlockSpec((TM, D_MODEL), lambda i: (jnp.minimum(i, NBP - 1), 0)),
            _const_spec((RS, D_MODEL)),
            _const_spec((D_SSM, D_SSM)), _const_spec((1, D_SSM)),
            _const_spec((D_MODEL, D_MODEL)), _const_spec((1, D_MODEL)), _const_spec((1, D_MODEL)),
            *cast_specs,
        ],
        out_specs=(row(D_MODEL), *cast_specs),
        compiler_params=_cparams(("arbitrary",)),
        name="mix_ln1",
    )(a, z, xp, xs, w_glu, b_glu, w_out, g, b, *to_cast)
    return outs[0], outs[1:]


def _proj_kernel(x_ref, w_ref, o_ref):
    o_ref[...] = _dot(x_ref[...].astype(BF16), w_ref[...]).astype(o_ref.dtype)


def _qproj_sample(h, w):
    return pl.pallas_call(
        _proj_kernel,
        out_shape=jax.ShapeDtypeStruct((RS, D_MODEL), BF16),
        grid=(1,),
        in_specs=[pl.BlockSpec((RS, D_MODEL), lambda i: (RP // RS, 0)),
                  pl.BlockSpec((D_MODEL, D_MODEL), lambda i: (0, 0))],
        out_specs=pl.BlockSpec((RS, D_MODEL), lambda i: (0, 0)),
        compiler_params=_cparams(("arbitrary",)),
        name="qproj_sample",
    )(h, w)


def _memkv_kernel(m_ref, wk_ref, wv_ref, k_ref, v_ref, kb_ref, vb_ref):
    mb = m_ref[...].astype(BF16)
    k = _dot(mb, wk_ref[...].astype(BF16))
    v = _dot(mb, wv_ref[...].astype(BF16))
    k_ref[...] = k
    v_ref[...] = v
    kb_ref[...] = k.astype(BF16)
    vb_ref[...] = v.astype(BF16)


def _memkv(mem, wk, wv):
    rows = BATCH * N_MEM
    tn = 512
    col = pl.BlockSpec((rows, tn), lambda j: (0, j))
    wcol = pl.BlockSpec((D_MODEL, tn), lambda j: (0, j))
    return pl.pallas_call(
        _memkv_kernel,
        out_shape=(jax.ShapeDtypeStruct((rows, D_MODEL), F32), jax.ShapeDtypeStruct((rows, D_MODEL), F32),
                   jax.ShapeDtypeStruct((rows, D_MODEL), BF16), jax.ShapeDtypeStruct((rows, D_MODEL), BF16)),
        grid=(D_MODEL // tn,),
        in_specs=[_const_spec((rows, D_MODEL)), wcol, wcol],
        out_specs=(col, col, col, col),
        compiler_params=_cparams(("arbitrary",)),
        name="memkv",
    )(mem, wk, wv)


def _attend(q, k, v):
    outs = []
    for h in range(N_XHEADS):
        sl = slice(h * XHEAD_DIM, (h + 1) * XHEAD_DIM)
        s = lax.dot_general(q[:, sl], k[:, sl], (((1,), (1,)), ((), ())),
                            preferred_element_type=F32) * (XHEAD_DIM ** -0.5)
        s = s - jnp.max(s, axis=-1, keepdims=True)
        e = jnp.exp(s)
        p = e / jnp.sum(e, axis=-1, keepdims=True)
        outs.append(_dot(p.astype(BF16), v[:, sl]))
    return jnp.concatenate(outs, axis=1)


def _attn_block_kernel(h_ref, k_ref, v_ref, os_ref, wq_ref, wo_ref, g_ref, b_ref, *refs):
    n_cast = (len(refs) - 1) // 2
    o_ref = refs[n_cast]
    _cast_slices(refs[:n_cast], refs[n_cast + 1:])
    i = pl.program_id(0)

    def finish(rows, h, o):
        y = _dot(o, wo_ref[...])
        o_ref[rows, :] = _layer_norm(ALPHA * h + y, g_ref[...], b_ref[...])

    @pl.when(i < NBP)
    def _():
        for r in range(TM // SUB):
            rows = slice(r * SUB, (r + 1) * SUB)
            h = h_ref[rows, :]
            q = _dot(h.astype(BF16), wq_ref[...]).astype(BF16)
            finish(rows, h, _attend(q, k_ref[...], v_ref[...]).astype(BF16))

    @pl.when(i == NBP)
    def _():
        for r in range(TM // SUB):
            rows = slice(r * SUB, (r + 1) * SUB)
            finish(rows, h_ref[rows, :], os_ref[rows, :])


def _attn_block(h, kb, vb, o_sample, wq, wo, g, b, to_cast):
    kv = pl.BlockSpec((N_MEM, D_MODEL), lambda i: (jnp.minimum(i // BLOCKS_PER_SEQ, BATCH - 1), 0))
    row = pl.BlockSpec((TM, D_MODEL), lambda i: (i, 0))
    cast_specs, cast_shapes = _cast_specs(to_cast, NBP)
    outs = pl.pallas_call(
        _attn_block_kernel,
        out_shape=(jax.ShapeDtypeStruct((R, D_MODEL), F32), *cast_shapes),
        grid=(NB,),
        in_specs=[row, kv, kv, _const_spec((RS, D_MODEL)),
                  _const_spec((D_MODEL, D_MODEL)), _const_spec((D_MODEL, D_MODEL)),
                  _const_spec((1, D_MODEL)), _const_spec((1, D_MODEL)), *cast_specs],
        out_specs=(row, *cast_specs),
        compiler_params=_cparams(("arbitrary",)),
        name="attn_block",
    )(h, kb, vb, o_sample, wq, wo, g, b, *to_cast)
    return outs[0], outs[1:]


ATT_BB = 4
Q_PAD = 8
KV_ROWS = N_MEM * N_XHEADS


def _attn_sample_kernel(q_ref, k_ref, v_ref, o_ref):
    shape = (N_XHEADS * Q_PAD, KV_ROWS)
    same_head = (lax.broadcasted_iota(jnp.int32, shape, 0) // Q_PAD
                 == lax.broadcasted_iota(jnp.int32, shape, 1) % N_XHEADS)
    for b in range(ATT_BB):
        q = q_ref[b].astype(F32)
        qs = jnp.concatenate([q[:, h * XHEAD_DIM:(h + 1) * XHEAD_DIM] for h in range(N_XHEADS)], axis=0)
        k = k_ref[0, b].reshape(KV_ROWS, XHEAD_DIM).astype(BF16)
        v = v_ref[0, b].reshape(KV_ROWS, XHEAD_DIM).astype(BF16)
        s = lax.dot_general(qs.astype(BF16), k, (((1,), (1,)), ((), ())),
                            preferred_element_type=F32) * (XHEAD_DIM ** -0.5)
        s = jnp.where(same_head, s, -1e30)
        s = s - jnp.max(s, axis=-1, keepdims=True)
        e = jnp.exp(s)
        p = e / jnp.sum(e, axis=-1, keepdims=True)
        o = _dot(p.astype(BF16), v)
        for h in range(N_XHEADS):
            o_ref[b, :, h * XHEAD_DIM:(h + 1) * XHEAD_DIM] = o[h * Q_PAD:(h + 1) * Q_PAD]


def _attn_sample(q_pad, k, v):
    kv = pl.BlockSpec((1, ATT_BB, N_MEM, N_XHEADS, XHEAD_DIM), lambda i: (0, i, 0, 0, 0))
    qo = pl.BlockSpec((ATT_BB, Q_PAD, D_MODEL), lambda i: (i, 0, 0))
    return pl.pallas_call(
        _attn_sample_kernel,
        out_shape=jax.ShapeDtypeStruct((DEC_BATCH, Q_PAD, D_MODEL), F32),
        grid=(DEC_BATCH // ATT_BB,),
        in_specs=[qo, kv, kv],
        out_specs=qo,
        compiler_params=_cparams(("arbitrary",)),
        name="attn_sample",
    )(q_pad, k, v)


FTM = 1024
FNB = RP // FTM
FBLOCKS_PER_SEQ = SEQ // FTM
FSUB = 256


def _ffn_rows(n_rows, conv_taps, h_ref, hb_scr, g_scr, g_base, wg, wu, wd, cw_ref, cb_ref,
              lng_ref, lnb_ref, y_ref, first, last):
    for r in range(n_rows // FSUB):
        lo = r * FSUB
        if first:
            h = h_ref[lo:lo + FSUB, :]
            hb = h.astype(BF16)
            hb_scr[lo:lo + FSUB, :] = hb
        else:
            hb = hb_scr[lo:lo + FSUB, :]
        g = _dot(hb, wg)
        up = _dot(hb, wu)
        g_scr[g_base + lo:g_base + lo + FSUB, :] = g
        tap0, tap1 = conv_taps(lo)
        gc = cb_ref[...] + cw_ref[0:1, :] * tap0 + cw_ref[1:2, :] * tap1 + cw_ref[2:3, :] * g
        act = (jax.nn.silu(gc) * up).astype(BF16)
        y = _dot(act, wd) + (ALPHA * h if first else y_ref[lo:lo + FSUB, :])
        y_ref[lo:lo + FSUB, :] = _layer_norm(y, lng_ref[...], lnb_ref[...]) if last else y


def _ffn_chunk(f, run):
    @pl.when(f == 0)
    def _():
        run(True, False)

    @pl.when(jnp.logical_and(f > 0, f < NF - 1))
    def _():
        run(False, False)

    @pl.when(f == NF - 1)
    def _():
        run(False, True)


def _ffn_prompt_kernel(h_ref, wg_ref, wu_ref, wd_ref, cw_ref, cb_ref, lng_ref, lnb_ref,
                       y_ref, gtail_ref, hb_scr, g_scr, carry_scr):
    i = pl.program_id(0)
    f = pl.program_id(1)
    first_block = (i % FBLOCKS_PER_SEQ) == 0

    @pl.when(first_block)
    def _():
        g_scr[0:8, :] = jnp.zeros((8, TF), F32)

    @pl.when(jnp.logical_not(first_block))
    def _():
        g_scr[0:8, :] = carry_scr[f]

    taps = lambda lo: (g_scr[6 + lo:6 + lo + FSUB, :], g_scr[7 + lo:7 + lo + FSUB, :])
    _ffn_chunk(f, lambda first, last: _ffn_rows(
        FTM, taps, h_ref, hb_scr, g_scr, 8, wg_ref[...], wu_ref[...], wd_ref[...], cw_ref, cb_ref,
        lng_ref, lnb_ref, y_ref, first, last))
    tail = g_scr[FTM:FTM + 8, :]
    carry_scr[f] = tail
    gtail_ref[0] = tail


def _ffn_sample_kernel(h_ref, wg_ref, wu_ref, wd_ref, cw_ref, cb_ref, lng_ref, lnb_ref, st_ref,
                       y_ref, gnew_ref, wdb_ref, hb_scr, g_scr):
    f = pl.program_id(1)
    n_st = 2 * DEC_BATCH
    wdb_ref[...] = wd_ref[...].astype(BF16)
    g_scr[0:n_st, :] = st_ref[...]
    taps = lambda lo: (g_scr[lo:lo + FSUB, :], g_scr[DEC_BATCH + lo:DEC_BATCH + lo + FSUB, :])
    _ffn_chunk(f, lambda first, last: _ffn_rows(
        RS, taps, h_ref, hb_scr, g_scr, n_st, wg_ref[...], wu_ref[...], wdb_ref[...], cw_ref, cb_ref,
        lng_ref, lnb_ref, y_ref, first, last))
    gnew_ref[...] = g_scr[RS:RS + n_st, :]


def _ffn_specs(tm, row_block_of):
    return [
        pl.BlockSpec((tm, D_MODEL), lambda i, f: (row_block_of(i), 0)),
        pl.BlockSpec((D_MODEL, TF), lambda i, f: (0, f)),
        pl.BlockSpec((D_MODEL, TF), lambda i, f: (0, f)),
        pl.BlockSpec((TF, D_MODEL), lambda i, f: (f, 0)),
        pl.BlockSpec((CONV_W, TF), lambda i, f: (0, f)),
        pl.BlockSpec((1, TF), lambda i, f: (0, f)),
        pl.BlockSpec((1, D_MODEL), lambda i, f: (0, 0)),
        pl.BlockSpec((1, D_MODEL), lambda i, f: (0, 0)),
    ]


def _ffn_prompt(h, wg, wu, wd, cw, cb, lng, lnb):
    return pl.pallas_call(
        _ffn_prompt_kernel,
        out_shape=(jax.ShapeDtypeStruct((RP, D_MODEL), F32),
                   jax.ShapeDtypeStruct((FNB, 8, D_FF), F32)),
        grid=(FNB, NF),
        in_specs=_ffn_specs(FTM, lambda i: i),
        out_specs=(pl.BlockSpec((FTM, D_MODEL), lambda i, f: (i, 0)),
                   pl.BlockSpec((1, 8, TF), lambda i, f: (i, 0, f))),
        scratch_shapes=[pltpu.VMEM((FTM, D_MODEL), BF16),
                        pltpu.VMEM((FTM + 8, TF), F32),
                        pltpu.VMEM((NF, 8, TF), F32)],
        compiler_params=_cparams(("arbitrary", "arbitrary"), VMEM_LIMIT_FFN),
        name="ffn_prompt",
    )(h, wg, wu, wd, cw, cb, lng, lnb)


def _ffn_sample(h, wg, wu, wd, cw, cb, lng, lnb, conv_state):
    n_st = 2 * DEC_BATCH
    specs = _ffn_specs(RS, lambda i: RP // RS)
    return pl.pallas_call(
        _ffn_sample_kernel,
        out_shape=(jax.ShapeDtypeStruct((RS, D_MODEL), F32),
                   jax.ShapeDtypeStruct((n_st, D_FF), F32),
                   jax.ShapeDtypeStruct((D_FF, D_MODEL), BF16)),
        grid=(1, NF),
        in_specs=specs + [pl.BlockSpec((n_st, TF), lambda i, f: (0, f))],
        out_specs=(pl.BlockSpec((RS, D_MODEL), lambda i, f: (0, 0)),
                   pl.BlockSpec((n_st, TF), lambda i, f: (0, f)),
                   specs[3]),
        scratch_shapes=[pltpu.VMEM((RS, D_MODEL), BF16),
                        pltpu.VMEM((n_st + RS, TF), F32)],
        compiler_params=_cparams(("arbitrary", "arbitrary")),
        name="ffn_sample",
    )(h, wg, wu, wd, cw, cb, lng, lnb, conv_state)


def _state_to_pairs(s):
    return s.reshape(s.shape[0], N_PAIRS, 128).transpose(1, 0, 2)


def _state_from_pairs(s):
    return s.transpose(1, 0, 2).reshape(1, s.shape[1], N_SSM_GROUPS, SSM_STATE)


def kernel(x_prompt, x_sample, mem_prompt, state_pool, state_ssm_re, state_ssm_im, state_conv, cache_mem_k, cache_mem_v, w_in, w_pool, pool_scale, lambda_re, lambda_im, log_step, b_re, b_im, c_re, c_im, d_skip, w_glu, b_glu, w_out, ln1_g, ln1_b, w_q, w_k, w_v, w_o, ln2_g, ln2_b, w_gate, w_up, conv_w, conv_b, w_down, ln3_g, ln3_b):
    bf = lambda w: w[0].astype(BF16)
    row = lambda v: v[0].reshape(1, -1)
    xp = x_prompt.reshape(RP, D_MODEL)
    xs = x_sample.transpose(1, 0, 2).reshape(RS, D_MODEL)

    prep_args = _ssm_prep_args(lambda_re[0], lambda_im[0], log_step[0], b_re[0], b_im[0], c_re[0], c_im[0])
    (u_ssm, a, utail, us_pool), prep, (w_glu_b, w_out_b) = _front(
        xp, xs, bf(w_in), state_pool[0].transpose(1, 0, 2), bf(w_pool), row(pool_scale), prep_args,
        (w_glu[0], w_out[0]))
    prep["dskip"] = jnp.broadcast_to(d_skip[0].reshape(N_PAIRS, 2, 1, SSM_GROUP_CH),
                                     (N_PAIRS, 2, CHUNK, SSM_GROUP_CH)).reshape(N_PAIRS, 1, PAIR_W)
    (z, hp_re, hp_im, hs_re, hs_im), (w_q_b, w_o_b) = _ssm(
        u_ssm, prep, _state_to_pairs(state_ssm_re[0]), _state_to_pairs(state_ssm_im[0]), (w_q[0], w_o[0]))
    h1, (wg,) = _mix(a, z, xp, xs, w_glu_b, row(b_glu), w_out_b, row(ln1_g), row(ln1_b), (w_gate[0],))

    mk, mv, mkb, mvb = _memkv(mem_prompt.reshape(BATCH * N_MEM, D_MODEL), w_k[0], w_v[0])
    q_s = _qproj_sample(h1, w_q_b).reshape(DEC_SEQ, DEC_BATCH, D_MODEL).transpose(1, 0, 2)
    q_s = jnp.pad(q_s, ((0, 0), (0, Q_PAD - DEC_SEQ), (0, 0)))
    o_s = _attn_sample(q_s, cache_mem_k, cache_mem_v)
    o_s = o_s[:, :DEC_SEQ].transpose(1, 0, 2).reshape(RS, D_MODEL).astype(BF16)
    h2, (wu,) = _attn_block(h1, mkb, mvb, o_s, w_q_b, w_o_b, row(ln2_g), row(ln2_b), (w_up[0],))

    cw, cb = conv_w[0], row(conv_b)
    conv_st = state_conv[0].transpose(1, 0, 2).reshape(2 * DEC_BATCH, D_FF)
    y_s, g_new, wd = _ffn_sample(h2, wg, wu, w_down[0], cw, cb, row(ln3_g), row(ln3_b), conv_st)
    y_p, gtail = _ffn_prompt(h2, wg, wu, wd, cw, cb, row(ln3_g), row(ln3_b))

    y_prompt = y_p.reshape(BATCH, SEQ, D_MODEL)
    y_sample = y_s.reshape(DEC_SEQ, DEC_BATCH, D_MODEL).transpose(1, 0, 2)
    p_pool = utail[BLOCKS_PER_SEQ - 1:NBP:BLOCKS_PER_SEQ, 16 - POOL_BUF:][None]
    s_ext = jnp.concatenate([state_pool[0], us_pool.reshape(DEC_SEQ, DEC_BATCH, D_POOL).transpose(1, 0, 2)], axis=1)
    s_pool = s_ext[None, :, DEC_SEQ:]
    p_conv = gtail[FBLOCKS_PER_SEQ - 1::FBLOCKS_PER_SEQ, 6:8][None]
    s_conv = g_new.reshape(2, DEC_BATCH, D_FF).transpose(1, 0, 2)[None]
    shape_kv = (1, BATCH, N_MEM, N_XHEADS, XHEAD_DIM)
    return (y_prompt, y_sample,
            p_pool, _state_from_pairs(hp_re[:, :BATCH]), _state_from_pairs(hp_im[:, :BATCH]), p_conv,
            mk.reshape(shape_kv), mv.reshape(shape_kv),
            s_pool, _state_from_pairs(hs_re), _state_from_pairs(hs_im), s_conv)
```

```python
import functools
import math

import jax
import jax.numpy as jnp
from jax import lax
from jax.experimental import pallas as pl
from jax.experimental.pallas import tpu as pltpu

F32 = jnp.float32
BF16 = jnp.bfloat16

D_MODEL = 2048
BATCH = 4
SEQ = 2048
DEC_BATCH = 128
DEC_SEQ = 4
PAST_LEN = 16384
D_POOL = 1024
D_SSM = 1024
POOL_WINDOWS = (2, 4, 8, 16)
POOL_GROUP_DIM = 256
POOL_BUF = 15
SSM_GROUP_CH = 16
N_SSM_GROUPS = 64
SSM_STATE = 64
N_MEM = 256
N_XHEADS = 4
XHEAD_DIM = 512
D_FF = 5632
CONV_W = 3
ALPHA = 2.0 ** 0.25
LN_EPS = 1e-5

RP = BATCH * SEQ
RS = DEC_BATCH * DEC_SEQ
R = RP + RS
TM = 512
NB = R // TM
NBP = RP // TM
BLOCKS_PER_SEQ = SEQ // TM

CHUNK = 16
N_PAIRS = N_SSM_GROUPS // 2
PAIR_W = 2 * CHUNK * SSM_GROUP_CH
CHUNKS_PER_SEQ = SEQ // CHUNK
P_CHUNK_ROWS = BATCH * CHUNKS_PER_SEQ
SSM_ROWS = P_CHUNK_ROWS + DEC_BATCH
SCAN_PAD = 64
N_SCAN_STEPS = 7

SUB = 256
TF = 512
NF = D_FF // TF

VMEM_LIMIT = 56 * 1024 * 1024
VMEM_LIMIT_FFN = 60 * 1024 * 1024


def _cparams(sem, vmem_limit=VMEM_LIMIT):
    return pltpu.CompilerParams(dimension_semantics=sem, vmem_limit_bytes=vmem_limit)


def _const_spec(shape):
    n = len(shape)
    return pl.BlockSpec(shape, lambda *_: (0,) * n, pipeline_mode=pl.Buffered(1))


def _layer_norm(x, g, b):
    mu = jnp.mean(x, axis=-1, keepdims=True)
    xc = x - mu
    var = jnp.mean(xc * xc, axis=-1, keepdims=True)
    return xc * lax.rsqrt(var + LN_EPS) * g + b


def _dot(a, b):
    return jnp.dot(a, b, preferred_element_type=F32)


def _cast_spec(a, n_steps):
    return pl.BlockSpec((a.shape[0] // n_steps, a.shape[1]), lambda i: (jnp.minimum(i, n_steps - 1), 0))


def _cast_specs(arrays, n_steps):
    specs = [_cast_spec(a, n_steps) for a in arrays]
    return specs, [jax.ShapeDtypeStruct(a.shape, BF16) for a in arrays]


def _cast_slices(in_refs, out_refs):
    for src_ref, dst_ref in zip(in_refs, out_refs):
        dst_ref[...] = src_ref[...].astype(BF16)


N_PREP_IN, N_PREP_OUT = 9, 10
PREP_PAIRS = N_PAIRS // NBP


def _pool_mix(g, pooled, w_ref, scale_ref):
    sl = slice(g * POOL_GROUP_DIM, (g + 1) * POOL_GROUP_DIM)
    return (_dot(pooled.astype(BF16), w_ref[g]) * scale_ref[:, sl]).astype(BF16)


def _front_kernel(xp_ref, xs_ref, w_ref, st_ref, wp_ref, scale_ref, *refs):
    prep_in, refs = refs[:N_PREP_IN], refs[N_PREP_IN:]
    n_cast = (len(refs) - 4 - N_PREP_OUT - 1) // 2
    cast_in, refs = refs[:n_cast], refs[n_cast:]
    ussm_ref, a_ref, utail_ref, us_ref = refs[:4]
    prep_out, cast_out, ext_ref = refs[4:4 + N_PREP_OUT], refs[4 + N_PREP_OUT:-1], refs[-1]
    i = pl.program_id(0)

    x = jnp.where(i < NBP, xp_ref[...], xs_ref[...])
    u = _dot(x.astype(BF16), w_ref[...])
    ussm_ref[...] = u[:, D_POOL:]

    first = (i % BLOCKS_PER_SEQ) == 0
    ext_ref[0:16, :] = jnp.where(first, 0.0, ext_ref[TM:TM + 16, :])
    ext_ref[16:16 + TM, :] = u[:, :D_POOL]
    utail_ref[0] = ext_ref[TM:TM + 16, :]
    pos = (i % BLOCKS_PER_SEQ) * TM + lax.broadcasted_iota(jnp.int32, (TM, 1), 0)
    for g, w in enumerate(POOL_WINDOWS):
        sl = slice(g * POOL_GROUP_DIM, (g + 1) * POOL_GROUP_DIM)
        acc = ext_ref[16:16 + TM, sl]
        for k in range(1, w):
            acc = acc + ext_ref[16 - k:16 - k + TM, sl]
        cnt = jnp.minimum(pos + 1, w).astype(F32)
        a_ref[:, sl] = _pool_mix(g, acc / cnt - ext_ref[16:16 + TM, sl], wp_ref, scale_ref)

    for pp in range(PREP_PAIRS):
        _ssm_prep_pair(pp, *prep_in, *prep_out)
    _cast_slices(cast_in, cast_out)

    @pl.when(i == NBP)
    def _():
        us_ref[...] = ext_ref[16:16 + TM, :]
        tok = lambda t, sl: ext_ref[16 + t * DEC_BATCH:16 + (t + 1) * DEC_BATCH, sl]
        for j in range(DEC_SEQ):
            rows = slice(j * DEC_BATCH, (j + 1) * DEC_BATCH)
            for g, w in enumerate(POOL_WINDOWS):
                sl = slice(g * POOL_GROUP_DIM, (g + 1) * POOL_GROUP_DIM)
                acc = tok(j, sl)
                for k in range(1, w):
                    e = POOL_BUF + j - k
                    acc = acc + (tok(e - POOL_BUF, sl) if e >= POOL_BUF else st_ref[e, :, sl])
                cnt = float(min(PAST_LEN + j + 1, w))
                a_ref[rows, sl] = _pool_mix(g, acc / cnt - tok(j, sl), wp_ref, scale_ref)


def _front(xp, xs, w, state_t, w_pool, pool_scale, prep_args, to_cast):
    G, N, C, L = N_SSM_GROUPS, SSM_STATE, SSM_GROUP_CH, CHUNK
    wd = L * C
    step = lambda i: jnp.minimum(i, NBP - 1)
    g3 = lambda a, b: pl.BlockSpec((2 * PREP_PAIRS, a, b), lambda i: (step(i), 0, 0))
    p3 = lambda a, b: pl.BlockSpec((PREP_PAIRS, a, b), lambda i: (step(i), 0, 0))
    sds = jax.ShapeDtypeStruct
    cast_specs, cast_shapes = _cast_specs(to_cast, NBP)
    row = lambda wdt: pl.BlockSpec((TM, wdt), lambda i: (i, 0))
    outs = pl.pallas_call(
        _front_kernel,
        out_shape=(
            sds((R, D_SSM), F32), sds((R, D_POOL), BF16), sds((NB, 16, D_POOL), F32), sds((RS, D_POOL), F32),
            sds((N_PAIRS, 2, wd, wd), BF16),
            sds((N_PAIRS, PAIR_W, 128), BF16), sds((N_PAIRS, PAIR_W, 128), BF16),
            sds((N_PAIRS, PAIR_W, 128), BF16), sds((N_PAIRS, PAIR_W, 128), BF16),
            sds((N_PAIRS, 128, PAIR_W), BF16), sds((N_PAIRS, 128, PAIR_W), BF16),
            sds((N_PAIRS, 8, 128), F32), sds((N_PAIRS, 8, 128), F32), sds((N_PAIRS, 8, 128), F32),
            *cast_shapes,
        ),
        grid=(NB,),
        in_specs=[
            pl.BlockSpec((TM, D_MODEL), lambda i: (step(i), 0)),
            _const_spec((RS, D_MODEL)),
            _const_spec((D_MODEL, D_MODEL)),
            _const_spec((POOL_BUF, DEC_BATCH, D_POOL)),
            _const_spec((4, POOL_GROUP_DIM, POOL_GROUP_DIM)),
            _const_spec((1, D_POOL)),
            g3(1, N), g3(1, N), g3(N, 1), g3(N, 1), g3(1, 1), g3(C, N), g3(C, N), g3(N, wd), g3(N, wd),
            *cast_specs,
        ],
        out_specs=(
            row(D_SSM), row(D_POOL), pl.BlockSpec((1, 16, D_POOL), lambda i: (i, 0, 0)),
            pl.BlockSpec((RS, D_POOL), lambda i: (0, 0)),
            pl.BlockSpec((PREP_PAIRS, 2, wd, wd), lambda i: (step(i), 0, 0, 0)),
            p3(PAIR_W, 128), p3(PAIR_W, 128), p3(PAIR_W, 128), p3(PAIR_W, 128),
            p3(128, PAIR_W), p3(128, PAIR_W),
            p3(8, 128), p3(8, 128), p3(8, 128),
            *cast_specs,
        ),
        scratch_shapes=[pltpu.VMEM((TM + 16, D_POOL), F32)],
        compiler_params=_cparams(("arbitrary",)),
        name="front",
    )(xp, xs, w, state_t, w_pool, pool_scale, *prep_args, *to_cast)
    names = ("t", "pre", "pim", "p4re", "p4im", "qre", "qim", "apow_re", "apow_im", "lam4")
    return outs[:4], dict(zip(names, outs[4:4 + N_PREP_OUT])), outs[4 + N_PREP_OUT:]


SLAB_GROUPS = 128 // SSM_GROUP_CH
SLAB_PAIRS = SLAB_GROUPS // 2
N_SLABS = N_SSM_GROUPS // SLAB_GROUPS


def _ssm_pair(q, u, t_ref, pre_ref, pim_ref, p4re_ref, p4im_ref, qre_ref, qim_ref,
              apow_re_ref, apow_im_ref, lam4_ref, dskip_ref, h0re_ref, h0im_ref,
              hp_re_ref, hp_im_ref, hs_re_ref, hs_im_ref, hre_scr, him_scr):
    ub = u.astype(BF16)
    half = PAIR_W // 2
    y = jnp.concatenate([_dot(ub[:, :half], t_ref[q, 0]), _dot(ub[:, half:], t_ref[q, 1])], axis=1)

    ubp = ub[:P_CHUNK_ROWS]
    hre_scr[SCAN_PAD:SCAN_PAD + P_CHUNK_ROWS, :] = _dot(ubp, pre_ref[q])
    him_scr[SCAN_PAD:SCAN_PAD + P_CHUNK_ROWS, :] = _dot(ubp, pim_ref[q])
    kk = lax.broadcasted_iota(jnp.int32, (P_CHUNK_ROWS, 1), 0) % CHUNKS_PER_SEQ
    for s in range(N_SCAN_STEPS):
        d = 1 << s
        ar = apow_re_ref[q, s:s + 1, :]
        ai = apow_im_ref[q, s:s + 1, :]
        hr = hre_scr[SCAN_PAD:SCAN_PAD + P_CHUNK_ROWS, :]
        hi = him_scr[SCAN_PAD:SCAN_PAD + P_CHUNK_ROWS, :]
        pr = hre_scr[SCAN_PAD - d:SCAN_PAD - d + P_CHUNK_ROWS, :]
        pi = him_scr[SCAN_PAD - d:SCAN_PAD - d + P_CHUNK_ROWS, :]
        keep = kk >= d
        hre_scr[SCAN_PAD:SCAN_PAD + P_CHUNK_ROWS, :] = hr + jnp.where(keep, ar * pr - ai * pi, 0.0)
        him_scr[SCAN_PAD:SCAN_PAD + P_CHUNK_ROWS, :] = hi + jnp.where(keep, ar * pi + ai * pr, 0.0)
    hp_re_ref[q] = jnp.zeros((8, 128), F32)
    hp_im_ref[q] = jnp.zeros((8, 128), F32)
    for b in range(BATCH):
        last = SCAN_PAD + (b + 1) * CHUNKS_PER_SEQ - 1
        hp_re_ref[q, b:b + 1, :] = hre_scr[last:last + 1, :]
        hp_im_ref[q, b:b + 1, :] = him_scr[last:last + 1, :]
    prev_ok = kk >= 1
    hprev_re = jnp.where(prev_ok, hre_scr[SCAN_PAD - 1:SCAN_PAD - 1 + P_CHUNK_ROWS, :], 0.0)
    hprev_im = jnp.where(prev_ok, him_scr[SCAN_PAD - 1:SCAN_PAD - 1 + P_CHUNK_ROWS, :], 0.0)
    carry_p = _dot(hprev_re.astype(BF16), qre_ref[q]) + _dot(hprev_im.astype(BF16), qim_ref[q])

    ubs = ub[P_CHUNK_ROWS:]
    h0r = h0re_ref[q]
    h0i = h0im_ref[q]
    l4r = lam4_ref[q, 0:1, :]
    l4i = lam4_ref[q, 1:2, :]
    hs_re_ref[q] = l4r * h0r - l4i * h0i + _dot(ubs, p4re_ref[q])
    hs_im_ref[q] = l4r * h0i + l4i * h0r + _dot(ubs, p4im_ref[q])
    carry_s = _dot(h0r.astype(BF16), qre_ref[q]) + _dot(h0i.astype(BF16), qim_ref[q])

    y = y + jnp.concatenate([carry_p, carry_s], axis=0) + dskip_ref[q] * u
    return jax.nn.gelu(y)


N_SSM_IN, N_SSM_OUT = 13, 5


def _ssm_kernel(u_ref, *all_refs):
    n_cast = (len(all_refs) - N_SSM_IN - N_SSM_OUT - 2) // 2
    _cast_slices(all_refs[N_SSM_IN:N_SSM_IN + n_cast], all_refs[N_SSM_IN + n_cast + N_SSM_OUT:-2])
    refs = all_refs[:N_SSM_IN] + all_refs[N_SSM_IN + n_cast:N_SSM_IN + n_cast + N_SSM_OUT] + all_refs[-2:]
    z_ref = refs[13]
    hre_scr, him_scr = refs[18], refs[19]
    c = SSM_GROUP_CH
    hre_scr[0:SCAN_PAD, :] = jnp.zeros((SCAN_PAD, 128), F32)
    him_scr[0:SCAN_PAD, :] = jnp.zeros((SCAN_PAD, 128), F32)
    xt = []
    for i in range(CHUNK):
        xp = u_ref[pl.ds(i, P_CHUNK_ROWS, stride=CHUNK), :]
        if i < DEC_SEQ:
            xs = u_ref[RP + i * DEC_BATCH:RP + (i + 1) * DEC_BATCH, :]
        else:
            xs = jnp.zeros((DEC_BATCH, 128), F32)
        xt.append(jnp.concatenate([xp, xs], axis=0).T)
    zt = []
    for q in range(SLAB_PAIRS):
        halves = []
        for e in range(2):
            g = 2 * q + e
            bt = jnp.concatenate([xt[i][g * c:(g + 1) * c, :] for i in range(CHUNK)], axis=0)
            halves.append(bt.T)
        z = _ssm_pair(q, jnp.concatenate(halves, axis=1), *refs[:13], *refs[14:])
        zt.append(z[:, :PAIR_W // 2].T)
        zt.append(z[:, PAIR_W // 2:].T)
    for i in range(CHUNK):
        zi = jnp.concatenate([zt[g][i * c:(i + 1) * c, :] for g in range(SLAB_GROUPS)], axis=0).T
        z_ref[pl.ds(i, P_CHUNK_ROWS, stride=CHUNK), :] = zi[:P_CHUNK_ROWS]
        if i < DEC_SEQ:
            z_ref[RP + i * DEC_BATCH:RP + (i + 1) * DEC_BATCH, :] = zi[P_CHUNK_ROWS:]


def _ssm(u_ssm, prep, h0re, h0im, to_cast):
    sp = SLAB_PAIRS
    slab3 = lambda a, b: pl.BlockSpec((sp, a, b), lambda s: (s, 0, 0))
    cast_specs, cast_shapes = _cast_specs(to_cast, N_SLABS)
    outs = pl.pallas_call(
        _ssm_kernel,
        out_shape=(
            jax.ShapeDtypeStruct((R, D_SSM), F32),
            jax.ShapeDtypeStruct((N_PAIRS, 8, 128), F32),
            jax.ShapeDtypeStruct((N_PAIRS, 8, 128), F32),
            jax.ShapeDtypeStruct((N_PAIRS, DEC_BATCH, 128), F32),
            jax.ShapeDtypeStruct((N_PAIRS, DEC_BATCH, 128), F32),
            *cast_shapes,
        ),
        grid=(N_SLABS,),
        in_specs=[
            pl.BlockSpec((R, 128), lambda s: (0, s)),
            pl.BlockSpec((sp, 2, 256, 256), lambda s: (s, 0, 0, 0)),
            slab3(PAIR_W, 128), slab3(PAIR_W, 128), slab3(PAIR_W, 128), slab3(PAIR_W, 128),
            slab3(128, PAIR_W), slab3(128, PAIR_W),
            slab3(8, 128), slab3(8, 128), slab3(8, 128),
            slab3(1, PAIR_W),
            slab3(DEC_BATCH, 128), slab3(DEC_BATCH, 128),
            *cast_specs,
        ],
        out_specs=(
            pl.BlockSpec((R, 128), lambda s: (0, s)),
            slab3(8, 128), slab3(8, 128), slab3(DEC_BATCH, 128), slab3(DEC_BATCH, 128),
            *cast_specs,
        ),
        scratch_shapes=[pltpu.VMEM((SCAN_PAD + P_CHUNK_ROWS, 128), F32),
                        pltpu.VMEM((SCAN_PAD + P_CHUNK_ROWS, 128), F32)],
        compiler_params=_cparams(("arbitrary",)),
        name="ssm",
    )(u_ssm, prep["t"], prep["pre"], prep["pim"], prep["p4re"], prep["p4im"],
      prep["qre"], prep["qim"], prep["apow_re"], prep["apow_im"], prep["lam4"],
      prep["dskip"], h0re, h0im, *to_cast)
    return outs[:N_SSM_OUT], outs[N_SSM_OUT:]


def _cmul(ar, ai, br, bi):
    return ar * br - ai * bi, ar * bi + ai * br


def _dot3(a, b):
    ah = a.astype(BF16)
    bh = b.astype(BF16)
    al = (a - ah.astype(F32)).astype(BF16)
    bl = (b - bh.astype(F32)).astype(BF16)
    return _dot(ah, bh) + _dot(ah, bl) + _dot(al, bh)


def _ssm_prep_pair(pp, lrow_re_ref, lrow_im_ref, lcol_re_ref, lcol_im_ref, lstep_ref,
                   bt_re_ref, bt_im_ref, cx_re_ref, cx_im_ref,
                   t_ref, pre_ref, pim_ref, p4re_ref, p4im_ref, qre_ref, qim_ref,
                   apr_ref, api_ref, lam4_ref):
    L, C, N = CHUNK, SSM_GROUP_CH, SSM_STATE
    w = L * C
    p_re, p_im, p4_re, p4_im, q_re, q_im, a_re, a_im, l4 = [], [], [], [], [], [], [], [], []
    for e in range(2):
        ge = 2 * pp + e
        dt = jnp.exp(lstep_ref[ge])

        def lam_bar(lr, li):
            mag = jnp.exp(lr * dt)
            ang = li * dt
            return mag * jnp.cos(ang), mag * jnp.sin(ang)

        lr, li = lrow_re_ref[ge], lrow_im_ref[ge]
        zr, zi = lam_bar(lr, li)
        den = lr * lr + li * li
        fr = ((zr - 1.0) * lr + zi * li) / den
        fi = (zi * lr - (zr - 1.0) * li) / den
        bbr, bbi = _cmul(fr, fi, bt_re_ref[ge], bt_im_ref[ge])
        pr, pi = [jnp.ones((1, N), F32)], [jnp.zeros((1, N), F32)]
        for _ in range(L):
            nr, ni = _cmul(pr[-1], pi[-1], zr, zi)
            pr.append(nr)
            pi.append(ni)
        blocks = [_cmul(bbr, bbi, pr[L - 1 - i], pi[L - 1 - i]) for i in range(L)]
        p_re.append(jnp.concatenate([b[0] for b in blocks], axis=0))
        p_im.append(jnp.concatenate([b[1] for b in blocks], axis=0))
        blocks4 = [_cmul(bbr, bbi, pr[DEC_SEQ - 1 - i], pi[DEC_SEQ - 1 - i]) for i in range(DEC_SEQ)]
        pad = jnp.zeros(((L - DEC_SEQ) * C, N), F32)
        p4_re.append(jnp.concatenate([b[0] for b in blocks4] + [pad], axis=0))
        p4_im.append(jnp.concatenate([b[1] for b in blocks4] + [pad], axis=0))
        sr, si = [pr[L]], [pi[L]]
        for _ in range(N_SCAN_STEPS - 1):
            nr, ni = _cmul(sr[-1], si[-1], sr[-1], si[-1])
            sr.append(nr)
            si.append(ni)
        a_re.append(jnp.concatenate(sr + [jnp.zeros((8 - N_SCAN_STEPS, N), F32)], axis=0))
        a_im.append(jnp.concatenate(si + [jnp.zeros((8 - N_SCAN_STEPS, N), F32)], axis=0))
        l4.append(jnp.concatenate([pr[DEC_SEQ], pi[DEC_SEQ], jnp.zeros((6, N), F32)], axis=0))

        cr, ci = lam_bar(lcol_re_ref[ge], lcol_im_ref[ge])
        kk = lax.broadcasted_iota(jnp.int32, (N, w), 1) // C
        er, ei = jnp.ones((N, w), F32), jnp.zeros((N, w), F32)
        sqr, sqi = cr, ci
        for bit in range(4):
            nr, ni = _cmul(er, ei, sqr, sqi)
            has = ((kk >> bit) & 1) == 1
            er, ei = jnp.where(has, nr, er), jnp.where(has, ni, ei)
            sqr, sqi = _cmul(sqr, sqi, sqr, sqi)
        ckr, cki = _cmul(cx_re_ref[ge], cx_im_ref[ge], er, ei)
        qr, qi = _cmul(ckr, cki, cr, ci)
        q_re.append(qr)
        q_im.append(-qi)
        v = _dot3(bbr, ckr) - _dot3(bbi, cki)
        lane = lax.broadcasted_iota(jnp.int32, (C, w), 1)
        rows = [v] + [jnp.where(lane >= C * i, pltpu.roll(v, C * i, axis=1), 0.0) for i in range(1, L)]
        t_ref[pp, e] = jnp.concatenate(rows, axis=0).astype(BF16)

    def diag_rows(m):
        z = jnp.zeros_like(m[0])
        return jnp.concatenate([jnp.concatenate([m[0], z], axis=1),
                                jnp.concatenate([z, m[1]], axis=1)], axis=0)

    pre_ref[pp] = diag_rows(p_re).astype(BF16)
    pim_ref[pp] = diag_rows(p_im).astype(BF16)
    p4re_ref[pp] = diag_rows(p4_re).astype(BF16)
    p4im_ref[pp] = diag_rows(p4_im).astype(BF16)
    qre_ref[pp] = diag_rows(q_re).astype(BF16)
    qim_ref[pp] = diag_rows(q_im).astype(BF16)
    apr_ref[pp] = jnp.concatenate(a_re, axis=1)
    api_ref[pp] = jnp.concatenate(a_im, axis=1)
    lam4_ref[pp] = jnp.concatenate(l4, axis=1)


def _ssm_prep_args(lambda_re, lambda_im, log_step, b_re, b_im, c_re, c_im):
    cexp = lambda c: jnp.tile(c.transpose(0, 2, 1), (1, 1, CHUNK))
    return (lambda_re[:, None, :], lambda_im[:, None, :], lambda_re[:, :, None], lambda_im[:, :, None],
            log_step[:, None, None], b_re.transpose(0, 2, 1), b_im.transpose(0, 2, 1), cexp(c_re), cexp(c_im))


def _mix_kernel(a_ref, z_ref, xp_ref, xs_ref, wglu_ref, bglu_ref, wout_ref, g_ref, b_ref, *refs):
    n_cast = (len(refs) - 1) // 2
    o_ref = refs[n_cast]
    _cast_slices(refs[:n_cast], refs[n_cast + 1:])
    is_prompt = pl.program_id(0) < NBP
    for r in range(TM // SUB):
        rows = slice(r * SUB, (r + 1) * SUB)
        z = z_ref[rows, :]
        gate = _dot(z.astype(BF16), wglu_ref[...]) + bglu_ref[...]
        bmix = (z * jax.nn.sigmoid(gate)).astype(BF16)
        mix = _dot(a_ref[rows, :], wout_ref[0:D_POOL, :]) + _dot(bmix, wout_ref[D_POOL:, :])
        x = jnp.where(is_prompt, xp_ref[rows, :], xs_ref[rows, :])
        o_ref[rows, :] = _layer_norm(ALPHA * x + mix, g_ref[...], b_ref[...])


def _mix(a, z, xp, xs, w_glu, b_glu, w_out, g, b, to_cast):
    row = lambda w: pl.BlockSpec((TM, w), lambda i: (i, 0))
    cast_specs, cast_shapes = _cast_specs(to_cast, NBP)
    outs = pl.pallas_call(
        _mix_kernel,
        out_shape=(jax.ShapeDtypeStruct((R, D_MODEL), F32), *cast_shapes),
        grid=(NB,),
        in_specs=[
            row(D_POOL), row(D_SSM),
            pl.BlockSpec((TM, D_MODEL), lambda i: (jnp.minimum(i, NBP - 1), 0)),
            _const_spec((RS, D_MODEL)),
            _const_spec((D_SSM, D_SSM)), _const_spec((1, D_SSM)),
            _const_spec((D_MODEL, D_MODEL)), _const_spec((1, D_MODEL)), _const_spec((1, D_MODEL)),
            *cast_specs,
        ],
        out_specs=(row(D_MODEL), *cast_specs),
        compiler_params=_cparams(("arbitrary",)),
        name="mix_ln1",
    )(a, z, xp, xs, w_glu, b_glu, w_out, g, b, *to_cast)
    return outs[0], outs[1:]


def _proj_kernel(x_ref, w_ref, o_ref):
    o_ref[...] = _dot(x_ref[...].astype(BF16), w_ref[...]).astype(o_ref.dtype)


def _qproj_sample(h, w):
    return pl.pallas_call(
        _proj_kernel,
        out_shape=jax.ShapeDtypeStruct((RS, D_MODEL), BF16),
        grid=(1,),
        in_specs=[pl.BlockSpec((RS, D_MODEL), lambda i: (RP // RS, 0)),
                  pl.BlockSpec((D_MODEL, D_MODEL), lambda i: (0, 0))],
        out_specs=pl.BlockSpec((RS, D_MODEL), lambda i: (0, 0)),
        compiler_params=_cparams(("arbitrary",)),
        name="qproj_sample",
    )(h, w)


def _memkv_kernel(m_ref, wk_ref, wv_ref, k_ref, v_ref, kb_ref, vb_ref):
    mb = m_ref[...].astype(BF16)
    k = _dot(mb, wk_ref[...].astype(BF16))
    v = _dot(mb, wv_ref[...].astype(BF16))
    k_ref[...] = k
    v_ref[...] = v
    kb_ref[...] = k.astype(BF16)
    vb_ref[...] = v.astype(BF16)


def _memkv(mem, wk, wv):
    rows = BATCH * N_MEM
    tn = 512
    col = pl.BlockSpec((rows, tn), lambda j: (0, j))
    wcol = pl.BlockSpec((D_MODEL, tn), lambda j: (0, j))
    return pl.pallas_call(
        _memkv_kernel,
        out_shape=(jax.ShapeDtypeStruct((rows, D_MODEL), F32), jax.ShapeDtypeStruct((rows, D_MODEL), F32),
                   jax.ShapeDtypeStruct((rows, D_MODEL), BF16), jax.ShapeDtypeStruct((rows, D_MODEL), BF16)),
        grid=(D_MODEL // tn,),
        in_specs=[_const_spec((rows, D_MODEL)), wcol, wcol],
        out_specs=(col, col, col, col),
        compiler_params=_cparams(("arbitrary",)),
        name="memkv",
    )(mem, wk, wv)


def _attend(q, k, v):
    outs = []
    for h in range(N_XHEADS):
        sl = slice(h * XHEAD_DIM, (h + 1) * XHEAD_DIM)
        s = lax.dot_general(q[:, sl], k[:, sl], (((1,), (1,)), ((), ())),
                            preferred_element_type=F32) * (XHEAD_DIM ** -0.5)
        s = s - jnp.max(s, axis=-1, keepdims=True)
        e = jnp.exp(s)
        p = e / jnp.sum(e, axis=-1, keepdims=True)
        outs.append(_dot(p.astype(BF16), v[:, sl]))
    return jnp.concatenate(outs, axis=1)


def _attn_block_kernel(h_ref, q_ref, k_ref, v_ref, os_ref, wo_ref, g_ref, b_ref, *refs):
    n_cast = (len(refs) - 1) // 2
    o_ref = refs[n_cast]
    _cast_slices(refs[:n_cast], refs[n_cast + 1:])
    i = pl.program_id(0)

    def finish(rows, h, o):
        y = _dot(o, wo_ref[...])
        o_ref[rows, :] = _layer_norm(ALPHA * h + y, g_ref[...], b_ref[...])

    @pl.when(i < NBP)
    def _():
        for r in range(TM // SUB):
            rows = slice(r * SUB, (r + 1) * SUB)
            finish(rows, h_ref[rows, :], _attend(q_ref[rows, :], k_ref[...], v_ref[...]).astype(BF16))

    @pl.when(i == NBP)
    def _():
        for r in range(TM // SUB):
            rows = slice(r * SUB, (r + 1) * SUB)
            finish(rows, h_ref[rows, :], os_ref[rows, :])


def _attn_block(h, q_prompt, kb, vb, o_sample, wo, g, b, to_cast):
    kv = pl.BlockSpec((N_MEM, D_MODEL), lambda i: (jnp.minimum(i // BLOCKS_PER_SEQ, BATCH - 1), 0))
    row = pl.BlockSpec((TM, D_MODEL), lambda i: (i, 0))
    qrow = pl.BlockSpec((TM, D_MODEL), lambda i: (jnp.minimum(i, NBP - 1), 0))
    cast_specs, cast_shapes = _cast_specs(to_cast, NBP)
    outs = pl.pallas_call(
        _attn_block_kernel,
        out_shape=(jax.ShapeDtypeStruct((R, D_MODEL), F32), *cast_shapes),
        grid=(NB,),
        in_specs=[row, qrow, kv, kv, _const_spec((RS, D_MODEL)), _const_spec((D_MODEL, D_MODEL)),
                  _const_spec((1, D_MODEL)), _const_spec((1, D_MODEL)), *cast_specs],
        out_specs=(row, *cast_specs),
        compiler_params=_cparams(("arbitrary",)),
        name="attn_block",
    )(h, q_prompt, kb, vb, o_sample, wo, g, b, *to_cast)
    return outs[0], outs[1:]


ATT_BB = 4
Q_PAD = 8
KV_ROWS = N_MEM * N_XHEADS


QP_ROWS = RP // (DEC_BATCH // ATT_BB)


def _attn_sample_kernel(q_ref, k_ref, v_ref, h_ref, wq_ref, o_ref, qp_ref):
    qp_ref[...] = _dot(h_ref[...].astype(BF16), wq_ref[...]).astype(BF16)
    shape = (N_XHEADS * Q_PAD, KV_ROWS)
    same_head = (lax.broadcasted_iota(jnp.int32, shape, 0) // Q_PAD
                 == lax.broadcasted_iota(jnp.int32, shape, 1) % N_XHEADS)
    for b in range(ATT_BB):
        q = q_ref[b].astype(F32)
        qs = jnp.concatenate([q[:, h * XHEAD_DIM:(h + 1) * XHEAD_DIM] for h in range(N_XHEADS)], axis=0)
        k = k_ref[0, b].reshape(KV_ROWS, XHEAD_DIM).astype(BF16)
        v = v_ref[0, b].reshape(KV_ROWS, XHEAD_DIM).astype(BF16)
        s = lax.dot_general(qs.astype(BF16), k, (((1,), (1,)), ((), ())),
                            preferred_element_type=F32) * (XHEAD_DIM ** -0.5)
        s = jnp.where(same_head, s, -1e30)
        s = s - jnp.max(s, axis=-1, keepdims=True)
        e = jnp.exp(s)
        p = e / jnp.sum(e, axis=-1, keepdims=True)
        o = _dot(p.astype(BF16), v)
        for h in range(N_XHEADS):
            o_ref[b, :, h * XHEAD_DIM:(h + 1) * XHEAD_DIM] = o[h * Q_PAD:(h + 1) * Q_PAD]


def _attn_sample(q_pad, k, v, h, wq):
    kv = pl.BlockSpec((1, ATT_BB, N_MEM, N_XHEADS, XHEAD_DIM), lambda i: (0, i, 0, 0, 0))
    qo = pl.BlockSpec((ATT_BB, Q_PAD, D_MODEL), lambda i: (i, 0, 0))
    hq = pl.BlockSpec((QP_ROWS, D_MODEL), lambda i: (i, 0))
    return pl.pallas_call(
        _attn_sample_kernel,
        out_shape=(jax.ShapeDtypeStruct((DEC_BATCH, Q_PAD, D_MODEL), F32),
                   jax.ShapeDtypeStruct((RP, D_MODEL), BF16)),
        grid=(DEC_BATCH // ATT_BB,),
        in_specs=[qo, kv, kv, hq, _const_spec((D_MODEL, D_MODEL))],
        out_specs=(qo, hq),
        compiler_params=_cparams(("arbitrary",)),
        name="attn_sample",
    )(q_pad, k, v, h, wq)


FTM = 1024
FNB = RP // FTM
FBLOCKS_PER_SEQ = SEQ // FTM
FSUB = 256


def _ffn_rows(n_rows, conv_taps, h_ref, hb_scr, g_scr, g_base, wg, wu, wd, cw_ref, cb_ref,
              lng_ref, lnb_ref, y_ref, first, last):
    for r in range(n_rows // FSUB):
        lo = r * FSUB
        if first:
            h = h_ref[lo:lo + FSUB, :]
            hb = h.astype(BF16)
            hb_scr[lo:lo + FSUB, :] = hb
        else:
            hb = hb_scr[lo:lo + FSUB, :]
        g = _dot(hb, wg)
        up = _dot(hb, wu)
        g_scr[g_base + lo:g_base + lo + FSUB, :] = g
        tap0, tap1 = conv_taps(lo)
        gc = cb_ref[...] + cw_ref[0:1, :] * tap0 + cw_ref[1:2, :] * tap1 + cw_ref[2:3, :] * g
        act = (jax.nn.silu(gc) * up).astype(BF16)
        y = _dot(act, wd) + (ALPHA * h if first else y_ref[lo:lo + FSUB, :])
        y_ref[lo:lo + FSUB, :] = _layer_norm(y, lng_ref[...], lnb_ref[...]) if last else y


def _ffn_chunk(f, run):
    @pl.when(f == 0)
    def _():
        run(True, False)

    @pl.when(jnp.logical_and(f > 0, f < NF - 1))
    def _():
        run(False, False)

    @pl.when(f == NF - 1)
    def _():
        run(False, True)


def _ffn_prompt_kernel(h_ref, wg_ref, wu_ref, wd_ref, cw_ref, cb_ref, lng_ref, lnb_ref,
                       y_ref, gtail_ref, hb_scr, g_scr, carry_scr):
    i = pl.program_id(0)
    f = pl.program_id(1)
    first_block = (i % FBLOCKS_PER_SEQ) == 0

    @pl.when(first_block)
    def _():
        g_scr[0:8, :] = jnp.zeros((8, TF), F32)

    @pl.when(jnp.logical_not(first_block))
    def _():
        g_scr[0:8, :] = carry_scr[f]

    taps = lambda lo: (g_scr[6 + lo:6 + lo + FSUB, :], g_scr[7 + lo:7 + lo + FSUB, :])
    _ffn_chunk(f, lambda first, last: _ffn_rows(
        FTM, taps, h_ref, hb_scr, g_scr, 8, wg_ref[...], wu_ref[...], wd_ref[...], cw_ref, cb_ref,
        lng_ref, lnb_ref, y_ref, first, last))
    tail = g_scr[FTM:FTM + 8, :]
    carry_scr[f] = tail
    gtail_ref[0] = tail


def _ffn_sample_kernel(h_ref, wg_ref, wu_ref, wd_ref, cw_ref, cb_ref, lng_ref, lnb_ref, st_ref,
                       y_ref, gnew_ref, wdb_ref, hb_scr, g_scr):
    f = pl.program_id(1)
    n_st = 2 * DEC_BATCH
    wdb_ref[...] = wd_ref[...].astype(BF16)
    g_scr[0:n_st, :] = st_ref[...]
    taps = lambda lo: (g_scr[lo:lo + FSUB, :], g_scr[DEC_BATCH + lo:DEC_BATCH + lo + FSUB, :])
    _ffn_chunk(f, lambda first, last: _ffn_rows(
        RS, taps, h_ref, hb_scr, g_scr, n_st, wg_ref[...], wu_ref[...], wdb_ref[...], cw_ref, cb_ref,
        lng_ref, lnb_ref, y_ref, first, last))
    gnew_ref[...] = g_scr[RS:RS + n_st, :]


def _ffn_specs(tm, row_block_of):
    return [
        pl.BlockSpec((tm, D_MODEL), lambda i, f: (row_block_of(i), 0)),
        pl.BlockSpec((D_MODEL, TF), lambda i, f: (0, f)),
        pl.BlockSpec((D_MODEL, TF), lambda i, f: (0, f)),
        pl.BlockSpec((TF, D_MODEL), lambda i, f: (f, 0)),
        pl.BlockSpec((CONV_W, TF), lambda i, f: (0, f)),
        pl.BlockSpec((1, TF), lambda i, f: (0, f)),
        pl.BlockSpec((1, D_MODEL), lambda i, f: (0, 0)),
        pl.BlockSpec((1, D_MODEL), lambda i, f: (0, 0)),
    ]


def _ffn_prompt(h, wg, wu, wd, cw, cb, lng, lnb):
    return pl.pallas_call(
        _ffn_prompt_kernel,
        out_shape=(jax.ShapeDtypeStruct((RP, D_MODEL), F32),
                   jax.ShapeDtypeStruct((FNB, 8, D_FF), F32)),
        grid=(FNB, NF),
        in_specs=_ffn_specs(FTM, lambda i: i),
        out_specs=(pl.BlockSpec((FTM, D_MODEL), lambda i, f: (i, 0)),
                   pl.BlockSpec((1, 8, TF), lambda i, f: (i, 0, f))),
        scratch_shapes=[pltpu.VMEM((FTM, D_MODEL), BF16),
                        pltpu.VMEM((FTM + 8, TF), F32),
                        pltpu.VMEM((NF, 8, TF), F32)],
        compiler_params=_cparams(("arbitrary", "arbitrary"), VMEM_LIMIT_FFN),
        name="ffn_prompt",
    )(h, wg, wu, wd, cw, cb, lng, lnb)


def _ffn_sample(h, wg, wu, wd, cw, cb, lng, lnb, conv_state):
    n_st = 2 * DEC_BATCH
    specs = _ffn_specs(RS, lambda i: RP // RS)
    return pl.pallas_call(
        _ffn_sample_kernel,
        out_shape=(jax.ShapeDtypeStruct((RS, D_MODEL), F32),
                   jax.ShapeDtypeStruct((n_st, D_FF), F32),
                   jax.ShapeDtypeStruct((D_FF, D_MODEL), BF16)),
        grid=(1, NF),
        in_specs=specs + [pl.BlockSpec((n_st, TF), lambda i, f: (0, f))],
        out_specs=(pl.BlockSpec((RS, D_MODEL), lambda i, f: (0, 0)),
                   pl.BlockSpec((n_st, TF), lambda i, f: (0, f)),
                   specs[3]),
        scratch_shapes=[pltpu.VMEM((RS, D_MODEL), BF16),
                        pltpu.VMEM((n_st + RS, TF), F32)],
        compiler_params=_cparams(("arbitrary", "arbitrary")),
        name="ffn_sample",
    )(h, wg, wu, wd, cw, cb, lng, lnb, conv_state)


def _state_to_pairs(s):
    return s.reshape(s.shape[0], N_PAIRS, 128).transpose(1, 0, 2)


def _state_from_pairs(s):
    return s.transpose(1, 0, 2).reshape(1, s.shape[1], N_SSM_GROUPS, SSM_STATE)


def kernel(x_prompt, x_sample, mem_prompt, state_pool, state_ssm_re, state_ssm_im, state_conv, cache_mem_k, cache_mem_v, w_in, w_pool, pool_scale, lambda_re, lambda_im, log_step, b_re, b_im, c_re, c_im, d_skip, w_glu, b_glu, w_out, ln1_g, ln1_b, w_q, w_k, w_v, w_o, ln2_g, ln2_b, w_gate, w_up, conv_w, conv_b, w_down, ln3_g, ln3_b):
    bf = lambda w: w[0].astype(BF16)
    row = lambda v: v[0].reshape(1, -1)
    xp = x_prompt.reshape(RP, D_MODEL)
    xs = x_sample.transpose(1, 0, 2).reshape(RS, D_MODEL)

    prep_args = _ssm_prep_args(lambda_re[0], lambda_im[0], log_step[0], b_re[0], b_im[0], c_re[0], c_im[0])
    (u_ssm, a, utail, us_pool), prep, (w_glu_b, w_out_b) = _front(
        xp, xs, bf(w_in), state_pool[0].transpose(1, 0, 2), bf(w_pool), row(pool_scale), prep_args,
        (w_glu[0], w_out[0]))
    prep["dskip"] = jnp.broadcast_to(d_skip[0].reshape(N_PAIRS, 2, 1, SSM_GROUP_CH),
                                     (N_PAIRS, 2, CHUNK, SSM_GROUP_CH)).reshape(N_PAIRS, 1, PAIR_W)
    (z, hp_re, hp_im, hs_re, hs_im), (w_q_b, w_o_b) = _ssm(
        u_ssm, prep, _state_to_pairs(state_ssm_re[0]), _state_to_pairs(state_ssm_im[0]), (w_q[0], w_o[0]))
    h1, (wg,) = _mix(a, z, xp, xs, w_glu_b, row(b_glu), w_out_b, row(ln1_g), row(ln1_b), (w_gate[0],))

    mk, mv, mkb, mvb = _memkv(mem_prompt.reshape(BATCH * N_MEM, D_MODEL), w_k[0], w_v[0])
    q_s = _qproj_sample(h1, w_q_b).reshape(DEC_SEQ, DEC_BATCH, D_MODEL).transpose(1, 0, 2)
    q_s = jnp.pad(q_s, ((0, 0), (0, Q_PAD - DEC_SEQ), (0, 0)))
    o_s, q_p = _attn_sample(q_s, cache_mem_k, cache_mem_v, h1, w_q_b)
    o_s = o_s[:, :DEC_SEQ].transpose(1, 0, 2).reshape(RS, D_MODEL).astype(BF16)
    h2, (wu,) = _attn_block(h1, q_p, mkb, mvb, o_s, w_o_b, row(ln2_g), row(ln2_b), (w_up[0],))

    cw, cb = conv_w[0], row(conv_b)
    conv_st = state_conv[0].transpose(1, 0, 2).reshape(2 * DEC_BATCH, D_FF)
    y_s, g_new, wd = _ffn_sample(h2, wg, wu, w_down[0], cw, cb, row(ln3_g), row(ln3_b), conv_st)
    y_p, gtail = _ffn_prompt(h2, wg, wu, wd, cw, cb, row(ln3_g), row(ln3_b))

    y_prompt = y_p.reshape(BATCH, SEQ, D_MODEL)
    y_sample = y_s.reshape(DEC_SEQ, DEC_BATCH, D_MODEL).transpose(1, 0, 2)
    p_pool = utail[BLOCKS_PER_SEQ - 1:NBP:BLOCKS_PER_SEQ, 16 - POOL_BUF:][None]
    s_ext = jnp.concatenate([state_pool[0], us_pool.reshape(DEC_SEQ, DEC_BATCH, D_POOL).transpose(1, 0, 2)], axis=1)
    s_pool = s_ext[None, :, DEC_SEQ:]
    p_conv = gtail[FBLOCKS_PER_SEQ - 1::FBLOCKS_PER_SEQ, 6:8][None]
    s_conv = g_new.reshape(2, DEC_BATCH, D_FF).transpose(1, 0, 2)[None]
    shape_kv = (1, BATCH, N_MEM, N_XHEADS, XHEAD_DIM)
    return (y_prompt, y_sample,
            p_pool, _state_from_pairs(hp_re[:, :BATCH]), _state_from_pairs(hp_im[:, :BATCH]), p_conv,
            mk.reshape(shape_kv), mv.reshape(shape_kv),
            s_pool, _state_from_pairs(hs_re), _state_from_pairs(hs_im), s_conv)
```

```python
import functools
import math

import jax
import jax.numpy as jnp
from jax import lax
from jax.experimental import pallas as pl
from jax.experimental.pallas import tpu as pltpu

F32 = jnp.float32
BF16 = jnp.bfloat16

D_MODEL = 2048
BATCH = 4
SEQ = 2048
DEC_BATCH = 128
DEC_SEQ = 4
PAST_LEN = 16384
D_POOL = 1024
D_SSM = 1024
POOL_WINDOWS = (2, 4, 8, 16)
POOL_GROUP_DIM = 256
POOL_BUF = 15
SSM_GROUP_CH = 16
N_SSM_GROUPS = 64
SSM_STATE = 64
N_MEM = 256
N_XHEADS = 4
XHEAD_DIM = 512
D_FF = 5632
CONV_W = 3
ALPHA = 2.0 ** 0.25
LN_EPS = 1e-5

RP = BATCH * SEQ
RS = DEC_BATCH * DEC_SEQ
R = RP + RS
TM = 512
NB = R // TM
NBP = RP // TM
BLOCKS_PER_SEQ = SEQ // TM

CHUNK = 16
N_PAIRS = N_SSM_GROUPS // 2
PAIR_W = 2 * CHUNK * SSM_GROUP_CH
CHUNKS_PER_SEQ = SEQ // CHUNK
P_CHUNK_ROWS = BATCH * CHUNKS_PER_SEQ
SSM_ROWS = P_CHUNK_ROWS + DEC_BATCH
SCAN_PAD = 64
N_SCAN_STEPS = 7

SUB = 256
TF = 512
NF = D_FF // TF

VMEM_LIMIT = 56 * 1024 * 1024
VMEM_LIMIT_FFN = 60 * 1024 * 1024


def _cparams(sem, vmem_limit=VMEM_LIMIT):
    return pltpu.CompilerParams(dimension_semantics=sem, vmem_limit_bytes=vmem_limit)


def _const_spec(shape):
    n = len(shape)
    return pl.BlockSpec(shape, lambda *_: (0,) * n, pipeline_mode=pl.Buffered(1))


def _layer_norm(x, g, b):
    mu = jnp.mean(x, axis=-1, keepdims=True)
    xc = x - mu
    var = jnp.mean(xc * xc, axis=-1, keepdims=True)
    return xc * lax.rsqrt(var + LN_EPS) * g + b


def _dot(a, b):
    return jnp.dot(a, b, preferred_element_type=F32)


def _cast_spec(a, n_steps):
    return pl.BlockSpec((a.shape[0] // n_steps, a.shape[1]), lambda i: (jnp.minimum(i, n_steps - 1), 0))


def _cast_specs(arrays, n_steps):
    specs = [_cast_spec(a, n_steps) for a in arrays]
    return specs, [jax.ShapeDtypeStruct(a.shape, BF16) for a in arrays]


def _cast_slices(in_refs, out_refs):
    for src_ref, dst_ref in zip(in_refs, out_refs):
        dst_ref[...] = src_ref[...].astype(BF16)


N_PREP_IN, N_PREP_OUT = 9, 10
PREP_PAIRS = N_PAIRS // NBP


def _pool_mix(g, pooled, w_ref, scale_ref):
    sl = slice(g * POOL_GROUP_DIM, (g + 1) * POOL_GROUP_DIM)
    return (_dot(pooled.astype(BF16), w_ref[g]) * scale_ref[:, sl]).astype(BF16)


def _front_kernel(xp_ref, xs_ref, w_ref, st_ref, wp_ref, scale_ref, *refs):
    prep_in, refs = refs[:N_PREP_IN], refs[N_PREP_IN:]
    n_cast = (len(refs) - 4 - N_PREP_OUT - 1) // 2
    cast_in, refs = refs[:n_cast], refs[n_cast:]
    ussm_ref, a_ref, utail_ref, us_ref = refs[:4]
    prep_out, cast_out, ext_ref = refs[4:4 + N_PREP_OUT], refs[4 + N_PREP_OUT:-1], refs[-1]
    i = pl.program_id(0)

    xb = jnp.where(i < NBP, xp_ref[...], xs_ref[...]).astype(BF16)

    first = (i % BLOCKS_PER_SEQ) == 0
    ext_ref[0:16, :] = jnp.where(first, 0.0, ext_ref[TM:TM + 16, :])
    ext_ref[16:16 + TM, :] = _dot(xb, w_ref[:, 0:D_POOL])
    utail_ref[0] = ext_ref[TM:TM + 16, :]
    pos = (i % BLOCKS_PER_SEQ) * TM + lax.broadcasted_iota(jnp.int32, (TM, 1), 0)
    qc = D_SSM // len(POOL_WINDOWS)
    for g, w in enumerate(POOL_WINDOWS):
        ussm_ref[:, g * qc:(g + 1) * qc] = _dot(xb, w_ref[:, D_POOL + g * qc:D_POOL + (g + 1) * qc])
        sl = slice(g * POOL_GROUP_DIM, (g + 1) * POOL_GROUP_DIM)
        acc = ext_ref[16:16 + TM, sl]
        for k in range(1, w):
            acc = acc + ext_ref[16 - k:16 - k + TM, sl]
        cnt = jnp.minimum(pos + 1, w).astype(F32)
        a_ref[:, sl] = _pool_mix(g, acc / cnt - ext_ref[16:16 + TM, sl], wp_ref, scale_ref)

    for pp in range(PREP_PAIRS):
        _ssm_prep_pair(pp, *prep_in, *prep_out)
    _cast_slices(cast_in, cast_out)

    @pl.when(i == NBP)
    def _():
        us_ref[...] = ext_ref[16:16 + TM, :]
        tok = lambda t, sl: ext_ref[16 + t * DEC_BATCH:16 + (t + 1) * DEC_BATCH, sl]
        for j in range(DEC_SEQ):
            rows = slice(j * DEC_BATCH, (j + 1) * DEC_BATCH)
            for g, w in enumerate(POOL_WINDOWS):
                sl = slice(g * POOL_GROUP_DIM, (g + 1) * POOL_GROUP_DIM)
                acc = tok(j, sl)
                for k in range(1, w):
                    e = POOL_BUF + j - k
                    acc = acc + (tok(e - POOL_BUF, sl) if e >= POOL_BUF else st_ref[e, :, sl])
                cnt = float(min(PAST_LEN + j + 1, w))
                a_ref[rows, sl] = _pool_mix(g, acc / cnt - tok(j, sl), wp_ref, scale_ref)


def _front(xp, xs, w, state_t, w_pool, pool_scale, prep_args, to_cast):
    G, N, C, L = N_SSM_GROUPS, SSM_STATE, SSM_GROUP_CH, CHUNK
    wd = L * C
    step = lambda i: jnp.minimum(i, NBP - 1)
    g3 = lambda a, b: pl.BlockSpec((2 * PREP_PAIRS, a, b), lambda i: (step(i), 0, 0))
    p3 = lambda a, b: pl.BlockSpec((PREP_PAIRS, a, b), lambda i: (step(i), 0, 0))
    sds = jax.ShapeDtypeStruct
    cast_specs, cast_shapes = _cast_specs(to_cast, NBP)
    row = lambda wdt: pl.BlockSpec((TM, wdt), lambda i: (i, 0))
    outs = pl.pallas_call(
        _front_kernel,
        out_shape=(
            sds((R, D_SSM), F32), sds((R, D_POOL), BF16), sds((NB, 16, D_POOL), F32), sds((RS, D_POOL), F32),
            sds((N_PAIRS, 2, wd, wd), BF16),
            sds((N_PAIRS, PAIR_W, 128), BF16), sds((N_PAIRS, PAIR_W, 128), BF16),
            sds((N_PAIRS, PAIR_W, 128), BF16), sds((N_PAIRS, PAIR_W, 128), BF16),
            sds((N_PAIRS, 128, PAIR_W), BF16), sds((N_PAIRS, 128, PAIR_W), BF16),
            sds((N_PAIRS, 8, 128), F32), sds((N_PAIRS, 8, 128), F32), sds((N_PAIRS, 8, 128), F32),
            *cast_shapes,
        ),
        grid=(NB,),
        in_specs=[
            pl.BlockSpec((TM, D_MODEL), lambda i: (step(i), 0)),
            _const_spec((RS, D_MODEL)),
            _const_spec((D_MODEL, D_MODEL)),
            _const_spec((POOL_BUF, DEC_BATCH, D_POOL)),
            _const_spec((4, POOL_GROUP_DIM, POOL_GROUP_DIM)),
            _const_spec((1, D_POOL)),
            g3(1, N), g3(1, N), g3(N, 1), g3(N, 1), g3(1, 1), g3(C, N), g3(C, N), g3(N, wd), g3(N, wd),
            *cast_specs,
        ],
        out_specs=(
            row(D_SSM), row(D_POOL), pl.BlockSpec((1, 16, D_POOL), lambda i: (i, 0, 0)),
            pl.BlockSpec((RS, D_POOL), lambda i: (0, 0)),
            pl.BlockSpec((PREP_PAIRS, 2, wd, wd), lambda i: (step(i), 0, 0, 0)),
            p3(PAIR_W, 128), p3(PAIR_W, 128), p3(PAIR_W, 128), p3(PAIR_W, 128),
            p3(128, PAIR_W), p3(128, PAIR_W),
            p3(8, 128), p3(8, 128), p3(8, 128),
            *cast_specs,
        ),
        scratch_shapes=[pltpu.VMEM((TM + 16, D_POOL), F32)],
        compiler_params=_cparams(("arbitrary",)),
        name="front",
    )(xp, xs, w, state_t, w_pool, pool_scale, *prep_args, *to_cast)
    names = ("t", "pre", "pim", "p4re", "p4im", "qre", "qim", "apow_re", "apow_im", "lam4")
    return outs[:4], dict(zip(names, outs[4:4 + N_PREP_OUT])), outs[4 + N_PREP_OUT:]


SLAB_GROUPS = 128 // SSM_GROUP_CH
SLAB_PAIRS = SLAB_GROUPS // 2
N_SLABS = N_SSM_GROUPS // SLAB_GROUPS


def _ssm_pair(q, u, t_ref, pre_ref, pim_ref, p4re_ref, p4im_ref, qre_ref, qim_ref,
              apow_re_ref, apow_im_ref, lam4_ref, dskip_ref, h0re_ref, h0im_ref,
              hp_re_ref, hp_im_ref, hs_re_ref, hs_im_ref, hre_scr, him_scr):
    ub = u.astype(BF16)
    half = PAIR_W // 2
    y = jnp.concatenate([_dot(ub[:, :half], t_ref[q, 0]), _dot(ub[:, half:], t_ref[q, 1])], axis=1)

    ubp = ub[:P_CHUNK_ROWS]
    hre_scr[SCAN_PAD:SCAN_PAD + P_CHUNK_ROWS, :] = _dot(ubp, pre_ref[q])
    him_scr[SCAN_PAD:SCAN_PAD + P_CHUNK_ROWS, :] = _dot(ubp, pim_ref[q])
    kk = lax.broadcasted_iota(jnp.int32, (P_CHUNK_ROWS, 1), 0) % CHUNKS_PER_SEQ
    for s in range(N_SCAN_STEPS):
        d = 1 << s
        ar = apow_re_ref[q, s:s + 1, :]
        ai = apow_im_ref[q, s:s + 1, :]
        hr = hre_scr[SCAN_PAD:SCAN_PAD + P_CHUNK_ROWS, :]
        hi = him_scr[SCAN_PAD:SCAN_PAD + P_CHUNK_ROWS, :]
        pr = hre_scr[SCAN_PAD - d:SCAN_PAD - d + P_CHUNK_ROWS, :]
        pi = him_scr[SCAN_PAD - d:SCAN_PAD - d + P_CHUNK_ROWS, :]
        keep = kk >= d
        hre_scr[SCAN_PAD:SCAN_PAD + P_CHUNK_ROWS, :] = hr + jnp.where(keep, ar * pr - ai * pi, 0.0)
        him_scr[SCAN_PAD:SCAN_PAD + P_CHUNK_ROWS, :] = hi + jnp.where(keep, ar * pi + ai * pr, 0.0)
    hp_re_ref[q] = jnp.zeros((8, 128), F32)
    hp_im_ref[q] = jnp.zeros((8, 128), F32)
    for b in range(BATCH):
        last = SCAN_PAD + (b + 1) * CHUNKS_PER_SEQ - 1
        hp_re_ref[q, b:b + 1, :] = hre_scr[last:last + 1, :]
        hp_im_ref[q, b:b + 1, :] = him_scr[last:last + 1, :]
    prev_ok = kk >= 1
    hprev_re = jnp.where(prev_ok, hre_scr[SCAN_PAD - 1:SCAN_PAD - 1 + P_CHUNK_ROWS, :], 0.0)
    hprev_im = jnp.where(prev_ok, him_scr[SCAN_PAD - 1:SCAN_PAD - 1 + P_CHUNK_ROWS, :], 0.0)
    carry_p = _dot(hprev_re.astype(BF16), qre_ref[q]) + _dot(hprev_im.astype(BF16), qim_ref[q])

    ubs = ub[P_CHUNK_ROWS:]
    h0r = h0re_ref[q]
    h0i = h0im_ref[q]
    l4r = lam4_ref[q, 0:1, :]
    l4i = lam4_ref[q, 1:2, :]
    hs_re_ref[q] = l4r * h0r - l4i * h0i + _dot(ubs, p4re_ref[q])
    hs_im_ref[q] = l4r * h0i + l4i * h0r + _dot(ubs, p4im_ref[q])
    carry_s = _dot(h0r.astype(BF16), qre_ref[q]) + _dot(h0i.astype(BF16), qim_ref[q])

    y = y + jnp.concatenate([carry_p, carry_s], axis=0) + dskip_ref[q] * u
    return jax.nn.gelu(y)


N_SSM_IN, N_SSM_OUT = 13, 5


def _ssm_kernel(u_ref, *all_refs):
    n_cast = (len(all_refs) - N_SSM_IN - N_SSM_OUT - 2) // 2
    _cast_slices(all_refs[N_SSM_IN:N_SSM_IN + n_cast], all_refs[N_SSM_IN + n_cast + N_SSM_OUT:-2])
    refs = all_refs[:N_SSM_IN] + all_refs[N_SSM_IN + n_cast:N_SSM_IN + n_cast + N_SSM_OUT] + all_refs[-2:]
    z_ref = refs[13]
    hre_scr, him_scr = refs[18], refs[19]
    c = SSM_GROUP_CH
    hre_scr[0:SCAN_PAD, :] = jnp.zeros((SCAN_PAD, 128), F32)
    him_scr[0:SCAN_PAD, :] = jnp.zeros((SCAN_PAD, 128), F32)
    xt = []
    for i in range(CHUNK):
        xp = u_ref[pl.ds(i, P_CHUNK_ROWS, stride=CHUNK), :]
        if i < DEC_SEQ:
            xs = u_ref[RP + i * DEC_BATCH:RP + (i + 1) * DEC_BATCH, :]
        else:
            xs = jnp.zeros((DEC_BATCH, 128), F32)
        xt.append(jnp.concatenate([xp, xs], axis=0).T)
    def pair_input(q):
        halves = []
        for e in range(2):
            g = 2 * q + e
            bt = jnp.concatenate([xt[i][g * c:(g + 1) * c, :] for i in range(CHUNK)], axis=0)
            halves.append(bt.T)
        return jnp.concatenate(halves, axis=1)

    zt = []
    u_pair = pair_input(0)
    for q in range(SLAB_PAIRS):
        u_next = pair_input(q + 1) if q + 1 < SLAB_PAIRS else None
        z = _ssm_pair(q, u_pair, *refs[:13], *refs[14:])
        zt.append(z[:, :PAIR_W // 2].T)
        zt.append(z[:, PAIR_W // 2:].T)
        u_pair = u_next
    for i in range(CHUNK):
        zi = jnp.concatenate([zt[g][i * c:(i + 1) * c, :] for g in range(SLAB_GROUPS)], axis=0).T
        z_ref[pl.ds(i, P_CHUNK_ROWS, stride=CHUNK), :] = zi[:P_CHUNK_ROWS]
        if i < DEC_SEQ:
            z_ref[RP + i * DEC_BATCH:RP + (i + 1) * DEC_BATCH, :] = zi[P_CHUNK_ROWS:]


def _ssm(u_ssm, prep, h0re, h0im, to_cast):
    sp = SLAB_PAIRS
    slab3 = lambda a, b: pl.BlockSpec((sp, a, b), lambda s: (s, 0, 0))
    cast_specs, cast_shapes = _cast_specs(to_cast, N_SLABS)
    outs = pl.pallas_call(
        _ssm_kernel,
        out_shape=(
            jax.ShapeDtypeStruct((R, D_SSM), F32),
            jax.ShapeDtypeStruct((N_PAIRS, 8, 128), F32),
            jax.ShapeDtypeStruct((N_PAIRS, 8, 128), F32),
            jax.ShapeDtypeStruct((N_PAIRS, DEC_BATCH, 128), F32),
            jax.ShapeDtypeStruct((N_PAIRS, DEC_BATCH, 128), F32),
            *cast_shapes,
        ),
        grid=(N_SLABS,),
        in_specs=[
            pl.BlockSpec((R, 128), lambda s: (0, s)),
            pl.BlockSpec((sp, 2, 256, 256), lambda s: (s, 0, 0, 0)),
            slab3(PAIR_W, 128), slab3(PAIR_W, 128), slab3(PAIR_W, 128), slab3(PAIR_W, 128),
            slab3(128, PAIR_W), slab3(128, PAIR_W),
            slab3(8, 128), slab3(8, 128), slab3(8, 128),
            slab3(1, PAIR_W),
            slab3(DEC_BATCH, 128), slab3(DEC_BATCH, 128),
            *cast_specs,
        ],
        out_specs=(
            pl.BlockSpec((R, 128), lambda s: (0, s)),
            slab3(8, 128), slab3(8, 128), slab3(DEC_BATCH, 128), slab3(DEC_BATCH, 128),
            *cast_specs,
        ),
        scratch_shapes=[pltpu.VMEM((SCAN_PAD + P_CHUNK_ROWS, 128), F32),
                        pltpu.VMEM((SCAN_PAD + P_CHUNK_ROWS, 128), F32)],
        compiler_params=_cparams(("arbitrary",)),
        name="ssm",
    )(u_ssm, prep["t"], prep["pre"], prep["pim"], prep["p4re"], prep["p4im"],
      prep["qre"], prep["qim"], prep["apow_re"], prep["apow_im"], prep["lam4"],
      prep["dskip"], h0re, h0im, *to_cast)
    return outs[:N_SSM_OUT], outs[N_SSM_OUT:]


def _cmul(ar, ai, br, bi):
    return ar * br - ai * bi, ar * bi + ai * br


def _dot3(a, b):
    ah = a.astype(BF16)
    bh = b.astype(BF16)
    al = (a - ah.astype(F32)).astype(BF16)
    bl = (b - bh.astype(F32)).astype(BF16)
    return _dot(ah, bh) + _dot(ah, bl) + _dot(al, bh)


def _ssm_prep_pair(pp, lrow_re_ref, lrow_im_ref, lcol_re_ref, lcol_im_ref, lstep_ref,
                   bt_re_ref, bt_im_ref, cx_re_ref, cx_im_ref,
                   t_ref, pre_ref, pim_ref, p4re_ref, p4im_ref, qre_ref, qim_ref,
                   apr_ref, api_ref, lam4_ref):
    L, C, N = CHUNK, SSM_GROUP_CH, SSM_STATE
    w = L * C
    p_re, p_im, p4_re, p4_im, q_re, q_im, a_re, a_im, l4 = [], [], [], [], [], [], [], [], []
    for e in range(2):
        ge = 2 * pp + e
        dt = jnp.exp(lstep_ref[ge])

        def lam_bar(lr, li):
            mag = jnp.exp(lr * dt)
            ang = li * dt
            return mag * jnp.cos(ang), mag * jnp.sin(ang)

        lr, li = lrow_re_ref[ge], lrow_im_ref[ge]
        zr, zi = lam_bar(lr, li)
        den = lr * lr + li * li
        fr = ((zr - 1.0) * lr + zi * li) / den
        fi = (zi * lr - (zr - 1.0) * li) / den
        bbr, bbi = _cmul(fr, fi, bt_re_ref[ge], bt_im_ref[ge])
        pr, pi = [jnp.ones((1, N), F32)], [jnp.zeros((1, N), F32)]
        for _ in range(L):
            nr, ni = _cmul(pr[-1], pi[-1], zr, zi)
            pr.append(nr)
            pi.append(ni)
        blocks = [_cmul(bbr, bbi, pr[L - 1 - i], pi[L - 1 - i]) for i in range(L)]
        p_re.append(jnp.concatenate([b[0] for b in blocks], axis=0))
        p_im.append(jnp.concatenate([b[1] for b in blocks], axis=0))
        blocks4 = [_cmul(bbr, bbi, pr[DEC_SEQ - 1 - i], pi[DEC_SEQ - 1 - i]) for i in range(DEC_SEQ)]
        pad = jnp.zeros(((L - DEC_SEQ) * C, N), F32)
        p4_re.append(jnp.concatenate([b[0] for b in blocks4] + [pad], axis=0))
        p4_im.append(jnp.concatenate([b[1] for b in blocks4] + [pad], axis=0))
        sr, si = [pr[L]], [pi[L]]
        for _ in range(N_SCAN_STEPS - 1):
            nr, ni = _cmul(sr[-1], si[-1], sr[-1], si[-1])
            sr.append(nr)
            si.append(ni)
        a_re.append(jnp.concatenate(sr + [jnp.zeros((8 - N_SCAN_STEPS, N), F32)], axis=0))
        a_im.append(jnp.concatenate(si + [jnp.zeros((8 - N_SCAN_STEPS, N), F32)], axis=0))
        l4.append(jnp.concatenate([pr[DEC_SEQ], pi[DEC_SEQ], jnp.zeros((6, N), F32)], axis=0))

        cr, ci = lam_bar(lcol_re_ref[ge], lcol_im_ref[ge])
        kk = lax.broadcasted_iota(jnp.int32, (N, w), 1) // C
        er, ei = jnp.ones((N, w), F32), jnp.zeros((N, w), F32)
        sqr, sqi = cr, ci
        for bit in range(4):
            nr, ni = _cmul(er, ei, sqr, sqi)
            has = ((kk >> bit) & 1) == 1
            er, ei = jnp.where(has, nr, er), jnp.where(has, ni, ei)
            sqr, sqi = _cmul(sqr, sqi, sqr, sqi)
        ckr, cki = _cmul(cx_re_ref[ge], cx_im_ref[ge], er, ei)
        qr, qi = _cmul(ckr, cki, cr, ci)
        q_re.append(qr)
        q_im.append(-qi)
        v = _dot3(bbr, ckr) - _dot3(bbi, cki)
        lane = lax.broadcasted_iota(jnp.int32, (C, w), 1)
        rows = [v] + [jnp.where(lane >= C * i, pltpu.roll(v, C * i, axis=1), 0.0) for i in range(1, L)]
        t_ref[pp, e] = jnp.concatenate(rows, axis=0).astype(BF16)

    def diag_rows(m):
        z = jnp.zeros_like(m[0])
        return jnp.concatenate([jnp.concatenate([m[0], z], axis=1),
                                jnp.concatenate([z, m[1]], axis=1)], axis=0)

    pre_ref[pp] = diag_rows(p_re).astype(BF16)
    pim_ref[pp] = diag_rows(p_im).astype(BF16)
    p4re_ref[pp] = diag_rows(p4_re).astype(BF16)
    p4im_ref[pp] = diag_rows(p4_im).astype(BF16)
    qre_ref[pp] = diag_rows(q_re).astype(BF16)
    qim_ref[pp] = diag_rows(q_im).astype(BF16)
    apr_ref[pp] = jnp.concatenate(a_re, axis=1)
    api_ref[pp] = jnp.concatenate(a_im, axis=1)
    lam4_ref[pp] = jnp.concatenate(l4, axis=1)


def _ssm_prep_args(lambda_re, lambda_im, log_step, b_re, b_im, c_re, c_im):
    cexp = lambda c: jnp.tile(c.transpose(0, 2, 1), (1, 1, CHUNK))
    return (lambda_re[:, None, :], lambda_im[:, None, :], lambda_re[:, :, None], lambda_im[:, :, None],
            log_step[:, None, None], b_re.transpose(0, 2, 1), b_im.transpose(0, 2, 1), cexp(c_re), cexp(c_im))


def _mix_kernel(a_ref, z_ref, xp_ref, xs_ref, wglu_ref, bglu_ref, wout_ref, g_ref, b_ref, *refs):
    n_cast = (len(refs) - 1) // 2
    o_ref = refs[n_cast]
    _cast_slices(refs[:n_cast], refs[n_cast + 1:])
    is_prompt = pl.program_id(0) < NBP
    n_sub = TM // SUB

    def glu(r):
        z = z_ref[r * SUB:(r + 1) * SUB, :]
        gate = _dot(z.astype(BF16), wglu_ref[...]) + bglu_ref[...]
        return (z * jax.nn.sigmoid(gate)).astype(BF16)

    bmix = glu(0)
    for r in range(n_sub):
        rows = slice(r * SUB, (r + 1) * SUB)
        bmix_next = glu(r + 1) if r + 1 < n_sub else None
        mix = _dot(a_ref[rows, :], wout_ref[0:D_POOL, :]) + _dot(bmix, wout_ref[D_POOL:, :])
        x = jnp.where(is_prompt, xp_ref[rows, :], xs_ref[rows, :])
        o_ref[rows, :] = _layer_norm(ALPHA * x + mix, g_ref[...], b_ref[...])
        bmix = bmix_next


def _mix(a, z, xp, xs, w_glu, b_glu, w_out, g, b, to_cast):
    row = lambda w: pl.BlockSpec((TM, w), lambda i: (i, 0))
    cast_specs, cast_shapes = _cast_specs(to_cast, NBP)
    outs = pl.pallas_call(
        _mix_kernel,
        out_shape=(jax.ShapeDtypeStruct((R, D_MODEL), F32), *cast_shapes),
        grid=(NB,),
        in_specs=[
            row(D_POOL), row(D_SSM),
            pl.BlockSpec((TM, D_MODEL), lambda i: (jnp.minimum(i, NBP - 1), 0)),
            _const_spec((RS, D_MODEL)),
            _const_spec((D_SSM, D_SSM)), _const_spec((1, D_SSM)),
            _const_spec((D_MODEL, D_MODEL)), _const_spec((1, D_MODEL)), _const_spec((1, D_MODEL)),
            *cast_specs,
        ],
        out_specs=(row(D_MODEL), *cast_specs),
        compiler_params=_cparams(("arbitrary",)),
        name="mix_ln1",
    )(a, z, xp, xs, w_glu, b_glu, w_out, g, b, *to_cast)
    return outs[0], outs[1:]


def _proj_kernel(x_ref, w_ref, o_ref):
    o_ref[...] = _dot(x_ref[...].astype(BF16), w_ref[...]).astype(o_ref.dtype)


def _qproj_sample(h, w):
    return pl.pallas_call(
        _proj_kernel,
        out_shape=jax.ShapeDtypeStruct((RS, D_MODEL), BF16),
        grid=(1,),
        in_specs=[pl.BlockSpec((RS, D_MODEL), lambda i: (RP // RS, 0)),
                  pl.BlockSpec((D_MODEL, D_MODEL), lambda i: (0, 0))],
        out_specs=pl.BlockSpec((RS, D_MODEL), lambda i: (0, 0)),
        compiler_params=_cparams(("arbitrary",)),
        name="qproj_sample",
    )(h, w)


def _memkv_kernel(m_ref, wk_ref, wv_ref, k_ref, v_ref, kb_ref, vb_ref):
    mb = m_ref[...].astype(BF16)
    k = _dot(mb, wk_ref[...].astype(BF16))
    v = _dot(mb, wv_ref[...].astype(BF16))
    k_ref[...] = k
    v_ref[...] = v
    kb_ref[...] = k.astype(BF16)
    vb_ref[...] = v.astype(BF16)


def _memkv(mem, wk, wv):
    rows = BATCH * N_MEM
    tn = 512
    col = pl.BlockSpec((rows, tn), lambda j: (0, j))
    wcol = pl.BlockSpec((D_MODEL, tn), lambda j: (0, j))
    return pl.pallas_call(
        _memkv_kernel,
        out_shape=(jax.ShapeDtypeStruct((rows, D_MODEL), F32), jax.ShapeDtypeStruct((rows, D_MODEL), F32),
                   jax.ShapeDtypeStruct((rows, D_MODEL), BF16), jax.ShapeDtypeStruct((rows, D_MODEL), BF16)),
        grid=(D_MODEL // tn,),
        in_specs=[_const_spec((rows, D_MODEL)), wcol, wcol],
        out_specs=(col, col, col, col),
        compiler_params=_cparams(("arbitrary",)),
        name="memkv",
    )(mem, wk, wv)


def _attend(q, k, v, between=None):
    def scores(h):
        sl = slice(h * XHEAD_DIM, (h + 1) * XHEAD_DIM)
        return lax.dot_general(q[:, sl], k[:, sl], (((1,), (1,)), ((), ())),
                               preferred_element_type=F32) * (XHEAD_DIM ** -0.5)

    def head_out(h, s):
        s = s - jnp.max(s, axis=-1, keepdims=True)
        e = jnp.exp(s)
        p = e / jnp.sum(e, axis=-1, keepdims=True)
        return _dot(p.astype(BF16), v[:, h * XHEAD_DIM:(h + 1) * XHEAD_DIM])

    outs = []
    s = scores(0)
    for h in range(N_XHEADS):
        if between is not None:
            between(h)
        s_next = scores(h + 1) if h + 1 < N_XHEADS else None
        outs.append(head_out(h, s))
        s = s_next
    return jnp.concatenate(outs, axis=1)


def _attn_block_kernel(h_ref, q_ref, k_ref, v_ref, os_ref, wo_ref, g_ref, b_ref, *refs):
    n_cast = (len(refs) - 1) // 2
    o_ref = refs[n_cast]
    _cast_slices(refs[:n_cast], refs[n_cast + 1:])
    i = pl.program_id(0)
    n_sub = TM // SUB

    def normed(rows, y):
        o_ref[rows, :] = _layer_norm(ALPHA * h_ref[rows, :] + y, g_ref[...], b_ref[...])

    @pl.when(i < NBP)
    def _():
        prev_rows, prev_o = None, None
        for r in range(n_sub):
            rows = slice(r * SUB, (r + 1) * SUB)
            ys = []
            between = None
            if prev_o is not None:
                between = lambda hh, po=prev_o: ys.append(
                    _dot(po, wo_ref[:, hh * XHEAD_DIM:(hh + 1) * XHEAD_DIM]))
            o = _attend(q_ref[rows, :], k_ref[...], v_ref[...], between).astype(BF16)
            if prev_o is not None:
                normed(prev_rows, jnp.concatenate(ys, axis=1))
            prev_rows, prev_o = rows, o
        normed(prev_rows, _dot(prev_o, wo_ref[...]))

    @pl.when(i == NBP)
    def _():
        for r in range(n_sub):
            rows = slice(r * SUB, (r + 1) * SUB)
            normed(rows, _dot(os_ref[rows, :], wo_ref[...]))


def _attn_block(h, q_prompt, kb, vb, o_sample, wo, g, b, to_cast):
    kv = pl.BlockSpec((N_MEM, D_MODEL), lambda i: (jnp.minimum(i // BLOCKS_PER_SEQ, BATCH - 1), 0))
    row = pl.BlockSpec((TM, D_MODEL), lambda i: (i, 0))
    qrow = pl.BlockSpec((TM, D_MODEL), lambda i: (jnp.minimum(i, NBP - 1), 0))
    cast_specs, cast_shapes = _cast_specs(to_cast, NBP)
    outs = pl.pallas_call(
        _attn_block_kernel,
        out_shape=(jax.ShapeDtypeStruct((R, D_MODEL), F32), *cast_shapes),
        grid=(NB,),
        in_specs=[row, qrow, kv, kv, _const_spec((RS, D_MODEL)), _const_spec((D_MODEL, D_MODEL)),
                  _const_spec((1, D_MODEL)), _const_spec((1, D_MODEL)), *cast_specs],
        out_specs=(row, *cast_specs),
        compiler_params=_cparams(("arbitrary",)),
        name="attn_block",
    )(h, q_prompt, kb, vb, o_sample, wo, g, b, *to_cast)
    return outs[0], outs[1:]


ATT_BB = 4
Q_PAD = 8
KV_ROWS = N_MEM * N_XHEADS


QP_ROWS = RP // (DEC_BATCH // ATT_BB)


def _attn_sample_kernel(q_ref, k_ref, v_ref, h_ref, wq_ref, o_ref, qp_ref):
    hb = h_ref[...].astype(BF16)
    qcols = D_MODEL // ATT_BB
    shape = (N_XHEADS * Q_PAD, KV_ROWS)
    same_head = (lax.broadcasted_iota(jnp.int32, shape, 0) // Q_PAD
                 == lax.broadcasted_iota(jnp.int32, shape, 1) % N_XHEADS)
    def scores(b):
        q = q_ref[b].astype(F32)
        qs = jnp.concatenate([q[:, h * XHEAD_DIM:(h + 1) * XHEAD_DIM] for h in range(N_XHEADS)], axis=0)
        k = k_ref[0, b].reshape(KV_ROWS, XHEAD_DIM).astype(BF16)
        return lax.dot_general(qs.astype(BF16), k, (((1,), (1,)), ((), ())),
                               preferred_element_type=F32) * (XHEAD_DIM ** -0.5)

    def attend(b, s):
        s = jnp.where(same_head, s, -1e30)
        s = s - jnp.max(s, axis=-1, keepdims=True)
        e = jnp.exp(s)
        p = e / jnp.sum(e, axis=-1, keepdims=True)
        v = v_ref[0, b].reshape(KV_ROWS, XHEAD_DIM).astype(BF16)
        o = _dot(p.astype(BF16), v)
        for h in range(N_XHEADS):
            o_ref[b, :, h * XHEAD_DIM:(h + 1) * XHEAD_DIM] = o[h * Q_PAD:(h + 1) * Q_PAD]

    s_prev = scores(0)
    for b in range(ATT_BB):
        cols = slice(b * qcols, (b + 1) * qcols)
        qp_ref[:, cols] = _dot(hb, wq_ref[:, cols]).astype(BF16)
        s_next = scores(b + 1) if b + 1 < ATT_BB else None
        attend(b, s_prev)
        s_prev = s_next


def _attn_sample(q_pad, k, v, h, wq):
    kv = pl.BlockSpec((1, ATT_BB, N_MEM, N_XHEADS, XHEAD_DIM), lambda i: (0, i, 0, 0, 0))
    qo = pl.BlockSpec((ATT_BB, Q_PAD, D_MODEL), lambda i: (i, 0, 0))
    hq = pl.BlockSpec((QP_ROWS, D_MODEL), lambda i: (i, 0))
    return pl.pallas_call(
        _attn_sample_kernel,
        out_shape=(jax.ShapeDtypeStruct((DEC_BATCH, Q_PAD, D_MODEL), F32),
                   jax.ShapeDtypeStruct((RP, D_MODEL), BF16)),
        grid=(DEC_BATCH // ATT_BB,),
        in_specs=[qo, kv, kv, hq, _const_spec((D_MODEL, D_MODEL))],
        out_specs=(qo, hq),
        compiler_params=_cparams(("arbitrary",)),
        name="attn_sample",
    )(q_pad, k, v, h, wq)


FTM = 1024
FNB = RP // FTM
FBLOCKS_PER_SEQ = SEQ // FTM
FSUB = 256


def _ffn_rows(n_rows, conv_taps, h_ref, hb_scr, g_scr, g_base, wg, wu, wd, cw_ref, cb_ref,
              lng_ref, lnb_ref, y_ref, first, last):
    def gate_up(lo):
        if first:
            h = h_ref[lo:lo + FSUB, :]
            hb = h.astype(BF16)
            hb_scr[lo:lo + FSUB, :] = hb
        else:
            h, hb = None, hb_scr[lo:lo + FSUB, :]
        g = _dot(hb, wg)
        up = _dot(hb, wu)
        g_scr[g_base + lo:g_base + lo + FSUB, :] = g
        tap0, tap1 = conv_taps(lo)
        gc = cb_ref[...] + cw_ref[0:1, :] * tap0 + cw_ref[1:2, :] * tap1 + cw_ref[2:3, :] * g
        return h, (jax.nn.silu(gc) * up).astype(BF16)

    def down(lo, h, act):
        y = _dot(act, wd) + (ALPHA * h if first else y_ref[lo:lo + FSUB, :])
        y_ref[lo:lo + FSUB, :] = _layer_norm(y, lng_ref[...], lnb_ref[...]) if last else y

    n_sub = n_rows // FSUB
    pending = gate_up(0)
    for r in range(n_sub):
        nxt = gate_up((r + 1) * FSUB) if r + 1 < n_sub else None
        down(r * FSUB, *pending)
        pending = nxt


def _ffn_chunk(f, run):
    @pl.when(f == 0)
    def _():
        run(True, False)

    @pl.when(jnp.logical_and(f > 0, f < NF - 1))
    def _():
        run(False, False)

    @pl.when(f == NF - 1)
    def _():
        run(False, True)


def _ffn_prompt_kernel(h_ref, wg_ref, wu_ref, wd_ref, cw_ref, cb_ref, lng_ref, lnb_ref,
                       y_ref, gtail_ref, hb_scr, g_scr, carry_scr):
    i = pl.program_id(0)
    f = pl.program_id(1)
    first_block = (i % FBLOCKS_PER_SEQ) == 0

    @pl.when(first_block)
    def _():
        g_scr[0:8, :] = jnp.zeros((8, TF), F32)

    @pl.when(jnp.logical_not(first_block))
    def _():
        g_scr[0:8, :] = carry_scr[f]

    taps = lambda lo: (g_scr[6 + lo:6 + lo + FSUB, :], g_scr[7 + lo:7 + lo + FSUB, :])
    _ffn_chunk(f, lambda first, last: _ffn_rows(
        FTM, taps, h_ref, hb_scr, g_scr, 8, wg_ref[...], wu_ref[...], wd_ref[...], cw_ref, cb_ref,
        lng_ref, lnb_ref, y_ref, first, last))
    tail = g_scr[FTM:FTM + 8, :]
    carry_scr[f] = tail
    gtail_ref[0] = tail


def _ffn_sample_kernel(h_ref, wg_ref, wu_ref, wd_ref, cw_ref, cb_ref, lng_ref, lnb_ref, st_ref,
                       y_ref, gnew_ref, wdb_ref, hb_scr, g_scr):
    f = pl.program_id(1)
    n_st = 2 * DEC_BATCH
    wdb_ref[...] = wd_ref[...].astype(BF16)
    g_scr[0:n_st, :] = st_ref[...]
    taps = lambda lo: (g_scr[lo:lo + FSUB, :], g_scr[DEC_BATCH + lo:DEC_BATCH + lo + FSUB, :])
    _ffn_chunk(f, lambda first, last: _ffn_rows(
        RS, taps, h_ref, hb_scr, g_scr, n_st, wg_ref[...], wu_ref[...], wdb_ref[...], cw_ref, cb_ref,
        lng_ref, lnb_ref, y_ref, first, last))
    gnew_ref[...] = g_scr[RS:RS + n_st, :]


def _ffn_specs(tm, row_block_of):
    return [
        pl.BlockSpec((tm, D_MODEL), lambda i, f: (row_block_of(i), 0)),
        pl.BlockSpec((D_MODEL, TF), lambda i, f: (0, f)),
        pl.BlockSpec((D_MODEL, TF), lambda i, f: (0, f)),
        pl.BlockSpec((TF, D_MODEL), lambda i, f: (f, 0)),
        pl.BlockSpec((CONV_W, TF), lambda i, f: (0, f)),
        pl.BlockSpec((1, TF), lambda i, f: (0, f)),
        pl.BlockSpec((1, D_MODEL), lambda i, f: (0, 0)),
        pl.BlockSpec((1, D_MODEL), lambda i, f: (0, 0)),
    ]


def _ffn_prompt(h, wg, wu, wd, cw, cb, lng, lnb):
    return pl.pallas_call(
        _ffn_prompt_kernel,
        out_shape=(jax.ShapeDtypeStruct((RP, D_MODEL), F32),
                   jax.ShapeDtypeStruct((FNB, 8, D_FF), F32)),
        grid=(FNB, NF),
        in_specs=_ffn_specs(FTM, lambda i: i),
        out_specs=(pl.BlockSpec((FTM, D_MODEL), lambda i, f: (i, 0)),
                   pl.BlockSpec((1, 8, TF), lambda i, f: (i, 0, f))),
        scratch_shapes=[pltpu.VMEM((FTM, D_MODEL), BF16),
                        pltpu.VMEM((FTM + 8, TF), F32),
                        pltpu.VMEM((NF, 8, TF), F32)],
        compiler_params=_cparams(("arbitrary", "arbitrary"), VMEM_LIMIT_FFN),
        name="ffn_prompt",
    )(h, wg, wu, wd, cw, cb, lng, lnb)


def _ffn_sample(h, wg, wu, wd, cw, cb, lng, lnb, conv_state):
    n_st = 2 * DEC_BATCH
    specs = _ffn_specs(RS, lambda i: RP // RS)
    return pl.pallas_call(
        _ffn_sample_kernel,
        out_shape=(jax.ShapeDtypeStruct((RS, D_MODEL), F32),
                   jax.ShapeDtypeStruct((n_st, D_FF), F32),
                   jax.ShapeDtypeStruct((D_FF, D_MODEL), BF16)),
        grid=(1, NF),
        in_specs=specs + [pl.BlockSpec((n_st, TF), lambda i, f: (0, f))],
        out_specs=(pl.BlockSpec((RS, D_MODEL), lambda i, f: (0, 0)),
                   pl.BlockSpec((n_st, TF), lambda i, f: (0, f)),
                   specs[3]),
        scratch_shapes=[pltpu.VMEM((RS, D_MODEL), BF16),
                        pltpu.VMEM((n_st + RS, TF), F32)],
        compiler_params=_cparams(("arbitrary", "arbitrary")),
        name="ffn_sample",
    )(h, wg, wu, wd, cw, cb, lng, lnb, conv_state)


def _state_to_pairs(s):
    return s.reshape(s.shape[0], N_PAIRS, 128).transpose(1, 0, 2)


def _state_from_pairs(s):
    return s.transpose(1, 0, 2).reshape(1, s.shape[1], N_SSM_GROUPS, SSM_STATE)


def kernel(x_prompt, x_sample, mem_prompt, state_pool, state_ssm_re, state_ssm_im, state_conv, cache_mem_k, cache_mem_v, w_in, w_pool, pool_scale, lambda_re, lambda_im, log_step, b_re, b_im, c_re, c_im, d_skip, w_glu, b_glu, w_out, ln1_g, ln1_b, w_q, w_k, w_v, w_o, ln2_g, ln2_b, w_gate, w_up, conv_w, conv_b, w_down, ln3_g, ln3_b):
    bf = lambda w: w[0].astype(BF16)
    row = lambda v: v[0].reshape(1, -1)
    xp = x_prompt.reshape(RP, D_MODEL)
    xs = x_sample.transpose(1, 0, 2).reshape(RS, D_MODEL)

    prep_args = _ssm_prep_args(lambda_re[0], lambda_im[0], log_step[0], b_re[0], b_im[0], c_re[0], c_im[0])
    (u_ssm, a, utail, us_pool), prep, (w_glu_b, w_out_b) = _front(
        xp, xs, bf(w_in), state_pool[0].transpose(1, 0, 2), bf(w_pool), row(pool_scale), prep_args,
        (w_glu[0], w_out[0]))
    prep["dskip"] = jnp.broadcast_to(d_skip[0].reshape(N_PAIRS, 2, 1, SSM_GROUP_CH),
                                     (N_PAIRS, 2, CHUNK, SSM_GROUP_CH)).reshape(N_PAIRS, 1, PAIR_W)
    (z, hp_re, hp_im, hs_re, hs_im), (w_q_b, w_o_b) = _ssm(
        u_ssm, prep, _state_to_pairs(state_ssm_re[0]), _state_to_pairs(state_ssm_im[0]), (w_q[0], w_o[0]))
    h1, (wg,) = _mix(a, z, xp, xs, w_glu_b, row(b_glu), w_out_b, row(ln1_g), row(ln1_b), (w_gate[0],))

    mk, mv, mkb, mvb = _memkv(mem_prompt.reshape(BATCH * N_MEM, D_MODEL), w_k[0], w_v[0])
    q_s = _qproj_sample(h1, w_q_b).reshape(DEC_SEQ, DEC_BATCH, D_MODEL).transpose(1, 0, 2)
    q_s = jnp.pad(q_s, ((0, 0), (0, Q_PAD - DEC_SEQ), (0, 0)))
    o_s, q_p = _attn_sample(q_s, cache_mem_k, cache_mem_v, h1, w_q_b)
    o_s = o_s[:, :DEC_SEQ].transpose(1, 0, 2).reshape(RS, D_MODEL).astype(BF16)
    h2, (wu,) = _attn_block(h1, q_p, mkb, mvb, o_s, w_o_b, row(ln2_g), row(ln2_b), (w_up[0],))

    cw, cb = conv_w[0], row(conv_b)
    conv_st = state_conv[0].transpose(1, 0, 2).reshape(2 * DEC_BATCH, D_FF)
    y_s, g_new, wd = _ffn_sample(h2, wg, wu, w_down[0], cw, cb, row(ln3_g), row(ln3_b), conv_st)
    y_p, gtail = _ffn_prompt(h2, wg, wu, wd, cw, cb, row(ln3_g), row(ln3_b))

    y_prompt = y_p.reshape(BATCH, SEQ, D_MODEL)
    y_sample = y_s.reshape(DEC_SEQ, DEC_BATCH, D_MODEL).transpose(1, 0, 2)
    p_pool = utail[BLOCKS_PER_SEQ - 1:NBP:BLOCKS_PER_SEQ, 16 - POOL_BUF:][None]
    s_ext = jnp.concatenate([state_pool[0], us_pool.reshape(DEC_SEQ, DEC_BATCH, D_POOL).transpose(1, 0, 2)], axis=1)
    s_pool = s_ext[None, :, DEC_SEQ:]
    p_conv = gtail[FBLOCKS_PER_SEQ - 1::FBLOCKS_PER_SEQ, 6:8][None]
    s_conv = g_new.reshape(2, DEC_BATCH, D_FF).transpose(1, 0, 2)[None]
    shape_kv = (1, BATCH, N_MEM, N_XHEADS, XHEAD_DIM)
    return (y_prompt, y_sample,
            p_pool, _state_from_pairs(hp_re[:, :BATCH]), _state_from_pairs(hp_im[:, :BATCH]), p_conv,
            mk.reshape(shape_kv), mv.reshape(shape_kv),
            s_pool, _state_from_pairs(hs_re), _state_from_pairs(hs_im), s_conv)
```

```python
import functools
import math

import jax
import jax.numpy as jnp
from jax import lax
from jax.experimental import pallas as pl
from jax.experimental.pallas import tpu as pltpu

F32 = jnp.float32
BF16 = jnp.bfloat16

D_MODEL = 2048
BATCH = 4
SEQ = 2048
DEC_BATCH = 128
DEC_SEQ = 4
PAST_LEN = 16384
D_POOL = 1024
D_SSM = 1024
POOL_WINDOWS = (2, 4, 8, 16)
POOL_GROUP_DIM = 256
POOL_BUF = 15
SSM_GROUP_CH = 16
N_SSM_GROUPS = 64
SSM_STATE = 64
N_MEM = 256
N_XHEADS = 4
XHEAD_DIM = 512
D_FF = 5632
CONV_W = 3
ALPHA = 2.0 ** 0.25
LN_EPS = 1e-5

RP = BATCH * SEQ
RS = DEC_BATCH * DEC_SEQ
R = RP + RS
TM = 512
NB = R // TM
NBP = RP // TM
BLOCKS_PER_SEQ = SEQ // TM

CHUNK = 16
N_PAIRS = N_SSM_GROUPS // 2
PAIR_W = 2 * CHUNK * SSM_GROUP_CH
CHUNKS_PER_SEQ = SEQ // CHUNK
P_CHUNK_ROWS = BATCH * CHUNKS_PER_SEQ
SSM_ROWS = P_CHUNK_ROWS + DEC_BATCH
SCAN_PAD = 64
N_SCAN_STEPS = 7

SUB = 256
TF = 512
NF = D_FF // TF

VMEM_LIMIT = 56 * 1024 * 1024
VMEM_LIMIT_FFN = 60 * 1024 * 1024


def _cparams(sem, vmem_limit=VMEM_LIMIT):
    return pltpu.CompilerParams(dimension_semantics=sem, vmem_limit_bytes=vmem_limit)


def _const_spec(shape):
    n = len(shape)
    return pl.BlockSpec(shape, lambda *_: (0,) * n, pipeline_mode=pl.Buffered(1))


def _layer_norm(x, g, b):
    mu = jnp.mean(x, axis=-1, keepdims=True)
    xc = x - mu
    var = jnp.mean(xc * xc, axis=-1, keepdims=True)
    return xc * lax.rsqrt(var + LN_EPS) * g + b


def _dot(a, b):
    return jnp.dot(a, b, preferred_element_type=F32)


def _dot_nt(a, b):
    return lax.dot_general(a, b, (((1,), (1,)), ((), ())), preferred_element_type=F32)


def _cast_spec(a, n_steps):
    return pl.BlockSpec((a.shape[0] // n_steps, a.shape[1]), lambda i: (jnp.minimum(i, n_steps - 1), 0))


def _cast_specs(arrays, n_steps):
    specs = [_cast_spec(a, n_steps) for a in arrays]
    return specs, [jax.ShapeDtypeStruct(a.shape, BF16) for a in arrays]


def _cast_slices(in_refs, out_refs):
    for src_ref, dst_ref in zip(in_refs, out_refs):
        dst_ref[...] = src_ref[...].astype(BF16)


N_PREP_IN, N_PREP_OUT = 7, 10
PREP_PAIRS = N_PAIRS // NBP


def _pool_mix(g, pooled, w_ref, scale_ref):
    sl = slice(g * POOL_GROUP_DIM, (g + 1) * POOL_GROUP_DIM)
    return (_dot(pooled.astype(BF16), w_ref[g]) * scale_ref[:, sl]).astype(BF16)


def _front_kernel(xp_ref, xs_ref, w_ref, st_ref, wp_ref, scale_ref, *refs):
    prep_in, refs = refs[:N_PREP_IN], refs[N_PREP_IN:]
    n_cast = (len(refs) - 4 - N_PREP_OUT - 1) // 2
    cast_in, refs = refs[:n_cast], refs[n_cast:]
    ussm_ref, a_ref, utail_ref, us_ref = refs[:4]
    prep_out, cast_out, ext_ref = refs[4:4 + N_PREP_OUT], refs[4 + N_PREP_OUT:-1], refs[-1]
    i = pl.program_id(0)

    xb = jnp.where(i < NBP, xp_ref[...], xs_ref[...]).astype(BF16)

    first = (i % BLOCKS_PER_SEQ) == 0
    ext_ref[0:16, :] = jnp.where(first, 0.0, ext_ref[TM:TM + 16, :])
    ext_ref[16:16 + TM, :] = _dot(xb, w_ref[:, 0:D_POOL])
    utail_ref[0] = ext_ref[TM:TM + 16, :]
    pos = (i % BLOCKS_PER_SEQ) * TM + lax.broadcasted_iota(jnp.int32, (TM, 1), 0)
    qc = D_SSM // len(POOL_WINDOWS)
    for g, w in enumerate(POOL_WINDOWS):
        ussm_ref[:, g * qc:(g + 1) * qc] = _dot(xb, w_ref[:, D_POOL + g * qc:D_POOL + (g + 1) * qc])
        sl = slice(g * POOL_GROUP_DIM, (g + 1) * POOL_GROUP_DIM)
        acc = ext_ref[16:16 + TM, sl]
        for k in range(1, w):
            acc = acc + ext_ref[16 - k:16 - k + TM, sl]
        cnt = jnp.minimum(pos + 1, w).astype(F32)
        a_ref[:, sl] = _pool_mix(g, acc / cnt - ext_ref[16:16 + TM, sl], wp_ref, scale_ref)

    for pp in range(PREP_PAIRS):
        _ssm_prep_pair(pp, *prep_in, *prep_out)
    _cast_slices(cast_in, cast_out)

    @pl.when(i == NBP)
    def _():
        us_ref[...] = ext_ref[16:16 + TM, :]
        tok = lambda t, sl: ext_ref[16 + t * DEC_BATCH:16 + (t + 1) * DEC_BATCH, sl]
        for j in range(DEC_SEQ):
            rows = slice(j * DEC_BATCH, (j + 1) * DEC_BATCH)
            for g, w in enumerate(POOL_WINDOWS):
                sl = slice(g * POOL_GROUP_DIM, (g + 1) * POOL_GROUP_DIM)
                acc = tok(j, sl)
                for k in range(1, w):
                    e = POOL_BUF + j - k
                    acc = acc + (tok(e - POOL_BUF, sl) if e >= POOL_BUF else st_ref[e, :, sl])
                cnt = float(min(PAST_LEN + j + 1, w))
                a_ref[rows, sl] = _pool_mix(g, acc / cnt - tok(j, sl), wp_ref, scale_ref)


def _front(xp, xs, w, state_t, w_pool, pool_scale, prep_args, to_cast):
    G, N, C, L = N_SSM_GROUPS, SSM_STATE, SSM_GROUP_CH, CHUNK
    wd = L * C
    step = lambda i: jnp.minimum(i, NBP - 1)
    g3 = lambda a, b: pl.BlockSpec((2 * PREP_PAIRS, a, b), lambda i: (step(i), 0, 0))
    p3 = lambda a, b: pl.BlockSpec((PREP_PAIRS, a, b), lambda i: (step(i), 0, 0))
    sds = jax.ShapeDtypeStruct
    cast_specs, cast_shapes = _cast_specs(to_cast, NBP)
    row = lambda wdt: pl.BlockSpec((TM, wdt), lambda i: (i, 0))
    outs = pl.pallas_call(
        _front_kernel,
        out_shape=(
            sds((R, D_SSM), F32), sds((R, D_POOL), BF16), sds((NB, 16, D_POOL), F32), sds((RS, D_POOL), F32),
            sds((N_PAIRS, 2, wd, wd), BF16),
            sds((N_PAIRS, PAIR_W, 128), BF16), sds((N_PAIRS, PAIR_W, 128), BF16),
            sds((N_PAIRS, PAIR_W, 128), BF16), sds((N_PAIRS, PAIR_W, 128), BF16),
            sds((N_PAIRS, PAIR_W, 128), BF16), sds((N_PAIRS, PAIR_W, 128), BF16),
            sds((N_PAIRS, 8, 128), F32), sds((N_PAIRS, 8, 128), F32), sds((N_PAIRS, 8, 128), F32),
            *cast_shapes,
        ),
        grid=(NB,),
        in_specs=[
            pl.BlockSpec((TM, D_MODEL), lambda i: (step(i), 0)),
            _const_spec((RS, D_MODEL)),
            _const_spec((D_MODEL, D_MODEL)),
            _const_spec((POOL_BUF, DEC_BATCH, D_POOL)),
            _const_spec((4, POOL_GROUP_DIM, POOL_GROUP_DIM)),
            _const_spec((1, D_POOL)),
            g3(1, N), g3(1, N), g3(1, 1), g3(C, N), g3(C, N), g3(C, N), g3(C, N),
            *cast_specs,
        ],
        out_specs=(
            row(D_SSM), row(D_POOL), pl.BlockSpec((1, 16, D_POOL), lambda i: (i, 0, 0)),
            pl.BlockSpec((RS, D_POOL), lambda i: (0, 0)),
            pl.BlockSpec((PREP_PAIRS, 2, wd, wd), lambda i: (step(i), 0, 0, 0)),
            p3(PAIR_W, 128), p3(PAIR_W, 128), p3(PAIR_W, 128), p3(PAIR_W, 128),
            p3(PAIR_W, 128), p3(PAIR_W, 128),
            p3(8, 128), p3(8, 128), p3(8, 128),
            *cast_specs,
        ),
        scratch_shapes=[pltpu.VMEM((TM + 16, D_POOL), F32)],
        compiler_params=_cparams(("arbitrary",)),
        name="front",
    )(xp, xs, w, state_t, w_pool, pool_scale, *prep_args, *to_cast)
    names = ("t", "pre", "pim", "p4re", "p4im", "qre", "qim", "apow_re", "apow_im", "lam4")
    return outs[:4], dict(zip(names, outs[4:4 + N_PREP_OUT])), outs[4 + N_PREP_OUT:]


SLAB_GROUPS = 128 // SSM_GROUP_CH
SLAB_PAIRS = SLAB_GROUPS // 2
N_SLABS = N_SSM_GROUPS // SLAB_GROUPS


def _ssm_pair(q, u, t_ref, pre_ref, pim_ref, p4re_ref, p4im_ref, qre_ref, qim_ref,
              apow_re_ref, apow_im_ref, lam4_ref, dskip_ref, h0re_ref, h0im_ref,
              hp_re_ref, hp_im_ref, hs_re_ref, hs_im_ref, hre_scr, him_scr):
    ub = u.astype(BF16)
    half = PAIR_W // 2
    y = jnp.concatenate([_dot(ub[:, :half], t_ref[q, 0]), _dot(ub[:, half:], t_ref[q, 1])], axis=1)

    ubp = ub[:P_CHUNK_ROWS]
    hre_scr[SCAN_PAD:SCAN_PAD + P_CHUNK_ROWS, :] = _dot(ubp, pre_ref[q])
    him_scr[SCAN_PAD:SCAN_PAD + P_CHUNK_ROWS, :] = _dot(ubp, pim_ref[q])
    kk = lax.broadcasted_iota(jnp.int32, (P_CHUNK_ROWS, 1), 0) % CHUNKS_PER_SEQ
    for s in range(N_SCAN_STEPS):
        d = 1 << s
        ar = apow_re_ref[q, s:s + 1, :]
        ai = apow_im_ref[q, s:s + 1, :]
        hr = hre_scr[SCAN_PAD:SCAN_PAD + P_CHUNK_ROWS, :]
        hi = him_scr[SCAN_PAD:SCAN_PAD + P_CHUNK_ROWS, :]
        pr = hre_scr[SCAN_PAD - d:SCAN_PAD - d + P_CHUNK_ROWS, :]
        pi = him_scr[SCAN_PAD - d:SCAN_PAD - d + P_CHUNK_ROWS, :]
        keep = kk >= d
        hre_scr[SCAN_PAD:SCAN_PAD + P_CHUNK_ROWS, :] = hr + jnp.where(keep, ar * pr - ai * pi, 0.0)
        him_scr[SCAN_PAD:SCAN_PAD + P_CHUNK_ROWS, :] = hi + jnp.where(keep, ar * pi + ai * pr, 0.0)
    hp_re_ref[q] = jnp.zeros((8, 128), F32)
    hp_im_ref[q] = jnp.zeros((8, 128), F32)
    for b in range(BATCH):
        last = SCAN_PAD + (b + 1) * CHUNKS_PER_SEQ - 1
        hp_re_ref[q, b:b + 1, :] = hre_scr[last:last + 1, :]
        hp_im_ref[q, b:b + 1, :] = him_scr[last:last + 1, :]
    prev_ok = kk >= 1
    hprev_re = jnp.where(prev_ok, hre_scr[SCAN_PAD - 1:SCAN_PAD - 1 + P_CHUNK_ROWS, :], 0.0)
    hprev_im = jnp.where(prev_ok, him_scr[SCAN_PAD - 1:SCAN_PAD - 1 + P_CHUNK_ROWS, :], 0.0)
    carry_p = _dot_nt(hprev_re.astype(BF16), qre_ref[q]) + _dot_nt(hprev_im.astype(BF16), qim_ref[q])

    ubs = ub[P_CHUNK_ROWS:]
    h0r = h0re_ref[q]
    h0i = h0im_ref[q]
    l4r = lam4_ref[q, 0:1, :]
    l4i = lam4_ref[q, 1:2, :]
    hs_re_ref[q] = l4r * h0r - l4i * h0i + _dot(ubs, p4re_ref[q])
    hs_im_ref[q] = l4r * h0i + l4i * h0r + _dot(ubs, p4im_ref[q])
    carry_s = _dot_nt(h0r.astype(BF16), qre_ref[q]) + _dot_nt(h0i.astype(BF16), qim_ref[q])

    y = y + jnp.concatenate([carry_p, carry_s], axis=0) + dskip_ref[q] * u
    return jax.nn.gelu(y)


N_SSM_IN, N_SSM_OUT = 13, 5


def _ssm_kernel(u_ref, *all_refs):
    n_cast = (len(all_refs) - N_SSM_IN - N_SSM_OUT - 2) // 2
    _cast_slices(all_refs[N_SSM_IN:N_SSM_IN + n_cast], all_refs[N_SSM_IN + n_cast + N_SSM_OUT:-2])
    refs = all_refs[:N_SSM_IN] + all_refs[N_SSM_IN + n_cast:N_SSM_IN + n_cast + N_SSM_OUT] + all_refs[-2:]
    z_ref = refs[13]
    hre_scr, him_scr = refs[18], refs[19]
    c = SSM_GROUP_CH
    hre_scr[0:SCAN_PAD, :] = jnp.zeros((SCAN_PAD, 128), F32)
    him_scr[0:SCAN_PAD, :] = jnp.zeros((SCAN_PAD, 128), F32)
    xt = []
    for i in range(CHUNK):
        xp = u_ref[pl.ds(i, P_CHUNK_ROWS, stride=CHUNK), :]
        if i < DEC_SEQ:
            xs = u_ref[RP + i * DEC_BATCH:RP + (i + 1) * DEC_BATCH, :]
        else:
            xs = jnp.zeros((DEC_BATCH, 128), F32)
        xt.append(jnp.concatenate([xp, xs], axis=0).T)
    def pair_input(q):
        halves = []
        for e in range(2):
            g = 2 * q + e
            bt = jnp.concatenate([xt[i][g * c:(g + 1) * c, :] for i in range(CHUNK)], axis=0)
            halves.append(bt.T)
        return jnp.concatenate(halves, axis=1)

    zt = []
    u_pair = pair_input(0)
    for q in range(SLAB_PAIRS):
        u_next = pair_input(q + 1) if q + 1 < SLAB_PAIRS else None
        z = _ssm_pair(q, u_pair, *refs[:13], *refs[14:])
        zt.append(z[:, :PAIR_W // 2].T)
        zt.append(z[:, PAIR_W // 2:].T)
        u_pair = u_next
    for i in range(CHUNK):
        zi = jnp.concatenate([zt[g][i * c:(i + 1) * c, :] for g in range(SLAB_GROUPS)], axis=0).T
        z_ref[pl.ds(i, P_CHUNK_ROWS, stride=CHUNK), :] = zi[:P_CHUNK_ROWS]
        if i < DEC_SEQ:
            z_ref[RP + i * DEC_BATCH:RP + (i + 1) * DEC_BATCH, :] = zi[P_CHUNK_ROWS:]


def _ssm(u_ssm, prep, h0re, h0im, to_cast):
    sp = SLAB_PAIRS
    slab3 = lambda a, b: pl.BlockSpec((sp, a, b), lambda s: (s, 0, 0))
    cast_specs, cast_shapes = _cast_specs(to_cast, N_SLABS)
    outs = pl.pallas_call(
        _ssm_kernel,
        out_shape=(
            jax.ShapeDtypeStruct((R, D_SSM), F32),
            jax.ShapeDtypeStruct((N_PAIRS, 8, 128), F32),
            jax.ShapeDtypeStruct((N_PAIRS, 8, 128), F32),
            jax.ShapeDtypeStruct((N_PAIRS, DEC_BATCH, 128), F32),
            jax.ShapeDtypeStruct((N_PAIRS, DEC_BATCH, 128), F32),
            *cast_shapes,
        ),
        grid=(N_SLABS,),
        in_specs=[
            pl.BlockSpec((R, 128), lambda s: (0, s)),
            pl.BlockSpec((sp, 2, 256, 256), lambda s: (s, 0, 0, 0)),
            slab3(PAIR_W, 128), slab3(PAIR_W, 128), slab3(PAIR_W, 128), slab3(PAIR_W, 128),
            slab3(PAIR_W, 128), slab3(PAIR_W, 128),
            slab3(8, 128), slab3(8, 128), slab3(8, 128),
            slab3(1, PAIR_W),
            slab3(DEC_BATCH, 128), slab3(DEC_BATCH, 128),
            *cast_specs,
        ],
        out_specs=(
            pl.BlockSpec((R, 128), lambda s: (0, s)),
            slab3(8, 128), slab3(8, 128), slab3(DEC_BATCH, 128), slab3(DEC_BATCH, 128),
            *cast_specs,
        ),
        scratch_shapes=[pltpu.VMEM((SCAN_PAD + P_CHUNK_ROWS, 128), F32),
                        pltpu.VMEM((SCAN_PAD + P_CHUNK_ROWS, 128), F32)],
        compiler_params=_cparams(("arbitrary",)),
        name="ssm",
    )(u_ssm, prep["t"], prep["pre"], prep["pim"], prep["p4re"], prep["p4im"],
      prep["qre"], prep["qim"], prep["apow_re"], prep["apow_im"], prep["lam4"],
      prep["dskip"], h0re, h0im, *to_cast)
    return outs[:N_SSM_OUT], outs[N_SSM_OUT:]


def _cmul(ar, ai, br, bi):
    return ar * br - ai * bi, ar * bi + ai * br


def _dot3(a, b):
    ah = a.astype(BF16)
    bh = b.astype(BF16)
    al = (a - ah.astype(F32)).astype(BF16)
    bl = (b - bh.astype(F32)).astype(BF16)
    return _dot(ah, bh) + _dot(ah, bl) + _dot(al, bh)


def _dot3_nt(a, b):
    nt = _dot_nt
    ah = a.astype(BF16)
    bh = b.astype(BF16)
    al = (a - ah.astype(F32)).astype(BF16)
    bl = (b - bh.astype(F32)).astype(BF16)
    return nt(ah, bh) + nt(ah, bl) + nt(al, bh)


def _ssm_prep_pair(pp, lam_re_ref, lam_im_ref, lstep_ref, bt_re_ref, bt_im_ref, c_re_ref, c_im_ref,
                   t_ref, pre_ref, pim_ref, p4re_ref, p4im_ref, qre_ref, qim_ref,
                   apr_ref, api_ref, lam4_ref):
    L, C, N = CHUNK, SSM_GROUP_CH, SSM_STATE
    w = L * C
    p_re, p_im, p4_re, p4_im, q_re, q_im, a_re, a_im, l4 = [], [], [], [], [], [], [], [], []
    for e in range(2):
        ge = 2 * pp + e
        dt = jnp.exp(lstep_ref[ge])
        lr, li = lam_re_ref[ge], lam_im_ref[ge]
        mag = jnp.exp(lr * dt)
        ang = li * dt
        zr, zi = mag * jnp.cos(ang), mag * jnp.sin(ang)
        den = lr * lr + li * li
        fr = ((zr - 1.0) * lr + zi * li) / den
        fi = (zi * lr - (zr - 1.0) * li) / den
        bbr, bbi = _cmul(fr, fi, bt_re_ref[ge], bt_im_ref[ge])
        pr, pi = [jnp.ones((1, N), F32)], [jnp.zeros((1, N), F32)]
        for _ in range(L):
            nr, ni = _cmul(pr[-1], pi[-1], zr, zi)
            pr.append(nr)
            pi.append(ni)
        stack = lambda blocks, part: jnp.concatenate([b[part] for b in blocks], axis=0)
        blocks = [_cmul(bbr, bbi, pr[L - 1 - i], pi[L - 1 - i]) for i in range(L)]
        p_re.append(stack(blocks, 0))
        p_im.append(stack(blocks, 1))
        blocks4 = [_cmul(bbr, bbi, pr[DEC_SEQ - 1 - i], pi[DEC_SEQ - 1 - i]) for i in range(DEC_SEQ)]
        pad = jnp.zeros(((L - DEC_SEQ) * C, N), F32)
        p4_re.append(jnp.concatenate([stack(blocks4, 0), pad], axis=0))
        p4_im.append(jnp.concatenate([stack(blocks4, 1), pad], axis=0))
        sr, si = [pr[L]], [pi[L]]
        for _ in range(N_SCAN_STEPS - 1):
            nr, ni = _cmul(sr[-1], si[-1], sr[-1], si[-1])
            sr.append(nr)
            si.append(ni)
        a_re.append(jnp.concatenate(sr + [jnp.zeros((8 - N_SCAN_STEPS, N), F32)], axis=0))
        a_im.append(jnp.concatenate(si + [jnp.zeros((8 - N_SCAN_STEPS, N), F32)], axis=0))
        l4.append(jnp.concatenate([pr[DEC_SEQ], pi[DEC_SEQ], jnp.zeros((6, N), F32)], axis=0))

        cr, ci = c_re_ref[ge], c_im_ref[ge]
        ck = [_cmul(cr, ci, pr[k], pi[k]) for k in range(L + 1)]
        q_re.append(stack(ck[1:], 0))
        q_im.append(-stack(ck[1:], 1))
        v = _dot3_nt(bbr, stack(ck[:L], 0)) - _dot3_nt(bbi, stack(ck[:L], 1))
        lane = lax.broadcasted_iota(jnp.int32, (C, w), 1)
        rows = [v] + [jnp.where(lane >= C * i, pltpu.roll(v, C * i, axis=1), 0.0) for i in range(1, L)]
        t_ref[pp, e] = jnp.concatenate(rows, axis=0).astype(BF16)

    def diag_rows(m):
        z = jnp.zeros_like(m[0])
        return jnp.concatenate([jnp.concatenate([m[0], z], axis=1),
                                jnp.concatenate([z, m[1]], axis=1)], axis=0)

    pre_ref[pp] = diag_rows(p_re).astype(BF16)
    pim_ref[pp] = diag_rows(p_im).astype(BF16)
    p4re_ref[pp] = diag_rows(p4_re).astype(BF16)
    p4im_ref[pp] = diag_rows(p4_im).astype(BF16)
    qre_ref[pp] = diag_rows(q_re).astype(BF16)
    qim_ref[pp] = diag_rows(q_im).astype(BF16)
    apr_ref[pp] = jnp.concatenate(a_re, axis=1)
    api_ref[pp] = jnp.concatenate(a_im, axis=1)
    lam4_ref[pp] = jnp.concatenate(l4, axis=1)


def _ssm_prep_args(lambda_re, lambda_im, log_step, b_re, b_im, c_re, c_im):
    return (lambda_re[:, None, :], lambda_im[:, None, :], log_step[:, None, None],
            b_re.transpose(0, 2, 1), b_im.transpose(0, 2, 1), c_re, c_im)


def _mix_kernel(a_ref, z_ref, xp_ref, xs_ref, wglu_ref, bglu_ref, wout_ref, g_ref, b_ref, *refs):
    n_cast = (len(refs) - 2) // 2
    o_ref, ob_ref = refs[n_cast], refs[n_cast + 1]
    _cast_slices(refs[:n_cast], refs[n_cast + 2:])
    is_prompt = pl.program_id(0) < NBP
    n_sub = TM // SUB

    def glu(r):
        z = z_ref[r * SUB:(r + 1) * SUB, :]
        gate = _dot(z.astype(BF16), wglu_ref[...]) + bglu_ref[...]
        return (z * jax.nn.sigmoid(gate)).astype(BF16)

    bmix = glu(0)
    for r in range(n_sub):
        rows = slice(r * SUB, (r + 1) * SUB)
        bmix_next = glu(r + 1) if r + 1 < n_sub else None
        mix = _dot(a_ref[rows, :], wout_ref[0:D_POOL, :]) + _dot(bmix, wout_ref[D_POOL:, :])
        x = jnp.where(is_prompt, xp_ref[rows, :], xs_ref[rows, :])
        h1 = _layer_norm(ALPHA * x + mix, g_ref[...], b_ref[...])
        o_ref[rows, :] = h1
        ob_ref[rows, :] = h1.astype(BF16)
        bmix = bmix_next


def _mix(a, z, xp, xs, w_glu, b_glu, w_out, g, b, to_cast):
    row = lambda w: pl.BlockSpec((TM, w), lambda i: (i, 0))
    cast_specs, cast_shapes = _cast_specs(to_cast, NBP)
    outs = pl.pallas_call(
        _mix_kernel,
        out_shape=(jax.ShapeDtypeStruct((R, D_MODEL), F32), jax.ShapeDtypeStruct((R, D_MODEL), BF16), *cast_shapes),
        grid=(NB,),
        in_specs=[
            row(D_POOL), row(D_SSM),
            pl.BlockSpec((TM, D_MODEL), lambda i: (jnp.minimum(i, NBP - 1), 0)),
            _const_spec((RS, D_MODEL)),
            _const_spec((D_SSM, D_SSM)), _const_spec((1, D_SSM)),
            _const_spec((D_MODEL, D_MODEL)), _const_spec((1, D_MODEL)), _const_spec((1, D_MODEL)),
            *cast_specs,
        ],
        out_specs=(row(D_MODEL), row(D_MODEL), *cast_specs),
        compiler_params=_cparams(("arbitrary",)),
        name="mix_ln1",
    )(a, z, xp, xs, w_glu, b_glu, w_out, g, b, *to_cast)
    return outs[0], outs[1], outs[2:]


def _proj_kernel(x_ref, w_ref, o_ref):
    o_ref[...] = _dot(x_ref[...], w_ref[...]).astype(o_ref.dtype)


def _qproj_sample(h, w):
    return pl.pallas_call(
        _proj_kernel,
        out_shape=jax.ShapeDtypeStruct((RS, D_MODEL), BF16),
        grid=(1,),
        in_specs=[pl.BlockSpec((RS, D_MODEL), lambda i: (RP // RS, 0)),
                  pl.BlockSpec((D_MODEL, D_MODEL), lambda i: (0, 0))],
        out_specs=pl.BlockSpec((RS, D_MODEL), lambda i: (0, 0)),
        compiler_params=_cparams(("arbitrary",)),
        name="qproj_sample",
    )(h, w)


def _memkv_kernel(m_ref, wk_ref, wv_ref, k_ref, v_ref, kb_ref, vb_ref):
    mb = m_ref[...].astype(BF16)
    k = _dot(mb, wk_ref[...].astype(BF16))
    v = _dot(mb, wv_ref[...].astype(BF16))
    k_ref[...] = k
    v_ref[...] = v
    kb_ref[...] = k.astype(BF16)
    vb_ref[...] = v.astype(BF16)


def _memkv(mem, wk, wv):
    rows = BATCH * N_MEM
    tn = 512
    col = pl.BlockSpec((rows, tn), lambda j: (0, j))
    wcol = pl.BlockSpec((D_MODEL, tn), lambda j: (0, j))
    return pl.pallas_call(
        _memkv_kernel,
        out_shape=(jax.ShapeDtypeStruct((rows, D_MODEL), F32), jax.ShapeDtypeStruct((rows, D_MODEL), F32),
                   jax.ShapeDtypeStruct((rows, D_MODEL), BF16), jax.ShapeDtypeStruct((rows, D_MODEL), BF16)),
        grid=(D_MODEL // tn,),
        in_specs=[_const_spec((rows, D_MODEL)), wcol, wcol],
        out_specs=(col, col, col, col),
        compiler_params=_cparams(("arbitrary",)),
        name="memkv",
    )(mem, wk, wv)


def _attend(q, k, v, between=None):
    def scores(h):
        sl = slice(h * XHEAD_DIM, (h + 1) * XHEAD_DIM)
        return lax.dot_general(q[:, sl], k[:, sl], (((1,), (1,)), ((), ())),
                               preferred_element_type=F32) * (XHEAD_DIM ** -0.5)

    def head_out(h, s):
        s = s - jnp.max(s, axis=-1, keepdims=True)
        e = jnp.exp(s)
        p = e / jnp.sum(e, axis=-1, keepdims=True)
        return _dot(p.astype(BF16), v[:, h * XHEAD_DIM:(h + 1) * XHEAD_DIM])

    outs = []
    s = scores(0)
    for h in range(N_XHEADS):
        if between is not None:
            between(h)
        s_next = scores(h + 1) if h + 1 < N_XHEADS else None
        outs.append(head_out(h, s))
        s = s_next
    return jnp.concatenate(outs, axis=1)


def _attn_block_kernel(h_ref, q_ref, k_ref, v_ref, os_ref, wo_ref, g_ref, b_ref, *refs):
    n_cast = (len(refs) - 1) // 2
    o_ref = refs[n_cast]
    _cast_slices(refs[:n_cast], refs[n_cast + 1:])
    i = pl.program_id(0)
    n_sub = TM // SUB

    def normed(rows, y):
        o_ref[rows, :] = _layer_norm(ALPHA * h_ref[rows, :] + y, g_ref[...], b_ref[...])

    @pl.when(i < NBP)
    def _():
        prev_rows, prev_o = None, None
        for r in range(n_sub):
            rows = slice(r * SUB, (r + 1) * SUB)
            ys = []
            between = None
            if prev_o is not None:
                between = lambda hh, po=prev_o: ys.append(
                    _dot(po, wo_ref[:, hh * XHEAD_DIM:(hh + 1) * XHEAD_DIM]))
            o = _attend(q_ref[rows, :], k_ref[...], v_ref[...], between).astype(BF16)
            if prev_o is not None:
                normed(prev_rows, jnp.concatenate(ys, axis=1))
            prev_rows, prev_o = rows, o
        normed(prev_rows, _dot(prev_o, wo_ref[...]))

    @pl.when(i == NBP)
    def _():
        for r in range(n_sub):
            rows = slice(r * SUB, (r + 1) * SUB)
            normed(rows, _dot(os_ref[rows, :], wo_ref[...]))


def _attn_block(h, q_prompt, kb, vb, o_sample, wo, g, b, to_cast):
    kv = pl.BlockSpec((N_MEM, D_MODEL), lambda i: (jnp.minimum(i // BLOCKS_PER_SEQ, BATCH - 1), 0))
    row = pl.BlockSpec((TM, D_MODEL), lambda i: (i, 0))
    qrow = pl.BlockSpec((TM, D_MODEL), lambda i: (jnp.minimum(i, NBP - 1), 0))
    cast_specs, cast_shapes = _cast_specs(to_cast, NBP)
    outs = pl.pallas_call(
        _attn_block_kernel,
        out_shape=(jax.ShapeDtypeStruct((R, D_MODEL), F32), *cast_shapes),
        grid=(NB,),
        in_specs=[row, qrow, kv, kv, _const_spec((RS, D_MODEL)), _const_spec((D_MODEL, D_MODEL)),
                  _const_spec((1, D_MODEL)), _const_spec((1, D_MODEL)), *cast_specs],
        out_specs=(row, *cast_specs),
        compiler_params=_cparams(("arbitrary",)),
        name="attn_block",
    )(h, q_prompt, kb, vb, o_sample, wo, g, b, *to_cast)
    return outs[0], outs[1:]


ATT_BB = 4
Q_PAD = 8
KV_ROWS = N_MEM * N_XHEADS


QP_ROWS = RP // (DEC_BATCH // ATT_BB)


def _attn_sample_kernel(q_ref, k_ref, v_ref, h_ref, wq_ref, o_ref, qp_ref):
    hb = h_ref[...]
    qcols = D_MODEL // ATT_BB
    shape = (N_XHEADS * Q_PAD, KV_ROWS)
    same_head = (lax.broadcasted_iota(jnp.int32, shape, 0) // Q_PAD
                 == lax.broadcasted_iota(jnp.int32, shape, 1) % N_XHEADS)
    def scores(b):
        q = q_ref[b].astype(F32)
        qs = jnp.concatenate([q[:, h * XHEAD_DIM:(h + 1) * XHEAD_DIM] for h in range(N_XHEADS)], axis=0)
        k = k_ref[0, b].reshape(KV_ROWS, XHEAD_DIM).astype(BF16)
        return lax.dot_general(qs.astype(BF16), k, (((1,), (1,)), ((), ())),
                               preferred_element_type=F32) * (XHEAD_DIM ** -0.5)

    def attend(b, s):
        s = jnp.where(same_head, s, -1e30)
        s = s - jnp.max(s, axis=-1, keepdims=True)
        e = jnp.exp(s)
        p = e / jnp.sum(e, axis=-1, keepdims=True)
        v = v_ref[0, b].reshape(KV_ROWS, XHEAD_DIM).astype(BF16)
        o = _dot(p.astype(BF16), v)
        for h in range(N_XHEADS):
            o_ref[b, :, h * XHEAD_DIM:(h + 1) * XHEAD_DIM] = o[h * Q_PAD:(h + 1) * Q_PAD]

    s_prev = scores(0)
    for b in range(ATT_BB):
        cols = slice(b * qcols, (b + 1) * qcols)
        qp_ref[:, cols] = _dot(hb, wq_ref[:, cols]).astype(BF16)
        s_next = scores(b + 1) if b + 1 < ATT_BB else None
        attend(b, s_prev)
        s_prev = s_next


def _attn_sample(q_pad, k, v, h, wq):
    kv = pl.BlockSpec((1, ATT_BB, N_MEM, N_XHEADS, XHEAD_DIM), lambda i: (0, i, 0, 0, 0))
    qo = pl.BlockSpec((ATT_BB, Q_PAD, D_MODEL), lambda i: (i, 0, 0))
    hq = pl.BlockSpec((QP_ROWS, D_MODEL), lambda i: (i, 0))
    return pl.pallas_call(
        _attn_sample_kernel,
        out_shape=(jax.ShapeDtypeStruct((DEC_BATCH, Q_PAD, D_MODEL), F32),
                   jax.ShapeDtypeStruct((RP, D_MODEL), BF16)),
        grid=(DEC_BATCH // ATT_BB,),
        in_specs=[qo, kv, kv, hq, _const_spec((D_MODEL, D_MODEL))],
        out_specs=(qo, hq),
        compiler_params=_cparams(("arbitrary",)),
        name="attn_sample",
    )(q_pad, k, v, h, wq)


FTM = 1024
FNB = RP // FTM
FBLOCKS_PER_SEQ = SEQ // FTM
FSUB = 256


def _ffn_rows(n_rows, conv_taps, h_ref, hb_scr, g_scr, g_base, wg, wu, wd, cw_ref, cb_ref,
              lng_ref, lnb_ref, y_ref, first, last):
    def gate_up(lo):
        if first:
            h = h_ref[lo:lo + FSUB, :]
            hb = h.astype(BF16)
            hb_scr[lo:lo + FSUB, :] = hb
        else:
            h, hb = None, hb_scr[lo:lo + FSUB, :]
        g = _dot(hb, wg)
        up = _dot(hb, wu)
        g_scr[g_base + lo:g_base + lo + FSUB, :] = g
        tap0, tap1 = conv_taps(lo)
        gc = cb_ref[...] + cw_ref[0:1, :] * tap0 + cw_ref[1:2, :] * tap1 + cw_ref[2:3, :] * g
        return h, (jax.nn.silu(gc) * up).astype(BF16)

    def down(lo, h, act):
        y = _dot(act, wd) + (ALPHA * h if first else y_ref[lo:lo + FSUB, :])
        y_ref[lo:lo + FSUB, :] = _layer_norm(y, lng_ref[...], lnb_ref[...]) if last else y

    n_sub = n_rows // FSUB
    pending = gate_up(0)
    for r in range(n_sub):
        nxt = gate_up((r + 1) * FSUB) if r + 1 < n_sub else None
        down(r * FSUB, *pending)
        pending = nxt


def _ffn_chunk(f, run):
    @pl.when(f == 0)
    def _():
        run(True, False)

    @pl.when(jnp.logical_and(f > 0, f < NF - 1))
    def _():
        run(False, False)

    @pl.when(f == NF - 1)
    def _():
        run(False, True)


def _ffn_prompt_kernel(h_ref, wg_ref, wu_ref, wd_ref, cw_ref, cb_ref, lng_ref, lnb_ref,
                       y_ref, gtail_ref, hb_scr, g_scr, carry_scr):
    i = pl.program_id(0)
    f = pl.program_id(1)
    first_block = (i % FBLOCKS_PER_SEQ) == 0

    @pl.when(first_block)
    def _():
        g_scr[0:8, :] = jnp.zeros((8, TF), F32)

    @pl.when(jnp.logical_not(first_block))
    def _():
        g_scr[0:8, :] = carry_scr[f]

    taps = lambda lo: (g_scr[6 + lo:6 + lo + FSUB, :], g_scr[7 + lo:7 + lo + FSUB, :])
    _ffn_chunk(f, lambda first, last: _ffn_rows(
        FTM, taps, h_ref, hb_scr, g_scr, 8, wg_ref[...], wu_ref[...], wd_ref[...], cw_ref, cb_ref,
        lng_ref, lnb_ref, y_ref, first, last))
    tail = g_scr[FTM:FTM + 8, :]
    carry_scr[f] = tail
    gtail_ref[0] = tail


def _ffn_sample_kernel(h_ref, wg_ref, wu_ref, wd_ref, cw_ref, cb_ref, lng_ref, lnb_ref, st_ref,
                       y_ref, gnew_ref, wdb_ref, hb_scr, g_scr):
    f = pl.program_id(1)
    n_st = 2 * DEC_BATCH
    wdb_ref[...] = wd_ref[...].astype(BF16)
    g_scr[0:n_st, :] = st_ref[...]
    taps = lambda lo: (g_scr[lo:lo + FSUB, :], g_scr[DEC_BATCH + lo:DEC_BATCH + lo + FSUB, :])
    _ffn_chunk(f, lambda first, last: _ffn_rows(
        RS, taps, h_ref, hb_scr, g_scr, n_st, wg_ref[...], wu_ref[...], wdb_ref[...], cw_ref, cb_ref,
        lng_ref, lnb_ref, y_ref, first, last))
    gnew_ref[...] = g_scr[RS:RS + n_st, :]


def _ffn_specs(tm, row_block_of):
    return [
        pl.BlockSpec((tm, D_MODEL), lambda i, f: (row_block_of(i), 0)),
        pl.BlockSpec((D_MODEL, TF), lambda i, f: (0, f)),
        pl.BlockSpec((D_MODEL, TF), lambda i, f: (0, f)),
        pl.BlockSpec((TF, D_MODEL), lambda i, f: (f, 0)),
        pl.BlockSpec((CONV_W, TF), lambda i, f: (0, f)),
        pl.BlockSpec((1, TF), lambda i, f: (0, f)),
        pl.BlockSpec((1, D_MODEL), lambda i, f: (0, 0)),
        pl.BlockSpec((1, D_MODEL), lambda i, f: (0, 0)),
    ]


def _ffn_prompt(h, wg, wu, wd, cw, cb, lng, lnb):
    return pl.pallas_call(
        _ffn_prompt_kernel,
        out_shape=(jax.ShapeDtypeStruct((RP, D_MODEL), F32),
                   jax.ShapeDtypeStruct((FNB, 8, D_FF), F32)),
        grid=(FNB, NF),
        in_specs=_ffn_specs(FTM, lambda i: i),
        out_specs=(pl.BlockSpec((FTM, D_MODEL), lambda i, f: (i, 0)),
                   pl.BlockSpec((1, 8, TF), lambda i, f: (i, 0, f))),
        scratch_shapes=[pltpu.VMEM((FTM, D_MODEL), BF16),
                        pltpu.VMEM((FTM + 8, TF), F32),
                        pltpu.VMEM((NF, 8, TF), F32)],
        compiler_params=_cparams(("arbitrary", "arbitrary"), VMEM_LIMIT_FFN),
        name="ffn_prompt",
    )(h, wg, wu, wd, cw, cb, lng, lnb)


def _ffn_sample(h, wg, wu, wd, cw, cb, lng, lnb, conv_state):
    n_st = 2 * DEC_BATCH
    specs = _ffn_specs(RS, lambda i: RP // RS)
    return pl.pallas_call(
        _ffn_sample_kernel,
        out_shape=(jax.ShapeDtypeStruct((RS, D_MODEL), F32),
                   jax.ShapeDtypeStruct((n_st, D_FF), F32),
                   jax.ShapeDtypeStruct((D_FF, D_MODEL), BF16)),
        grid=(1, NF),
        in_specs=specs + [pl.BlockSpec((n_st, TF), lambda i, f: (0, f))],
        out_specs=(pl.BlockSpec((RS, D_MODEL), lambda i, f: (0, 0)),
                   pl.BlockSpec((n_st, TF), lambda i, f: (0, f)),
                   specs[3]),
        scratch_shapes=[pltpu.VMEM((RS, D_MODEL), BF16),
                        pltpu.VMEM((n_st + RS, TF), F32)],
        compiler_params=_cparams(("arbitrary", "arbitrary")),
        name="ffn_sample",
    )(h, wg, wu, wd, cw, cb, lng, lnb, conv_state)


def _state_to_pairs(s):
    return s.reshape(s.shape[0], N_PAIRS, 128).transpose(1, 0, 2)


def _state_from_pairs(s):
    return s.transpose(1, 0, 2).reshape(1, s.shape[1], N_SSM_GROUPS, SSM_STATE)


def kernel(x_prompt, x_sample, mem_prompt, state_pool, state_ssm_re, state_ssm_im, state_conv, cache_mem_k, cache_mem_v, w_in, w_pool, pool_scale, lambda_re, lambda_im, log_step, b_re, b_im, c_re, c_im, d_skip, w_glu, b_glu, w_out, ln1_g, ln1_b, w_q, w_k, w_v, w_o, ln2_g, ln2_b, w_gate, w_up, conv_w, conv_b, w_down, ln3_g, ln3_b):
    bf = lambda w: w[0].astype(BF16)
    row = lambda v: v[0].reshape(1, -1)
    xp = x_prompt.reshape(RP, D_MODEL)
    xs = x_sample.transpose(1, 0, 2).reshape(RS, D_MODEL)

    prep_args = _ssm_prep_args(lambda_re[0], lambda_im[0], log_step[0], b_re[0], b_im[0], c_re[0], c_im[0])
    (u_ssm, a, utail, us_pool), prep, (w_glu_b, w_out_b) = _front(
        xp, xs, bf(w_in), state_pool[0].transpose(1, 0, 2), bf(w_pool), row(pool_scale), prep_args,
        (w_glu[0], w_out[0]))
    prep["dskip"] = jnp.broadcast_to(d_skip[0].reshape(N_PAIRS, 2, 1, SSM_GROUP_CH),
                                     (N_PAIRS, 2, CHUNK, SSM_GROUP_CH)).reshape(N_PAIRS, 1, PAIR_W)
    (z, hp_re, hp_im, hs_re, hs_im), (w_q_b, w_o_b) = _ssm(
        u_ssm, prep, _state_to_pairs(state_ssm_re[0]), _state_to_pairs(state_ssm_im[0]), (w_q[0], w_o[0]))
    h1, h1b, (wg,) = _mix(a, z, xp, xs, w_glu_b, row(b_glu), w_out_b, row(ln1_g), row(ln1_b), (w_gate[0],))

    mk, mv, mkb, mvb = _memkv(mem_prompt.reshape(BATCH * N_MEM, D_MODEL), w_k[0], w_v[0])
    q_s = _qproj_sample(h1b, w_q_b).reshape(DEC_SEQ, DEC_BATCH, D_MODEL).transpose(1, 0, 2)
    q_s = jnp.pad(q_s, ((0, 0), (0, Q_PAD - DEC_SEQ), (0, 0)))
    o_s, q_p = _attn_sample(q_s, cache_mem_k, cache_mem_v, h1b, w_q_b)
    o_s = o_s[:, :DEC_SEQ].transpose(1, 0, 2).reshape(RS, D_MODEL).astype(BF16)
    h2, (wu,) = _attn_block(h1, q_p, mkb, mvb, o_s, w_o_b, row(ln2_g), row(ln2_b), (w_up[0],))

    cw, cb = conv_w[0], row(conv_b)
    conv_st = state_conv[0].transpose(1, 0, 2).reshape(2 * DEC_BATCH, D_FF)
    y_s, g_new, wd = _ffn_sample(h2, wg, wu, w_down[0], cw, cb, row(ln3_g), row(ln3_b), conv_st)
    y_p, gtail = _ffn_prompt(h2, wg, wu, wd, cw, cb, row(ln3_g), row(ln3_b))

    y_prompt = y_p.reshape(BATCH, SEQ, D_MODEL)
    y_sample = y_s.reshape(DEC_SEQ, DEC_BATCH, D_MODEL).transpose(1, 0, 2)
    p_pool = utail[BLOCKS_PER_SEQ - 1:NBP:BLOCKS_PER_SEQ, 16 - POOL_BUF:][None]
    s_ext = jnp.concatenate([state_pool[0], us_pool.reshape(DEC_SEQ, DEC_BATCH, D_POOL).transpose(1, 0, 2)], axis=1)
    s_pool = s_ext[None, :, DEC_SEQ:]
    p_conv = gtail[FBLOCKS_PER_SEQ - 1::FBLOCKS_PER_SEQ, 6:8][None]
    s_conv = g_new.reshape(2, DEC_BATCH, D_FF).transpose(1, 0, 2)[None]
    shape_kv = (1, BATCH, N_MEM, N_XHEADS, XHEAD_DIM)
    return (y_prompt, y_sample,
            p_pool, _state_from_pairs(hp_re[:, :BATCH]), _state_from_pairs(hp_im[:, :BATCH]), p_conv,
            mk.reshape(shape_kv), mv.reshape(shape_kv),
            s_pool, _state_from_pairs(hs_re), _state_from_pairs(hs_im), s_conv)
```

```python
import functools
import math

import jax
import jax.numpy as jnp
from jax import lax
from jax.experimental import pallas as pl
from jax.experimental.pallas import tpu as pltpu

F32 = jnp.float32
BF16 = jnp.bfloat16

D_MODEL = 2048
BATCH = 4
SEQ = 2048
DEC_BATCH = 128
DEC_SEQ = 4
PAST_LEN = 16384
D_POOL = 1024
D_SSM = 1024
POOL_WINDOWS = (2, 4, 8, 16)
POOL_GROUP_DIM = 256
POOL_BUF = 15
SSM_GROUP_CH = 16
N_SSM_GROUPS = 64
SSM_STATE = 64
N_MEM = 256
N_XHEADS = 4
XHEAD_DIM = 512
D_FF = 5632
CONV_W = 3
ALPHA = 2.0 ** 0.25
LN_EPS = 1e-5

RP = BATCH * SEQ
RS = DEC_BATCH * DEC_SEQ
R = RP + RS
TM = 512
NB = R // TM
NBP = RP // TM
BLOCKS_PER_SEQ = SEQ // TM

CHUNK = 16
N_PAIRS = N_SSM_GROUPS // 2
PAIR_W = 2 * CHUNK * SSM_GROUP_CH
CHUNKS_PER_SEQ = SEQ // CHUNK
P_CHUNK_ROWS = BATCH * CHUNKS_PER_SEQ
SSM_ROWS = P_CHUNK_ROWS + DEC_BATCH
SCAN_PAD = 64
N_SCAN_STEPS = 7

SUB = 256
TF = 512
NF = D_FF // TF

VMEM_LIMIT = 56 * 1024 * 1024
VMEM_LIMIT_FFN = 60 * 1024 * 1024


def _cparams(sem, vmem_limit=VMEM_LIMIT):
    return pltpu.CompilerParams(dimension_semantics=sem, vmem_limit_bytes=vmem_limit)


def _const_spec(shape):
    n = len(shape)
    return pl.BlockSpec(shape, lambda *_: (0,) * n, pipeline_mode=pl.Buffered(1))


def _layer_norm(x, g, b):
    mu = jnp.mean(x, axis=-1, keepdims=True)
    xc = x - mu
    var = jnp.mean(xc * xc, axis=-1, keepdims=True)
    return xc * lax.rsqrt(var + LN_EPS) * g + b


def _dot(a, b):
    return jnp.dot(a, b, preferred_element_type=F32)


def _dot_nt(a, b):
    return lax.dot_general(a, b, (((1,), (1,)), ((), ())), preferred_element_type=F32)


def _cast_spec(a, n_steps):
    return pl.BlockSpec((a.shape[0] // n_steps, a.shape[1]), lambda i: (jnp.minimum(i, n_steps - 1), 0))


def _cast_specs(arrays, n_steps):
    specs = [_cast_spec(a, n_steps) for a in arrays]
    return specs, [jax.ShapeDtypeStruct(a.shape, BF16) for a in arrays]


def _cast_slices(in_refs, out_refs):
    for src_ref, dst_ref in zip(in_refs, out_refs):
        dst_ref[...] = src_ref[...].astype(BF16)


N_PREP_IN, N_PREP_OUT = 7, 10
PREP_PAIRS = N_PAIRS // NBP


def _pool_mix(g, pooled, w_ref, scale_ref):
    sl = slice(g * POOL_GROUP_DIM, (g + 1) * POOL_GROUP_DIM)
    return (_dot(pooled.astype(BF16), w_ref[g]) * scale_ref[:, sl]).astype(BF16)


def _front_kernel(xp_ref, xs_ref, w_ref, st_ref, wp_ref, scale_ref, *refs):
    prep_in, refs = refs[:N_PREP_IN], refs[N_PREP_IN:]
    n_cast = (len(refs) - 4 - N_PREP_OUT - 1) // 2
    cast_in, refs = refs[:n_cast], refs[n_cast:]
    ussm_ref, a_ref, utail_ref, us_ref = refs[:4]
    prep_out, cast_out, ext_ref = refs[4:4 + N_PREP_OUT], refs[4 + N_PREP_OUT:-1], refs[-1]
    i = pl.program_id(0)

    xb = jnp.where(i < NBP, xp_ref[...], xs_ref[...]).astype(BF16)

    first = (i % BLOCKS_PER_SEQ) == 0
    ext_ref[0:16, :] = jnp.where(first, 0.0, ext_ref[TM:TM + 16, :])
    ext_ref[16:16 + TM, :] = _dot(xb, w_ref[:, 0:D_POOL])
    utail_ref[0] = ext_ref[TM:TM + 16, :]
    pos = (i % BLOCKS_PER_SEQ) * TM + lax.broadcasted_iota(jnp.int32, (TM, 1), 0)
    qc = D_SSM // len(POOL_WINDOWS)
    for g, w in enumerate(POOL_WINDOWS):
        ussm_ref[:, g * qc:(g + 1) * qc] = _dot(xb, w_ref[:, D_POOL + g * qc:D_POOL + (g + 1) * qc])
        sl = slice(g * POOL_GROUP_DIM, (g + 1) * POOL_GROUP_DIM)
        acc = ext_ref[16:16 + TM, sl]
        for k in range(1, w):
            acc = acc + ext_ref[16 - k:16 - k + TM, sl]
        cnt = jnp.minimum(pos + 1, w).astype(F32)
        a_ref[:, sl] = _pool_mix(g, acc / cnt - ext_ref[16:16 + TM, sl], wp_ref, scale_ref)

    for pp in range(PREP_PAIRS):
        _ssm_prep_pair(pp, *prep_in, *prep_out)
    _cast_slices(cast_in, cast_out)

    @pl.when(i == NBP)
    def _():
        us_ref[...] = ext_ref[16:16 + TM, :]
        tok = lambda t, sl: ext_ref[16 + t * DEC_BATCH:16 + (t + 1) * DEC_BATCH, sl]
        for j in range(DEC_SEQ):
            rows = slice(j * DEC_BATCH, (j + 1) * DEC_BATCH)
            for g, w in enumerate(POOL_WINDOWS):
                sl = slice(g * POOL_GROUP_DIM, (g + 1) * POOL_GROUP_DIM)
                acc = tok(j, sl)
                for k in range(1, w):
                    e = POOL_BUF + j - k
                    acc = acc + (tok(e - POOL_BUF, sl) if e >= POOL_BUF else st_ref[e, :, sl])
                cnt = float(min(PAST_LEN + j + 1, w))
                a_ref[rows, sl] = _pool_mix(g, acc / cnt - tok(j, sl), wp_ref, scale_ref)


def _front(xp, xs, w, state_t, w_pool, pool_scale, prep_args, to_cast):
    G, N, C, L = N_SSM_GROUPS, SSM_STATE, SSM_GROUP_CH, CHUNK
    wd = L * C
    step = lambda i: jnp.minimum(i, NBP - 1)
    g3 = lambda a, b: pl.BlockSpec((2 * PREP_PAIRS, a, b), lambda i: (step(i), 0, 0))
    p3 = lambda a, b: pl.BlockSpec((PREP_PAIRS, a, b), lambda i: (step(i), 0, 0))
    sds = jax.ShapeDtypeStruct
    cast_specs, cast_shapes = _cast_specs(to_cast, NBP)
    row = lambda wdt: pl.BlockSpec((TM, wdt), lambda i: (i, 0))
    outs = pl.pallas_call(
        _front_kernel,
        out_shape=(
            sds((R, D_SSM), F32), sds((R, D_POOL), BF16), sds((NB, 16, D_POOL), F32), sds((RS, D_POOL), F32),
            sds((N_PAIRS, 2, wd, wd), BF16),
            sds((N_PAIRS, PAIR_W, 128), BF16), sds((N_PAIRS, PAIR_W, 128), BF16),
            sds((N_PAIRS, PAIR_W, 128), BF16), sds((N_PAIRS, PAIR_W, 128), BF16),
            sds((N_PAIRS, PAIR_W, 128), BF16), sds((N_PAIRS, PAIR_W, 128), BF16),
            sds((N_PAIRS, 8, 128), F32), sds((N_PAIRS, 8, 128), F32), sds((N_PAIRS, 8, 128), F32),
            *cast_shapes,
        ),
        grid=(NB,),
        in_specs=[
            pl.BlockSpec((TM, D_MODEL), lambda i: (step(i), 0)),
            _const_spec((RS, D_MODEL)),
            _const_spec((D_MODEL, D_MODEL)),
            _const_spec((POOL_BUF, DEC_BATCH, D_POOL)),
            _const_spec((4, POOL_GROUP_DIM, POOL_GROUP_DIM)),
            _const_spec((1, D_POOL)),
            g3(1, N), g3(1, N), g3(1, 1), g3(C, N), g3(C, N), g3(C, N), g3(C, N),
            *cast_specs,
        ],
        out_specs=(
            row(D_SSM), row(D_POOL), pl.BlockSpec((1, 16, D_POOL), lambda i: (i, 0, 0)),
            pl.BlockSpec((RS, D_POOL), lambda i: (0, 0)),
            pl.BlockSpec((PREP_PAIRS, 2, wd, wd), lambda i: (step(i), 0, 0, 0)),
            p3(PAIR_W, 128), p3(PAIR_W, 128), p3(PAIR_W, 128), p3(PAIR_W, 128),
            p3(PAIR_W, 128), p3(PAIR_W, 128),
            p3(8, 128), p3(8, 128), p3(8, 128),
            *cast_specs,
        ),
        scratch_shapes=[pltpu.VMEM((TM + 16, D_POOL), F32)],
        compiler_params=_cparams(("arbitrary",)),
        name="front",
    )(xp, xs, w, state_t, w_pool, pool_scale, *prep_args, *to_cast)
    names = ("t", "pre", "pim", "p4re", "p4im", "qre", "qim", "apow_re", "apow_im", "lam4")
    return outs[:4], dict(zip(names, outs[4:4 + N_PREP_OUT])), outs[4 + N_PREP_OUT:]


SLAB_GROUPS = 128 // SSM_GROUP_CH
SLAB_PAIRS = SLAB_GROUPS // 2
N_SLABS = N_SSM_GROUPS // SLAB_GROUPS


def _ssm_pair(q, u, t_ref, pre_ref, pim_ref, p4re_ref, p4im_ref, qre_ref, qim_ref,
              apow_re_ref, apow_im_ref, lam4_ref, dskip_ref, h0re_ref, h0im_ref,
              hp_re_ref, hp_im_ref, hs_re_ref, hs_im_ref, hre_scr, him_scr):
    ub = u.astype(BF16)
    half = PAIR_W // 2
    y = jnp.concatenate([_dot(ub[:, :half], t_ref[q, 0]), _dot(ub[:, half:], t_ref[q, 1])], axis=1)

    ubp = ub[:P_CHUNK_ROWS]
    hre_scr[SCAN_PAD:SCAN_PAD + P_CHUNK_ROWS, :] = _dot(ubp, pre_ref[q])
    him_scr[SCAN_PAD:SCAN_PAD + P_CHUNK_ROWS, :] = _dot(ubp, pim_ref[q])
    kk = lax.broadcasted_iota(jnp.int32, (P_CHUNK_ROWS, 1), 0) % CHUNKS_PER_SEQ
    for s in range(N_SCAN_STEPS):
        d = 1 << s
        ar = apow_re_ref[q, s:s + 1, :]
        ai = apow_im_ref[q, s:s + 1, :]
        hr = hre_scr[SCAN_PAD:SCAN_PAD + P_CHUNK_ROWS, :]
        hi = him_scr[SCAN_PAD:SCAN_PAD + P_CHUNK_ROWS, :]
        pr = hre_scr[SCAN_PAD - d:SCAN_PAD - d + P_CHUNK_ROWS, :]
        pi = him_scr[SCAN_PAD - d:SCAN_PAD - d + P_CHUNK_ROWS, :]
        keep = kk >= d
        hre_scr[SCAN_PAD:SCAN_PAD + P_CHUNK_ROWS, :] = hr + jnp.where(keep, ar * pr - ai * pi, 0.0)
        him_scr[SCAN_PAD:SCAN_PAD + P_CHUNK_ROWS, :] = hi + jnp.where(keep, ar * pi + ai * pr, 0.0)
    hp_re_ref[q] = jnp.zeros((8, 128), F32)
    hp_im_ref[q] = jnp.zeros((8, 128), F32)
    for b in range(BATCH):
        last = SCAN_PAD + (b + 1) * CHUNKS_PER_SEQ - 1
        hp_re_ref[q, b:b + 1, :] = hre_scr[last:last + 1, :]
        hp_im_ref[q, b:b + 1, :] = him_scr[last:last + 1, :]
    prev_ok = kk >= 1
    hprev_re = jnp.where(prev_ok, hre_scr[SCAN_PAD - 1:SCAN_PAD - 1 + P_CHUNK_ROWS, :], 0.0)
    hprev_im = jnp.where(prev_ok, him_scr[SCAN_PAD - 1:SCAN_PAD - 1 + P_CHUNK_ROWS, :], 0.0)
    carry_p = _dot_nt(hprev_re.astype(BF16), qre_ref[q]) + _dot_nt(hprev_im.astype(BF16), qim_ref[q])

    ubs = ub[P_CHUNK_ROWS:]
    h0r = h0re_ref[q]
    h0i = h0im_ref[q]
    l4r = lam4_ref[q, 0:1, :]
    l4i = lam4_ref[q, 1:2, :]
    hs_re_ref[q] = l4r * h0r - l4i * h0i + _dot(ubs, p4re_ref[q])
    hs_im_ref[q] = l4r * h0i + l4i * h0r + _dot(ubs, p4im_ref[q])
    carry_s = _dot_nt(h0r.astype(BF16), qre_ref[q]) + _dot_nt(h0i.astype(BF16), qim_ref[q])

    y = y + jnp.concatenate([carry_p, carry_s], axis=0) + dskip_ref[q] * u
    return jax.nn.gelu(y)


N_SSM_IN, N_SSM_OUT = 13, 5


def _ssm_kernel(u_ref, *all_refs):
    n_cast = (len(all_refs) - N_SSM_IN - N_SSM_OUT - 2) // 2
    _cast_slices(all_refs[N_SSM_IN:N_SSM_IN + n_cast], all_refs[N_SSM_IN + n_cast + N_SSM_OUT:-2])
    refs = all_refs[:N_SSM_IN] + all_refs[N_SSM_IN + n_cast:N_SSM_IN + n_cast + N_SSM_OUT] + all_refs[-2:]
    z_ref = refs[13]
    hre_scr, him_scr = refs[18], refs[19]
    c = SSM_GROUP_CH
    hre_scr[0:SCAN_PAD, :] = jnp.zeros((SCAN_PAD, 128), F32)
    him_scr[0:SCAN_PAD, :] = jnp.zeros((SCAN_PAD, 128), F32)
    xt = []
    for i in range(CHUNK):
        xp = u_ref[pl.ds(i, P_CHUNK_ROWS, stride=CHUNK), :]
        if i < DEC_SEQ:
            xs = u_ref[RP + i * DEC_BATCH:RP + (i + 1) * DEC_BATCH, :]
        else:
            xs = jnp.zeros((DEC_BATCH, 128), F32)
        xt.append(jnp.concatenate([xp, xs], axis=0).T)
    def pair_input(q):
        halves = []
        for e in range(2):
            g = 2 * q + e
            bt = jnp.concatenate([xt[i][g * c:(g + 1) * c, :] for i in range(CHUNK)], axis=0)
            halves.append(bt.T)
        return jnp.concatenate(halves, axis=1)

    zt = []
    u_pair = pair_input(0)
    for q in range(SLAB_PAIRS):
        u_next = pair_input(q + 1) if q + 1 < SLAB_PAIRS else None
        z = _ssm_pair(q, u_pair, *refs[:13], *refs[14:])
        zt.append(z[:, :PAIR_W // 2].T)
        zt.append(z[:, PAIR_W // 2:].T)
        u_pair = u_next
    for i in range(CHUNK):
        zi = jnp.concatenate([zt[g][i * c:(i + 1) * c, :] for g in range(SLAB_GROUPS)], axis=0).T
        z_ref[pl.ds(i, P_CHUNK_ROWS, stride=CHUNK), :] = zi[:P_CHUNK_ROWS]
        if i < DEC_SEQ:
            z_ref[RP + i * DEC_BATCH:RP + (i + 1) * DEC_BATCH, :] = zi[P_CHUNK_ROWS:]


def _ssm(u_ssm, prep, h0re, h0im, to_cast):
    sp = SLAB_PAIRS
    slab3 = lambda a, b: pl.BlockSpec((sp, a, b), lambda s: (s, 0, 0))
    cast_specs, cast_shapes = _cast_specs(to_cast, N_SLABS)
    outs = pl.pallas_call(
        _ssm_kernel,
        out_shape=(
            jax.ShapeDtypeStruct((R, D_SSM), F32),
            jax.ShapeDtypeStruct((N_PAIRS, 8, 128), F32),
            jax.ShapeDtypeStruct((N_PAIRS, 8, 128), F32),
            jax.ShapeDtypeStruct((N_PAIRS, DEC_BATCH, 128), F32),
            jax.ShapeDtypeStruct((N_PAIRS, DEC_BATCH, 128), F32),
            *cast_shapes,
        ),
        grid=(N_SLABS,),
        in_specs=[
            pl.BlockSpec((R, 128), lambda s: (0, s)),
            pl.BlockSpec((sp, 2, 256, 256), lambda s: (s, 0, 0, 0)),
            slab3(PAIR_W, 128), slab3(PAIR_W, 128), slab3(PAIR_W, 128), slab3(PAIR_W, 128),
            slab3(PAIR_W, 128), slab3(PAIR_W, 128),
            slab3(8, 128), slab3(8, 128), slab3(8, 128),
            slab3(1, PAIR_W),
            slab3(DEC_BATCH, 128), slab3(DEC_BATCH, 128),
            *cast_specs,
        ],
        out_specs=(
            pl.BlockSpec((R, 128), lambda s: (0, s)),
            slab3(8, 128), slab3(8, 128), slab3(DEC_BATCH, 128), slab3(DEC_BATCH, 128),
            *cast_specs,
        ),
        scratch_shapes=[pltpu.VMEM((SCAN_PAD + P_CHUNK_ROWS, 128), F32),
                        pltpu.VMEM((SCAN_PAD + P_CHUNK_ROWS, 128), F32)],
        compiler_params=_cparams(("arbitrary",)),
        name="ssm",
    )(u_ssm, prep["t"], prep["pre"], prep["pim"], prep["p4re"], prep["p4im"],
      prep["qre"], prep["qim"], prep["apow_re"], prep["apow_im"], prep["lam4"],
      prep["dskip"], h0re, h0im, *to_cast)
    return outs[:N_SSM_OUT], outs[N_SSM_OUT:]


def _cmul(ar, ai, br, bi):
    return ar * br - ai * bi, ar * bi + ai * br


def _dot3(a, b):
    ah = a.astype(BF16)
    bh = b.astype(BF16)
    al = (a - ah.astype(F32)).astype(BF16)
    bl = (b - bh.astype(F32)).astype(BF16)
    return _dot(ah, bh) + _dot(ah, bl) + _dot(al, bh)


def _dot3_nt(a, b):
    nt = _dot_nt
    ah = a.astype(BF16)
    bh = b.astype(BF16)
    al = (a - ah.astype(F32)).astype(BF16)
    bl = (b - bh.astype(F32)).astype(BF16)
    return nt(ah, bh) + nt(ah, bl) + nt(al, bh)


def _ssm_prep_pair(pp, lam_re_ref, lam_im_ref, lstep_ref, bt_re_ref, bt_im_ref, c_re_ref, c_im_ref,
                   t_ref, pre_ref, pim_ref, p4re_ref, p4im_ref, qre_ref, qim_ref,
                   apr_ref, api_ref, lam4_ref):
    L, C, N = CHUNK, SSM_GROUP_CH, SSM_STATE
    w = L * C
    p_re, p_im, p4_re, p4_im, q_re, q_im, a_re, a_im, l4 = [], [], [], [], [], [], [], [], []
    for e in range(2):
        ge = 2 * pp + e
        dt = jnp.exp(lstep_ref[ge])
        lr, li = lam_re_ref[ge], lam_im_ref[ge]
        mag = jnp.exp(lr * dt)
        ang = li * dt
        zr, zi = mag * jnp.cos(ang), mag * jnp.sin(ang)
        den = lr * lr + li * li
        fr = ((zr - 1.0) * lr + zi * li) / den
        fi = (zi * lr - (zr - 1.0) * li) / den
        bbr, bbi = _cmul(fr, fi, bt_re_ref[ge], bt_im_ref[ge])
        pr, pi = [jnp.ones((1, N), F32)], [jnp.zeros((1, N), F32)]
        for _ in range(L):
            nr, ni = _cmul(pr[-1], pi[-1], zr, zi)
            pr.append(nr)
            pi.append(ni)
        stack = lambda blocks, part: jnp.concatenate([b[part] for b in blocks], axis=0)
        blocks = [_cmul(bbr, bbi, pr[L - 1 - i], pi[L - 1 - i]) for i in range(L)]
        p_re.append(stack(blocks, 0))
        p_im.append(stack(blocks, 1))
        blocks4 = [_cmul(bbr, bbi, pr[DEC_SEQ - 1 - i], pi[DEC_SEQ - 1 - i]) for i in range(DEC_SEQ)]
        pad = jnp.zeros(((L - DEC_SEQ) * C, N), F32)
        p4_re.append(jnp.concatenate([stack(blocks4, 0), pad], axis=0))
        p4_im.append(jnp.concatenate([stack(blocks4, 1), pad], axis=0))
        sr, si = [pr[L]], [pi[L]]
        for _ in range(N_SCAN_STEPS - 1):
            nr, ni = _cmul(sr[-1], si[-1], sr[-1], si[-1])
            sr.append(nr)
            si.append(ni)
        a_re.append(jnp.concatenate(sr + [jnp.zeros((8 - N_SCAN_STEPS, N), F32)], axis=0))
        a_im.append(jnp.concatenate(si + [jnp.zeros((8 - N_SCAN_STEPS, N), F32)], axis=0))
        l4.append(jnp.concatenate([pr[DEC_SEQ], pi[DEC_SEQ], jnp.zeros((6, N), F32)], axis=0))

        cr, ci = c_re_ref[ge], c_im_ref[ge]
        ck = [_cmul(cr, ci, pr[k], pi[k]) for k in range(L + 1)]
        q_re.append(stack(ck[1:], 0))
        q_im.append(-stack(ck[1:], 1))
        v = _dot3_nt(bbr, stack(ck[:L], 0)) - _dot3_nt(bbi, stack(ck[:L], 1))
        lane = lax.broadcasted_iota(jnp.int32, (C, w), 1)
        rows = [v] + [jnp.where(lane >= C * i, pltpu.roll(v, C * i, axis=1), 0.0) for i in range(1, L)]
        t_ref[pp, e] = jnp.concatenate(rows, axis=0).astype(BF16)

    def diag_rows(m):
        z = jnp.zeros_like(m[0])
        return jnp.concatenate([jnp.concatenate([m[0], z], axis=1),
                                jnp.concatenate([z, m[1]], axis=1)], axis=0)

    pre_ref[pp] = diag_rows(p_re).astype(BF16)
    pim_ref[pp] = diag_rows(p_im).astype(BF16)
    p4re_ref[pp] = diag_rows(p4_re).astype(BF16)
    p4im_ref[pp] = diag_rows(p4_im).astype(BF16)
    qre_ref[pp] = diag_rows(q_re).astype(BF16)
    qim_ref[pp] = diag_rows(q_im).astype(BF16)
    apr_ref[pp] = jnp.concatenate(a_re, axis=1)
    api_ref[pp] = jnp.concatenate(a_im, axis=1)
    lam4_ref[pp] = jnp.concatenate(l4, axis=1)


def _ssm_prep_args(lambda_re, lambda_im, log_step, b_re, b_im, c_re, c_im):
    return (lambda_re[:, None, :], lambda_im[:, None, :], log_step[:, None, None],
            b_re.transpose(0, 2, 1), b_im.transpose(0, 2, 1), c_re, c_im)


def _mix_kernel(a_ref, z_ref, xp_ref, xs_ref, wglu_ref, bglu_ref, wout_ref, g_ref, b_ref, *refs):
    n_cast = (len(refs) - 2) // 2
    o_ref, ob_ref = refs[n_cast], refs[n_cast + 1]
    _cast_slices(refs[:n_cast], refs[n_cast + 2:])
    is_prompt = pl.program_id(0) < NBP
    n_sub = TM // SUB

    def glu(r):
        z = z_ref[r * SUB:(r + 1) * SUB, :]
        gate = _dot(z.astype(BF16), wglu_ref[...]) + bglu_ref[...]
        return (z * jax.nn.sigmoid(gate)).astype(BF16)

    bmix = glu(0)
    for r in range(n_sub):
        rows = slice(r * SUB, (r + 1) * SUB)
        bmix_next = glu(r + 1) if r + 1 < n_sub else None
        mix = _dot(a_ref[rows, :], wout_ref[0:D_POOL, :]) + _dot(bmix, wout_ref[D_POOL:, :])
        x = jnp.where(is_prompt, xp_ref[rows, :], xs_ref[rows, :])
        h1 = _layer_norm(ALPHA * x + mix, g_ref[...], b_ref[...])
        o_ref[rows, :] = h1
        ob_ref[rows, :] = h1.astype(BF16)
        bmix = bmix_next


def _mix(a, z, xp, xs, w_glu, b_glu, w_out, g, b, to_cast):
    row = lambda w: pl.BlockSpec((TM, w), lambda i: (i, 0))
    cast_specs, cast_shapes = _cast_specs(to_cast, NBP)
    outs = pl.pallas_call(
        _mix_kernel,
        out_shape=(jax.ShapeDtypeStruct((R, D_MODEL), F32), jax.ShapeDtypeStruct((R, D_MODEL), BF16), *cast_shapes),
        grid=(NB,),
        in_specs=[
            row(D_POOL), row(D_SSM),
            pl.BlockSpec((TM, D_MODEL), lambda i: (jnp.minimum(i, NBP - 1), 0)),
            _const_spec((RS, D_MODEL)),
            _const_spec((D_SSM, D_SSM)), _const_spec((1, D_SSM)),
            _const_spec((D_MODEL, D_MODEL)), _const_spec((1, D_MODEL)), _const_spec((1, D_MODEL)),
            *cast_specs,
        ],
        out_specs=(row(D_MODEL), row(D_MODEL), *cast_specs),
        compiler_params=_cparams(("arbitrary",)),
        name="mix_ln1",
    )(a, z, xp, xs, w_glu, b_glu, w_out, g, b, *to_cast)
    return outs[0], outs[1], outs[2:]


def _proj_kernel(x_ref, w_ref, o_ref):
    o_ref[...] = _dot(x_ref[...], w_ref[...]).astype(o_ref.dtype)


def _qproj_sample(h, w):
    return pl.pallas_call(
        _proj_kernel,
        out_shape=jax.ShapeDtypeStruct((RS, D_MODEL), BF16),
        grid=(1,),
        in_specs=[pl.BlockSpec((RS, D_MODEL), lambda i: (RP // RS, 0)),
                  pl.BlockSpec((D_MODEL, D_MODEL), lambda i: (0, 0))],
        out_specs=pl.BlockSpec((RS, D_MODEL), lambda i: (0, 0)),
        compiler_params=_cparams(("arbitrary",)),
        name="qproj_sample",
    )(h, w)


def _memproj_kernel(m_ref, w_ref, o_ref, ob_ref, wb_scr):
    @pl.when(pl.program_id(0) == 0)
    def _():
        wb_scr[...] = w_ref[...].astype(BF16)

    k = _dot(m_ref[...].astype(BF16), wb_scr[...])
    ob_ref[...] = k.astype(BF16)
    for h in range(N_XHEADS):
        o_ref[0, 0, :, h, :] = k[:, h * XHEAD_DIM:(h + 1) * XHEAD_DIM]


def _memproj(mem, w, name):
    rows = BATCH * N_MEM
    seq = pl.BlockSpec((N_MEM, D_MODEL), lambda b: (b, 0))
    return pl.pallas_call(
        _memproj_kernel,
        out_shape=(jax.ShapeDtypeStruct((1, BATCH, N_MEM, N_XHEADS, XHEAD_DIM), F32),
                   jax.ShapeDtypeStruct((rows, D_MODEL), BF16)),
        grid=(BATCH,),
        in_specs=[seq, _const_spec((D_MODEL, D_MODEL))],
        out_specs=(pl.BlockSpec((1, 1, N_MEM, N_XHEADS, XHEAD_DIM), lambda b: (0, b, 0, 0, 0)), seq),
        scratch_shapes=[pltpu.VMEM((D_MODEL, D_MODEL), BF16)],
        compiler_params=_cparams(("arbitrary",)),
        name=name,
    )(mem, w)


def _attend(q, k, v, between=None):
    def scores(h):
        sl = slice(h * XHEAD_DIM, (h + 1) * XHEAD_DIM)
        return lax.dot_general(q[:, sl], k[:, sl], (((1,), (1,)), ((), ())),
                               preferred_element_type=F32) * (XHEAD_DIM ** -0.5)

    def head_out(h, s):
        s = s - jnp.max(s, axis=-1, keepdims=True)
        e = jnp.exp(s)
        p = e / jnp.sum(e, axis=-1, keepdims=True)
        return _dot(p.astype(BF16), v[:, h * XHEAD_DIM:(h + 1) * XHEAD_DIM])

    outs = []
    s = scores(0)
    for h in range(N_XHEADS):
        if between is not None:
            between(h)
        s_next = scores(h + 1) if h + 1 < N_XHEADS else None
        outs.append(head_out(h, s))
        s = s_next
    return jnp.concatenate(outs, axis=1)


def _attn_block_kernel(h_ref, q_ref, k_ref, v_ref, os_ref, wo_ref, g_ref, b_ref, *refs):
    n_cast = (len(refs) - 1) // 2
    o_ref = refs[n_cast]
    _cast_slices(refs[:n_cast], refs[n_cast + 1:])
    i = pl.program_id(0)
    n_sub = TM // SUB

    def normed(rows, y):
        o_ref[rows, :] = _layer_norm(ALPHA * h_ref[rows, :] + y, g_ref[...], b_ref[...])

    @pl.when(i < NBP)
    def _():
        prev_rows, prev_o = None, None
        for r in range(n_sub):
            rows = slice(r * SUB, (r + 1) * SUB)
            ys = []
            between = None
            if prev_o is not None:
                between = lambda hh, po=prev_o: ys.append(
                    _dot(po, wo_ref[:, hh * XHEAD_DIM:(hh + 1) * XHEAD_DIM]))
            o = _attend(q_ref[rows, :], k_ref[...], v_ref[...], between).astype(BF16)
            if prev_o is not None:
                normed(prev_rows, jnp.concatenate(ys, axis=1))
            prev_rows, prev_o = rows, o
        normed(prev_rows, _dot(prev_o, wo_ref[...]))

    @pl.when(i == NBP)
    def _():
        for r in range(n_sub):
            rows = slice(r * SUB, (r + 1) * SUB)
            normed(rows, _dot(os_ref[rows, :], wo_ref[...]))


def _attn_block(h, q_prompt, kb, vb, o_sample, wo, g, b, to_cast):
    kv = pl.BlockSpec((N_MEM, D_MODEL), lambda i: (jnp.minimum(i // BLOCKS_PER_SEQ, BATCH - 1), 0))
    row = pl.BlockSpec((TM, D_MODEL), lambda i: (i, 0))
    qrow = pl.BlockSpec((TM, D_MODEL), lambda i: (jnp.minimum(i, NBP - 1), 0))
    cast_specs, cast_shapes = _cast_specs(to_cast, NBP)
    outs = pl.pallas_call(
        _attn_block_kernel,
        out_shape=(jax.ShapeDtypeStruct((R, D_MODEL), F32), *cast_shapes),
        grid=(NB,),
        in_specs=[row, qrow, kv, kv, _const_spec((RS, D_MODEL)), _const_spec((D_MODEL, D_MODEL)),
                  _const_spec((1, D_MODEL)), _const_spec((1, D_MODEL)), *cast_specs],
        out_specs=(row, *cast_specs),
        compiler_params=_cparams(("arbitrary",)),
        name="attn_block",
    )(h, q_prompt, kb, vb, o_sample, wo, g, b, *to_cast)
    return outs[0], outs[1:]


ATT_BB = 4
Q_PAD = 8
KV_ROWS = N_MEM * N_XHEADS


QP_ROWS = RP // (DEC_BATCH // ATT_BB)


def _attn_sample_kernel(q_ref, k_ref, v_ref, h_ref, wq_ref, o_ref, qp_ref):
    hb = h_ref[...]
    qcols = D_MODEL // ATT_BB
    shape = (N_XHEADS * Q_PAD, KV_ROWS)
    same_head = (lax.broadcasted_iota(jnp.int32, shape, 0) // Q_PAD
                 == lax.broadcasted_iota(jnp.int32, shape, 1) % N_XHEADS)
    def scores(b):
        q = q_ref[b].astype(F32)
        qs = jnp.concatenate([q[:, h * XHEAD_DIM:(h + 1) * XHEAD_DIM] for h in range(N_XHEADS)], axis=0)
        k = k_ref[0, b].reshape(KV_ROWS, XHEAD_DIM).astype(BF16)
        return lax.dot_general(qs.astype(BF16), k, (((1,), (1,)), ((), ())),
                               preferred_element_type=F32) * (XHEAD_DIM ** -0.5)

    def attend(b, s):
        s = jnp.where(same_head, s, -1e30)
        s = s - jnp.max(s, axis=-1, keepdims=True)
        e = jnp.exp(s)
        p = e / jnp.sum(e, axis=-1, keepdims=True)
        v = v_ref[0, b].reshape(KV_ROWS, XHEAD_DIM).astype(BF16)
        o = _dot(p.astype(BF16), v)
        for h in range(N_XHEADS):
            o_ref[b, :, h * XHEAD_DIM:(h + 1) * XHEAD_DIM] = o[h * Q_PAD:(h + 1) * Q_PAD]

    s_prev = scores(0)
    for b in range(ATT_BB):
        cols = slice(b * qcols, (b + 1) * qcols)
        qp_ref[:, cols] = _dot(hb, wq_ref[:, cols]).astype(BF16)
        s_next = scores(b + 1) if b + 1 < ATT_BB else None
        attend(b, s_prev)
        s_prev = s_next


def _attn_sample(q_pad, k, v, h, wq):
    kv = pl.BlockSpec((1, ATT_BB, N_MEM, N_XHEADS, XHEAD_DIM), lambda i: (0, i, 0, 0, 0))
    qo = pl.BlockSpec((ATT_BB, Q_PAD, D_MODEL), lambda i: (i, 0, 0))
    hq = pl.BlockSpec((QP_ROWS, D_MODEL), lambda i: (i, 0))
    return pl.pallas_call(
        _attn_sample_kernel,
        out_shape=(jax.ShapeDtypeStruct((DEC_BATCH, Q_PAD, D_MODEL), F32),
                   jax.ShapeDtypeStruct((RP, D_MODEL), BF16)),
        grid=(DEC_BATCH // ATT_BB,),
        in_specs=[qo, kv, kv, hq, _const_spec((D_MODEL, D_MODEL))],
        out_specs=(qo, hq),
        compiler_params=_cparams(("arbitrary",)),
        name="attn_sample",
    )(q_pad, k, v, h, wq)


FTM = 1024
FNB = RP // FTM
FBLOCKS_PER_SEQ = SEQ // FTM
FSUB = 512
FSUB_LAST = 128
FSUB_SAMPLE = 256


def _ffn_rows(n_rows, sub, conv_taps, h_ref, hb_scr, g_scr, g_base, wg, wu, wd, cw_ref, cb_ref,
              lng_ref, lnb_ref, y_ref, first, last):
    def gate_up(lo):
        if first:
            h = h_ref[lo:lo + sub, :]
            hb = h.astype(BF16)
            hb_scr[lo:lo + sub, :] = hb
        else:
            h, hb = None, hb_scr[lo:lo + sub, :]
        g = _dot(hb, wg)
        up = _dot(hb, wu)
        g_scr[g_base + lo:g_base + lo + sub, :] = g
        tap0, tap1 = conv_taps(lo, sub)
        gc = cb_ref[...] + cw_ref[0:1, :] * tap0 + cw_ref[1:2, :] * tap1 + cw_ref[2:3, :] * g
        return h, (jax.nn.silu(gc) * up).astype(BF16)

    def down(lo, h, act):
        y = _dot(act, wd) + (ALPHA * h if first else y_ref[lo:lo + sub, :])
        y_ref[lo:lo + sub, :] = _layer_norm(y, lng_ref[...], lnb_ref[...]) if last else y

    n_sub = n_rows // sub
    pending = gate_up(0)
    for r in range(n_sub):
        nxt = gate_up((r + 1) * sub) if r + 1 < n_sub else None
        down(r * sub, *pending)
        pending = nxt


def _ffn_chunk(f, run):
    @pl.when(f == 0)
    def _():
        run(True, False)

    @pl.when(jnp.logical_and(f > 0, f < NF - 1))
    def _():
        run(False, False)

    @pl.when(f == NF - 1)
    def _():
        run(False, True)


def _ffn_prompt_kernel(h_ref, wg_ref, wu_ref, wd_ref, cw_ref, cb_ref, lng_ref, lnb_ref,
                       y_ref, gtail_ref, hb_scr, g_scr, carry_scr):
    i = pl.program_id(0)
    f = pl.program_id(1)
    first_block = (i % FBLOCKS_PER_SEQ) == 0

    @pl.when(first_block)
    def _():
        g_scr[0:8, :] = jnp.zeros((8, TF), F32)

    @pl.when(jnp.logical_not(first_block))
    def _():
        g_scr[0:8, :] = carry_scr[f]

    taps = lambda lo, sub: (g_scr[6 + lo:6 + lo + sub, :], g_scr[7 + lo:7 + lo + sub, :])
    _ffn_chunk(f, lambda first, last: _ffn_rows(
        FTM, FSUB_LAST if last else FSUB, taps, h_ref, hb_scr, g_scr, 8, wg_ref[...], wu_ref[...], wd_ref[...], cw_ref, cb_ref,
        lng_ref, lnb_ref, y_ref, first, last))
    tail = g_scr[FTM:FTM + 8, :]
    carry_scr[f] = tail
    gtail_ref[0] = tail


def _ffn_sample_kernel(h_ref, wg_ref, wu_ref, wd_ref, cw_ref, cb_ref, lng_ref, lnb_ref, st_ref,
                       y_ref, gnew_ref, wdb_ref, hb_scr, g_scr):
    f = pl.program_id(1)
    n_st = 2 * DEC_BATCH
    wdb_ref[...] = wd_ref[...].astype(BF16)
    g_scr[0:n_st, :] = st_ref[...]
    taps = lambda lo, sub: (g_scr[lo:lo + sub, :], g_scr[DEC_BATCH + lo:DEC_BATCH + lo + sub, :])
    _ffn_chunk(f, lambda first, last: _ffn_rows(
        RS, FSUB_SAMPLE, taps, h_ref, hb_scr, g_scr, n_st, wg_ref[...], wu_ref[...], wdb_ref[...], cw_ref, cb_ref,
        lng_ref, lnb_ref, y_ref, first, last))
    gnew_ref[...] = g_scr[RS:RS + n_st, :]


def _ffn_specs(tm, row_block_of):
    return [
        pl.BlockSpec((tm, D_MODEL), lambda i, f: (row_block_of(i), 0)),
        pl.BlockSpec((D_MODEL, TF), lambda i, f: (0, f)),
        pl.BlockSpec((D_MODEL, TF), lambda i, f: (0, f)),
        pl.BlockSpec((TF, D_MODEL), lambda i, f: (f, 0)),
        pl.BlockSpec((CONV_W, TF), lambda i, f: (0, f)),
        pl.BlockSpec((1, TF), lambda i, f: (0, f)),
        pl.BlockSpec((1, D_MODEL), lambda i, f: (0, 0)),
        pl.BlockSpec((1, D_MODEL), lambda i, f: (0, 0)),
    ]


def _ffn_prompt(h, wg, wu, wd, cw, cb, lng, lnb):
    return pl.pallas_call(
        _ffn_prompt_kernel,
        out_shape=(jax.ShapeDtypeStruct((RP, D_MODEL), F32),
                   jax.ShapeDtypeStruct((FNB, 8, D_FF), F32)),
        grid=(FNB, NF),
        in_specs=_ffn_specs(FTM, lambda i: i),
        out_specs=(pl.BlockSpec((FTM, D_MODEL), lambda i, f: (i, 0)),
                   pl.BlockSpec((1, 8, TF), lambda i, f: (i, 0, f))),
        scratch_shapes=[pltpu.VMEM((FTM, D_MODEL), BF16),
                        pltpu.VMEM((FTM + 8, TF), F32),
                        pltpu.VMEM((NF, 8, TF), F32)],
        compiler_params=_cparams(("arbitrary", "arbitrary"), VMEM_LIMIT_FFN),
        name="ffn_prompt",
    )(h, wg, wu, wd, cw, cb, lng, lnb)


def _ffn_sample(h, wg, wu, wd, cw, cb, lng, lnb, conv_state):
    n_st = 2 * DEC_BATCH
    specs = _ffn_specs(RS, lambda i: RP // RS)
    return pl.pallas_call(
        _ffn_sample_kernel,
        out_shape=(jax.ShapeDtypeStruct((RS, D_MODEL), F32),
                   jax.ShapeDtypeStruct((n_st, D_FF), F32),
                   jax.ShapeDtypeStruct((D_FF, D_MODEL), BF16)),
        grid=(1, NF),
        in_specs=specs + [pl.BlockSpec((n_st, TF), lambda i, f: (0, f))],
        out_specs=(pl.BlockSpec((RS, D_MODEL), lambda i, f: (0, 0)),
                   pl.BlockSpec((n_st, TF), lambda i, f: (0, f)),
                   specs[3]),
        scratch_shapes=[pltpu.VMEM((RS, D_MODEL), BF16),
                        pltpu.VMEM((n_st + RS, TF), F32)],
        compiler_params=_cparams(("arbitrary", "arbitrary")),
        name="ffn_sample",
    )(h, wg, wu, wd, cw, cb, lng, lnb, conv_state)


def _state_to_pairs(s):
    return s.reshape(s.shape[0], N_PAIRS, 128).transpose(1, 0, 2)


def _state_from_pairs(s):
    return s.transpose(1, 0, 2).reshape(1, s.shape[1], N_SSM_GROUPS, SSM_STATE)


def kernel(x_prompt, x_sample, mem_prompt, state_pool, state_ssm_re, state_ssm_im, state_conv, cache_mem_k, cache_mem_v, w_in, w_pool, pool_scale, lambda_re, lambda_im, log_step, b_re, b_im, c_re, c_im, d_skip, w_glu, b_glu, w_out, ln1_g, ln1_b, w_q, w_k, w_v, w_o, ln2_g, ln2_b, w_gate, w_up, conv_w, conv_b, w_down, ln3_g, ln3_b):
    bf = lambda w: w[0].astype(BF16)
    row = lambda v: v[0].reshape(1, -1)
    xp = x_prompt.reshape(RP, D_MODEL)
    xs = x_sample.transpose(1, 0, 2).reshape(RS, D_MODEL)

    prep_args = _ssm_prep_args(lambda_re[0], lambda_im[0], log_step[0], b_re[0], b_im[0], c_re[0], c_im[0])
    (u_ssm, a, utail, us_pool), prep, (w_glu_b, w_out_b) = _front(
        xp, xs, bf(w_in), state_pool[0].transpose(1, 0, 2), bf(w_pool), row(pool_scale), prep_args,
        (w_glu[0], w_out[0]))
    prep["dskip"] = jnp.broadcast_to(d_skip[0].reshape(N_PAIRS, 2, 1, SSM_GROUP_CH),
                                     (N_PAIRS, 2, CHUNK, SSM_GROUP_CH)).reshape(N_PAIRS, 1, PAIR_W)
    (z, hp_re, hp_im, hs_re, hs_im), (w_q_b, w_o_b) = _ssm(
        u_ssm, prep, _state_to_pairs(state_ssm_re[0]), _state_to_pairs(state_ssm_im[0]), (w_q[0], w_o[0]))
    h1, h1b, (wg,) = _mix(a, z, xp, xs, w_glu_b, row(b_glu), w_out_b, row(ln1_g), row(ln1_b), (w_gate[0],))

    mem = mem_prompt.reshape(BATCH * N_MEM, D_MODEL)
    mk, mkb = _memproj(mem, w_k[0], "mem_k")
    mv, mvb = _memproj(mem, w_v[0], "mem_v")
    q_s = _qproj_sample(h1b, w_q_b).reshape(DEC_SEQ, DEC_BATCH, D_MODEL).transpose(1, 0, 2)
    q_s = jnp.pad(q_s, ((0, 0), (0, Q_PAD - DEC_SEQ), (0, 0)))
    o_s, q_p = _attn_sample(q_s, cache_mem_k, cache_mem_v, h1b, w_q_b)
    o_s = o_s[:, :DEC_SEQ].transpose(1, 0, 2).reshape(RS, D_MODEL).astype(BF16)
    h2, (wu,) = _attn_block(h1, q_p, mkb, mvb, o_s, w_o_b, row(ln2_g), row(ln2_b), (w_up[0],))

    cw, cb = conv_w[0], row(conv_b)
    conv_st = state_conv[0].transpose(1, 0, 2).reshape(2 * DEC_BATCH, D_FF)
    y_s, g_new, wd = _ffn_sample(h2, wg, wu, w_down[0], cw, cb, row(ln3_g), row(ln3_b), conv_st)
    y_p, gtail = _ffn_prompt(h2, wg, wu, wd, cw, cb, row(ln3_g), row(ln3_b))

    y_prompt = y_p.reshape(BATCH, SEQ, D_MODEL)
    y_sample = y_s.reshape(DEC_SEQ, DEC_BATCH, D_MODEL).transpose(1, 0, 2)
    p_pool = utail[BLOCKS_PER_SEQ - 1:NBP:BLOCKS_PER_SEQ, 16 - POOL_BUF:][None]
    s_ext = jnp.concatenate([state_pool[0], us_pool.reshape(DEC_SEQ, DEC_BATCH, D_POOL).transpose(1, 0, 2)], axis=1)
    s_pool = s_ext[None, :, DEC_SEQ:]
    p_conv = gtail[FBLOCKS_PER_SEQ - 1::FBLOCKS_PER_SEQ, 6:8][None]
    s_conv = g_new.reshape(2, DEC_BATCH, D_FF).transpose(1, 0, 2)[None]
    return (y_prompt, y_sample,
            p_pool, _state_from_pairs(hp_re[:, :BATCH]), _state_from_pairs(hp_im[:, :BATCH]), p_conv,
            mk, mv,
            s_pool, _state_from_pairs(hs_re), _state_from_pairs(hs_im), s_conv)
```

```python
import functools
import math

import jax
import jax.numpy as jnp
from jax import lax
from jax.experimental import pallas as pl
from jax.experimental.pallas import tpu as pltpu

F32 = jnp.float32
BF16 = jnp.bfloat16

D_MODEL = 2048
BATCH = 4
SEQ = 2048
DEC_BATCH = 128
DEC_SEQ = 4
PAST_LEN = 16384
D_POOL = 1024
D_SSM = 1024
POOL_WINDOWS = (2, 4, 8, 16)
POOL_GROUP_DIM = 256
POOL_BUF = 15
SSM_GROUP_CH = 16
N_SSM_GROUPS = 64
SSM_STATE = 64
N_MEM = 256
N_XHEADS = 4
XHEAD_DIM = 512
D_FF = 5632
CONV_W = 3
ALPHA = 2.0 ** 0.25
LN_EPS = 1e-5

RP = BATCH * SEQ
RS = DEC_BATCH * DEC_SEQ
R = RP + RS
TM = 512
NB = R // TM
NBP = RP // TM
BLOCKS_PER_SEQ = SEQ // TM

CHUNK = 16
N_PAIRS = N_SSM_GROUPS // 2
PAIR_W = 2 * CHUNK * SSM_GROUP_CH
CHUNKS_PER_SEQ = SEQ // CHUNK
P_CHUNK_ROWS = BATCH * CHUNKS_PER_SEQ
SSM_ROWS = P_CHUNK_ROWS + DEC_BATCH
SCAN_PAD = 64
N_SCAN_STEPS = 7

SUB = 256
TF = 512
NF = D_FF // TF

VMEM_LIMIT = 56 * 1024 * 1024
VMEM_LIMIT_FFN = 60 * 1024 * 1024


def _cparams(sem, vmem_limit=VMEM_LIMIT):
    return pltpu.CompilerParams(dimension_semantics=sem, vmem_limit_bytes=vmem_limit)


def _const_spec(shape):
    n = len(shape)
    return pl.BlockSpec(shape, lambda *_: (0,) * n, pipeline_mode=pl.Buffered(1))


def _layer_norm(x, g, b):
    mu = jnp.mean(x, axis=-1, keepdims=True)
    xc = x - mu
    var = jnp.mean(xc * xc, axis=-1, keepdims=True)
    return xc * lax.rsqrt(var + LN_EPS) * g + b


def _dot(a, b):
    return jnp.dot(a, b, preferred_element_type=F32)


def _dot_nt(a, b):
    return lax.dot_general(a, b, (((1,), (1,)), ((), ())), preferred_element_type=F32)


def _cast_spec(a, n_steps):
    return pl.BlockSpec((a.shape[0] // n_steps, a.shape[1]), lambda i: (jnp.minimum(i, n_steps - 1), 0))


def _cast_specs(arrays, n_steps):
    specs = [_cast_spec(a, n_steps) for a in arrays]
    return specs, [jax.ShapeDtypeStruct(a.shape, BF16) for a in arrays]


def _cast_slices(in_refs, out_refs):
    for src_ref, dst_ref in zip(in_refs, out_refs):
        dst_ref[...] = src_ref[...].astype(BF16)


N_PREP_IN, N_PREP_OUT = 7, 10
PREP_PAIRS = N_PAIRS // NBP


def _pool_mix(g, pooled, w_ref, scale_ref):
    sl = slice(g * POOL_GROUP_DIM, (g + 1) * POOL_GROUP_DIM)
    return (_dot(pooled.astype(BF16), w_ref[g]) * scale_ref[:, sl]).astype(BF16)


def _front_kernel(xp_ref, xs_ref, w_ref, st_ref, wp_ref, scale_ref, *refs):
    prep_in, refs = refs[:N_PREP_IN], refs[N_PREP_IN:]
    n_cast = (len(refs) - 4 - N_PREP_OUT - 1) // 2
    cast_in, refs = refs[:n_cast], refs[n_cast:]
    ussm_ref, a_ref, utail_ref, us_ref = refs[:4]
    prep_out, cast_out, ext_ref = refs[4:4 + N_PREP_OUT], refs[4 + N_PREP_OUT:-1], refs[-1]
    i = pl.program_id(0)

    xb = jnp.where(i < NBP, xp_ref[...], xs_ref[...]).astype(BF16)

    first = (i % BLOCKS_PER_SEQ) == 0
    ext_ref[0:16, :] = jnp.where(first, 0.0, ext_ref[TM:TM + 16, :])
    ext_ref[16:16 + TM, :] = _dot(xb, w_ref[:, 0:D_POOL])
    utail_ref[0] = ext_ref[TM:TM + 16, :]
    pos = (i % BLOCKS_PER_SEQ) * TM + lax.broadcasted_iota(jnp.int32, (TM, 1), 0)
    qc = D_SSM // len(POOL_WINDOWS)
    for g, w in enumerate(POOL_WINDOWS):
        ussm_ref[:, g * qc:(g + 1) * qc] = _dot(xb, w_ref[:, D_POOL + g * qc:D_POOL + (g + 1) * qc])
        sl = slice(g * POOL_GROUP_DIM, (g + 1) * POOL_GROUP_DIM)
        acc = ext_ref[16:16 + TM, sl]
        for k in range(1, w):
            acc = acc + ext_ref[16 - k:16 - k + TM, sl]
        cnt = jnp.minimum(pos + 1, w).astype(F32)
        a_ref[:, sl] = _pool_mix(g, acc / cnt - ext_ref[16:16 + TM, sl], wp_ref, scale_ref)

    for pp in range(PREP_PAIRS):
        _ssm_prep_pair(pp, *prep_in, *prep_out)
    _cast_slices(cast_in, cast_out)

    @pl.when(i == NBP)
    def _():
        us_ref[...] = ext_ref[16:16 + TM, :]
        tok = lambda t, sl: ext_ref[16 + t * DEC_BATCH:16 + (t + 1) * DEC_BATCH, sl]
        for j in range(DEC_SEQ):
            rows = slice(j * DEC_BATCH, (j + 1) * DEC_BATCH)
            for g, w in enumerate(POOL_WINDOWS):
                sl = slice(g * POOL_GROUP_DIM, (g + 1) * POOL_GROUP_DIM)
                acc = tok(j, sl)
                for k in range(1, w):
                    e = POOL_BUF + j - k
                    acc = acc + (tok(e - POOL_BUF, sl) if e >= POOL_BUF else st_ref[e, :, sl])
                cnt = float(min(PAST_LEN + j + 1, w))
                a_ref[rows, sl] = _pool_mix(g, acc / cnt - tok(j, sl), wp_ref, scale_ref)


def _front(xp, xs, w, state_t, w_pool, pool_scale, prep_args, to_cast):
    G, N, C, L = N_SSM_GROUPS, SSM_STATE, SSM_GROUP_CH, CHUNK
    wd = L * C
    step = lambda i: jnp.minimum(i, NBP - 1)
    g3 = lambda a, b: pl.BlockSpec((2 * PREP_PAIRS, a, b), lambda i: (step(i), 0, 0))
    p3 = lambda a, b: pl.BlockSpec((PREP_PAIRS, a, b), lambda i: (step(i), 0, 0))
    sds = jax.ShapeDtypeStruct
    cast_specs, cast_shapes = _cast_specs(to_cast, NBP)
    row = lambda wdt: pl.BlockSpec((TM, wdt), lambda i: (i, 0))
    outs = pl.pallas_call(
        _front_kernel,
        out_shape=(
            sds((R, D_SSM), F32), sds((R, D_POOL), BF16), sds((NB, 16, D_POOL), F32), sds((RS, D_POOL), F32),
            sds((N_PAIRS, 2, wd, wd), BF16),
            sds((N_PAIRS, PAIR_W, 128), BF16), sds((N_PAIRS, PAIR_W, 128), BF16),
            sds((N_PAIRS, PAIR_W, 128), BF16), sds((N_PAIRS, PAIR_W, 128), BF16),
            sds((N_PAIRS, PAIR_W, 128), BF16), sds((N_PAIRS, PAIR_W, 128), BF16),
            sds((N_PAIRS, 8, 128), F32), sds((N_PAIRS, 8, 128), F32), sds((N_PAIRS, 8, 128), F32),
            *cast_shapes,
        ),
        grid=(NB,),
        in_specs=[
            pl.BlockSpec((TM, D_MODEL), lambda i: (step(i), 0)),
            _const_spec((RS, D_MODEL)),
            _const_spec((D_MODEL, D_MODEL)),
            _const_spec((POOL_BUF, DEC_BATCH, D_POOL)),
            _const_spec((4, POOL_GROUP_DIM, POOL_GROUP_DIM)),
            _const_spec((1, D_POOL)),
            g3(1, N), g3(1, N), g3(1, 1), g3(C, N), g3(C, N), g3(C, N), g3(C, N),
            *cast_specs,
        ],
        out_specs=(
            row(D_SSM), row(D_POOL), pl.BlockSpec((1, 16, D_POOL), lambda i: (i, 0, 0)),
            pl.BlockSpec((RS, D_POOL), lambda i: (0, 0)),
            pl.BlockSpec((PREP_PAIRS, 2, wd, wd), lambda i: (step(i), 0, 0, 0)),
            p3(PAIR_W, 128), p3(PAIR_W, 128), p3(PAIR_W, 128), p3(PAIR_W, 128),
            p3(PAIR_W, 128), p3(PAIR_W, 128),
            p3(8, 128), p3(8, 128), p3(8, 128),
            *cast_specs,
        ),
        scratch_shapes=[pltpu.VMEM((TM + 16, D_POOL), F32)],
        compiler_params=_cparams(("arbitrary",)),
        name="front",
    )(xp, xs, w, state_t, w_pool, pool_scale, *prep_args, *to_cast)
    names = ("t", "pre", "pim", "p4re", "p4im", "qre", "qim", "apow_re", "apow_im", "lam4")
    return outs[:4], dict(zip(names, outs[4:4 + N_PREP_OUT])), outs[4 + N_PREP_OUT:]


SLAB_GROUPS = 128 // SSM_GROUP_CH
SLAB_PAIRS = SLAB_GROUPS // 2
N_SLABS = N_SSM_GROUPS // SLAB_GROUPS


def _ssm_pair(q, u, t_ref, pre_ref, pim_ref, p4re_ref, p4im_ref, qre_ref, qim_ref,
              apow_re_ref, apow_im_ref, lam4_ref, dskip_ref, h0re_ref, h0im_ref,
              hp_re_ref, hp_im_ref, hs_re_ref, hs_im_ref, hre_scr, him_scr):
    ub = u.astype(BF16)
    half = PAIR_W // 2
    y = jnp.concatenate([_dot(ub[:, :half], t_ref[q, 0]), _dot(ub[:, half:], t_ref[q, 1])], axis=1)

    ubp = ub[:P_CHUNK_ROWS]
    hre_scr[SCAN_PAD:SCAN_PAD + P_CHUNK_ROWS, :] = _dot(ubp, pre_ref[q])
    him_scr[SCAN_PAD:SCAN_PAD + P_CHUNK_ROWS, :] = _dot(ubp, pim_ref[q])
    kk = lax.broadcasted_iota(jnp.int32, (P_CHUNK_ROWS, 1), 0) % CHUNKS_PER_SEQ
    for s in range(N_SCAN_STEPS):
        d = 1 << s
        ar = apow_re_ref[q, s:s + 1, :]
        ai = apow_im_ref[q, s:s + 1, :]
        hr = hre_scr[SCAN_PAD:SCAN_PAD + P_CHUNK_ROWS, :]
        hi = him_scr[SCAN_PAD:SCAN_PAD + P_CHUNK_ROWS, :]
        pr = hre_scr[SCAN_PAD - d:SCAN_PAD - d + P_CHUNK_ROWS, :]
        pi = him_scr[SCAN_PAD - d:SCAN_PAD - d + P_CHUNK_ROWS, :]
        keep = kk >= d
        hre_scr[SCAN_PAD:SCAN_PAD + P_CHUNK_ROWS, :] = hr + jnp.where(keep, ar * pr - ai * pi, 0.0)
        him_scr[SCAN_PAD:SCAN_PAD + P_CHUNK_ROWS, :] = hi + jnp.where(keep, ar * pi + ai * pr, 0.0)
    hp_re_ref[q] = jnp.zeros((8, 128), F32)
    hp_im_ref[q] = jnp.zeros((8, 128), F32)
    for b in range(BATCH):
        last = SCAN_PAD + (b + 1) * CHUNKS_PER_SEQ - 1
        hp_re_ref[q, b:b + 1, :] = hre_scr[last:last + 1, :]
        hp_im_ref[q, b:b + 1, :] = him_scr[last:last + 1, :]
    prev_ok = kk >= 1
    hprev_re = jnp.where(prev_ok, hre_scr[SCAN_PAD - 1:SCAN_PAD - 1 + P_CHUNK_ROWS, :], 0.0)
    hprev_im = jnp.where(prev_ok, him_scr[SCAN_PAD - 1:SCAN_PAD - 1 + P_CHUNK_ROWS, :], 0.0)
    carry_p = _dot_nt(hprev_re.astype(BF16), qre_ref[q]) + _dot_nt(hprev_im.astype(BF16), qim_ref[q])

    ubs = ub[P_CHUNK_ROWS:]
    h0r = h0re_ref[q]
    h0i = h0im_ref[q]
    l4r = lam4_ref[q, 0:1, :]
    l4i = lam4_ref[q, 1:2, :]
    hs_re_ref[q] = l4r * h0r - l4i * h0i + _dot(ubs, p4re_ref[q])
    hs_im_ref[q] = l4r * h0i + l4i * h0r + _dot(ubs, p4im_ref[q])
    carry_s = _dot_nt(h0r.astype(BF16), qre_ref[q]) + _dot_nt(h0i.astype(BF16), qim_ref[q])

    y = y + jnp.concatenate([carry_p, carry_s], axis=0) + dskip_ref[q] * u
    return jax.nn.gelu(y)


N_SSM_IN, N_SSM_OUT = 13, 5


def _ssm_kernel(u_ref, *all_refs):
    n_cast = (len(all_refs) - N_SSM_IN - N_SSM_OUT - 2) // 2
    _cast_slices(all_refs[N_SSM_IN:N_SSM_IN + n_cast], all_refs[N_SSM_IN + n_cast + N_SSM_OUT:-2])
    refs = all_refs[:N_SSM_IN] + all_refs[N_SSM_IN + n_cast:N_SSM_IN + n_cast + N_SSM_OUT] + all_refs[-2:]
    z_ref = refs[13]
    hre_scr, him_scr = refs[18], refs[19]
    c = SSM_GROUP_CH
    hre_scr[0:SCAN_PAD, :] = jnp.zeros((SCAN_PAD, 128), F32)
    him_scr[0:SCAN_PAD, :] = jnp.zeros((SCAN_PAD, 128), F32)
    xt = []
    for i in range(CHUNK):
        xp = u_ref[pl.ds(i, P_CHUNK_ROWS, stride=CHUNK), :]
        if i < DEC_SEQ:
            xs = u_ref[RP + i * DEC_BATCH:RP + (i + 1) * DEC_BATCH, :]
        else:
            xs = jnp.zeros((DEC_BATCH, 128), F32)
        xt.append(jnp.concatenate([xp, xs], axis=0).T)
    def pair_input(q):
        halves = []
        for e in range(2):
            g = 2 * q + e
            bt = jnp.concatenate([xt[i][g * c:(g + 1) * c, :] for i in range(CHUNK)], axis=0)
            halves.append(bt.T)
        return jnp.concatenate(halves, axis=1)

    zt = []
    u_pair = pair_input(0)
    for q in range(SLAB_PAIRS):
        u_next = pair_input(q + 1) if q + 1 < SLAB_PAIRS else None
        z = _ssm_pair(q, u_pair, *refs[:13], *refs[14:])
        zt.append(z[:, :PAIR_W // 2].T)
        zt.append(z[:, PAIR_W // 2:].T)
        u_pair = u_next
    for i in range(CHUNK):
        zi = jnp.concatenate([zt[g][i * c:(i + 1) * c, :] for g in range(SLAB_GROUPS)], axis=0).T
        z_ref[pl.ds(i, P_CHUNK_ROWS, stride=CHUNK), :] = zi[:P_CHUNK_ROWS]
        if i < DEC_SEQ:
            z_ref[RP + i * DEC_BATCH:RP + (i + 1) * DEC_BATCH, :] = zi[P_CHUNK_ROWS:]


def _ssm(u_ssm, prep, h0re, h0im, to_cast):
    sp = SLAB_PAIRS
    slab3 = lambda a, b: pl.BlockSpec((sp, a, b), lambda s: (s, 0, 0))
    cast_specs, cast_shapes = _cast_specs(to_cast, N_SLABS)
    outs = pl.pallas_call(
        _ssm_kernel,
        out_shape=(
            jax.ShapeDtypeStruct((R, D_SSM), F32),
            jax.ShapeDtypeStruct((N_PAIRS, 8, 128), F32),
            jax.ShapeDtypeStruct((N_PAIRS, 8, 128), F32),
            jax.ShapeDtypeStruct((N_PAIRS, DEC_BATCH, 128), F32),
            jax.ShapeDtypeStruct((N_PAIRS, DEC_BATCH, 128), F32),
            *cast_shapes,
        ),
        grid=(N_SLABS,),
        in_specs=[
            pl.BlockSpec((R, 128), lambda s: (0, s)),
            pl.BlockSpec((sp, 2, 256, 256), lambda s: (s, 0, 0, 0)),
            slab3(PAIR_W, 128), slab3(PAIR_W, 128), slab3(PAIR_W, 128), slab3(PAIR_W, 128),
            slab3(PAIR_W, 128), slab3(PAIR_W, 128),
            slab3(8, 128), slab3(8, 128), slab3(8, 128),
            slab3(1, PAIR_W),
            slab3(DEC_BATCH, 128), slab3(DEC_BATCH, 128),
            *cast_specs,
        ],
        out_specs=(
            pl.BlockSpec((R, 128), lambda s: (0, s)),
            slab3(8, 128), slab3(8, 128), slab3(DEC_BATCH, 128), slab3(DEC_BATCH, 128),
            *cast_specs,
        ),
        scratch_shapes=[pltpu.VMEM((SCAN_PAD + P_CHUNK_ROWS, 128), F32),
                        pltpu.VMEM((SCAN_PAD + P_CHUNK_ROWS, 128), F32)],
        compiler_params=_cparams(("arbitrary",)),
        name="ssm",
    )(u_ssm, prep["t"], prep["pre"], prep["pim"], prep["p4re"], prep["p4im"],
      prep["qre"], prep["qim"], prep["apow_re"], prep["apow_im"], prep["lam4"],
      prep["dskip"], h0re, h0im, *to_cast)
    return outs[:N_SSM_OUT], outs[N_SSM_OUT:]


def _cmul(ar, ai, br, bi):
    return ar * br - ai * bi, ar * bi + ai * br


def _dot3(a, b):
    ah = a.astype(BF16)
    bh = b.astype(BF16)
    al = (a - ah.astype(F32)).astype(BF16)
    bl = (b - bh.astype(F32)).astype(BF16)
    return _dot(ah, bh) + _dot(ah, bl) + _dot(al, bh)


def _dot3_nt(a, b):
    nt = _dot_nt
    ah = a.astype(BF16)
    bh = b.astype(BF16)
    al = (a - ah.astype(F32)).astype(BF16)
    bl = (b - bh.astype(F32)).astype(BF16)
    return nt(ah, bh) + nt(ah, bl) + nt(al, bh)


def _ssm_prep_pair(pp, lam_re_ref, lam_im_ref, lstep_ref, bt_re_ref, bt_im_ref, c_re_ref, c_im_ref,
                   t_ref, pre_ref, pim_ref, p4re_ref, p4im_ref, qre_ref, qim_ref,
                   apr_ref, api_ref, lam4_ref):
    L, C, N = CHUNK, SSM_GROUP_CH, SSM_STATE
    w = L * C
    p_re, p_im, p4_re, p4_im, q_re, q_im, a_re, a_im, l4 = [], [], [], [], [], [], [], [], []
    for e in range(2):
        ge = 2 * pp + e
        dt = jnp.exp(lstep_ref[ge])
        lr, li = lam_re_ref[ge], lam_im_ref[ge]
        mag = jnp.exp(lr * dt)
        ang = li * dt
        zr, zi = mag * jnp.cos(ang), mag * jnp.sin(ang)
        den = lr * lr + li * li
        fr = ((zr - 1.0) * lr + zi * li) / den
        fi = (zi * lr - (zr - 1.0) * li) / den
        bbr, bbi = _cmul(fr, fi, bt_re_ref[ge], bt_im_ref[ge])
        pr, pi = [jnp.ones((1, N), F32)], [jnp.zeros((1, N), F32)]
        for _ in range(L):
            nr, ni = _cmul(pr[-1], pi[-1], zr, zi)
            pr.append(nr)
            pi.append(ni)
        stack = lambda blocks, part: jnp.concatenate([b[part] for b in blocks], axis=0)
        blocks = [_cmul(bbr, bbi, pr[L - 1 - i], pi[L - 1 - i]) for i in range(L)]
        p_re.append(stack(blocks, 0))
        p_im.append(stack(blocks, 1))
        blocks4 = [_cmul(bbr, bbi, pr[DEC_SEQ - 1 - i], pi[DEC_SEQ - 1 - i]) for i in range(DEC_SEQ)]
        pad = jnp.zeros(((L - DEC_SEQ) * C, N), F32)
        p4_re.append(jnp.concatenate([stack(blocks4, 0), pad], axis=0))
        p4_im.append(jnp.concatenate([stack(blocks4, 1), pad], axis=0))
        sr, si = [pr[L]], [pi[L]]
        for _ in range(N_SCAN_STEPS - 1):
            nr, ni = _cmul(sr[-1], si[-1], sr[-1], si[-1])
            sr.append(nr)
            si.append(ni)
        a_re.append(jnp.concatenate(sr + [jnp.zeros((8 - N_SCAN_STEPS, N), F32)], axis=0))
        a_im.append(jnp.concatenate(si + [jnp.zeros((8 - N_SCAN_STEPS, N), F32)], axis=0))
        l4.append(jnp.concatenate([pr[DEC_SEQ], pi[DEC_SEQ], jnp.zeros((6, N), F32)], axis=0))

        cr, ci = c_re_ref[ge], c_im_ref[ge]
        ck = [_cmul(cr, ci, pr[k], pi[k]) for k in range(L + 1)]
        q_re.append(stack(ck[1:], 0))
        q_im.append(-stack(ck[1:], 1))
        v = _dot3_nt(bbr, stack(ck[:L], 0)) - _dot3_nt(bbi, stack(ck[:L], 1))
        lane = lax.broadcasted_iota(jnp.int32, (C, w), 1)
        rows = [v] + [jnp.where(lane >= C * i, pltpu.roll(v, C * i, axis=1), 0.0) for i in range(1, L)]
        t_ref[pp, e] = jnp.concatenate(rows, axis=0).astype(BF16)

    def diag_rows(m):
        z = jnp.zeros_like(m[0])
        return jnp.concatenate([jnp.concatenate([m[0], z], axis=1),
                                jnp.concatenate([z, m[1]], axis=1)], axis=0)

    pre_ref[pp] = diag_rows(p_re).astype(BF16)
    pim_ref[pp] = diag_rows(p_im).astype(BF16)
    p4re_ref[pp] = diag_rows(p4_re).astype(BF16)
    p4im_ref[pp] = diag_rows(p4_im).astype(BF16)
    qre_ref[pp] = diag_rows(q_re).astype(BF16)
    qim_ref[pp] = diag_rows(q_im).astype(BF16)
    apr_ref[pp] = jnp.concatenate(a_re, axis=1)
    api_ref[pp] = jnp.concatenate(a_im, axis=1)
    lam4_ref[pp] = jnp.concatenate(l4, axis=1)


def _ssm_prep_args(lambda_re, lambda_im, log_step, b_re, b_im, c_re, c_im):
    return (lambda_re[:, None, :], lambda_im[:, None, :], log_step[:, None, None],
            b_re.transpose(0, 2, 1), b_im.transpose(0, 2, 1), c_re, c_im)


def _mix_kernel(a_ref, z_ref, xp_ref, xs_ref, wglu_ref, bglu_ref, wout_ref, g_ref, b_ref, *refs):
    n_cast = (len(refs) - 2) // 2
    o_ref, ob_ref = refs[n_cast], refs[n_cast + 1]
    _cast_slices(refs[:n_cast], refs[n_cast + 2:])
    is_prompt = pl.program_id(0) < NBP
    n_sub = TM // SUB

    def glu(r):
        z = z_ref[r * SUB:(r + 1) * SUB, :]
        gate = _dot(z.astype(BF16), wglu_ref[...]) + bglu_ref[...]
        return (z * jax.nn.sigmoid(gate)).astype(BF16)

    bmix = glu(0)
    for r in range(n_sub):
        rows = slice(r * SUB, (r + 1) * SUB)
        bmix_next = glu(r + 1) if r + 1 < n_sub else None
        mix = _dot(a_ref[rows, :], wout_ref[0:D_POOL, :]) + _dot(bmix, wout_ref[D_POOL:, :])
        x = jnp.where(is_prompt, xp_ref[rows, :], xs_ref[rows, :])
        h1 = _layer_norm(ALPHA * x + mix, g_ref[...], b_ref[...])
        o_ref[rows, :] = h1
        ob_ref[rows, :] = h1.astype(BF16)
        bmix = bmix_next


def _mix(a, z, xp, xs, w_glu, b_glu, w_out, g, b, to_cast):
    row = lambda w: pl.BlockSpec((TM, w), lambda i: (i, 0))
    cast_specs, cast_shapes = _cast_specs(to_cast, NBP)
    outs = pl.pallas_call(
        _mix_kernel,
        out_shape=(jax.ShapeDtypeStruct((R, D_MODEL), F32), jax.ShapeDtypeStruct((R, D_MODEL), BF16), *cast_shapes),
        grid=(NB,),
        in_specs=[
            row(D_POOL), row(D_SSM),
            pl.BlockSpec((TM, D_MODEL), lambda i: (jnp.minimum(i, NBP - 1), 0)),
            _const_spec((RS, D_MODEL)),
            _const_spec((D_SSM, D_SSM)), _const_spec((1, D_SSM)),
            _const_spec((D_MODEL, D_MODEL)), _const_spec((1, D_MODEL)), _const_spec((1, D_MODEL)),
            *cast_specs,
        ],
        out_specs=(row(D_MODEL), row(D_MODEL), *cast_specs),
        compiler_params=_cparams(("arbitrary",)),
        name="mix_ln1",
    )(a, z, xp, xs, w_glu, b_glu, w_out, g, b, *to_cast)
    return outs[0], outs[1], outs[2:]


def _qproj_sample_kernel(x_ref, w_ref, o_ref):
    q = _dot(x_ref[...], w_ref[...])
    for j in range(DEC_SEQ):
        o_ref[:, j, :] = q[j * DEC_BATCH:(j + 1) * DEC_BATCH, :]
    o_ref[:, DEC_SEQ:, :] = jnp.zeros((DEC_BATCH, Q_PAD - DEC_SEQ, D_MODEL), F32)


def _qproj_sample(h, w):
    return pl.pallas_call(
        _qproj_sample_kernel,
        out_shape=jax.ShapeDtypeStruct((DEC_BATCH, Q_PAD, D_MODEL), F32),
        grid=(1,),
        in_specs=[pl.BlockSpec((RS, D_MODEL), lambda i: (RP // RS, 0)),
                  pl.BlockSpec((D_MODEL, D_MODEL), lambda i: (0, 0))],
        out_specs=pl.BlockSpec((DEC_BATCH, Q_PAD, D_MODEL), lambda i: (0, 0, 0)),
        compiler_params=_cparams(("arbitrary",)),
        name="qproj_sample",
    )(h, w)


def _memproj_kernel(m_ref, w_ref, o_ref, ob_ref, wb_scr):
    @pl.when(pl.program_id(0) == 0)
    def _():
        wb_scr[...] = w_ref[...].astype(BF16)

    k = _dot(m_ref[...].astype(BF16), wb_scr[...])
    ob_ref[...] = k.astype(BF16)
    for h in range(N_XHEADS):
        o_ref[0, 0, :, h, :] = k[:, h * XHEAD_DIM:(h + 1) * XHEAD_DIM]


def _memproj(mem, w, name):
    rows = BATCH * N_MEM
    seq = pl.BlockSpec((N_MEM, D_MODEL), lambda b: (b, 0))
    return pl.pallas_call(
        _memproj_kernel,
        out_shape=(jax.ShapeDtypeStruct((1, BATCH, N_MEM, N_XHEADS, XHEAD_DIM), F32),
                   jax.ShapeDtypeStruct((rows, D_MODEL), BF16)),
        grid=(BATCH,),
        in_specs=[seq, _const_spec((D_MODEL, D_MODEL))],
        out_specs=(pl.BlockSpec((1, 1, N_MEM, N_XHEADS, XHEAD_DIM), lambda b: (0, b, 0, 0, 0)), seq),
        scratch_shapes=[pltpu.VMEM((D_MODEL, D_MODEL), BF16)],
        compiler_params=_cparams(("arbitrary",)),
        name=name,
    )(mem, w)


def _attend(q, k, v, between=None):
    def scores(h):
        sl = slice(h * XHEAD_DIM, (h + 1) * XHEAD_DIM)
        return lax.dot_general(q[:, sl], k[:, sl], (((1,), (1,)), ((), ())),
                               preferred_element_type=F32) * (XHEAD_DIM ** -0.5)

    def head_out(h, s):
        s = s - jnp.max(s, axis=-1, keepdims=True)
        e = jnp.exp(s)
        p = e / jnp.sum(e, axis=-1, keepdims=True)
        return _dot(p.astype(BF16), v[:, h * XHEAD_DIM:(h + 1) * XHEAD_DIM])

    outs = []
    s = scores(0)
    for h in range(N_XHEADS):
        if between is not None:
            between(h)
        s_next = scores(h + 1) if h + 1 < N_XHEADS else None
        outs.append(head_out(h, s))
        s = s_next
    return jnp.concatenate(outs, axis=1)


def _attn_block_kernel(h_ref, q_ref, k_ref, v_ref, os_ref, wo_ref, g_ref, b_ref, *refs):
    n_cast = (len(refs) - 1) // 2
    o_ref = refs[n_cast]
    _cast_slices(refs[:n_cast], refs[n_cast + 1:])
    i = pl.program_id(0)
    n_sub = TM // SUB

    def normed(rows, y):
        o_ref[rows, :] = _layer_norm(ALPHA * h_ref[rows, :] + y, g_ref[...], b_ref[...])

    @pl.when(i < NBP)
    def _():
        prev_rows, prev_o = None, None
        for r in range(n_sub):
            rows = slice(r * SUB, (r + 1) * SUB)
            ys = []
            between = None
            if prev_o is not None:
                between = lambda hh, po=prev_o: ys.append(
                    _dot(po, wo_ref[:, hh * XHEAD_DIM:(hh + 1) * XHEAD_DIM]))
            o = _attend(q_ref[rows, :], k_ref[...], v_ref[...], between).astype(BF16)
            if prev_o is not None:
                normed(prev_rows, jnp.concatenate(ys, axis=1))
            prev_rows, prev_o = rows, o
        normed(prev_rows, _dot(prev_o, wo_ref[...]))

    @pl.when(i == NBP)
    def _():
        for j in range(DEC_SEQ):
            rows = slice(j * DEC_BATCH, (j + 1) * DEC_BATCH)
            normed(rows, _dot(os_ref[:, j, :].astype(BF16), wo_ref[...]))


def _attn_block(h, q_prompt, kb, vb, o_sample, wo, g, b, to_cast):
    kv = pl.BlockSpec((N_MEM, D_MODEL), lambda i: (jnp.minimum(i // BLOCKS_PER_SEQ, BATCH - 1), 0))
    row = pl.BlockSpec((TM, D_MODEL), lambda i: (i, 0))
    qrow = pl.BlockSpec((TM, D_MODEL), lambda i: (jnp.minimum(i, NBP - 1), 0))
    cast_specs, cast_shapes = _cast_specs(to_cast, NBP)
    outs = pl.pallas_call(
        _attn_block_kernel,
        out_shape=(jax.ShapeDtypeStruct((R, D_MODEL), F32), *cast_shapes),
        grid=(NB,),
        in_specs=[row, qrow, kv, kv, _const_spec((DEC_BATCH, Q_PAD, D_MODEL)), _const_spec((D_MODEL, D_MODEL)),
                  _const_spec((1, D_MODEL)), _const_spec((1, D_MODEL)), *cast_specs],
        out_specs=(row, *cast_specs),
        compiler_params=_cparams(("arbitrary",)),
        name="attn_block",
    )(h, q_prompt, kb, vb, o_sample, wo, g, b, *to_cast)
    return outs[0], outs[1:]


ATT_BB = 4
Q_PAD = 8
KV_ROWS = N_MEM * N_XHEADS


QP_ROWS = RP // (DEC_BATCH // ATT_BB)


def _attn_sample_kernel(q_ref, k_ref, v_ref, h_ref, wq_ref, o_ref, qp_ref):
    hb = h_ref[...]
    qcols = D_MODEL // ATT_BB
    shape = (N_XHEADS * Q_PAD, KV_ROWS)
    same_head = (lax.broadcasted_iota(jnp.int32, shape, 0) // Q_PAD
                 == lax.broadcasted_iota(jnp.int32, shape, 1) % N_XHEADS)
    def scores(b):
        q = q_ref[b]
        qs = jnp.concatenate([q[:, h * XHEAD_DIM:(h + 1) * XHEAD_DIM] for h in range(N_XHEADS)], axis=0)
        k = k_ref[0, b].reshape(KV_ROWS, XHEAD_DIM).astype(BF16)
        return lax.dot_general(qs.astype(BF16), k, (((1,), (1,)), ((), ())),
                               preferred_element_type=F32) * (XHEAD_DIM ** -0.5)

    def attend(b, s):
        s = jnp.where(same_head, s, -1e30)
        s = s - jnp.max(s, axis=-1, keepdims=True)
        e = jnp.exp(s)
        p = e / jnp.sum(e, axis=-1, keepdims=True)
        v = v_ref[0, b].reshape(KV_ROWS, XHEAD_DIM).astype(BF16)
        o = _dot(p.astype(BF16), v)
        for h in range(N_XHEADS):
            o_ref[b, :, h * XHEAD_DIM:(h + 1) * XHEAD_DIM] = o[h * Q_PAD:(h + 1) * Q_PAD]

    s_prev = scores(0)
    for b in range(ATT_BB):
        cols = slice(b * qcols, (b + 1) * qcols)
        qp_ref[:, cols] = _dot(hb, wq_ref[:, cols]).astype(BF16)
        s_next = scores(b + 1) if b + 1 < ATT_BB else None
        attend(b, s_prev)
        s_prev = s_next


def _attn_sample(q_pad, k, v, h, wq):
    kv = pl.BlockSpec((1, ATT_BB, N_MEM, N_XHEADS, XHEAD_DIM), lambda i: (0, i, 0, 0, 0))
    qo = pl.BlockSpec((ATT_BB, Q_PAD, D_MODEL), lambda i: (i, 0, 0))
    hq = pl.BlockSpec((QP_ROWS, D_MODEL), lambda i: (i, 0))
    return pl.pallas_call(
        _attn_sample_kernel,
        out_shape=(jax.ShapeDtypeStruct((DEC_BATCH, Q_PAD, D_MODEL), F32),
                   jax.ShapeDtypeStruct((RP, D_MODEL), BF16)),
        grid=(DEC_BATCH // ATT_BB,),
        in_specs=[qo, kv, kv, hq, _const_spec((D_MODEL, D_MODEL))],
        out_specs=(qo, hq),
        compiler_params=_cparams(("arbitrary",)),
        name="attn_sample",
    )(q_pad, k, v, h, wq)


FTM = 1024
FNB = RP // FTM
FBLOCKS_PER_SEQ = SEQ // FTM
FSUB = 512
FSUB_LAST = 128
FSUB_SAMPLE = 256


def _ffn_rows(n_rows, sub, conv_taps, h_ref, hb_scr, g_scr, g_base, wg, wu, wd, cw_ref, cb_ref,
              lng_ref, lnb_ref, y_ref, first, last):
    def gate_up(lo):
        if first:
            h = h_ref[lo:lo + sub, :]
            hb = h.astype(BF16)
            hb_scr[lo:lo + sub, :] = hb
        else:
            h, hb = None, hb_scr[lo:lo + sub, :]
        g = _dot(hb, wg)
        up = _dot(hb, wu)
        g_scr[g_base + lo:g_base + lo + sub, :] = g
        tap0, tap1 = conv_taps(lo, sub)
        gc = cb_ref[...] + cw_ref[0:1, :] * tap0 + cw_ref[1:2, :] * tap1 + cw_ref[2:3, :] * g
        return h, (jax.nn.silu(gc) * up).astype(BF16)

    def down(lo, h, act):
        y = _dot(act, wd) + (ALPHA * h if first else y_ref[lo:lo + sub, :])
        y_ref[lo:lo + sub, :] = _layer_norm(y, lng_ref[...], lnb_ref[...]) if last else y

    n_sub = n_rows // sub
    pending = gate_up(0)
    for r in range(n_sub):
        nxt = gate_up((r + 1) * sub) if r + 1 < n_sub else None
        down(r * sub, *pending)
        pending = nxt


def _ffn_chunk(f, run):
    @pl.when(f == 0)
    def _():
        run(True, False)

    @pl.when(jnp.logical_and(f > 0, f < NF - 1))
    def _():
        run(False, False)

    @pl.when(f == NF - 1)
    def _():
        run(False, True)


def _ffn_prompt_kernel(h_ref, wg_ref, wu_ref, wd_ref, cw_ref, cb_ref, lng_ref, lnb_ref,
                       y_ref, gtail_ref, hb_scr, g_scr, carry_scr):
    i = pl.program_id(0)
    f = pl.program_id(1)
    first_block = (i % FBLOCKS_PER_SEQ) == 0

    @pl.when(first_block)
    def _():
        g_scr[0:8, :] = jnp.zeros((8, TF), F32)

    @pl.when(jnp.logical_not(first_block))
    def _():
        g_scr[0:8, :] = carry_scr[f]

    taps = lambda lo, sub: (g_scr[6 + lo:6 + lo + sub, :], g_scr[7 + lo:7 + lo + sub, :])
    _ffn_chunk(f, lambda first, last: _ffn_rows(
        FTM, FSUB_LAST if last else FSUB, taps, h_ref, hb_scr, g_scr, 8, wg_ref[...], wu_ref[...], wd_ref[...], cw_ref, cb_ref,
        lng_ref, lnb_ref, y_ref, first, last))
    tail = g_scr[FTM:FTM + 8, :]
    carry_scr[f] = tail
    gtail_ref[0] = tail


def _ffn_sample_kernel(h_ref, wg_ref, wu_ref, wd_ref, cw_ref, cb_ref, lng_ref, lnb_ref, st_ref,
                       y_ref, gnew_ref, wdb_ref, hb_scr, g_scr):
    f = pl.program_id(1)
    n_st = 2 * DEC_BATCH
    wdb_ref[...] = wd_ref[...].astype(BF16)
    g_scr[0:n_st, :] = st_ref[...]
    taps = lambda lo, sub: (g_scr[lo:lo + sub, :], g_scr[DEC_BATCH + lo:DEC_BATCH + lo + sub, :])
    _ffn_chunk(f, lambda first, last: _ffn_rows(
        RS, FSUB_SAMPLE, taps, h_ref, hb_scr, g_scr, n_st, wg_ref[...], wu_ref[...], wdb_ref[...], cw_ref, cb_ref,
        lng_ref, lnb_ref, y_ref, first, last))
    for t in range(CONV_W - 1):
        gnew_ref[:, t, :] = g_scr[RS + t * DEC_BATCH:RS + (t + 1) * DEC_BATCH, :]


def _ffn_specs(tm, row_block_of):
    return [
        pl.BlockSpec((tm, D_MODEL), lambda i, f: (row_block_of(i), 0)),
        pl.BlockSpec((D_MODEL, TF), lambda i, f: (0, f)),
        pl.BlockSpec((D_MODEL, TF), lambda i, f: (0, f)),
        pl.BlockSpec((TF, D_MODEL), lambda i, f: (f, 0)),
        pl.BlockSpec((CONV_W, TF), lambda i, f: (0, f)),
        pl.BlockSpec((1, TF), lambda i, f: (0, f)),
        pl.BlockSpec((1, D_MODEL), lambda i, f: (0, 0)),
        pl.BlockSpec((1, D_MODEL), lambda i, f: (0, 0)),
    ]


def _ffn_prompt(h, wg, wu, wd, cw, cb, lng, lnb):
    return pl.pallas_call(
        _ffn_prompt_kernel,
        out_shape=(jax.ShapeDtypeStruct((RP, D_MODEL), F32),
                   jax.ShapeDtypeStruct((FNB, 8, D_FF), F32)),
        grid=(FNB, NF),
        in_specs=_ffn_specs(FTM, lambda i: i),
        out_specs=(pl.BlockSpec((FTM, D_MODEL), lambda i, f: (i, 0)),
                   pl.BlockSpec((1, 8, TF), lambda i, f: (i, 0, f))),
        scratch_shapes=[pltpu.VMEM((FTM, D_MODEL), BF16),
                        pltpu.VMEM((FTM + 8, TF), F32),
                        pltpu.VMEM((NF, 8, TF), F32)],
        compiler_params=_cparams(("arbitrary", "arbitrary"), VMEM_LIMIT_FFN),
        name="ffn_prompt",
    )(h, wg, wu, wd, cw, cb, lng, lnb)


def _ffn_sample(h, wg, wu, wd, cw, cb, lng, lnb, conv_state):
    n_st = 2 * DEC_BATCH
    specs = _ffn_specs(RS, lambda i: RP // RS)
    return pl.pallas_call(
        _ffn_sample_kernel,
        out_shape=(jax.ShapeDtypeStruct((RS, D_MODEL), F32),
                   jax.ShapeDtypeStruct((DEC_BATCH, CONV_W - 1, D_FF), F32),
                   jax.ShapeDtypeStruct((D_FF, D_MODEL), BF16)),
        grid=(1, NF),
        in_specs=specs + [pl.BlockSpec((n_st, TF), lambda i, f: (0, f))],
        out_specs=(pl.BlockSpec((RS, D_MODEL), lambda i, f: (0, 0)),
                   pl.BlockSpec((DEC_BATCH, CONV_W - 1, TF), lambda i, f: (0, 0, f)),
                   specs[3]),
        scratch_shapes=[pltpu.VMEM((RS, D_MODEL), BF16),
                        pltpu.VMEM((n_st + RS, TF), F32)],
        compiler_params=_cparams(("arbitrary", "arbitrary")),
        name="ffn_sample",
    )(h, wg, wu, wd, cw, cb, lng, lnb, conv_state)


def _state_to_pairs(s):
    return s.reshape(s.shape[0], N_PAIRS, 128).transpose(1, 0, 2)


def _state_from_pairs(s):
    return s.transpose(1, 0, 2).reshape(1, s.shape[1], N_SSM_GROUPS, SSM_STATE)


def kernel(x_prompt, x_sample, mem_prompt, state_pool, state_ssm_re, state_ssm_im, state_conv, cache_mem_k, cache_mem_v, w_in, w_pool, pool_scale, lambda_re, lambda_im, log_step, b_re, b_im, c_re, c_im, d_skip, w_glu, b_glu, w_out, ln1_g, ln1_b, w_q, w_k, w_v, w_o, ln2_g, ln2_b, w_gate, w_up, conv_w, conv_b, w_down, ln3_g, ln3_b):
    bf = lambda w: w[0].astype(BF16)
    row = lambda v: v[0].reshape(1, -1)
    xp = x_prompt.reshape(RP, D_MODEL)
    xs = x_sample.transpose(1, 0, 2).reshape(RS, D_MODEL)

    prep_args = _ssm_prep_args(lambda_re[0], lambda_im[0], log_step[0], b_re[0], b_im[0], c_re[0], c_im[0])
    (u_ssm, a, utail, us_pool), prep, (w_glu_b, w_out_b) = _front(
        xp, xs, bf(w_in), state_pool[0].transpose(1, 0, 2), bf(w_pool), row(pool_scale), prep_args,
        (w_glu[0], w_out[0]))
    prep["dskip"] = jnp.broadcast_to(d_skip[0].reshape(N_PAIRS, 2, 1, SSM_GROUP_CH),
                                     (N_PAIRS, 2, CHUNK, SSM_GROUP_CH)).reshape(N_PAIRS, 1, PAIR_W)
    (z, hp_re, hp_im, hs_re, hs_im), (w_q_b, w_o_b) = _ssm(
        u_ssm, prep, _state_to_pairs(state_ssm_re[0]), _state_to_pairs(state_ssm_im[0]), (w_q[0], w_o[0]))
    h1, h1b, (wg,) = _mix(a, z, xp, xs, w_glu_b, row(b_glu), w_out_b, row(ln1_g), row(ln1_b), (w_gate[0],))

    mem = mem_prompt.reshape(BATCH * N_MEM, D_MODEL)
    mk, mkb = _memproj(mem, w_k[0], "mem_k")
    mv, mvb = _memproj(mem, w_v[0], "mem_v")
    o_s, q_p = _attn_sample(_qproj_sample(h1b, w_q_b), cache_mem_k, cache_mem_v, h1b, w_q_b)
    h2, (wu,) = _attn_block(h1, q_p, mkb, mvb, o_s, w_o_b, row(ln2_g), row(ln2_b), (w_up[0],))

    cw, cb = conv_w[0], row(conv_b)
    conv_st = state_conv[0].transpose(1, 0, 2).reshape(2 * DEC_BATCH, D_FF)
    y_s, g_new, wd = _ffn_sample(h2, wg, wu, w_down[0], cw, cb, row(ln3_g), row(ln3_b), conv_st)
    y_p, gtail = _ffn_prompt(h2, wg, wu, wd, cw, cb, row(ln3_g), row(ln3_b))

    y_prompt = y_p.reshape(BATCH, SEQ, D_MODEL)
    y_sample = y_s.reshape(DEC_SEQ, DEC_BATCH, D_MODEL).transpose(1, 0, 2)
    p_pool = utail[BLOCKS_PER_SEQ - 1:NBP:BLOCKS_PER_SEQ, 16 - POOL_BUF:][None]
    s_ext = jnp.concatenate([state_pool[0], us_pool.reshape(DEC_SEQ, DEC_BATCH, D_POOL).transpose(1, 0, 2)], axis=1)
    s_pool = s_ext[None, :, DEC_SEQ:]
    p_conv = gtail[FBLOCKS_PER_SEQ - 1::FBLOCKS_PER_SEQ, 6:8][None]
    s_conv = g_new[None]
    return (y_prompt, y_sample,
            p_pool, _state_from_pairs(hp_re[:, :BATCH]), _state_from_pairs(hp_im[:, :BATCH]), p_conv,
            mk, mv,
            s_pool, _state_from_pairs(hs_re), _state_from_pairs(hs_im), s_conv)
```

```python
import jax
import jax.numpy as jnp
from jax import lax
from jax.experimental import pallas as pl
from jax.experimental.pallas import tpu as pltpu

F32 = jnp.float32
BF16 = jnp.bfloat16

D_MODEL = 2048
BATCH = 4
SEQ = 2048
DEC_BATCH = 128
DEC_SEQ = 4
PAST_LEN = 16384
D_POOL = 1024
D_SSM = 1024
POOL_WINDOWS = (2, 4, 8, 16)
POOL_GROUP_DIM = 256
POOL_BUF = 15
SSM_GROUP_CH = 16
N_SSM_GROUPS = 64
SSM_STATE = 64
N_MEM = 256
N_XHEADS = 4
XHEAD_DIM = 512
D_FF = 5632
CONV_W = 3
ALPHA = 2.0 ** 0.25
LN_EPS = 1e-5

RP = BATCH * SEQ
RS = DEC_BATCH * DEC_SEQ
R = RP + RS
TM = 512
NB = R // TM
NBP = RP // TM
BLOCKS_PER_SEQ = SEQ // TM

CHUNK = 16
N_PAIRS = N_SSM_GROUPS // 2
PAIR_W = 2 * CHUNK * SSM_GROUP_CH
CHUNKS_PER_SEQ = SEQ // CHUNK
P_CHUNK_ROWS = BATCH * CHUNKS_PER_SEQ
SCAN_PAD = 64
N_SCAN_STEPS = 7

SUB = 256
TF = 512
NF = D_FF // TF

VMEM_LIMIT = 56 * 1024 * 1024
VMEM_LIMIT_FFN = 60 * 1024 * 1024


def _cparams(sem, vmem_limit=VMEM_LIMIT):
    return pltpu.CompilerParams(dimension_semantics=sem, vmem_limit_bytes=vmem_limit)


def _const_spec(shape):
    n = len(shape)
    return pl.BlockSpec(shape, lambda *_: (0,) * n, pipeline_mode=pl.Buffered(1))


def _layer_norm(x, g, b):
    mu = jnp.mean(x, axis=-1, keepdims=True)
    xc = x - mu
    var = jnp.mean(xc * xc, axis=-1, keepdims=True)
    return xc * lax.rsqrt(var + LN_EPS) * g + b


def _dot(a, b):
    return jnp.dot(a, b, preferred_element_type=F32)


def _dot_nt(a, b):
    return lax.dot_general(a, b, (((1,), (1,)), ((), ())), preferred_element_type=F32)


def _cast_spec(a, n_steps):
    return pl.BlockSpec((a.shape[0] // n_steps, a.shape[1]), lambda i: (jnp.minimum(i, n_steps - 1), 0))


def _cast_specs(arrays, n_steps):
    specs = [_cast_spec(a, n_steps) for a in arrays]
    return specs, [jax.ShapeDtypeStruct(a.shape, BF16) for a in arrays]


def _cast_slices(in_refs, out_refs):
    for src_ref, dst_ref in zip(in_refs, out_refs):
        dst_ref[...] = src_ref[...].astype(BF16)


N_PREP_IN, N_PREP_OUT = 7, 10
PREP_PAIRS = N_PAIRS // NBP


def _pool_mix(g, pooled, w_ref, scale_ref):
    sl = slice(g * POOL_GROUP_DIM, (g + 1) * POOL_GROUP_DIM)
    return (_dot(pooled.astype(BF16), w_ref[g]) * scale_ref[:, sl]).astype(BF16)


def _front_kernel(xp_ref, xs_ref, w_ref, st_ref, wp_ref, scale_ref, *refs):
    prep_in, refs = refs[:N_PREP_IN], refs[N_PREP_IN:]
    n_cast = (len(refs) - 4 - N_PREP_OUT - 1) // 2
    cast_in, refs = refs[:n_cast], refs[n_cast:]
    ussm_ref, a_ref, utail_ref, us_ref = refs[:4]
    prep_out, cast_out, ext_ref = refs[4:4 + N_PREP_OUT], refs[4 + N_PREP_OUT:-1], refs[-1]
    i = pl.program_id(0)

    xb = jnp.where(i < NBP, xp_ref[...], xs_ref[...]).astype(BF16)

    first = (i % BLOCKS_PER_SEQ) == 0
    ext_ref[0:16, :] = jnp.where(first, 0.0, ext_ref[TM:TM + 16, :])
    ext_ref[16:16 + TM, :] = _dot(xb, w_ref[:, 0:D_POOL])
    utail_ref[0] = ext_ref[TM:TM + 16, :]
    pos = (i % BLOCKS_PER_SEQ) * TM + lax.broadcasted_iota(jnp.int32, (TM, 1), 0)
    qc = D_SSM // len(POOL_WINDOWS)
    for g, w in enumerate(POOL_WINDOWS):
        ussm_ref[:, g * qc:(g + 1) * qc] = _dot(xb, w_ref[:, D_POOL + g * qc:D_POOL + (g + 1) * qc])
        sl = slice(g * POOL_GROUP_DIM, (g + 1) * POOL_GROUP_DIM)
        acc = ext_ref[16:16 + TM, sl]
        for k in range(1, w):
            acc = acc + ext_ref[16 - k:16 - k + TM, sl]
        cnt = jnp.minimum(pos + 1, w).astype(F32)
        a_ref[:, sl] = _pool_mix(g, acc / cnt - ext_ref[16:16 + TM, sl], wp_ref, scale_ref)

    for pp in range(PREP_PAIRS):
        _ssm_prep_pair(pp, *prep_in, *prep_out)
    _cast_slices(cast_in, cast_out)

    @pl.when(i == NBP)
    def _():
        us_ref[...] = ext_ref[16:16 + TM, :]
        tok = lambda t, sl: ext_ref[16 + t * DEC_BATCH:16 + (t + 1) * DEC_BATCH, sl]
        for j in range(DEC_SEQ):
            rows = slice(j * DEC_BATCH, (j + 1) * DEC_BATCH)
            for g, w in enumerate(POOL_WINDOWS):
                sl = slice(g * POOL_GROUP_DIM, (g + 1) * POOL_GROUP_DIM)
                acc = tok(j, sl)
                for k in range(1, w):
                    e = POOL_BUF + j - k
                    acc = acc + (tok(e - POOL_BUF, sl) if e >= POOL_BUF else st_ref[e, :, sl])
                cnt = float(min(PAST_LEN + j + 1, w))
                a_ref[rows, sl] = _pool_mix(g, acc / cnt - tok(j, sl), wp_ref, scale_ref)


def _front(xp, xs, w, state_t, w_pool, pool_scale, prep_args, to_cast):
    G, N, C, L = N_SSM_GROUPS, SSM_STATE, SSM_GROUP_CH, CHUNK
    wd = L * C
    step = lambda i: jnp.minimum(i, NBP - 1)
    g3 = lambda a, b: pl.BlockSpec((2 * PREP_PAIRS, a, b), lambda i: (step(i), 0, 0))
    p3 = lambda a, b: pl.BlockSpec((PREP_PAIRS, a, b), lambda i: (step(i), 0, 0))
    sds = jax.ShapeDtypeStruct
    cast_specs, cast_shapes = _cast_specs(to_cast, NBP)
    row = lambda wdt: pl.BlockSpec((TM, wdt), lambda i: (i, 0))
    outs = pl.pallas_call(
        _front_kernel,
        out_shape=(
            sds((R, D_SSM), F32), sds((R, D_POOL), BF16), sds((NB, 16, D_POOL), F32), sds((RS, D_POOL), F32),
            sds((N_PAIRS, 2, wd, wd), BF16),
            sds((N_PAIRS, PAIR_W, 128), BF16), sds((N_PAIRS, PAIR_W, 128), BF16),
            sds((N_PAIRS, PAIR_W, 128), BF16), sds((N_PAIRS, PAIR_W, 128), BF16),
            sds((N_PAIRS, PAIR_W, 128), BF16), sds((N_PAIRS, PAIR_W, 128), BF16),
            sds((N_PAIRS, 8, 128), F32), sds((N_PAIRS, 8, 128), F32), sds((N_PAIRS, 8, 128), F32),
            *cast_shapes,
        ),
        grid=(NB,),
        in_specs=[
            pl.BlockSpec((TM, D_MODEL), lambda i: (step(i), 0)),
            _const_spec((RS, D_MODEL)),
            _const_spec((D_MODEL, D_MODEL)),
            _const_spec((POOL_BUF, DEC_BATCH, D_POOL)),
            _const_spec((4, POOL_GROUP_DIM, POOL_GROUP_DIM)),
            _const_spec((1, D_POOL)),
            g3(1, N), g3(1, N), g3(1, 1), g3(C, N), g3(C, N), g3(C, N), g3(C, N),
            *cast_specs,
        ],
        out_specs=(
            row(D_SSM), row(D_POOL), pl.BlockSpec((1, 16, D_POOL), lambda i: (i, 0, 0)),
            pl.BlockSpec((RS, D_POOL), lambda i: (0, 0)),
            pl.BlockSpec((PREP_PAIRS, 2, wd, wd), lambda i: (step(i), 0, 0, 0)),
            p3(PAIR_W, 128), p3(PAIR_W, 128), p3(PAIR_W, 128), p3(PAIR_W, 128),
            p3(PAIR_W, 128), p3(PAIR_W, 128),
            p3(8, 128), p3(8, 128), p3(8, 128),
            *cast_specs,
        ),
        scratch_shapes=[pltpu.VMEM((TM + 16, D_POOL), F32)],
        compiler_params=_cparams(("arbitrary",)),
        name="front",
    )(xp, xs, w, state_t, w_pool, pool_scale, *prep_args, *to_cast)
    names = ("t", "pre", "pim", "p4re", "p4im", "qre", "qim", "apow_re", "apow_im", "lam4")
    return outs[:4], dict(zip(names, outs[4:4 + N_PREP_OUT])), outs[4 + N_PREP_OUT:]


SLAB_GROUPS = 128 // SSM_GROUP_CH
SLAB_PAIRS = SLAB_GROUPS // 2
N_SLABS = N_SSM_GROUPS // SLAB_GROUPS


def _ssm_pair(q, u, t_ref, pre_ref, pim_ref, p4re_ref, p4im_ref, qre_ref, qim_ref,
              apow_re_ref, apow_im_ref, lam4_ref, dskip_ref, h0re_ref, h0im_ref,
              hp_re_ref, hp_im_ref, hs_re_ref, hs_im_ref, hre_scr, him_scr):
    ub = u.astype(BF16)
    half = PAIR_W // 2
    y = jnp.concatenate([_dot(ub[:, :half], t_ref[q, 0]), _dot(ub[:, half:], t_ref[q, 1])], axis=1)

    ubp = ub[:P_CHUNK_ROWS]
    hre_scr[SCAN_PAD:SCAN_PAD + P_CHUNK_ROWS, :] = _dot(ubp, pre_ref[q])
    him_scr[SCAN_PAD:SCAN_PAD + P_CHUNK_ROWS, :] = _dot(ubp, pim_ref[q])
    kk = lax.broadcasted_iota(jnp.int32, (P_CHUNK_ROWS, 1), 0) % CHUNKS_PER_SEQ
    for s in range(N_SCAN_STEPS):
        d = 1 << s
        ar = apow_re_ref[q, s:s + 1, :]
        ai = apow_im_ref[q, s:s + 1, :]
        hr = hre_scr[SCAN_PAD:SCAN_PAD + P_CHUNK_ROWS, :]
        hi = him_scr[SCAN_PAD:SCAN_PAD + P_CHUNK_ROWS, :]
        pr = hre_scr[SCAN_PAD - d:SCAN_PAD - d + P_CHUNK_ROWS, :]
        pi = him_scr[SCAN_PAD - d:SCAN_PAD - d + P_CHUNK_ROWS, :]
        keep = kk >= d
        hre_scr[SCAN_PAD:SCAN_PAD + P_CHUNK_ROWS, :] = hr + jnp.where(keep, ar * pr - ai * pi, 0.0)
        him_scr[SCAN_PAD:SCAN_PAD + P_CHUNK_ROWS, :] = hi + jnp.where(keep, ar * pi + ai * pr, 0.0)
    hp_re_ref[q] = jnp.zeros((8, 128), F32)
    hp_im_ref[q] = jnp.zeros((8, 128), F32)
    for b in range(BATCH):
        last = SCAN_PAD + (b + 1) * CHUNKS_PER_SEQ - 1
        hp_re_ref[q, b:b + 1, :] = hre_scr[last:last + 1, :]
        hp_im_ref[q, b:b + 1, :] = him_scr[last:last + 1, :]
    prev_ok = kk >= 1
    hprev_re = jnp.where(prev_ok, hre_scr[SCAN_PAD - 1:SCAN_PAD - 1 + P_CHUNK_ROWS, :], 0.0)
    hprev_im = jnp.where(prev_ok, him_scr[SCAN_PAD - 1:SCAN_PAD - 1 + P_CHUNK_ROWS, :], 0.0)
    carry_p = _dot_nt(hprev_re.astype(BF16), qre_ref[q]) + _dot_nt(hprev_im.astype(BF16), qim_ref[q])

    ubs = ub[P_CHUNK_ROWS:]
    h0r = h0re_ref[q]
    h0i = h0im_ref[q]
    l4r = lam4_ref[q, 0:1, :]
    l4i = lam4_ref[q, 1:2, :]
    hs_re_ref[q] = l4r * h0r - l4i * h0i + _dot(ubs, p4re_ref[q])
    hs_im_ref[q] = l4r * h0i + l4i * h0r + _dot(ubs, p4im_ref[q])
    carry_s = _dot_nt(h0r.astype(BF16), qre_ref[q]) + _dot_nt(h0i.astype(BF16), qim_ref[q])

    y = y + jnp.concatenate([carry_p, carry_s], axis=0) + dskip_ref[q] * u
    return jax.nn.gelu(y)


N_SSM_IN, N_SSM_OUT = 13, 5


def _ssm_kernel(u_ref, *all_refs):
    n_cast = (len(all_refs) - N_SSM_IN - N_SSM_OUT - 2) // 2
    _cast_slices(all_refs[N_SSM_IN:N_SSM_IN + n_cast], all_refs[N_SSM_IN + n_cast + N_SSM_OUT:-2])
    refs = all_refs[:N_SSM_IN] + all_refs[N_SSM_IN + n_cast:N_SSM_IN + n_cast + N_SSM_OUT] + all_refs[-2:]
    z_ref = refs[13]
    hre_scr, him_scr = refs[18], refs[19]
    c = SSM_GROUP_CH
    hre_scr[0:SCAN_PAD, :] = jnp.zeros((SCAN_PAD, 128), F32)
    him_scr[0:SCAN_PAD, :] = jnp.zeros((SCAN_PAD, 128), F32)
    xt = []
    for i in range(CHUNK):
        xp = u_ref[pl.ds(i, P_CHUNK_ROWS, stride=CHUNK), :]
        if i < DEC_SEQ:
            xs = u_ref[RP + i * DEC_BATCH:RP + (i + 1) * DEC_BATCH, :]
        else:
            xs = jnp.zeros((DEC_BATCH, 128), F32)
        xt.append(jnp.concatenate([xp, xs], axis=0).T)
    def pair_input(q):
        halves = []
        for e in range(2):
            g = 2 * q + e
            bt = jnp.concatenate([xt[i][g * c:(g + 1) * c, :] for i in range(CHUNK)], axis=0)
            halves.append(bt.T)
        return jnp.concatenate(halves, axis=1)

    zt = []
    u_pair = pair_input(0)
    for q in range(SLAB_PAIRS):
        u_next = pair_input(q + 1) if q + 1 < SLAB_PAIRS else None
        z = _ssm_pair(q, u_pair, *refs[:13], *refs[14:])
        zt.append(z[:, :PAIR_W // 2].T)
        zt.append(z[:, PAIR_W // 2:].T)
        u_pair = u_next
    for i in range(CHUNK):
        zi = jnp.concatenate([zt[g][i * c:(i + 1) * c, :] for g in range(SLAB_GROUPS)], axis=0).T
        z_ref[pl.ds(i, P_CHUNK_ROWS, stride=CHUNK), :] = zi[:P_CHUNK_ROWS]
        if i < DEC_SEQ:
            z_ref[RP + i * DEC_BATCH:RP + (i + 1) * DEC_BATCH, :] = zi[P_CHUNK_ROWS:]


def _ssm(u_ssm, prep, h0re, h0im, to_cast):
    sp = SLAB_PAIRS
    slab3 = lambda a, b: pl.BlockSpec((sp, a, b), lambda s: (s, 0, 0))
    cast_specs, cast_shapes = _cast_specs(to_cast, N_SLABS)
    outs = pl.pallas_call(
        _ssm_kernel,
        out_shape=(
            jax.ShapeDtypeStruct((R, D_SSM), F32),
            jax.ShapeDtypeStruct((N_PAIRS, 8, 128), F32),
            jax.ShapeDtypeStruct((N_PAIRS, 8, 128), F32),
            jax.ShapeDtypeStruct((N_PAIRS, DEC_BATCH, 128), F32),
            jax.ShapeDtypeStruct((N_PAIRS, DEC_BATCH, 128), F32),
            *cast_shapes,
        ),
        grid=(N_SLABS,),
        in_specs=[
            pl.BlockSpec((R, 128), lambda s: (0, s)),
            pl.BlockSpec((sp, 2, 256, 256), lambda s: (s, 0, 0, 0)),
            slab3(PAIR_W, 128), slab3(PAIR_W, 128), slab3(PAIR_W, 128), slab3(PAIR_W, 128),
            slab3(PAIR_W, 128), slab3(PAIR_W, 128),
            slab3(8, 128), slab3(8, 128), slab3(8, 128),
            slab3(1, PAIR_W),
            slab3(DEC_BATCH, 128), slab3(DEC_BATCH, 128),
            *cast_specs,
        ],
        out_specs=(
            pl.BlockSpec((R, 128), lambda s: (0, s)),
            slab3(8, 128), slab3(8, 128), slab3(DEC_BATCH, 128), slab3(DEC_BATCH, 128),
            *cast_specs,
        ),
        scratch_shapes=[pltpu.VMEM((SCAN_PAD + P_CHUNK_ROWS, 128), F32),
                        pltpu.VMEM((SCAN_PAD + P_CHUNK_ROWS, 128), F32)],
        compiler_params=_cparams(("arbitrary",)),
        name="ssm",
    )(u_ssm, prep["t"], prep["pre"], prep["pim"], prep["p4re"], prep["p4im"],
      prep["qre"], prep["qim"], prep["apow_re"], prep["apow_im"], prep["lam4"],
      prep["dskip"], h0re, h0im, *to_cast)
    return outs[:N_SSM_OUT], outs[N_SSM_OUT:]


def _cmul(ar, ai, br, bi):
    return ar * br - ai * bi, ar * bi + ai * br


def _dot3_nt(a, b):
    ah = a.astype(BF16)
    bh = b.astype(BF16)
    al = (a - ah.astype(F32)).astype(BF16)
    bl = (b - bh.astype(F32)).astype(BF16)
    return _dot_nt(ah, bh) + _dot_nt(ah, bl) + _dot_nt(al, bh)


def _ssm_prep_pair(pp, lam_re_ref, lam_im_ref, lstep_ref, bt_re_ref, bt_im_ref, c_re_ref, c_im_ref,
                   t_ref, pre_ref, pim_ref, p4re_ref, p4im_ref, qre_ref, qim_ref,
                   apr_ref, api_ref, lam4_ref):
    L, C, N = CHUNK, SSM_GROUP_CH, SSM_STATE
    w = L * C
    p_re, p_im, p4_re, p4_im, q_re, q_im, a_re, a_im, l4 = [], [], [], [], [], [], [], [], []
    for e in range(2):
        ge = 2 * pp + e
        dt = jnp.exp(lstep_ref[ge])
        lr, li = lam_re_ref[ge], lam_im_ref[ge]
        mag = jnp.exp(lr * dt)
        ang = li * dt
        zr, zi = mag * jnp.cos(ang), mag * jnp.sin(ang)
        den = lr * lr + li * li
        fr = ((zr - 1.0) * lr + zi * li) / den
        fi = (zi * lr - (zr - 1.0) * li) / den
        bbr, bbi = _cmul(fr, fi, bt_re_ref[ge], bt_im_ref[ge])
        pr, pi = [jnp.ones((1, N), F32)], [jnp.zeros((1, N), F32)]
        for _ in range(L):
            nr, ni = _cmul(pr[-1], pi[-1], zr, zi)
            pr.append(nr)
            pi.append(ni)
        stack = lambda blocks, part: jnp.concatenate([b[part] for b in blocks], axis=0)
        blocks = [_cmul(bbr, bbi, pr[L - 1 - i], pi[L - 1 - i]) for i in range(L)]
        p_re.append(stack(blocks, 0))
        p_im.append(stack(blocks, 1))
        blocks4 = [_cmul(bbr, bbi, pr[DEC_SEQ - 1 - i], pi[DEC_SEQ - 1 - i]) for i in range(DEC_SEQ)]
        pad = jnp.zeros(((L - DEC_SEQ) * C, N), F32)
        p4_re.append(jnp.concatenate([stack(blocks4, 0), pad], axis=0))
        p4_im.append(jnp.concatenate([stack(blocks4, 1), pad], axis=0))
        sr, si = [pr[L]], [pi[L]]
        for _ in range(N_SCAN_STEPS - 1):
            nr, ni = _cmul(sr[-1], si[-1], sr[-1], si[-1])
            sr.append(nr)
            si.append(ni)
        a_re.append(jnp.concatenate(sr + [jnp.zeros((8 - N_SCAN_STEPS, N), F32)], axis=0))
        a_im.append(jnp.concatenate(si + [jnp.zeros((8 - N_SCAN_STEPS, N), F32)], axis=0))
        l4.append(jnp.concatenate([pr[DEC_SEQ], pi[DEC_SEQ], jnp.zeros((6, N), F32)], axis=0))

        cr, ci = c_re_ref[ge], c_im_ref[ge]
        ck = [_cmul(cr, ci, pr[k], pi[k]) for k in range(L + 1)]
        q_re.append(stack(ck[1:], 0))
        q_im.append(-stack(ck[1:], 1))
        v = _dot3_nt(bbr, stack(ck[:L], 0)) - _dot3_nt(bbi, stack(ck[:L], 1))
        lane = lax.broadcasted_iota(jnp.int32, (C, w), 1)
        rows = [v] + [jnp.where(lane >= C * i, pltpu.roll(v, C * i, axis=1), 0.0) for i in range(1, L)]
        t_ref[pp, e] = jnp.concatenate(rows, axis=0).astype(BF16)

    def diag_rows(m):
        z = jnp.zeros_like(m[0])
        return jnp.concatenate([jnp.concatenate([m[0], z], axis=1),
                                jnp.concatenate([z, m[1]], axis=1)], axis=0)

    pre_ref[pp] = diag_rows(p_re).astype(BF16)
    pim_ref[pp] = diag_rows(p_im).astype(BF16)
    p4re_ref[pp] = diag_rows(p4_re).astype(BF16)
    p4im_ref[pp] = diag_rows(p4_im).astype(BF16)
    qre_ref[pp] = diag_rows(q_re).astype(BF16)
    qim_ref[pp] = diag_rows(q_im).astype(BF16)
    apr_ref[pp] = jnp.concatenate(a_re, axis=1)
    api_ref[pp] = jnp.concatenate(a_im, axis=1)
    lam4_ref[pp] = jnp.concatenate(l4, axis=1)


def _ssm_prep_args(lambda_re, lambda_im, log_step, b_re, b_im, c_re, c_im):
    return (lambda_re[:, None, :], lambda_im[:, None, :], log_step[:, None, None],
            b_re.transpose(0, 2, 1), b_im.transpose(0, 2, 1), c_re, c_im)


def _mix_kernel(a_ref, z_ref, xp_ref, xs_ref, wglu_ref, bglu_ref, wout_ref, g_ref, b_ref, *refs):
    n_cast = (len(refs) - 2) // 2
    o_ref, ob_ref = refs[n_cast], refs[n_cast + 1]
    _cast_slices(refs[:n_cast], refs[n_cast + 2:])
    is_prompt = pl.program_id(0) < NBP
    n_sub = TM // SUB

    def glu(r):
        z = z_ref[r * SUB:(r + 1) * SUB, :]
        gate = _dot(z.astype(BF16), wglu_ref[...]) + bglu_ref[...]
        return (z * jax.nn.sigmoid(gate)).astype(BF16)

    bmix = glu(0)
    for r in range(n_sub):
        rows = slice(r * SUB, (r + 1) * SUB)
        bmix_next = glu(r + 1) if r + 1 < n_sub else None
        mix = _dot(a_ref[rows, :], wout_ref[0:D_POOL, :]) + _dot(bmix, wout_ref[D_POOL:, :])
        x = jnp.where(is_prompt, xp_ref[rows, :], xs_ref[rows, :])
        h1 = _layer_norm(ALPHA * x + mix, g_ref[...], b_ref[...])
        o_ref[rows, :] = h1
        ob_ref[rows, :] = h1.astype(BF16)
        bmix = bmix_next


def _mix(a, z, xp, xs, w_glu, b_glu, w_out, g, b, to_cast):
    row = lambda w: pl.BlockSpec((TM, w), lambda i: (i, 0))
    cast_specs, cast_shapes = _cast_specs(to_cast, NBP)
    outs = pl.pallas_call(
        _mix_kernel,
        out_shape=(jax.ShapeDtypeStruct((R, D_MODEL), F32), jax.ShapeDtypeStruct((R, D_MODEL), BF16), *cast_shapes),
        grid=(NB,),
        in_specs=[
            row(D_POOL), row(D_SSM),
            pl.BlockSpec((TM, D_MODEL), lambda i: (jnp.minimum(i, NBP - 1), 0)),
            _const_spec((RS, D_MODEL)),
            _const_spec((D_SSM, D_SSM)), _const_spec((1, D_SSM)),
            _const_spec((D_MODEL, D_MODEL)), _const_spec((1, D_MODEL)), _const_spec((1, D_MODEL)),
            *cast_specs,
        ],
        out_specs=(row(D_MODEL), row(D_MODEL), *cast_specs),
        compiler_params=_cparams(("arbitrary",)),
        name="mix_ln1",
    )(a, z, xp, xs, w_glu, b_glu, w_out, g, b, *to_cast)
    return outs[0], outs[1], outs[2:]


def _qproj_sample_kernel(x_ref, w_ref, o_ref):
    q = _dot(x_ref[...], w_ref[...])
    for j in range(DEC_SEQ):
        o_ref[:, j, :] = q[j * DEC_BATCH:(j + 1) * DEC_BATCH, :]
    o_ref[:, DEC_SEQ:, :] = jnp.zeros((DEC_BATCH, Q_PAD - DEC_SEQ, D_MODEL), F32)


def _qproj_sample(h, w):
    return pl.pallas_call(
        _qproj_sample_kernel,
        out_shape=jax.ShapeDtypeStruct((DEC_BATCH, Q_PAD, D_MODEL), F32),
        grid=(1,),
        in_specs=[pl.BlockSpec((RS, D_MODEL), lambda i: (RP // RS, 0)),
                  pl.BlockSpec((D_MODEL, D_MODEL), lambda i: (0, 0))],
        out_specs=pl.BlockSpec((DEC_BATCH, Q_PAD, D_MODEL), lambda i: (0, 0, 0)),
        compiler_params=_cparams(("arbitrary",)),
        name="qproj_sample",
    )(h, w)


def _memproj_kernel(m_ref, w_ref, o_ref, ob_ref, wb_scr):
    @pl.when(pl.program_id(0) == 0)
    def _():
        wb_scr[...] = w_ref[...].astype(BF16)

    k = _dot(m_ref[...].astype(BF16), wb_scr[...])
    ob_ref[...] = k.astype(BF16)
    for h in range(N_XHEADS):
        o_ref[0, 0, :, h, :] = k[:, h * XHEAD_DIM:(h + 1) * XHEAD_DIM]


def _memproj(mem, w, name):
    rows = BATCH * N_MEM
    seq = pl.BlockSpec((N_MEM, D_MODEL), lambda b: (b, 0))
    return pl.pallas_call(
        _memproj_kernel,
        out_shape=(jax.ShapeDtypeStruct((1, BATCH, N_MEM, N_XHEADS, XHEAD_DIM), F32),
                   jax.ShapeDtypeStruct((rows, D_MODEL), BF16)),
        grid=(BATCH,),
        in_specs=[seq, _const_spec((D_MODEL, D_MODEL))],
        out_specs=(pl.BlockSpec((1, 1, N_MEM, N_XHEADS, XHEAD_DIM), lambda b: (0, b, 0, 0, 0)), seq),
        scratch_shapes=[pltpu.VMEM((D_MODEL, D_MODEL), BF16)],
        compiler_params=_cparams(("arbitrary",)),
        name=name,
    )(mem, w)


def _attend(q, k, v, between=None):
    def scores(h):
        sl = slice(h * XHEAD_DIM, (h + 1) * XHEAD_DIM)
        return lax.dot_general(q[:, sl], k[:, sl], (((1,), (1,)), ((), ())),
                               preferred_element_type=F32) * (XHEAD_DIM ** -0.5)

    def head_out(h, s):
        s = s - jnp.max(s, axis=-1, keepdims=True)
        e = jnp.exp(s)
        p = e / jnp.sum(e, axis=-1, keepdims=True)
        return _dot(p.astype(BF16), v[:, h * XHEAD_DIM:(h + 1) * XHEAD_DIM])

    outs = []
    s = scores(0)
    for h in range(N_XHEADS):
        if between is not None:
            between(h)
        s_next = scores(h + 1) if h + 1 < N_XHEADS else None
        outs.append(head_out(h, s))
        s = s_next
    return jnp.concatenate(outs, axis=1)


def _attn_block_kernel(h_ref, q_ref, k_ref, v_ref, os_ref, wo_ref, g_ref, b_ref, *refs):
    n_cast = (len(refs) - 1) // 2
    o_ref = refs[n_cast]
    _cast_slices(refs[:n_cast], refs[n_cast + 1:])
    i = pl.program_id(0)
    n_sub = TM // SUB

    def normed(rows, y):
        o_ref[rows, :] = _layer_norm(ALPHA * h_ref[rows, :] + y, g_ref[...], b_ref[...])

    @pl.when(i < NBP)
    def _():
        prev_rows, prev_o = None, None
        for r in range(n_sub):
            rows = slice(r * SUB, (r + 1) * SUB)
            ys = []
            between = None
            if prev_o is not None:
                between = lambda hh, po=prev_o: ys.append(
                    _dot(po, wo_ref[:, hh * XHEAD_DIM:(hh + 1) * XHEAD_DIM]))
            o = _attend(q_ref[rows, :], k_ref[...], v_ref[...], between).astype(BF16)
            if prev_o is not None:
                normed(prev_rows, jnp.concatenate(ys, axis=1))
            prev_rows, prev_o = rows, o
        normed(prev_rows, _dot(prev_o, wo_ref[...]))

    @pl.when(i == NBP)
    def _():
        for r in range(n_sub):
            rows = slice(r * SUB, (r + 1) * SUB)
            steps = range(r * SUB // DEC_BATCH, (r + 1) * SUB // DEC_BATCH)
            o = jnp.concatenate([os_ref[:, j, :] for j in steps], axis=0).astype(BF16)
            normed(rows, _dot(o, wo_ref[...]))


def _attn_block(h, q_prompt, kb, vb, o_sample, wo, g, b, to_cast):
    kv = pl.BlockSpec((N_MEM, D_MODEL), lambda i: (jnp.minimum(i // BLOCKS_PER_SEQ, BATCH - 1), 0))
    row = pl.BlockSpec((TM, D_MODEL), lambda i: (i, 0))
    qrow = pl.BlockSpec((TM, D_MODEL), lambda i: (jnp.minimum(i, NBP - 1), 0))
    cast_specs, cast_shapes = _cast_specs(to_cast, NBP)
    outs = pl.pallas_call(
        _attn_block_kernel,
        out_shape=(jax.ShapeDtypeStruct((R, D_MODEL), F32), *cast_shapes),
        grid=(NB,),
        in_specs=[row, qrow, kv, kv, _const_spec((DEC_BATCH, Q_PAD, D_MODEL)), _const_spec((D_MODEL, D_MODEL)),
                  _const_spec((1, D_MODEL)), _const_spec((1, D_MODEL)), *cast_specs],
        out_specs=(row, *cast_specs),
        compiler_params=_cparams(("arbitrary",)),
        name="attn_block",
    )(h, q_prompt, kb, vb, o_sample, wo, g, b, *to_cast)
    return outs[0], outs[1:]


ATT_BB = 4
Q_PAD = 8
KV_ROWS = N_MEM * N_XHEADS


QP_ROWS = RP // (DEC_BATCH // ATT_BB)


def _attn_sample_kernel(q_ref, k_ref, v_ref, h_ref, wq_ref, o_ref, qp_ref):
    hb = h_ref[...]
    qcols = D_MODEL // ATT_BB
    shape = (N_XHEADS * Q_PAD, KV_ROWS)
    same_head = (lax.broadcasted_iota(jnp.int32, shape, 0) // Q_PAD
                 == lax.broadcasted_iota(jnp.int32, shape, 1) % N_XHEADS)
    def scores(b):
        q = q_ref[b]
        qs = jnp.concatenate([q[:, h * XHEAD_DIM:(h + 1) * XHEAD_DIM] for h in range(N_XHEADS)], axis=0)
        k = k_ref[0, b].reshape(KV_ROWS, XHEAD_DIM).astype(BF16)
        return lax.dot_general(qs.astype(BF16), k, (((1,), (1,)), ((), ())),
                               preferred_element_type=F32) * (XHEAD_DIM ** -0.5)

    def attend(b, s):
        s = jnp.where(same_head, s, -1e30)
        s = s - jnp.max(s, axis=-1, keepdims=True)
        e = jnp.exp(s)
        p = e / jnp.sum(e, axis=-1, keepdims=True)
        v = v_ref[0, b].reshape(KV_ROWS, XHEAD_DIM).astype(BF16)
        o = _dot(p.astype(BF16), v)
        for h in range(N_XHEADS):
            o_ref[b, :, h * XHEAD_DIM:(h + 1) * XHEAD_DIM] = o[h * Q_PAD:(h + 1) * Q_PAD]

    s_prev = scores(0)
    for b in range(ATT_BB):
        cols = slice(b * qcols, (b + 1) * qcols)
        qp_ref[:, cols] = _dot(hb, wq_ref[:, cols]).astype(BF16)
        s_next = scores(b + 1) if b + 1 < ATT_BB else None
        attend(b, s_prev)
        s_prev = s_next


def _attn_sample(q_pad, k, v, h, wq):
    kv = pl.BlockSpec((1, ATT_BB, N_MEM, N_XHEADS, XHEAD_DIM), lambda i: (0, i, 0, 0, 0))
    qo = pl.BlockSpec((ATT_BB, Q_PAD, D_MODEL), lambda i: (i, 0, 0))
    hq = pl.BlockSpec((QP_ROWS, D_MODEL), lambda i: (i, 0))
    return pl.pallas_call(
        _attn_sample_kernel,
        out_shape=(jax.ShapeDtypeStruct((DEC_BATCH, Q_PAD, D_MODEL), F32),
                   jax.ShapeDtypeStruct((RP, D_MODEL), BF16)),
        grid=(DEC_BATCH // ATT_BB,),
        in_specs=[qo, kv, kv, hq, _const_spec((D_MODEL, D_MODEL))],
        out_specs=(qo, hq),
        compiler_params=_cparams(("arbitrary",)),
        name="attn_sample",
    )(q_pad, k, v, h, wq)


FTM = 1024
FNB = RP // FTM
FBLOCKS_PER_SEQ = SEQ // FTM
FSUB = 512
FSUB_LAST = 128
FSUB_SAMPLE = 256


def _ffn_rows(n_rows, sub, conv_taps, h_ref, hb_scr, g_scr, g_base, wg, wu, wd, cw_ref, cb_ref,
              lng_ref, lnb_ref, y_ref, first, last):
    def gate_up(lo):
        if first:
            h = h_ref[lo:lo + sub, :]
            hb = h.astype(BF16)
            hb_scr[lo:lo + sub, :] = hb
        else:
            h, hb = None, hb_scr[lo:lo + sub, :]
        g = _dot(hb, wg)
        up = _dot(hb, wu)
        g_scr[g_base + lo:g_base + lo + sub, :] = g
        tap0, tap1 = conv_taps(lo, sub)
        gc = cb_ref[...] + cw_ref[0:1, :] * tap0 + cw_ref[1:2, :] * tap1 + cw_ref[2:3, :] * g
        return h, (jax.nn.silu(gc) * up).astype(BF16)

    def down(lo, h, act):
        y = _dot(act, wd) + (ALPHA * h if first else y_ref[lo:lo + sub, :])
        y_ref[lo:lo + sub, :] = _layer_norm(y, lng_ref[...], lnb_ref[...]) if last else y

    n_sub = n_rows // sub
    pending = gate_up(0)
    for r in range(n_sub):
        nxt = gate_up((r + 1) * sub) if r + 1 < n_sub else None
        down(r * sub, *pending)
        pending = nxt


def _ffn_chunk(f, run):
    @pl.when(f == 0)
    def _():
        run(True, False)

    @pl.when(jnp.logical_and(f > 0, f < NF - 1))
    def _():
        run(False, False)

    @pl.when(f == NF - 1)
    def _():
        run(False, True)


def _ffn_prompt_kernel(h_ref, wg_ref, wu_ref, wd_ref, cw_ref, cb_ref, lng_ref, lnb_ref,
                       y_ref, gtail_ref, hb_scr, g_scr, carry_scr):
    i = pl.program_id(0)
    f = pl.program_id(1)
    first_block = (i % FBLOCKS_PER_SEQ) == 0

    @pl.when(first_block)
    def _():
        g_scr[0:8, :] = jnp.zeros((8, TF), F32)

    @pl.when(jnp.logical_not(first_block))
    def _():
        g_scr[0:8, :] = carry_scr[f]

    taps = lambda lo, sub: (g_scr[6 + lo:6 + lo + sub, :], g_scr[7 + lo:7 + lo + sub, :])
    _ffn_chunk(f, lambda first, last: _ffn_rows(
        FTM, FSUB_LAST if last else FSUB, taps, h_ref, hb_scr, g_scr, 8, wg_ref[...], wu_ref[...], wd_ref[...], cw_ref, cb_ref,
        lng_ref, lnb_ref, y_ref, first, last))
    tail = g_scr[FTM:FTM + 8, :]
    carry_scr[f] = tail
    gtail_ref[0] = tail


def _ffn_sample_kernel(h_ref, wg_ref, wu_ref, wd_ref, cw_ref, cb_ref, lng_ref, lnb_ref, st_ref,
                       y_ref, gnew_ref, wdb_ref, hb_scr, g_scr):
    f = pl.program_id(1)
    n_st = 2 * DEC_BATCH
    wdb_ref[...] = wd_ref[...].astype(BF16)
    g_scr[0:n_st, :] = st_ref[...]
    taps = lambda lo, sub: (g_scr[lo:lo + sub, :], g_scr[DEC_BATCH + lo:DEC_BATCH + lo + sub, :])
    _ffn_chunk(f, lambda first, last: _ffn_rows(
        RS, FSUB_SAMPLE, taps, h_ref, hb_scr, g_scr, n_st, wg_ref[...], wu_ref[...], wdb_ref[...], cw_ref, cb_ref,
        lng_ref, lnb_ref, y_ref, first, last))
    for t in range(CONV_W - 1):
        gnew_ref[:, t, :] = g_scr[RS + t * DEC_BATCH:RS + (t + 1) * DEC_BATCH, :]


def _ffn_specs(tm, row_block_of):
    return [
        pl.BlockSpec((tm, D_MODEL), lambda i, f: (row_block_of(i), 0)),
        pl.BlockSpec((D_MODEL, TF), lambda i, f: (0, f)),
        pl.BlockSpec((D_MODEL, TF), lambda i, f: (0, f)),
        pl.BlockSpec((TF, D_MODEL), lambda i, f: (f, 0)),
        pl.BlockSpec((CONV_W, TF), lambda i, f: (0, f)),
        pl.BlockSpec((1, TF), lambda i, f: (0, f)),
        pl.BlockSpec((1, D_MODEL), lambda i, f: (0, 0)),
        pl.BlockSpec((1, D_MODEL), lambda i, f: (0, 0)),
    ]


def _ffn_prompt(h, wg, wu, wd, cw, cb, lng, lnb):
    return pl.pallas_call(
        _ffn_prompt_kernel,
        out_shape=(jax.ShapeDtypeStruct((RP, D_MODEL), F32),
                   jax.ShapeDtypeStruct((FNB, 8, D_FF), F32)),
        grid=(FNB, NF),
        in_specs=_ffn_specs(FTM, lambda i: i),
        out_specs=(pl.BlockSpec((FTM, D_MODEL), lambda i, f: (i, 0)),
                   pl.BlockSpec((1, 8, TF), lambda i, f: (i, 0, f))),
        scratch_shapes=[pltpu.VMEM((FTM, D_MODEL), BF16),
                        pltpu.VMEM((FTM + 8, TF), F32),
                        pltpu.VMEM((NF, 8, TF), F32)],
        compiler_params=_cparams(("arbitrary", "arbitrary"), VMEM_LIMIT_FFN),
        name="ffn_prompt",
    )(h, wg, wu, wd, cw, cb, lng, lnb)


def _ffn_sample(h, wg, wu, wd, cw, cb, lng, lnb, conv_state):
    n_st = 2 * DEC_BATCH
    specs = _ffn_specs(RS, lambda i: RP // RS)
    return pl.pallas_call(
        _ffn_sample_kernel,
        out_shape=(jax.ShapeDtypeStruct((RS, D_MODEL), F32),
                   jax.ShapeDtypeStruct((DEC_BATCH, CONV_W - 1, D_FF), F32),
                   jax.ShapeDtypeStruct((D_FF, D_MODEL), BF16)),
        grid=(1, NF),
        in_specs=specs + [pl.BlockSpec((n_st, TF), lambda i, f: (0, f))],
        out_specs=(pl.BlockSpec((RS, D_MODEL), lambda i, f: (0, 0)),
                   pl.BlockSpec((DEC_BATCH, CONV_W - 1, TF), lambda i, f: (0, 0, f)),
                   specs[3]),
        scratch_shapes=[pltpu.VMEM((RS, D_MODEL), BF16),
                        pltpu.VMEM((n_st + RS, TF), F32)],
        compiler_params=_cparams(("arbitrary", "arbitrary")),
        name="ffn_sample",
    )(h, wg, wu, wd, cw, cb, lng, lnb, conv_state)


def _state_to_pairs(s):
    return s.reshape(s.shape[0], N_PAIRS, 128).transpose(1, 0, 2)


def _state_from_pairs(s):
    return s.transpose(1, 0, 2).reshape(1, s.shape[1], N_SSM_GROUPS, SSM_STATE)


def kernel(x_prompt, x_sample, mem_prompt, state_pool, state_ssm_re, state_ssm_im, state_conv, cache_mem_k, cache_mem_v, w_in, w_pool, pool_scale, lambda_re, lambda_im, log_step, b_re, b_im, c_re, c_im, d_skip, w_glu, b_glu, w_out, ln1_g, ln1_b, w_q, w_k, w_v, w_o, ln2_g, ln2_b, w_gate, w_up, conv_w, conv_b, w_down, ln3_g, ln3_b):
    bf = lambda w: w[0].astype(BF16)
    row = lambda v: v[0].reshape(1, -1)
    xp = x_prompt.reshape(RP, D_MODEL)
    xs = x_sample.transpose(1, 0, 2).reshape(RS, D_MODEL)

    prep_args = _ssm_prep_args(lambda_re[0], lambda_im[0], log_step[0], b_re[0], b_im[0], c_re[0], c_im[0])
    (u_ssm, a, utail, us_pool), prep, (w_glu_b, w_out_b) = _front(
        xp, xs, bf(w_in), state_pool[0].transpose(1, 0, 2), bf(w_pool), row(pool_scale), prep_args,
        (w_glu[0], w_out[0]))
    prep["dskip"] = jnp.broadcast_to(d_skip[0].reshape(N_PAIRS, 2, 1, SSM_GROUP_CH),
                                     (N_PAIRS, 2, CHUNK, SSM_GROUP_CH)).reshape(N_PAIRS, 1, PAIR_W)
    (z, hp_re, hp_im, hs_re, hs_im), (w_q_b, w_o_b) = _ssm(
        u_ssm, prep, _state_to_pairs(state_ssm_re[0]), _state_to_pairs(state_ssm_im[0]), (w_q[0], w_o[0]))
    h1, h1b, (wg,) = _mix(a, z, xp, xs, w_glu_b, row(b_glu), w_out_b, row(ln1_g), row(ln1_b), (w_gate[0],))

    mem = mem_prompt.reshape(BATCH * N_MEM, D_MODEL)
    mk, mkb = _memproj(mem, w_k[0], "mem_k")
    mv, mvb = _memproj(mem, w_v[0], "mem_v")
    o_s, q_p = _attn_sample(_qproj_sample(h1b, w_q_b), cache_mem_k, cache_mem_v, h1b, w_q_b)
    h2, (wu,) = _attn_block(h1, q_p, mkb, mvb, o_s, w_o_b, row(ln2_g), row(ln2_b), (w_up[0],))

    cw, cb = conv_w[0], row(conv_b)
    conv_st = state_conv[0].transpose(1, 0, 2).reshape(2 * DEC_BATCH, D_FF)
    y_s, g_new, wd = _ffn_sample(h2, wg, wu, w_down[0], cw, cb, row(ln3_g), row(ln3_b), conv_st)
    y_p, gtail = _ffn_prompt(h2, wg, wu, wd, cw, cb, row(ln3_g), row(ln3_b))

    y_prompt = y_p.reshape(BATCH, SEQ, D_MODEL)
    y_sample = y_s.reshape(DEC_SEQ, DEC_BATCH, D_MODEL).transpose(1, 0, 2)
    p_pool = utail[BLOCKS_PER_SEQ - 1:NBP:BLOCKS_PER_SEQ, 16 - POOL_BUF:][None]
    s_ext = jnp.concatenate([state_pool[0], us_pool.reshape(DEC_SEQ, DEC_BATCH, D_POOL).transpose(1, 0, 2)], axis=1)
    s_pool = s_ext[None, :, DEC_SEQ:]
    p_conv = gtail[FBLOCKS_PER_SEQ - 1::FBLOCKS_PER_SEQ, 6:8][None]
    s_conv = g_new[None]
    return (y_prompt, y_sample,
            p_pool, _state_from_pairs(hp_re[:, :BATCH]), _state_from_pairs(hp_im[:, :BATCH]), p_conv,
            mk, mv,
            s_pool, _state_from_pairs(hs_re), _state_from_pairs(hs_im), s_conv)
```

```python
import jax
import jax.numpy as jnp
from jax import lax
from jax.experimental import pallas as pl
from jax.experimental.pallas import tpu as pltpu

F32 = jnp.float32
BF16 = jnp.bfloat16

D_MODEL = 2048
BATCH = 4
SEQ = 2048
DEC_BATCH = 128
DEC_SEQ = 4
PAST_LEN = 16384
D_POOL = 1024
D_SSM = 1024
POOL_WINDOWS = (2, 4, 8, 16)
POOL_GROUP_DIM = 256
POOL_BUF = 15
SSM_GROUP_CH = 16
N_SSM_GROUPS = 64
SSM_STATE = 64
N_MEM = 256
N_XHEADS = 4
XHEAD_DIM = 512
D_FF = 5632
CONV_W = 3
ALPHA = 2.0 ** 0.25
LN_EPS = 1e-5

RP = BATCH * SEQ
RS = DEC_BATCH * DEC_SEQ
R = RP + RS
TM = 512
NB = R // TM
NBP = RP // TM
BLOCKS_PER_SEQ = SEQ // TM

CHUNK = 16
N_PAIRS = N_SSM_GROUPS // 2
PAIR_W = 2 * CHUNK * SSM_GROUP_CH
CHUNKS_PER_SEQ = SEQ // CHUNK
P_CHUNK_ROWS = BATCH * CHUNKS_PER_SEQ
SCAN_PAD = 64
N_SCAN_STEPS = 7

SUB = 256
TF = 512
NF = D_FF // TF

VMEM_LIMIT = 56 * 1024 * 1024
VMEM_LIMIT_FFN = 60 * 1024 * 1024


def _cparams(sem, vmem_limit=VMEM_LIMIT):
    return pltpu.CompilerParams(dimension_semantics=sem, vmem_limit_bytes=vmem_limit)


def _const_spec(shape):
    n = len(shape)
    return pl.BlockSpec(shape, lambda *_: (0,) * n, pipeline_mode=pl.Buffered(1))


def _layer_norm(x, g, b):
    mu = jnp.mean(x, axis=-1, keepdims=True)
    xc = x - mu
    var = jnp.mean(xc * xc, axis=-1, keepdims=True)
    return xc * lax.rsqrt(var + LN_EPS) * g + b


def _dot(a, b):
    return jnp.dot(a, b, preferred_element_type=F32)


def _dot_nt(a, b):
    return lax.dot_general(a, b, (((1,), (1,)), ((), ())), preferred_element_type=F32)


def _cast_spec(a, n_steps):
    return pl.BlockSpec((a.shape[0] // n_steps, a.shape[1]), lambda i: (jnp.minimum(i, n_steps - 1), 0))


def _cast_specs(arrays, n_steps):
    specs = [_cast_spec(a, n_steps) for a in arrays]
    return specs, [jax.ShapeDtypeStruct(a.shape, BF16) for a in arrays]


def _cast_slices(in_refs, out_refs):
    for src_ref, dst_ref in zip(in_refs, out_refs):
        dst_ref[...] = src_ref[...].astype(BF16)


N_PREP_IN, N_PREP_OUT = 7, 10
POOL_HIST = 32
PREP_PAIRS = N_PAIRS // NBP


def _pool_mix(g, pooled, w_ref, scale_ref):
    sl = slice(g * POOL_GROUP_DIM, (g + 1) * POOL_GROUP_DIM)
    return (_dot(pooled.astype(BF16), w_ref[g]) * scale_ref[:, sl]).astype(BF16)


def _front_kernel(xp_ref, xs_ref, w_ref, st_ref, wp_ref, scale_ref, *refs):
    prep_in, refs = refs[:N_PREP_IN], refs[N_PREP_IN:]
    n_cast = (len(refs) - 4 - N_PREP_OUT - 2) // 2
    cast_in, refs = refs[:n_cast], refs[n_cast:]
    ussm_ref, a_ref, utail_ref, us_ref = refs[:4]
    prep_out, cast_out = refs[4:4 + N_PREP_OUT], refs[4 + N_PREP_OUT:-2]
    ext_ref, lvl_ref = refs[-2:]
    i = pl.program_id(0)

    xb = jnp.where(i < NBP, xp_ref[...], xs_ref[...]).astype(BF16)

    first = (i % BLOCKS_PER_SEQ) == 0
    hist = POOL_HIST
    ext_ref[0:hist, :] = jnp.where(first, 0.0, ext_ref[TM:TM + hist, :])
    ext_ref[hist:hist + TM, :] = _dot(xb, w_ref[:, 0:D_POOL])
    utail_ref[0] = ext_ref[hist + TM - 16:hist + TM, :]
    pos = (i % BLOCKS_PER_SEQ) * TM + lax.broadcasted_iota(jnp.int32, (TM, 1), 0)
    qc = D_SSM // len(POOL_WINDOWS)
    for g, w in enumerate(POOL_WINDOWS):
        ussm_ref[:, g * qc:(g + 1) * qc] = _dot(xb, w_ref[:, D_POOL + g * qc:D_POOL + (g + 1) * qc])
        sl = slice(g * POOL_GROUP_DIM, (g + 1) * POOL_GROUP_DIM)
        read = lambda lo, n: ext_ref[lo:lo + n, sl]
        for k in range(1, g + 2):
            lo, n = 8 * k, hist + TM - 8 * k
            val = read(lo, n) + read(lo - (1 << (k - 1)), n)
            if k < g + 1:
                buf = lvl_ref.at[k % 2]
                buf[lo:lo + n, :] = val
                read = lambda lo_, n_, buf=buf: buf[lo_:lo_ + n_, :]
        acc = val[hist - 8 * (g + 1):, :]
        cnt = jnp.minimum(pos + 1, w).astype(F32)
        a_ref[:, sl] = _pool_mix(g, acc / cnt - ext_ref[hist:hist + TM, sl], wp_ref, scale_ref)

    for pp in range(PREP_PAIRS):
        _ssm_prep_pair(pp, *prep_in, *prep_out)
    _cast_slices(cast_in, cast_out)

    @pl.when(i == NBP)
    def _():
        us_ref[...] = ext_ref[hist:hist + TM, :]
        tok = lambda t, sl: ext_ref[hist + t * DEC_BATCH:hist + (t + 1) * DEC_BATCH, sl]
        for j in range(DEC_SEQ):
            rows = slice(j * DEC_BATCH, (j + 1) * DEC_BATCH)
            for g, w in enumerate(POOL_WINDOWS):
                sl = slice(g * POOL_GROUP_DIM, (g + 1) * POOL_GROUP_DIM)
                acc = tok(j, sl)
                for k in range(1, w):
                    e = POOL_BUF + j - k
                    acc = acc + (tok(e - POOL_BUF, sl) if e >= POOL_BUF else st_ref[e, :, sl])
                cnt = float(min(PAST_LEN + j + 1, w))
                a_ref[rows, sl] = _pool_mix(g, acc / cnt - tok(j, sl), wp_ref, scale_ref)


def _front(xp, xs, w, state_t, w_pool, pool_scale, prep_args, to_cast):
    G, N, C, L = N_SSM_GROUPS, SSM_STATE, SSM_GROUP_CH, CHUNK
    wd = L * C
    step = lambda i: jnp.minimum(i, NBP - 1)
    g3 = lambda a, b: pl.BlockSpec((2 * PREP_PAIRS, a, b), lambda i: (step(i), 0, 0))
    p3 = lambda a, b: pl.BlockSpec((PREP_PAIRS, a, b), lambda i: (step(i), 0, 0))
    sds = jax.ShapeDtypeStruct
    cast_specs, cast_shapes = _cast_specs(to_cast, NBP)
    row = lambda wdt: pl.BlockSpec((TM, wdt), lambda i: (i, 0))
    outs = pl.pallas_call(
        _front_kernel,
        out_shape=(
            sds((R, D_SSM), F32), sds((R, D_POOL), BF16), sds((NB, 16, D_POOL), F32), sds((RS, D_POOL), F32),
            sds((N_PAIRS, 2, wd, wd), BF16),
            sds((N_PAIRS, PAIR_W, 128), BF16), sds((N_PAIRS, PAIR_W, 128), BF16),
            sds((N_PAIRS, PAIR_W, 128), BF16), sds((N_PAIRS, PAIR_W, 128), BF16),
            sds((N_PAIRS, PAIR_W, 128), BF16), sds((N_PAIRS, PAIR_W, 128), BF16),
            sds((N_PAIRS, 8, 128), F32), sds((N_PAIRS, 8, 128), F32), sds((N_PAIRS, 8, 128), F32),
            *cast_shapes,
        ),
        grid=(NB,),
        in_specs=[
            pl.BlockSpec((TM, D_MODEL), lambda i: (step(i), 0)),
            _const_spec((RS, D_MODEL)),
            _const_spec((D_MODEL, D_MODEL)),
            _const_spec((POOL_BUF, DEC_BATCH, D_POOL)),
            _const_spec((4, POOL_GROUP_DIM, POOL_GROUP_DIM)),
            _const_spec((1, D_POOL)),
            g3(1, N), g3(1, N), g3(1, 1), g3(C, N), g3(C, N), g3(C, N), g3(C, N),
            *cast_specs,
        ],
        out_specs=(
            row(D_SSM), row(D_POOL), pl.BlockSpec((1, 16, D_POOL), lambda i: (i, 0, 0)),
            pl.BlockSpec((RS, D_POOL), lambda i: (0, 0)),
            pl.BlockSpec((PREP_PAIRS, 2, wd, wd), lambda i: (step(i), 0, 0, 0)),
            p3(PAIR_W, 128), p3(PAIR_W, 128), p3(PAIR_W, 128), p3(PAIR_W, 128),
            p3(PAIR_W, 128), p3(PAIR_W, 128),
            p3(8, 128), p3(8, 128), p3(8, 128),
            *cast_specs,
        ),
        scratch_shapes=[pltpu.VMEM((TM + POOL_HIST, D_POOL), F32),
                        pltpu.VMEM((2, TM + POOL_HIST, POOL_GROUP_DIM), F32)],
        compiler_params=_cparams(("arbitrary",)),
        name="front",
    )(xp, xs, w, state_t, w_pool, pool_scale, *prep_args, *to_cast)
    names = ("t", "pre", "pim", "p4re", "p4im", "qre", "qim", "apow_re", "apow_im", "lam4")
    return outs[:4], dict(zip(names, outs[4:4 + N_PREP_OUT])), outs[4 + N_PREP_OUT:]


SLAB_GROUPS = 128 // SSM_GROUP_CH
SLAB_PAIRS = SLAB_GROUPS // 2
N_SLABS = N_SSM_GROUPS // SLAB_GROUPS


def _ssm_pair(q, u, t_ref, pre_ref, pim_ref, p4re_ref, p4im_ref, qre_ref, qim_ref,
              apow_re_ref, apow_im_ref, lam4_ref, dskip_ref, h0re_ref, h0im_ref,
              hp_re_ref, hp_im_ref, hs_re_ref, hs_im_ref, hre_scr, him_scr):
    ub = u.astype(BF16)
    half = PAIR_W // 2
    y = jnp.concatenate([_dot(ub[:, :half], t_ref[q, 0]), _dot(ub[:, half:], t_ref[q, 1])], axis=1)

    ubp = ub[:P_CHUNK_ROWS]
    hre_scr[SCAN_PAD:SCAN_PAD + P_CHUNK_ROWS, :] = _dot(ubp, pre_ref[q])
    him_scr[SCAN_PAD:SCAN_PAD + P_CHUNK_ROWS, :] = _dot(ubp, pim_ref[q])
    kk = lax.broadcasted_iota(jnp.int32, (P_CHUNK_ROWS, 1), 0) % CHUNKS_PER_SEQ
    for s in range(N_SCAN_STEPS):
        d = 1 << s
        ar = apow_re_ref[q, s:s + 1, :]
        ai = apow_im_ref[q, s:s + 1, :]
        hr = hre_scr[SCAN_PAD:SCAN_PAD + P_CHUNK_ROWS, :]
        hi = him_scr[SCAN_PAD:SCAN_PAD + P_CHUNK_ROWS, :]
        pr = hre_scr[SCAN_PAD - d:SCAN_PAD - d + P_CHUNK_ROWS, :]
        pi = him_scr[SCAN_PAD - d:SCAN_PAD - d + P_CHUNK_ROWS, :]
        keep = kk >= d
        hre_scr[SCAN_PAD:SCAN_PAD + P_CHUNK_ROWS, :] = hr + jnp.where(keep, ar * pr - ai * pi, 0.0)
        him_scr[SCAN_PAD:SCAN_PAD + P_CHUNK_ROWS, :] = hi + jnp.where(keep, ar * pi + ai * pr, 0.0)
    hp_re_ref[q] = jnp.zeros((8, 128), F32)
    hp_im_ref[q] = jnp.zeros((8, 128), F32)
    for b in range(BATCH):
        last = SCAN_PAD + (b + 1) * CHUNKS_PER_SEQ - 1
        hp_re_ref[q, b:b + 1, :] = hre_scr[last:last + 1, :]
        hp_im_ref[q, b:b + 1, :] = him_scr[last:last + 1, :]
    prev_ok = kk >= 1
    hprev_re = jnp.where(prev_ok, hre_scr[SCAN_PAD - 1:SCAN_PAD - 1 + P_CHUNK_ROWS, :], 0.0)
    hprev_im = jnp.where(prev_ok, him_scr[SCAN_PAD - 1:SCAN_PAD - 1 + P_CHUNK_ROWS, :], 0.0)
    carry_p = _dot_nt(hprev_re.astype(BF16), qre_ref[q]) + _dot_nt(hprev_im.astype(BF16), qim_ref[q])

    ubs = ub[P_CHUNK_ROWS:]
    h0r = h0re_ref[q]
    h0i = h0im_ref[q]
    l4r = lam4_ref[q, 0:1, :]
    l4i = lam4_ref[q, 1:2, :]
    hs_re_ref[q] = l4r * h0r - l4i * h0i + _dot(ubs, p4re_ref[q])
    hs_im_ref[q] = l4r * h0i + l4i * h0r + _dot(ubs, p4im_ref[q])
    carry_s = _dot_nt(h0r.astype(BF16), qre_ref[q]) + _dot_nt(h0i.astype(BF16), qim_ref[q])

    y = y + jnp.concatenate([carry_p, carry_s], axis=0) + dskip_ref[q] * u
    return jax.nn.gelu(y)


N_SSM_IN, N_SSM_OUT = 13, 5


def _ssm_kernel(u_ref, *all_refs):
    n_cast = (len(all_refs) - N_SSM_IN - N_SSM_OUT - 2) // 2
    _cast_slices(all_refs[N_SSM_IN:N_SSM_IN + n_cast], all_refs[N_SSM_IN + n_cast + N_SSM_OUT:-2])
    refs = all_refs[:N_SSM_IN] + all_refs[N_SSM_IN + n_cast:N_SSM_IN + n_cast + N_SSM_OUT] + all_refs[-2:]
    z_ref = refs[13]
    hre_scr, him_scr = refs[18], refs[19]
    c = SSM_GROUP_CH
    hre_scr[0:SCAN_PAD, :] = jnp.zeros((SCAN_PAD, 128), F32)
    him_scr[0:SCAN_PAD, :] = jnp.zeros((SCAN_PAD, 128), F32)
    xt = []
    for i in range(CHUNK):
        xp = u_ref[pl.ds(i, P_CHUNK_ROWS, stride=CHUNK), :]
        if i < DEC_SEQ:
            xs = u_ref[RP + i * DEC_BATCH:RP + (i + 1) * DEC_BATCH, :]
        else:
            xs = jnp.zeros((DEC_BATCH, 128), F32)
        xt.append(jnp.concatenate([xp, xs], axis=0).T)
    def pair_input(q):
        halves = []
        for e in range(2):
            g = 2 * q + e
            bt = jnp.concatenate([xt[i][g * c:(g + 1) * c, :] for i in range(CHUNK)], axis=0)
            halves.append(bt.T)
        return jnp.concatenate(halves, axis=1)

    zt = []
    u_pair = pair_input(0)
    for q in range(SLAB_PAIRS):
        u_next = pair_input(q + 1) if q + 1 < SLAB_PAIRS else None
        z = _ssm_pair(q, u_pair, *refs[:13], *refs[14:])
        zt.append(z[:, :PAIR_W // 2].T)
        zt.append(z[:, PAIR_W // 2:].T)
        u_pair = u_next
    for i in range(CHUNK):
        zi = jnp.concatenate([zt[g][i * c:(i + 1) * c, :] for g in range(SLAB_GROUPS)], axis=0).T
        z_ref[pl.ds(i, P_CHUNK_ROWS, stride=CHUNK), :] = zi[:P_CHUNK_ROWS]
        if i < DEC_SEQ:
            z_ref[RP + i * DEC_BATCH:RP + (i + 1) * DEC_BATCH, :] = zi[P_CHUNK_ROWS:]


def _ssm(u_ssm, prep, h0re, h0im, to_cast):
    sp = SLAB_PAIRS
    slab3 = lambda a, b: pl.BlockSpec((sp, a, b), lambda s: (s, 0, 0))
    cast_specs, cast_shapes = _cast_specs(to_cast, N_SLABS)
    outs = pl.pallas_call(
        _ssm_kernel,
        out_shape=(
            jax.ShapeDtypeStruct((R, D_SSM), F32),
            jax.ShapeDtypeStruct((N_PAIRS, 8, 128), F32),
            jax.ShapeDtypeStruct((N_PAIRS, 8, 128), F32),
            jax.ShapeDtypeStruct((N_PAIRS, DEC_BATCH, 128), F32),
            jax.ShapeDtypeStruct((N_PAIRS, DEC_BATCH, 128), F32),
            *cast_shapes,
        ),
        grid=(N_SLABS,),
        in_specs=[
            pl.BlockSpec((R, 128), lambda s: (0, s)),
            pl.BlockSpec((sp, 2, 256, 256), lambda s: (s, 0, 0, 0)),
            slab3(PAIR_W, 128), slab3(PAIR_W, 128), slab3(PAIR_W, 128), slab3(PAIR_W, 128),
            slab3(PAIR_W, 128), slab3(PAIR_W, 128),
            slab3(8, 128), slab3(8, 128), slab3(8, 128),
            slab3(1, PAIR_W),
            slab3(DEC_BATCH, 128), slab3(DEC_BATCH, 128),
            *cast_specs,
        ],
        out_specs=(
            pl.BlockSpec((R, 128), lambda s: (0, s)),
            slab3(8, 128), slab3(8, 128), slab3(DEC_BATCH, 128), slab3(DEC_BATCH, 128),
            *cast_specs,
        ),
        scratch_shapes=[pltpu.VMEM((SCAN_PAD + P_CHUNK_ROWS, 128), F32),
                        pltpu.VMEM((SCAN_PAD + P_CHUNK_ROWS, 128), F32)],
        compiler_params=_cparams(("arbitrary",)),
        name="ssm",
    )(u_ssm, prep["t"], prep["pre"], prep["pim"], prep["p4re"], prep["p4im"],
      prep["qre"], prep["qim"], prep["apow_re"], prep["apow_im"], prep["lam4"],
      prep["dskip"], h0re, h0im, *to_cast)
    return outs[:N_SSM_OUT], outs[N_SSM_OUT:]


def _cmul(ar, ai, br, bi):
    return ar * br - ai * bi, ar * bi + ai * br


def _dot3_nt(a, b):
    ah = a.astype(BF16)
    bh = b.astype(BF16)
    al = (a - ah.astype(F32)).astype(BF16)
    bl = (b - bh.astype(F32)).astype(BF16)
    return _dot_nt(ah, bh) + _dot_nt(ah, bl) + _dot_nt(al, bh)


def _ssm_prep_pair(pp, lam_re_ref, lam_im_ref, lstep_ref, bt_re_ref, bt_im_ref, c_re_ref, c_im_ref,
                   t_ref, pre_ref, pim_ref, p4re_ref, p4im_ref, qre_ref, qim_ref,
                   apr_ref, api_ref, lam4_ref):
    L, C, N = CHUNK, SSM_GROUP_CH, SSM_STATE
    w = L * C
    p_re, p_im, p4_re, p4_im, q_re, q_im, a_re, a_im, l4 = [], [], [], [], [], [], [], [], []
    for e in range(2):
        ge = 2 * pp + e
        dt = jnp.exp(lstep_ref[ge])
        lr, li = lam_re_ref[ge], lam_im_ref[ge]
        mag = jnp.exp(lr * dt)
        ang = li * dt
        zr, zi = mag * jnp.cos(ang), mag * jnp.sin(ang)
        den = lr * lr + li * li
        fr = ((zr - 1.0) * lr + zi * li) / den
        fi = (zi * lr - (zr - 1.0) * li) / den
        bbr, bbi = _cmul(fr, fi, bt_re_ref[ge], bt_im_ref[ge])
        pr, pi = [jnp.ones((1, N), F32)], [jnp.zeros((1, N), F32)]
        for _ in range(L):
            nr, ni = _cmul(pr[-1], pi[-1], zr, zi)
            pr.append(nr)
            pi.append(ni)
        stack = lambda blocks, part: jnp.concatenate([b[part] for b in blocks], axis=0)
        blocks = [_cmul(bbr, bbi, pr[L - 1 - i], pi[L - 1 - i]) for i in range(L)]
        p_re.append(stack(blocks, 0))
        p_im.append(stack(blocks, 1))
        blocks4 = [_cmul(bbr, bbi, pr[DEC_SEQ - 1 - i], pi[DEC_SEQ - 1 - i]) for i in range(DEC_SEQ)]
        pad = jnp.zeros(((L - DEC_SEQ) * C, N), F32)
        p4_re.append(jnp.concatenate([stack(blocks4, 0), pad], axis=0))
        p4_im.append(jnp.concatenate([stack(blocks4, 1), pad], axis=0))
        sr, si = [pr[L]], [pi[L]]
        for _ in range(N_SCAN_STEPS - 1):
            nr, ni = _cmul(sr[-1], si[-1], sr[-1], si[-1])
            sr.append(nr)
            si.append(ni)
        a_re.append(jnp.concatenate(sr + [jnp.zeros((8 - N_SCAN_STEPS, N), F32)], axis=0))
        a_im.append(jnp.concatenate(si + [jnp.zeros((8 - N_SCAN_STEPS, N), F32)], axis=0))
        l4.append(jnp.concatenate([pr[DEC_SEQ], pi[DEC_SEQ], jnp.zeros((6, N), F32)], axis=0))

        cr, ci = c_re_ref[ge], c_im_ref[ge]
        ck = [_cmul(cr, ci, pr[k], pi[k]) for k in range(L + 1)]
        q_re.append(stack(ck[1:], 0))
        q_im.append(-stack(ck[1:], 1))
        v = _dot3_nt(bbr, stack(ck[:L], 0)) - _dot3_nt(bbi, stack(ck[:L], 1))
        lane = lax.broadcasted_iota(jnp.int32, (C, w), 1)
        rows = [v] + [jnp.where(lane >= C * i, pltpu.roll(v, C * i, axis=1), 0.0) for i in range(1, L)]
        t_ref[pp, e] = jnp.concatenate(rows, axis=0).astype(BF16)

    def diag_rows(m):
        z = jnp.zeros_like(m[0])
        return jnp.concatenate([jnp.concatenate([m[0], z], axis=1),
                                jnp.concatenate([z, m[1]], axis=1)], axis=0)

    pre_ref[pp] = diag_rows(p_re).astype(BF16)
    pim_ref[pp] = diag_rows(p_im).astype(BF16)
    p4re_ref[pp] = diag_rows(p4_re).astype(BF16)
    p4im_ref[pp] = diag_rows(p4_im).astype(BF16)
    qre_ref[pp] = diag_rows(q_re).astype(BF16)
    qim_ref[pp] = diag_rows(q_im).astype(BF16)
    apr_ref[pp] = jnp.concatenate(a_re, axis=1)
    api_ref[pp] = jnp.concatenate(a_im, axis=1)
    lam4_ref[pp] = jnp.concatenate(l4, axis=1)


def _ssm_prep_args(lambda_re, lambda_im, log_step, b_re, b_im, c_re, c_im):
    return (lambda_re[:, None, :], lambda_im[:, None, :], log_step[:, None, None],
            b_re.transpose(0, 2, 1), b_im.transpose(0, 2, 1), c_re, c_im)


def _mix_kernel(a_ref, z_ref, xp_ref, xs_ref, wglu_ref, bglu_ref, wout_ref, g_ref, b_ref, *refs):
    n_cast = (len(refs) - 2) // 2
    o_ref, ob_ref = refs[n_cast], refs[n_cast + 1]
    _cast_slices(refs[:n_cast], refs[n_cast + 2:])
    is_prompt = pl.program_id(0) < NBP
    n_sub = TM // SUB

    def glu(r):
        z = z_ref[r * SUB:(r + 1) * SUB, :]
        gate = _dot(z.astype(BF16), wglu_ref[...]) + bglu_ref[...]
        return (z * jax.nn.sigmoid(gate)).astype(BF16)

    bmix = glu(0)
    for r in range(n_sub):
        rows = slice(r * SUB, (r + 1) * SUB)
        bmix_next = glu(r + 1) if r + 1 < n_sub else None
        mix = _dot(a_ref[rows, :], wout_ref[0:D_POOL, :]) + _dot(bmix, wout_ref[D_POOL:, :])
        x = jnp.where(is_prompt, xp_ref[rows, :], xs_ref[rows, :])
        h1 = _layer_norm(ALPHA * x + mix, g_ref[...], b_ref[...])
        o_ref[rows, :] = h1
        ob_ref[rows, :] = h1.astype(BF16)
        bmix = bmix_next


def _mix(a, z, xp, xs, w_glu, b_glu, w_out, g, b, to_cast):
    row = lambda w: pl.BlockSpec((TM, w), lambda i: (i, 0))
    cast_specs, cast_shapes = _cast_specs(to_cast, NBP)
    outs = pl.pallas_call(
        _mix_kernel,
        out_shape=(jax.ShapeDtypeStruct((R, D_MODEL), F32), jax.ShapeDtypeStruct((R, D_MODEL), BF16), *cast_shapes),
        grid=(NB,),
        in_specs=[
            row(D_POOL), row(D_SSM),
            pl.BlockSpec((TM, D_MODEL), lambda i: (jnp.minimum(i, NBP - 1), 0)),
            _const_spec((RS, D_MODEL)),
            _const_spec((D_SSM, D_SSM)), _const_spec((1, D_SSM)),
            _const_spec((D_MODEL, D_MODEL)), _const_spec((1, D_MODEL)), _const_spec((1, D_MODEL)),
            *cast_specs,
        ],
        out_specs=(row(D_MODEL), row(D_MODEL), *cast_specs),
        compiler_params=_cparams(("arbitrary",)),
        name="mix_ln1",
    )(a, z, xp, xs, w_glu, b_glu, w_out, g, b, *to_cast)
    return outs[0], outs[1], outs[2:]


def _qproj_sample_kernel(x_ref, w_ref, o_ref):
    q = _dot(x_ref[...], w_ref[...])
    for j in range(DEC_SEQ):
        o_ref[:, j, :] = q[j * DEC_BATCH:(j + 1) * DEC_BATCH, :]
    o_ref[:, DEC_SEQ:, :] = jnp.zeros((DEC_BATCH, Q_PAD - DEC_SEQ, D_MODEL), F32)


def _qproj_sample(h, w):
    return pl.pallas_call(
        _qproj_sample_kernel,
        out_shape=jax.ShapeDtypeStruct((DEC_BATCH, Q_PAD, D_MODEL), F32),
        grid=(1,),
        in_specs=[pl.BlockSpec((RS, D_MODEL), lambda i: (RP // RS, 0)),
                  pl.BlockSpec((D_MODEL, D_MODEL), lambda i: (0, 0))],
        out_specs=pl.BlockSpec((DEC_BATCH, Q_PAD, D_MODEL), lambda i: (0, 0, 0)),
        compiler_params=_cparams(("arbitrary",)),
        name="qproj_sample",
    )(h, w)


def _memproj_kernel(m_ref, w_ref, o_ref, ob_ref, mb_scr):
    j = pl.program_id(0)

    @pl.when(j == 0)
    def _():
        mb_scr[...] = m_ref[...].astype(BF16)

    k = _dot(mb_scr[...], w_ref[...].astype(BF16))
    ob_ref[...] = k.astype(BF16)
    for h in range(N_XHEADS):
        @pl.when(j == h)
        def _():
            for b in range(BATCH):
                o_ref[0, b, :, h, :] = k[b * N_MEM:(b + 1) * N_MEM, :]


def _memproj(mem, w, name):
    rows = BATCH * N_MEM
    return pl.pallas_call(
        _memproj_kernel,
        out_shape=(jax.ShapeDtypeStruct((1, BATCH, N_MEM, N_XHEADS, XHEAD_DIM), F32),
                   jax.ShapeDtypeStruct((rows, D_MODEL), BF16)),
        grid=(N_XHEADS,),
        in_specs=[_const_spec((rows, D_MODEL)), pl.BlockSpec((D_MODEL, XHEAD_DIM), lambda j: (0, j))],
        out_specs=(pl.BlockSpec((1, BATCH, N_MEM, N_XHEADS, XHEAD_DIM), lambda j: (0, 0, 0, 0, 0)),
                   pl.BlockSpec((rows, XHEAD_DIM), lambda j: (0, j))),
        scratch_shapes=[pltpu.VMEM((rows, D_MODEL), BF16)],
        compiler_params=_cparams(("arbitrary",)),
        name=name,
    )(mem, w)


def _attend(q, k, v, between=None):
    def scores(h):
        sl = slice(h * XHEAD_DIM, (h + 1) * XHEAD_DIM)
        return lax.dot_general(q[:, sl], k[:, sl], (((1,), (1,)), ((), ())),
                               preferred_element_type=F32) * (XHEAD_DIM ** -0.5)

    def head_out(h, s):
        s = s - jnp.max(s, axis=-1, keepdims=True)
        e = jnp.exp(s)
        p = e / jnp.sum(e, axis=-1, keepdims=True)
        return _dot(p.astype(BF16), v[:, h * XHEAD_DIM:(h + 1) * XHEAD_DIM])

    outs = []
    s = scores(0)
    for h in range(N_XHEADS):
        if between is not None:
            between(h)
        s_next = scores(h + 1) if h + 1 < N_XHEADS else None
        outs.append(head_out(h, s))
        s = s_next
    return jnp.concatenate(outs, axis=1)


def _attn_block_kernel(h_ref, q_ref, k_ref, v_ref, os_ref, wo_ref, g_ref, b_ref, *refs):
    n_cast = (len(refs) - 1) // 2
    o_ref = refs[n_cast]
    _cast_slices(refs[:n_cast], refs[n_cast + 1:])
    i = pl.program_id(0)
    n_sub = TM // SUB

    def normed(rows, y):
        o_ref[rows, :] = _layer_norm(ALPHA * h_ref[rows, :] + y, g_ref[...], b_ref[...])

    @pl.when(i < NBP)
    def _():
        prev_rows, prev_o = None, None
        for r in range(n_sub):
            rows = slice(r * SUB, (r + 1) * SUB)
            ys = []
            between = None
            if prev_o is not None:
                between = lambda hh, po=prev_o: ys.append(
                    _dot(po, wo_ref[:, hh * XHEAD_DIM:(hh + 1) * XHEAD_DIM]))
            o = _attend(q_ref[rows, :], k_ref[...], v_ref[...], between).astype(BF16)
            if prev_o is not None:
                normed(prev_rows, jnp.concatenate(ys, axis=1))
            prev_rows, prev_o = rows, o
        normed(prev_rows, _dot(prev_o, wo_ref[...]))

    @pl.when(i == NBP)
    def _():
        for r in range(n_sub):
            rows = slice(r * SUB, (r + 1) * SUB)
            steps = range(r * SUB // DEC_BATCH, (r + 1) * SUB // DEC_BATCH)
            o = jnp.concatenate([os_ref[:, j, :] for j in steps], axis=0).astype(BF16)
            normed(rows, _dot(o, wo_ref[...]))


def _attn_block(h, q_prompt, kb, vb, o_sample, wo, g, b, to_cast):
    kv = pl.BlockSpec((N_MEM, D_MODEL), lambda i: (jnp.minimum(i // BLOCKS_PER_SEQ, BATCH - 1), 0))
    row = pl.BlockSpec((TM, D_MODEL), lambda i: (i, 0))
    qrow = pl.BlockSpec((TM, D_MODEL), lambda i: (jnp.minimum(i, NBP - 1), 0))
    cast_specs, cast_shapes = _cast_specs(to_cast, NBP)
    outs = pl.pallas_call(
        _attn_block_kernel,
        out_shape=(jax.ShapeDtypeStruct((R, D_MODEL), F32), *cast_shapes),
        grid=(NB,),
        in_specs=[row, qrow, kv, kv, _const_spec((DEC_BATCH, Q_PAD, D_MODEL)), _const_spec((D_MODEL, D_MODEL)),
                  _const_spec((1, D_MODEL)), _const_spec((1, D_MODEL)), *cast_specs],
        out_specs=(row, *cast_specs),
        compiler_params=_cparams(("arbitrary",)),
        name="attn_block",
    )(h, q_prompt, kb, vb, o_sample, wo, g, b, *to_cast)
    return outs[0], outs[1:]


ATT_BB = 4
Q_PAD = 8
KV_ROWS = N_MEM * N_XHEADS


QP_ROWS = RP // (DEC_BATCH // ATT_BB)


def _attn_sample_kernel(q_ref, k_ref, v_ref, h_ref, wq_ref, o_ref, qp_ref):
    hb = h_ref[...]
    qcols = D_MODEL // ATT_BB
    shape = (N_XHEADS * Q_PAD, KV_ROWS)
    same_head = (lax.broadcasted_iota(jnp.int32, shape, 0) // Q_PAD
                 == lax.broadcasted_iota(jnp.int32, shape, 1) % N_XHEADS)
    def scores(b):
        q = q_ref[b]
        qs = jnp.concatenate([q[:, h * XHEAD_DIM:(h + 1) * XHEAD_DIM] for h in range(N_XHEADS)], axis=0)
        k = k_ref[0, b].reshape(KV_ROWS, XHEAD_DIM).astype(BF16)
        return lax.dot_general(qs.astype(BF16), k, (((1,), (1,)), ((), ())),
                               preferred_element_type=F32) * (XHEAD_DIM ** -0.5)

    def attend(b, s):
        s = jnp.where(same_head, s, -1e30)
        s = s - jnp.max(s, axis=-1, keepdims=True)
        e = jnp.exp(s)
        p = e / jnp.sum(e, axis=-1, keepdims=True)
        v = v_ref[0, b].reshape(KV_ROWS, XHEAD_DIM).astype(BF16)
        o = _dot(p.astype(BF16), v)
        for h in range(N_XHEADS):
            o_ref[b, :, h * XHEAD_DIM:(h + 1) * XHEAD_DIM] = o[h * Q_PAD:(h + 1) * Q_PAD]

    s_prev = scores(0)
    for b in range(ATT_BB):
        cols = slice(b * qcols, (b + 1) * qcols)
        qp_ref[:, cols] = _dot(hb, wq_ref[:, cols]).astype(BF16)
        s_next = scores(b + 1) if b + 1 < ATT_BB else None
        attend(b, s_prev)
        s_prev = s_next


def _attn_sample(q_pad, k, v, h, wq):
    kv = pl.BlockSpec((1, ATT_BB, N_MEM, N_XHEADS, XHEAD_DIM), lambda i: (0, i, 0, 0, 0))
    qo = pl.BlockSpec((ATT_BB, Q_PAD, D_MODEL), lambda i: (i, 0, 0))
    hq = pl.BlockSpec((QP_ROWS, D_MODEL), lambda i: (i, 0))
    return pl.pallas_call(
        _attn_sample_kernel,
        out_shape=(jax.ShapeDtypeStruct((DEC_BATCH, Q_PAD, D_MODEL), F32),
                   jax.ShapeDtypeStruct((RP, D_MODEL), BF16)),
        grid=(DEC_BATCH // ATT_BB,),
        in_specs=[qo, kv, kv, hq, _const_spec((D_MODEL, D_MODEL))],
        out_specs=(qo, hq),
        compiler_params=_cparams(("arbitrary",)),
        name="attn_sample",
    )(q_pad, k, v, h, wq)


FTM = 1024
FNB = RP // FTM
FBLOCKS_PER_SEQ = SEQ // FTM
FSUB = 512
FSUB_LAST = 128
FSUB_SAMPLE = 256


def _ffn_rows(n_rows, sub, conv_taps, h_ref, hb_scr, g_scr, g_base, wg, wu, wd, cw_ref, cb_ref,
              lng_ref, lnb_ref, y_ref, first, last):
    def gate_up(lo):
        if first:
            h = h_ref[lo:lo + sub, :]
            hb = h.astype(BF16)
            hb_scr[lo:lo + sub, :] = hb
        else:
            h, hb = None, hb_scr[lo:lo + sub, :]
        g = _dot(hb, wg)
        up = _dot(hb, wu)
        g_scr[g_base + lo:g_base + lo + sub, :] = g
        tap0, tap1 = conv_taps(lo, sub)
        gc = cb_ref[...] + cw_ref[0:1, :] * tap0 + cw_ref[1:2, :] * tap1 + cw_ref[2:3, :] * g
        return h, (jax.nn.silu(gc) * up).astype(BF16)

    def down(lo, h, act):
        y = _dot(act, wd) + (ALPHA * h if first else y_ref[lo:lo + sub, :])
        y_ref[lo:lo + sub, :] = _layer_norm(y, lng_ref[...], lnb_ref[...]) if last else y

    n_sub = n_rows // sub
    pending = gate_up(0)
    for r in range(n_sub):
        nxt = gate_up((r + 1) * sub) if r + 1 < n_sub else None
        down(r * sub, *pending)
        pending = nxt


def _ffn_chunk(f, run):
    @pl.when(f == 0)
    def _():
        run(True, False)

    @pl.when(jnp.logical_and(f > 0, f < NF - 1))
    def _():
        run(False, False)

    @pl.when(f == NF - 1)
    def _():
        run(False, True)


def _ffn_prompt_kernel(h_ref, wg_ref, wu_ref, wd_ref, cw_ref, cb_ref, lng_ref, lnb_ref,
                       y_ref, gtail_ref, hb_scr, g_scr, carry_scr):
    i = pl.program_id(0)
    f = pl.program_id(1)
    first_block = (i % FBLOCKS_PER_SEQ) == 0

    @pl.when(first_block)
    def _():
        g_scr[0:8, :] = jnp.zeros((8, TF), F32)

    @pl.when(jnp.logical_not(first_block))
    def _():
        g_scr[0:8, :] = carry_scr[f]

    taps = lambda lo, sub: (g_scr[6 + lo:6 + lo + sub, :], g_scr[7 + lo:7 + lo + sub, :])
    _ffn_chunk(f, lambda first, last: _ffn_rows(
        FTM, FSUB_LAST if last else FSUB, taps, h_ref, hb_scr, g_scr, 8, wg_ref[...], wu_ref[...], wd_ref[...], cw_ref, cb_ref,
        lng_ref, lnb_ref, y_ref, first, last))
    tail = g_scr[FTM:FTM + 8, :]
    carry_scr[f] = tail
    gtail_ref[0] = tail


def _ffn_sample_kernel(h_ref, wg_ref, wu_ref, wd_ref, cw_ref, cb_ref, lng_ref, lnb_ref, st_ref,
                       y_ref, gnew_ref, wdb_ref, hb_scr, g_scr):
    f = pl.program_id(1)
    n_st = 2 * DEC_BATCH
    wdb_ref[...] = wd_ref[...].astype(BF16)
    g_scr[0:n_st, :] = st_ref[...]
    taps = lambda lo, sub: (g_scr[lo:lo + sub, :], g_scr[DEC_BATCH + lo:DEC_BATCH + lo + sub, :])
    _ffn_chunk(f, lambda first, last: _ffn_rows(
        RS, FSUB_SAMPLE, taps, h_ref, hb_scr, g_scr, n_st, wg_ref[...], wu_ref[...], wdb_ref[...], cw_ref, cb_ref,
        lng_ref, lnb_ref, y_ref, first, last))
    for t in range(CONV_W - 1):
        gnew_ref[:, t, :] = g_scr[RS + t * DEC_BATCH:RS + (t + 1) * DEC_BATCH, :]


def _ffn_specs(tm, row_block_of):
    return [
        pl.BlockSpec((tm, D_MODEL), lambda i, f: (row_block_of(i), 0)),
        pl.BlockSpec((D_MODEL, TF), lambda i, f: (0, f)),
        pl.BlockSpec((D_MODEL, TF), lambda i, f: (0, f)),
        pl.BlockSpec((TF, D_MODEL), lambda i, f: (f, 0)),
        pl.BlockSpec((CONV_W, TF), lambda i, f: (0, f)),
        pl.BlockSpec((1, TF), lambda i, f: (0, f)),
        pl.BlockSpec((1, D_MODEL), lambda i, f: (0, 0)),
        pl.BlockSpec((1, D_MODEL), lambda i, f: (0, 0)),
    ]


def _ffn_prompt(h, wg, wu, wd, cw, cb, lng, lnb):
    return pl.pallas_call(
        _ffn_prompt_kernel,
        out_shape=(jax.ShapeDtypeStruct((RP, D_MODEL), F32),
                   jax.ShapeDtypeStruct((FNB, 8, D_FF), F32)),
        grid=(FNB, NF),
        in_specs=_ffn_specs(FTM, lambda i: i),
        out_specs=(pl.BlockSpec((FTM, D_MODEL), lambda i, f: (i, 0)),
                   pl.BlockSpec((1, 8, TF), lambda i, f: (i, 0, f))),
        scratch_shapes=[pltpu.VMEM((FTM, D_MODEL), BF16),
                        pltpu.VMEM((FTM + 8, TF), F32),
                        pltpu.VMEM((NF, 8, TF), F32)],
        compiler_params=_cparams(("arbitrary", "arbitrary"), VMEM_LIMIT_FFN),
        name="ffn_prompt",
    )(h, wg, wu, wd, cw, cb, lng, lnb)


def _ffn_sample(h, wg, wu, wd, cw, cb, lng, lnb, conv_state):
    n_st = 2 * DEC_BATCH
    specs = _ffn_specs(RS, lambda i: RP // RS)
    return pl.pallas_call(
        _ffn_sample_kernel,
        out_shape=(jax.ShapeDtypeStruct((RS, D_MODEL), F32),
                   jax.ShapeDtypeStruct((DEC_BATCH, CONV_W - 1, D_FF), F32),
                   jax.ShapeDtypeStruct((D_FF, D_MODEL), BF16)),
        grid=(1, NF),
        in_specs=specs + [pl.BlockSpec((n_st, TF), lambda i, f: (0, f))],
        out_specs=(pl.BlockSpec((RS, D_MODEL), lambda i, f: (0, 0)),
                   pl.BlockSpec((DEC_BATCH, CONV_W - 1, TF), lambda i, f: (0, 0, f)),
                   specs[3]),
        scratch_shapes=[pltpu.VMEM((RS, D_MODEL), BF16),
                        pltpu.VMEM((n_st + RS, TF), F32)],
        compiler_params=_cparams(("arbitrary", "arbitrary")),
        name="ffn_sample",
    )(h, wg, wu, wd, cw, cb, lng, lnb, conv_state)


def _state_to_pairs(s):
    return s.reshape(s.shape[0], N_PAIRS, 128).transpose(1, 0, 2)


def _state_from_pairs(s):
    return s.transpose(1, 0, 2).reshape(1, s.shape[1], N_SSM_GROUPS, SSM_STATE)


def kernel(x_prompt, x_sample, mem_prompt, state_pool, state_ssm_re, state_ssm_im, state_conv, cache_mem_k, cache_mem_v, w_in, w_pool, pool_scale, lambda_re, lambda_im, log_step, b_re, b_im, c_re, c_im, d_skip, w_glu, b_glu, w_out, ln1_g, ln1_b, w_q, w_k, w_v, w_o, ln2_g, ln2_b, w_gate, w_up, conv_w, conv_b, w_down, ln3_g, ln3_b):
    bf = lambda w: w[0].astype(BF16)
    row = lambda v: v[0].reshape(1, -1)
    xp = x_prompt.reshape(RP, D_MODEL)
    xs = x_sample.transpose(1, 0, 2).reshape(RS, D_MODEL)

    prep_args = _ssm_prep_args(lambda_re[0], lambda_im[0], log_step[0], b_re[0], b_im[0], c_re[0], c_im[0])
    (u_ssm, a, utail, us_pool), prep, (w_glu_b, w_out_b) = _front(
        xp, xs, bf(w_in), state_pool[0].transpose(1, 0, 2), bf(w_pool), row(pool_scale), prep_args,
        (w_glu[0], w_out[0]))
    prep["dskip"] = jnp.broadcast_to(d_skip[0].reshape(N_PAIRS, 2, 1, SSM_GROUP_CH),
                                     (N_PAIRS, 2, CHUNK, SSM_GROUP_CH)).reshape(N_PAIRS, 1, PAIR_W)
    (z, hp_re, hp_im, hs_re, hs_im), (w_q_b, w_o_b) = _ssm(
        u_ssm, prep, _state_to_pairs(state_ssm_re[0]), _state_to_pairs(state_ssm_im[0]), (w_q[0], w_o[0]))
    h1, h1b, (wg,) = _mix(a, z, xp, xs, w_glu_b, row(b_glu), w_out_b, row(ln1_g), row(ln1_b), (w_gate[0],))

    mem = mem_prompt.reshape(BATCH * N_MEM, D_MODEL)
    mk, mkb = _memproj(mem, w_k[0], "mem_k")
    mv, mvb = _memproj(mem, w_v[0], "mem_v")
    o_s, q_p = _attn_sample(_qproj_sample(h1b, w_q_b), cache_mem_k, cache_mem_v, h1b, w_q_b)
    h2, (wu,) = _attn_block(h1, q_p, mkb, mvb, o_s, w_o_b, row(ln2_g), row(ln2_b), (w_up[0],))

    cw, cb = conv_w[0], row(conv_b)
    conv_st = state_conv[0].transpose(1, 0, 2).reshape(2 * DEC_BATCH, D_FF)
    y_s, g_new, wd = _ffn_sample(h2, wg, wu, w_down[0], cw, cb, row(ln3_g), row(ln3_b), conv_st)
    y_p, gtail = _ffn_prompt(h2, wg, wu, wd, cw, cb, row(ln3_g), row(ln3_b))

    y_prompt = y_p.reshape(BATCH, SEQ, D_MODEL)
    y_sample = y_s.reshape(DEC_SEQ, DEC_BATCH, D_MODEL).transpose(1, 0, 2)
    p_pool = utail[BLOCKS_PER_SEQ - 1:NBP:BLOCKS_PER_SEQ, 16 - POOL_BUF:][None]
    s_ext = jnp.concatenate([state_pool[0], us_pool.reshape(DEC_SEQ, DEC_BATCH, D_POOL).transpose(1, 0, 2)], axis=1)
    s_pool = s_ext[None, :, DEC_SEQ:]
    p_conv = gtail[FBLOCKS_PER_SEQ - 1::FBLOCKS_PER_SEQ, 6:8][None]
    s_conv = g_new[None]
    return (y_prompt, y_sample,
            p_pool, _state_from_pairs(hp_re[:, :BATCH]), _state_from_pairs(hp_im[:, :BATCH]), p_conv,
            mk, mv,
            s_pool, _state_from_pairs(hs_re), _state_from_pairs(hs_im), s_conv)
```

```python
import jax
import jax.numpy as jnp
from jax import lax
from jax.experimental import pallas as pl
from jax.experimental.pallas import tpu as pltpu

F32 = jnp.float32
BF16 = jnp.bfloat16

D_MODEL = 2048
BATCH = 4
SEQ = 2048
DEC_BATCH = 128
DEC_SEQ = 4
PAST_LEN = 16384
D_POOL = 1024
D_SSM = 1024
POOL_WINDOWS = (2, 4, 8, 16)
POOL_GROUP_DIM = 256
POOL_BUF = 15
SSM_GROUP_CH = 16
N_SSM_GROUPS = 64
SSM_STATE = 64
N_MEM = 256
N_XHEADS = 4
XHEAD_DIM = 512
D_FF = 5632
CONV_W = 3
ALPHA = 2.0 ** 0.25
LN_EPS = 1e-5

RP = BATCH * SEQ
RS = DEC_BATCH * DEC_SEQ
R = RP + RS
TM = 512
NB = R // TM
NBP = RP // TM
BLOCKS_PER_SEQ = SEQ // TM

CHUNK = 16
N_PAIRS = N_SSM_GROUPS // 2
PAIR_W = 2 * CHUNK * SSM_GROUP_CH
CHUNKS_PER_SEQ = SEQ // CHUNK
P_CHUNK_ROWS = BATCH * CHUNKS_PER_SEQ
SCAN_PAD = 64
N_SCAN_STEPS = 7

SUB = 256
TF = 512
NF = D_FF // TF

VMEM_LIMIT = 56 * 1024 * 1024
VMEM_LIMIT_FFN = 60 * 1024 * 1024


def _cparams(sem, vmem_limit=VMEM_LIMIT):
    return pltpu.CompilerParams(dimension_semantics=sem, vmem_limit_bytes=vmem_limit)


def _const_spec(shape):
    n = len(shape)
    return pl.BlockSpec(shape, lambda *_: (0,) * n, pipeline_mode=pl.Buffered(1))


def _layer_norm(x, g, b):
    mu = jnp.mean(x, axis=-1, keepdims=True)
    xc = x - mu
    var = jnp.mean(xc * xc, axis=-1, keepdims=True)
    return xc * lax.rsqrt(var + LN_EPS) * g + b


def _dot(a, b):
    return jnp.dot(a, b, preferred_element_type=F32)


def _dot_nt(a, b):
    return lax.dot_general(a, b, (((1,), (1,)), ((), ())), preferred_element_type=F32)


def _cast_spec(a, n_steps):
    return pl.BlockSpec((a.shape[0] // n_steps, a.shape[1]), lambda i: (jnp.minimum(i, n_steps - 1), 0))


def _cast_specs(arrays, n_steps):
    specs = [_cast_spec(a, n_steps) for a in arrays]
    return specs, [jax.ShapeDtypeStruct(a.shape, BF16) for a in arrays]


def _cast_slices(in_refs, out_refs):
    for src_ref, dst_ref in zip(in_refs, out_refs):
        dst_ref[...] = src_ref[...].astype(BF16)


N_PREP_IN, N_PREP_OUT = 7, 10
PREP_PAIRS = N_PAIRS // NBP


def _pool_mix(g, pooled, w_ref, scale_ref):
    sl = slice(g * POOL_GROUP_DIM, (g + 1) * POOL_GROUP_DIM)
    return (_dot(pooled.astype(BF16), w_ref[g]) * scale_ref[:, sl]).astype(BF16)


def _front_kernel(xp_ref, xs_ref, w_ref, st_ref, wp_ref, scale_ref, *refs):
    prep_in, refs = refs[:N_PREP_IN], refs[N_PREP_IN:]
    n_cast = (len(refs) - 4 - N_PREP_OUT - 1) // 2
    cast_in, refs = refs[:n_cast], refs[n_cast:]
    ussm_ref, a_ref, utail_ref, us_ref = refs[:4]
    prep_out, cast_out, ext_ref = refs[4:4 + N_PREP_OUT], refs[4 + N_PREP_OUT:-1], refs[-1]
    i = pl.program_id(0)

    xb = jnp.where(i < NBP, xp_ref[...], xs_ref[...]).astype(BF16)

    first = (i % BLOCKS_PER_SEQ) == 0
    ext_ref[0:16, :] = jnp.where(first, 0.0, ext_ref[TM:TM + 16, :])
    ext_ref[16:16 + TM, :] = _dot(xb, w_ref[:, 0:D_POOL])
    utail_ref[0] = ext_ref[TM:TM + 16, :]
    pos = (i % BLOCKS_PER_SEQ) * TM + lax.broadcasted_iota(jnp.int32, (TM, 1), 0)
    qc = D_SSM // len(POOL_WINDOWS)
    for g, w in enumerate(POOL_WINDOWS):
        ussm_ref[:, g * qc:(g + 1) * qc] = _dot(xb, w_ref[:, D_POOL + g * qc:D_POOL + (g + 1) * qc])
        sl = slice(g * POOL_GROUP_DIM, (g + 1) * POOL_GROUP_DIM)
        acc = ext_ref[16:16 + TM, sl]
        for k in range(1, w):
            acc = acc + ext_ref[16 - k:16 - k + TM, sl]
        cnt = jnp.minimum(pos + 1, w).astype(F32)
        a_ref[:, sl] = _pool_mix(g, acc / cnt - ext_ref[16:16 + TM, sl], wp_ref, scale_ref)

    for pp in range(PREP_PAIRS):
        _ssm_prep_pair(pp, *prep_in, *prep_out)
    _cast_slices(cast_in, cast_out)

    @pl.when(i == NBP)
    def _():
        us_ref[...] = ext_ref[16:16 + TM, :]
        tok = lambda t, sl: ext_ref[16 + t * DEC_BATCH:16 + (t + 1) * DEC_BATCH, sl]
        for j in range(DEC_SEQ):
            rows = slice(j * DEC_BATCH, (j + 1) * DEC_BATCH)
            for g, w in enumerate(POOL_WINDOWS):
                sl = slice(g * POOL_GROUP_DIM, (g + 1) * POOL_GROUP_DIM)
                acc = tok(j, sl)
                for k in range(1, w):
                    e = POOL_BUF + j - k
                    acc = acc + (tok(e - POOL_BUF, sl) if e >= POOL_BUF else st_ref[e, :, sl])
                cnt = float(min(PAST_LEN + j + 1, w))
                a_ref[rows, sl] = _pool_mix(g, acc / cnt - tok(j, sl), wp_ref, scale_ref)


def _front(xp, xs, w, state_t, w_pool, pool_scale, prep_args, to_cast):
    G, N, C, L = N_SSM_GROUPS, SSM_STATE, SSM_GROUP_CH, CHUNK
    wd = L * C
    step = lambda i: jnp.minimum(i, NBP - 1)
    g3 = lambda a, b: pl.BlockSpec((2 * PREP_PAIRS, a, b), lambda i: (step(i), 0, 0))
    p3 = lambda a, b: pl.BlockSpec((PREP_PAIRS, a, b), lambda i: (step(i), 0, 0))
    sds = jax.ShapeDtypeStruct
    cast_specs, cast_shapes = _cast_specs(to_cast, NBP)
    row = lambda wdt: pl.BlockSpec((TM, wdt), lambda i: (i, 0))
    outs = pl.pallas_call(
        _front_kernel,
        out_shape=(
            sds((R, D_SSM), F32), sds((R, D_POOL), BF16), sds((NB, 16, D_POOL), F32), sds((RS, D_POOL), F32),
            sds((N_PAIRS, 2, wd, wd), BF16),
            sds((N_PAIRS, PAIR_W, 128), BF16), sds((N_PAIRS, PAIR_W, 128), BF16),
            sds((N_PAIRS, PAIR_W, 128), BF16), sds((N_PAIRS, PAIR_W, 128), BF16),
            sds((N_PAIRS, PAIR_W, 128), BF16), sds((N_PAIRS, PAIR_W, 128), BF16),
            sds((N_PAIRS, 8, 128), F32), sds((N_PAIRS, 8, 128), F32), sds((N_PAIRS, 8, 128), F32),
            *cast_shapes,
        ),
        grid=(NB,),
        in_specs=[
            pl.BlockSpec((TM, D_MODEL), lambda i: (step(i), 0)),
            _const_spec((RS, D_MODEL)),
            _const_spec((D_MODEL, D_MODEL)),
            _const_spec((POOL_BUF, DEC_BATCH, D_POOL)),
            _const_spec((4, POOL_GROUP_DIM, POOL_GROUP_DIM)),
            _const_spec((1, D_POOL)),
            g3(1, N), g3(1, N), g3(1, 1), g3(C, N), g3(C, N), g3(C, N), g3(C, N),
            *cast_specs,
        ],
        out_specs=(
            row(D_SSM), row(D_POOL), pl.BlockSpec((1, 16, D_POOL), lambda i: (i, 0, 0)),
            pl.BlockSpec((RS, D_POOL), lambda i: (0, 0)),
            pl.BlockSpec((PREP_PAIRS, 2, wd, wd), lambda i: (step(i), 0, 0, 0)),
            p3(PAIR_W, 128), p3(PAIR_W, 128), p3(PAIR_W, 128), p3(PAIR_W, 128),
            p3(PAIR_W, 128), p3(PAIR_W, 128),
            p3(8, 128), p3(8, 128), p3(8, 128),
            *cast_specs,
        ),
        scratch_shapes=[pltpu.VMEM((TM + 16, D_POOL), F32)],
        compiler_params=_cparams(("arbitrary",)),
        name="front",
    )(xp, xs, w, state_t, w_pool, pool_scale, *prep_args, *to_cast)
    names = ("t", "pre", "pim", "p4re", "p4im", "qre", "qim", "apow_re", "apow_im", "lam4")
    return outs[:4], dict(zip(names, outs[4:4 + N_PREP_OUT])), outs[4 + N_PREP_OUT:]


SLAB_GROUPS = 128 // SSM_GROUP_CH
SLAB_PAIRS = SLAB_GROUPS // 2
N_SLABS = N_SSM_GROUPS // SLAB_GROUPS


def _ssm_pair(q, u, t_ref, pre_ref, pim_ref, p4re_ref, p4im_ref, qre_ref, qim_ref,
              apow_re_ref, apow_im_ref, lam4_ref, dskip_ref, h0re_ref, h0im_ref,
              hp_re_ref, hp_im_ref, hs_re_ref, hs_im_ref, hre_scr, him_scr):
    ub = u.astype(BF16)
    half = PAIR_W // 2
    y = jnp.concatenate([_dot(ub[:, :half], t_ref[q, 0]), _dot(ub[:, half:], t_ref[q, 1])], axis=1)

    ubp = ub[:P_CHUNK_ROWS]
    hre_scr[SCAN_PAD:SCAN_PAD + P_CHUNK_ROWS, :] = _dot(ubp, pre_ref[q])
    him_scr[SCAN_PAD:SCAN_PAD + P_CHUNK_ROWS, :] = _dot(ubp, pim_ref[q])
    kk = lax.broadcasted_iota(jnp.int32, (P_CHUNK_ROWS, 1), 0) % CHUNKS_PER_SEQ
    for s in range(N_SCAN_STEPS):
        d = 1 << s
        ar = apow_re_ref[q, s:s + 1, :]
        ai = apow_im_ref[q, s:s + 1, :]
        hr = hre_scr[SCAN_PAD:SCAN_PAD + P_CHUNK_ROWS, :]
        hi = him_scr[SCAN_PAD:SCAN_PAD + P_CHUNK_ROWS, :]
        pr = hre_scr[SCAN_PAD - d:SCAN_PAD - d + P_CHUNK_ROWS, :]
        pi = him_scr[SCAN_PAD - d:SCAN_PAD - d + P_CHUNK_ROWS, :]
        keep = kk >= d
        hre_scr[SCAN_PAD:SCAN_PAD + P_CHUNK_ROWS, :] = hr + jnp.where(keep, ar * pr - ai * pi, 0.0)
        him_scr[SCAN_PAD:SCAN_PAD + P_CHUNK_ROWS, :] = hi + jnp.where(keep, ar * pi + ai * pr, 0.0)
    hp_re_ref[q] = jnp.zeros((8, 128), F32)
    hp_im_ref[q] = jnp.zeros((8, 128), F32)
    for b in range(BATCH):
        last = SCAN_PAD + (b + 1) * CHUNKS_PER_SEQ - 1
        hp_re_ref[q, b:b + 1, :] = hre_scr[last:last + 1, :]
        hp_im_ref[q, b:b + 1, :] = him_scr[last:last + 1, :]
    prev_ok = kk >= 1
    hprev_re = jnp.where(prev_ok, hre_scr[SCAN_PAD - 1:SCAN_PAD - 1 + P_CHUNK_ROWS, :], 0.0)
    hprev_im = jnp.where(prev_ok, him_scr[SCAN_PAD - 1:SCAN_PAD - 1 + P_CHUNK_ROWS, :], 0.0)
    carry_p = _dot_nt(hprev_re.astype(BF16), qre_ref[q]) + _dot_nt(hprev_im.astype(BF16), qim_ref[q])

    ubs = ub[P_CHUNK_ROWS:]
    h0r = h0re_ref[q]
    h0i = h0im_ref[q]
    l4r = lam4_ref[q, 0:1, :]
    l4i = lam4_ref[q, 1:2, :]
    hs_re_ref[q] = l4r * h0r - l4i * h0i + _dot(ubs, p4re_ref[q])
    hs_im_ref[q] = l4r * h0i + l4i * h0r + _dot(ubs, p4im_ref[q])
    carry_s = _dot_nt(h0r.astype(BF16), qre_ref[q]) + _dot_nt(h0i.astype(BF16), qim_ref[q])

    y = y + jnp.concatenate([carry_p, carry_s], axis=0) + dskip_ref[q] * u
    return jax.nn.gelu(y)


N_SSM_IN, N_SSM_OUT = 13, 5


def _ssm_kernel(u_ref, *all_refs):
    n_cast = (len(all_refs) - N_SSM_IN - N_SSM_OUT - 2) // 2
    _cast_slices(all_refs[N_SSM_IN:N_SSM_IN + n_cast], all_refs[N_SSM_IN + n_cast + N_SSM_OUT:-2])
    refs = all_refs[:N_SSM_IN] + all_refs[N_SSM_IN + n_cast:N_SSM_IN + n_cast + N_SSM_OUT] + all_refs[-2:]
    z_ref = refs[13]
    hre_scr, him_scr = refs[18], refs[19]
    c = SSM_GROUP_CH
    hre_scr[0:SCAN_PAD, :] = jnp.zeros((SCAN_PAD, 128), F32)
    him_scr[0:SCAN_PAD, :] = jnp.zeros((SCAN_PAD, 128), F32)
    xt = []
    for i in range(CHUNK):
        xp = u_ref[pl.ds(i, P_CHUNK_ROWS, stride=CHUNK), :]
        if i < DEC_SEQ:
            xs = u_ref[RP + i * DEC_BATCH:RP + (i + 1) * DEC_BATCH, :]
        else:
            xs = jnp.zeros((DEC_BATCH, 128), F32)
        xt.append(jnp.concatenate([xp, xs], axis=0).T)
    def pair_input(q):
        halves = []
        for e in range(2):
            g = 2 * q + e
            bt = jnp.concatenate([xt[i][g * c:(g + 1) * c, :] for i in range(CHUNK)], axis=0)
            halves.append(bt.T)
        return jnp.concatenate(halves, axis=1)

    zt = []
    u_pair = pair_input(0)
    for q in range(SLAB_PAIRS):
        u_next = pair_input(q + 1) if q + 1 < SLAB_PAIRS else None
        z = _ssm_pair(q, u_pair, *refs[:13], *refs[14:])
        zt.append(z[:, :PAIR_W // 2].T)
        zt.append(z[:, PAIR_W // 2:].T)
        u_pair = u_next
    for i in range(CHUNK):
        zi = jnp.concatenate([zt[g][i * c:(i + 1) * c, :] for g in range(SLAB_GROUPS)], axis=0).T
        z_ref[pl.ds(i, P_CHUNK_ROWS, stride=CHUNK), :] = zi[:P_CHUNK_ROWS]
        if i < DEC_SEQ:
            z_ref[RP + i * DEC_BATCH:RP + (i + 1) * DEC_BATCH, :] = zi[P_CHUNK_ROWS:]


def _ssm(u_ssm, prep, h0re, h0im, to_cast):
    sp = SLAB_PAIRS
    slab3 = lambda a, b: pl.BlockSpec((sp, a, b), lambda s: (s, 0, 0))
    cast_specs, cast_shapes = _cast_specs(to_cast, N_SLABS)
    outs = pl.pallas_call(
        _ssm_kernel,
        out_shape=(
            jax.ShapeDtypeStruct((R, D_SSM), F32),
            jax.ShapeDtypeStruct((N_PAIRS, 8, 128), F32),
            jax.ShapeDtypeStruct((N_PAIRS, 8, 128), F32),
            jax.ShapeDtypeStruct((N_PAIRS, DEC_BATCH, 128), F32),
            jax.ShapeDtypeStruct((N_PAIRS, DEC_BATCH, 128), F32),
            *cast_shapes,
        ),
        grid=(N_SLABS,),
        in_specs=[
            pl.BlockSpec((R, 128), lambda s: (0, s)),
            pl.BlockSpec((sp, 2, 256, 256), lambda s: (s, 0, 0, 0)),
            slab3(PAIR_W, 128), slab3(PAIR_W, 128), slab3(PAIR_W, 128), slab3(PAIR_W, 128),
            slab3(PAIR_W, 128), slab3(PAIR_W, 128),
            slab3(8, 128), slab3(8, 128), slab3(8, 128),
            slab3(1, PAIR_W),
            slab3(DEC_BATCH, 128), slab3(DEC_BATCH, 128),
            *cast_specs,
        ],
        out_specs=(
            pl.BlockSpec((R, 128), lambda s: (0, s)),
            slab3(8, 128), slab3(8, 128), slab3(DEC_BATCH, 128), slab3(DEC_BATCH, 128),
            *cast_specs,
        ),
        scratch_shapes=[pltpu.VMEM((SCAN_PAD + P_CHUNK_ROWS, 128), F32),
                        pltpu.VMEM((SCAN_PAD + P_CHUNK_ROWS, 128), F32)],
        compiler_params=_cparams(("arbitrary",)),
        name="ssm",
    )(u_ssm, prep["t"], prep["pre"], prep["pim"], prep["p4re"], prep["p4im"],
      prep["qre"], prep["qim"], prep["apow_re"], prep["apow_im"], prep["lam4"],
      prep["dskip"], h0re, h0im, *to_cast)
    return outs[:N_SSM_OUT], outs[N_SSM_OUT:]


def _cmul(ar, ai, br, bi):
    return ar * br - ai * bi, ar * bi + ai * br


def _dot3_nt(a, b):
    ah = a.astype(BF16)
    bh = b.astype(BF16)
    al = (a - ah.astype(F32)).astype(BF16)
    bl = (b - bh.astype(F32)).astype(BF16)
    return _dot_nt(ah, bh) + _dot_nt(ah, bl) + _dot_nt(al, bh)


def _ssm_prep_pair(pp, lam_re_ref, lam_im_ref, lstep_ref, bt_re_ref, bt_im_ref, c_re_ref, c_im_ref,
                   t_ref, pre_ref, pim_ref, p4re_ref, p4im_ref, qre_ref, qim_ref,
                   apr_ref, api_ref, lam4_ref):
    L, C, N = CHUNK, SSM_GROUP_CH, SSM_STATE
    w = L * C
    p_re, p_im, p4_re, p4_im, q_re, q_im, a_re, a_im, l4 = [], [], [], [], [], [], [], [], []
    for e in range(2):
        ge = 2 * pp + e
        dt = jnp.exp(lstep_ref[ge])
        lr, li = lam_re_ref[ge], lam_im_ref[ge]
        mag = jnp.exp(lr * dt)
        ang = li * dt
        zr, zi = mag * jnp.cos(ang), mag * jnp.sin(ang)
        den = lr * lr + li * li
        fr = ((zr - 1.0) * lr + zi * li) / den
        fi = (zi * lr - (zr - 1.0) * li) / den
        bbr, bbi = _cmul(fr, fi, bt_re_ref[ge], bt_im_ref[ge])
        pr, pi = [jnp.ones((1, N), F32)], [jnp.zeros((1, N), F32)]
        for _ in range(L):
            nr, ni = _cmul(pr[-1], pi[-1], zr, zi)
            pr.append(nr)
            pi.append(ni)
        stack = lambda blocks, part: jnp.concatenate([b[part] for b in blocks], axis=0)
        blocks = [_cmul(bbr, bbi, pr[L - 1 - i], pi[L - 1 - i]) for i in range(L)]
        p_re.append(stack(blocks, 0))
        p_im.append(stack(blocks, 1))
        blocks4 = [_cmul(bbr, bbi, pr[DEC_SEQ - 1 - i], pi[DEC_SEQ - 1 - i]) for i in range(DEC_SEQ)]
        pad = jnp.zeros(((L - DEC_SEQ) * C, N), F32)
        p4_re.append(jnp.concatenate([stack(blocks4, 0), pad], axis=0))
        p4_im.append(jnp.concatenate([stack(blocks4, 1), pad], axis=0))
        sr, si = [pr[L]], [pi[L]]
        for _ in range(N_SCAN_STEPS - 1):
            nr, ni = _cmul(sr[-1], si[-1], sr[-1], si[-1])
            sr.append(nr)
            si.append(ni)
        a_re.append(jnp.concatenate(sr + [jnp.zeros((8 - N_SCAN_STEPS, N), F32)], axis=0))
        a_im.append(jnp.concatenate(si + [jnp.zeros((8 - N_SCAN_STEPS, N), F32)], axis=0))
        l4.append(jnp.concatenate([pr[DEC_SEQ], pi[DEC_SEQ], jnp.zeros((6, N), F32)], axis=0))

        cr, ci = c_re_ref[ge], c_im_ref[ge]
        ck = [_cmul(cr, ci, pr[k], pi[k]) for k in range(L + 1)]
        q_re.append(stack(ck[1:], 0))
        q_im.append(-stack(ck[1:], 1))
        v = _dot3_nt(bbr, stack(ck[:L], 0)) - _dot3_nt(bbi, stack(ck[:L], 1))
        lane = lax.broadcasted_iota(jnp.int32, (C, w), 1)
        rows = [v] + [jnp.where(lane >= C * i, pltpu.roll(v, C * i, axis=1), 0.0) for i in range(1, L)]
        t_ref[pp, e] = jnp.concatenate(rows, axis=0).astype(BF16)

    def diag_rows(m):
        z = jnp.zeros_like(m[0])
        return jnp.concatenate([jnp.concatenate([m[0], z], axis=1),
                                jnp.concatenate([z, m[1]], axis=1)], axis=0)

    pre_ref[pp] = diag_rows(p_re).astype(BF16)
    pim_ref[pp] = diag_rows(p_im).astype(BF16)
    p4re_ref[pp] = diag_rows(p4_re).astype(BF16)
    p4im_ref[pp] = diag_rows(p4_im).astype(BF16)
    qre_ref[pp] = diag_rows(q_re).astype(BF16)
    qim_ref[pp] = diag_rows(q_im).astype(BF16)
    apr_ref[pp] = jnp.concatenate(a_re, axis=1)
    api_ref[pp] = jnp.concatenate(a_im, axis=1)
    lam4_ref[pp] = jnp.concatenate(l4, axis=1)


def _ssm_prep_args(lambda_re, lambda_im, log_step, b_re, b_im, c_re, c_im):
    return (lambda_re[:, None, :], lambda_im[:, None, :], log_step[:, None, None],
            b_re.transpose(0, 2, 1), b_im.transpose(0, 2, 1), c_re, c_im)


def _mix_kernel(a_ref, z_ref, xp_ref, xs_ref, wglu_ref, bglu_ref, wout_ref, g_ref, b_ref, *refs):
    n_cast = (len(refs) - 2) // 2
    o_ref, ob_ref = refs[n_cast], refs[n_cast + 1]
    _cast_slices(refs[:n_cast], refs[n_cast + 2:])
    is_prompt = pl.program_id(0) < NBP
    n_sub = TM // SUB

    def glu(r):
        z = z_ref[r * SUB:(r + 1) * SUB, :]
        gate = _dot(z.astype(BF16), wglu_ref[...]) + bglu_ref[...]
        return (z * jax.nn.sigmoid(gate)).astype(BF16)

    bmix = glu(0)
    for r in range(n_sub):
        rows = slice(r * SUB, (r + 1) * SUB)
        bmix_next = glu(r + 1) if r + 1 < n_sub else None
        mix = _dot(a_ref[rows, :], wout_ref[0:D_POOL, :]) + _dot(bmix, wout_ref[D_POOL:, :])
        x = jnp.where(is_prompt, xp_ref[rows, :], xs_ref[rows, :])
        h1 = _layer_norm(ALPHA * x + mix, g_ref[...], b_ref[...])
        o_ref[rows, :] = h1
        ob_ref[rows, :] = h1.astype(BF16)
        bmix = bmix_next


def _mix(a, z, xp, xs, w_glu, b_glu, w_out, g, b, to_cast):
    row = lambda w: pl.BlockSpec((TM, w), lambda i: (i, 0))
    cast_specs, cast_shapes = _cast_specs(to_cast, NBP)
    outs = pl.pallas_call(
        _mix_kernel,
        out_shape=(jax.ShapeDtypeStruct((R, D_MODEL), F32), jax.ShapeDtypeStruct((R, D_MODEL), BF16), *cast_shapes),
        grid=(NB,),
        in_specs=[
            row(D_POOL), row(D_SSM),
            pl.BlockSpec((TM, D_MODEL), lambda i: (jnp.minimum(i, NBP - 1), 0)),
            _const_spec((RS, D_MODEL)),
            _const_spec((D_SSM, D_SSM)), _const_spec((1, D_SSM)),
            _const_spec((D_MODEL, D_MODEL)), _const_spec((1, D_MODEL)), _const_spec((1, D_MODEL)),
            *cast_specs,
        ],
        out_specs=(row(D_MODEL), row(D_MODEL), *cast_specs),
        compiler_params=_cparams(("arbitrary",)),
        name="mix_ln1",
    )(a, z, xp, xs, w_glu, b_glu, w_out, g, b, *to_cast)
    return outs[0], outs[1], outs[2:]


def _qproj_sample_kernel(x_ref, w_ref, o_ref):
    q = _dot(x_ref[...], w_ref[...])
    for j in range(DEC_SEQ):
        o_ref[:, j, :] = q[j * DEC_BATCH:(j + 1) * DEC_BATCH, :]
    o_ref[:, DEC_SEQ:, :] = jnp.zeros((DEC_BATCH, Q_PAD - DEC_SEQ, D_MODEL), F32)


def _qproj_sample(h, w):
    return pl.pallas_call(
        _qproj_sample_kernel,
        out_shape=jax.ShapeDtypeStruct((DEC_BATCH, Q_PAD, D_MODEL), F32),
        grid=(1,),
        in_specs=[pl.BlockSpec((RS, D_MODEL), lambda i: (RP // RS, 0)),
                  pl.BlockSpec((D_MODEL, D_MODEL), lambda i: (0, 0))],
        out_specs=pl.BlockSpec((DEC_BATCH, Q_PAD, D_MODEL), lambda i: (0, 0, 0)),
        compiler_params=_cparams(("arbitrary",)),
        name="qproj_sample",
    )(h, w)


def _memproj_kernel(m_ref, w_ref, o_ref, ob_ref, wb_scr):
    @pl.when(pl.program_id(0) == 0)
    def _():
        wb_scr[...] = w_ref[...].astype(BF16)

    k = _dot(m_ref[...].astype(BF16), wb_scr[...])
    ob_ref[...] = k.astype(BF16)
    for h in range(N_XHEADS):
        o_ref[0, 0, :, h, :] = k[:, h * XHEAD_DIM:(h + 1) * XHEAD_DIM]


def _memproj(mem, w, name):
    rows = BATCH * N_MEM
    seq = pl.BlockSpec((N_MEM, D_MODEL), lambda b: (b, 0))
    return pl.pallas_call(
        _memproj_kernel,
        out_shape=(jax.ShapeDtypeStruct((1, BATCH, N_MEM, N_XHEADS, XHEAD_DIM), F32),
                   jax.ShapeDtypeStruct((rows, D_MODEL), BF16)),
        grid=(BATCH,),
        in_specs=[seq, _const_spec((D_MODEL, D_MODEL))],
        out_specs=(pl.BlockSpec((1, 1, N_MEM, N_XHEADS, XHEAD_DIM), lambda b: (0, b, 0, 0, 0)), seq),
        scratch_shapes=[pltpu.VMEM((D_MODEL, D_MODEL), BF16)],
        compiler_params=_cparams(("arbitrary",)),
        name=name,
    )(mem, w)


def _attend(q, k, v, between=None):
    def scores(h):
        sl = slice(h * XHEAD_DIM, (h + 1) * XHEAD_DIM)
        return lax.dot_general(q[:, sl], k[:, sl], (((1,), (1,)), ((), ())),
                               preferred_element_type=F32) * (XHEAD_DIM ** -0.5)

    def head_out(h, s):
        s = s - jnp.max(s, axis=-1, keepdims=True)
        e = jnp.exp(s)
        p = e / jnp.sum(e, axis=-1, keepdims=True)
        return _dot(p.astype(BF16), v[:, h * XHEAD_DIM:(h + 1) * XHEAD_DIM])

    outs = []
    s = scores(0)
    for h in range(N_XHEADS):
        if between is not None:
            between(h)
        s_next = scores(h + 1) if h + 1 < N_XHEADS else None
        outs.append(head_out(h, s))
        s = s_next
    return jnp.concatenate(outs, axis=1)


def _attn_block_kernel(h_ref, q_ref, k_ref, v_ref, os_ref, wo_ref, *refs):
    n_cast = (len(refs) - 1) // 2
    o_ref = refs[n_cast]
    _cast_slices(refs[:n_cast], refs[n_cast + 1:])
    i = pl.program_id(0)
    n_sub = TM // SUB

    def normed(rows, y):
        o_ref[rows, :] = ALPHA * h_ref[rows, :] + y

    @pl.when(i < NBP)
    def _():
        prev_rows, prev_o = None, None
        for r in range(n_sub):
            rows = slice(r * SUB, (r + 1) * SUB)
            ys = []
            between = None
            if prev_o is not None:
                between = lambda hh, po=prev_o: ys.append(
                    _dot(po, wo_ref[:, hh * XHEAD_DIM:(hh + 1) * XHEAD_DIM]))
            o = _attend(q_ref[rows, :], k_ref[...], v_ref[...], between).astype(BF16)
            if prev_o is not None:
                normed(prev_rows, jnp.concatenate(ys, axis=1))
            prev_rows, prev_o = rows, o
        normed(prev_rows, _dot(prev_o, wo_ref[...]))

    @pl.when(i == NBP)
    def _():
        for r in range(n_sub):
            rows = slice(r * SUB, (r + 1) * SUB)
            steps = range(r * SUB // DEC_BATCH, (r + 1) * SUB // DEC_BATCH)
            o = jnp.concatenate([os_ref[:, j, :] for j in steps], axis=0).astype(BF16)
            normed(rows, _dot(o, wo_ref[...]))


def _attn_block(h, q_prompt, kb, vb, o_sample, wo, to_cast):
    kv = pl.BlockSpec((N_MEM, D_MODEL), lambda i: (jnp.minimum(i // BLOCKS_PER_SEQ, BATCH - 1), 0))
    row = pl.BlockSpec((TM, D_MODEL), lambda i: (i, 0))
    qrow = pl.BlockSpec((TM, D_MODEL), lambda i: (jnp.minimum(i, NBP - 1), 0))
    cast_specs, cast_shapes = _cast_specs(to_cast, NBP)
    outs = pl.pallas_call(
        _attn_block_kernel,
        out_shape=(jax.ShapeDtypeStruct((R, D_MODEL), F32), *cast_shapes),
        grid=(NB,),
        in_specs=[row, qrow, kv, kv, _const_spec((DEC_BATCH, Q_PAD, D_MODEL)), _const_spec((D_MODEL, D_MODEL)),
                  *cast_specs],
        out_specs=(row, *cast_specs),
        compiler_params=_cparams(("arbitrary",)),
        name="attn_block",
    )(h, q_prompt, kb, vb, o_sample, wo, *to_cast)
    return outs[0], outs[1:]


ATT_BB = 4
Q_PAD = 8
KV_ROWS = N_MEM * N_XHEADS


QP_ROWS = RP // (DEC_BATCH // ATT_BB)


def _attn_sample_kernel(q_ref, k_ref, v_ref, h_ref, wq_ref, o_ref, qp_ref):
    hb = h_ref[...]
    qcols = D_MODEL // ATT_BB
    shape = (N_XHEADS * Q_PAD, KV_ROWS)
    same_head = (lax.broadcasted_iota(jnp.int32, shape, 0) // Q_PAD
                 == lax.broadcasted_iota(jnp.int32, shape, 1) % N_XHEADS)
    def scores(b):
        q = q_ref[b]
        qs = jnp.concatenate([q[:, h * XHEAD_DIM:(h + 1) * XHEAD_DIM] for h in range(N_XHEADS)], axis=0)
        k = k_ref[0, b].reshape(KV_ROWS, XHEAD_DIM).astype(BF16)
        return lax.dot_general(qs.astype(BF16), k, (((1,), (1,)), ((), ())),
                               preferred_element_type=F32) * (XHEAD_DIM ** -0.5)

    def attend(b, s):
        s = jnp.where(same_head, s, -1e30)
        s = s - jnp.max(s, axis=-1, keepdims=True)
        e = jnp.exp(s)
        p = e / jnp.sum(e, axis=-1, keepdims=True)
        v = v_ref[0, b].reshape(KV_ROWS, XHEAD_DIM).astype(BF16)
        o = _dot(p.astype(BF16), v)
        for h in range(N_XHEADS):
            o_ref[b, :, h * XHEAD_DIM:(h + 1) * XHEAD_DIM] = o[h * Q_PAD:(h + 1) * Q_PAD]

    s_prev = scores(0)
    for b in range(ATT_BB):
        cols = slice(b * qcols, (b + 1) * qcols)
        qp_ref[:, cols] = _dot(hb, wq_ref[:, cols]).astype(BF16)
        s_next = scores(b + 1) if b + 1 < ATT_BB else None
        attend(b, s_prev)
        s_prev = s_next


def _attn_sample(q_pad, k, v, h, wq):
    kv = pl.BlockSpec((1, ATT_BB, N_MEM, N_XHEADS, XHEAD_DIM), lambda i: (0, i, 0, 0, 0))
    qo = pl.BlockSpec((ATT_BB, Q_PAD, D_MODEL), lambda i: (i, 0, 0))
    hq = pl.BlockSpec((QP_ROWS, D_MODEL), lambda i: (i, 0))
    return pl.pallas_call(
        _attn_sample_kernel,
        out_shape=(jax.ShapeDtypeStruct((DEC_BATCH, Q_PAD, D_MODEL), F32),
                   jax.ShapeDtypeStruct((RP, D_MODEL), BF16)),
        grid=(DEC_BATCH // ATT_BB,),
        in_specs=[qo, kv, kv, hq, _const_spec((D_MODEL, D_MODEL))],
        out_specs=(qo, hq),
        compiler_params=_cparams(("arbitrary",)),
        name="attn_sample",
    )(q_pad, k, v, h, wq)


FTM = 1024
FNB = RP // FTM
FBLOCKS_PER_SEQ = SEQ // FTM
FSUB = 512
FSUB_LAST = 128
FSUB_SAMPLE = 256


def _ffn_rows(n_rows, sub, conv_taps, r_ref, hb_scr, g_scr, g_base, wg, wu, wd, cw_ref, cb_ref,
              ln_ref, y_ref, first, last):
    def gate_up(lo):
        if first:
            h = _layer_norm(r_ref[lo:lo + sub, :], ln_ref[0:1, :], ln_ref[1:2, :])
            y_ref[lo:lo + sub, :] = h
            hb = h.astype(BF16)
            hb_scr[lo:lo + sub, :] = hb
        else:
            hb = hb_scr[lo:lo + sub, :]
        g = _dot(hb, wg)
        up = _dot(hb, wu)
        g_scr[g_base + lo:g_base + lo + sub, :] = g
        tap0, tap1 = conv_taps(lo, sub)
        gc = cb_ref[...] + cw_ref[0:1, :] * tap0 + cw_ref[1:2, :] * tap1 + cw_ref[2:3, :] * g
        return (jax.nn.silu(gc) * up).astype(BF16)

    def down(lo, act):
        acc = y_ref[lo:lo + sub, :]
        y = _dot(act, wd) + (ALPHA * acc if first else acc)
        y_ref[lo:lo + sub, :] = _layer_norm(y, ln_ref[2:3, :], ln_ref[3:4, :]) if last else y

    n_sub = n_rows // sub
    pending = gate_up(0)
    for r in range(n_sub):
        nxt = gate_up((r + 1) * sub) if r + 1 < n_sub else None
        down(r * sub, pending)
        pending = nxt


def _ffn_chunk(f, run):
    @pl.when(f == 0)
    def _():
        run(True, False)

    @pl.when(jnp.logical_and(f > 0, f < NF - 1))
    def _():
        run(False, False)

    @pl.when(f == NF - 1)
    def _():
        run(False, True)


def _ffn_prompt_kernel(r_ref, wg_ref, wu_ref, wd_ref, cw_ref, cb_ref, ln_ref,
                       y_ref, gtail_ref, hb_scr, g_scr, carry_scr):
    i = pl.program_id(0)
    f = pl.program_id(1)
    first_block = (i % FBLOCKS_PER_SEQ) == 0

    @pl.when(first_block)
    def _():
        g_scr[0:8, :] = jnp.zeros((8, TF), F32)

    @pl.when(jnp.logical_not(first_block))
    def _():
        g_scr[0:8, :] = carry_scr[f]

    taps = lambda lo, sub: (g_scr[6 + lo:6 + lo + sub, :], g_scr[7 + lo:7 + lo + sub, :])
    _ffn_chunk(f, lambda first, last: _ffn_rows(
        FTM, FSUB_LAST if last else FSUB, taps, r_ref, hb_scr, g_scr, 8, wg_ref[...], wu_ref[...], wd_ref[...], cw_ref, cb_ref,
        ln_ref, y_ref, first, last))
    tail = g_scr[FTM:FTM + 8, :]
    carry_scr[f] = tail
    gtail_ref[0] = tail


def _ffn_sample_kernel(r_ref, wg_ref, wu_ref, wd_ref, cw_ref, cb_ref, ln_ref, st_ref,
                       y_ref, gnew_ref, wdb_ref, hb_scr, g_scr):
    f = pl.program_id(1)
    n_st = 2 * DEC_BATCH
    wdb_ref[...] = wd_ref[...].astype(BF16)
    g_scr[0:n_st, :] = st_ref[...]
    taps = lambda lo, sub: (g_scr[lo:lo + sub, :], g_scr[DEC_BATCH + lo:DEC_BATCH + lo + sub, :])
    _ffn_chunk(f, lambda first, last: _ffn_rows(
        RS, FSUB_SAMPLE, taps, r_ref, hb_scr, g_scr, n_st, wg_ref[...], wu_ref[...], wdb_ref[...], cw_ref, cb_ref,
        ln_ref, y_ref, first, last))
    for t in range(CONV_W - 1):
        gnew_ref[:, t, :] = g_scr[RS + t * DEC_BATCH:RS + (t + 1) * DEC_BATCH, :]


def _ffn_specs(tm, row_block_of):
    return [
        pl.BlockSpec((tm, D_MODEL), lambda i, f: (row_block_of(i), 0)),
        pl.BlockSpec((D_MODEL, TF), lambda i, f: (0, f)),
        pl.BlockSpec((D_MODEL, TF), lambda i, f: (0, f)),
        pl.BlockSpec((TF, D_MODEL), lambda i, f: (f, 0)),
        pl.BlockSpec((CONV_W, TF), lambda i, f: (0, f)),
        pl.BlockSpec((1, TF), lambda i, f: (0, f)),
        pl.BlockSpec((4, D_MODEL), lambda i, f: (0, 0)),
    ]


def _ffn_prompt(r2, wg, wu, wd, cw, cb, ln):
    return pl.pallas_call(
        _ffn_prompt_kernel,
        out_shape=(jax.ShapeDtypeStruct((RP, D_MODEL), F32),
                   jax.ShapeDtypeStruct((FNB, 8, D_FF), F32)),
        grid=(FNB, NF),
        in_specs=_ffn_specs(FTM, lambda i: i),
        out_specs=(pl.BlockSpec((FTM, D_MODEL), lambda i, f: (i, 0)),
                   pl.BlockSpec((1, 8, TF), lambda i, f: (i, 0, f))),
        scratch_shapes=[pltpu.VMEM((FTM, D_MODEL), BF16),
                        pltpu.VMEM((FTM + 8, TF), F32),
                        pltpu.VMEM((NF, 8, TF), F32)],
        compiler_params=_cparams(("arbitrary", "arbitrary"), VMEM_LIMIT_FFN),
        name="ffn_prompt",
    )(r2, wg, wu, wd, cw, cb, ln)


def _ffn_sample(r2, wg, wu, wd, cw, cb, ln, conv_state):
    n_st = 2 * DEC_BATCH
    specs = _ffn_specs(RS, lambda i: RP // RS)
    return pl.pallas_call(
        _ffn_sample_kernel,
        out_shape=(jax.ShapeDtypeStruct((RS, D_MODEL), F32),
                   jax.ShapeDtypeStruct((DEC_BATCH, CONV_W - 1, D_FF), F32),
                   jax.ShapeDtypeStruct((D_FF, D_MODEL), BF16)),
        grid=(1, NF),
        in_specs=specs + [pl.BlockSpec((n_st, TF), lambda i, f: (0, f))],
        out_specs=(pl.BlockSpec((RS, D_MODEL), lambda i, f: (0, 0)),
                   pl.BlockSpec((DEC_BATCH, CONV_W - 1, TF), lambda i, f: (0, 0, f)),
                   specs[3]),
        scratch_shapes=[pltpu.VMEM((RS, D_MODEL), BF16),
                        pltpu.VMEM((n_st + RS, TF), F32)],
        compiler_params=_cparams(("arbitrary", "arbitrary")),
        name="ffn_sample",
    )(r2, wg, wu, wd, cw, cb, ln, conv_state)


def _state_to_pairs(s):
    return s.reshape(s.shape[0], N_PAIRS, 128).transpose(1, 0, 2)


def _state_from_pairs(s):
    return s.transpose(1, 0, 2).reshape(1, s.shape[1], N_SSM_GROUPS, SSM_STATE)


def kernel(x_prompt, x_sample, mem_prompt, state_pool, state_ssm_re, state_ssm_im, state_conv, cache_mem_k, cache_mem_v, w_in, w_pool, pool_scale, lambda_re, lambda_im, log_step, b_re, b_im, c_re, c_im, d_skip, w_glu, b_glu, w_out, ln1_g, ln1_b, w_q, w_k, w_v, w_o, ln2_g, ln2_b, w_gate, w_up, conv_w, conv_b, w_down, ln3_g, ln3_b):
    bf = lambda w: w[0].astype(BF16)
    row = lambda v: v[0].reshape(1, -1)
    xp = x_prompt.reshape(RP, D_MODEL)
    xs = x_sample.transpose(1, 0, 2).reshape(RS, D_MODEL)

    prep_args = _ssm_prep_args(lambda_re[0], lambda_im[0], log_step[0], b_re[0], b_im[0], c_re[0], c_im[0])
    (u_ssm, a, utail, us_pool), prep, (w_glu_b, w_out_b) = _front(
        xp, xs, bf(w_in), state_pool[0].transpose(1, 0, 2), bf(w_pool), row(pool_scale), prep_args,
        (w_glu[0], w_out[0]))
    prep["dskip"] = jnp.broadcast_to(d_skip[0].reshape(N_PAIRS, 2, 1, SSM_GROUP_CH),
                                     (N_PAIRS, 2, CHUNK, SSM_GROUP_CH)).reshape(N_PAIRS, 1, PAIR_W)
    (z, hp_re, hp_im, hs_re, hs_im), (w_q_b, w_o_b) = _ssm(
        u_ssm, prep, _state_to_pairs(state_ssm_re[0]), _state_to_pairs(state_ssm_im[0]), (w_q[0], w_o[0]))
    h1, h1b, (wg,) = _mix(a, z, xp, xs, w_glu_b, row(b_glu), w_out_b, row(ln1_g), row(ln1_b), (w_gate[0],))

    mem = mem_prompt.reshape(BATCH * N_MEM, D_MODEL)
    mk, mkb = _memproj(mem, w_k[0], "mem_k")
    mv, mvb = _memproj(mem, w_v[0], "mem_v")
    o_s, q_p = _attn_sample(_qproj_sample(h1b, w_q_b), cache_mem_k, cache_mem_v, h1b, w_q_b)
    r2, (wu,) = _attn_block(h1, q_p, mkb, mvb, o_s, w_o_b, (w_up[0],))

    cw, cb = conv_w[0], row(conv_b)
    conv_st = state_conv[0].transpose(1, 0, 2).reshape(2 * DEC_BATCH, D_FF)
    ln23 = jnp.concatenate([row(ln2_g), row(ln2_b), row(ln3_g), row(ln3_b)], axis=0)
    y_s, g_new, wd = _ffn_sample(r2, wg, wu, w_down[0], cw, cb, ln23, conv_st)
    y_p, gtail = _ffn_prompt(r2, wg, wu, wd, cw, cb, ln23)

    y_prompt = y_p.reshape(BATCH, SEQ, D_MODEL)
    y_sample = y_s.reshape(DEC_SEQ, DEC_BATCH, D_MODEL).transpose(1, 0, 2)
    p_pool = utail[BLOCKS_PER_SEQ - 1:NBP:BLOCKS_PER_SEQ, 16 - POOL_BUF:][None]
    s_ext = jnp.concatenate([state_pool[0], us_pool.reshape(DEC_SEQ, DEC_BATCH, D_POOL).transpose(1, 0, 2)], axis=1)
    s_pool = s_ext[None, :, DEC_SEQ:]
    p_conv = gtail[FBLOCKS_PER_SEQ - 1::FBLOCKS_PER_SEQ, 6:8][None]
    s_conv = g_new[None]
    return (y_prompt, y_sample,
            p_pool, _state_from_pairs(hp_re[:, :BATCH]), _state_from_pairs(hp_im[:, :BATCH]), p_conv,
            mk, mv,
            s_pool, _state_from_pairs(hs_re), _state_from_pairs(hs_im), s_conv)
```

```python
import jax
import jax.numpy as jnp
from jax import lax
from jax.experimental import pallas as pl
from jax.experimental.pallas import tpu as pltpu

F32 = jnp.float32
BF16 = jnp.bfloat16

D_MODEL = 2048
BATCH = 4
SEQ = 2048
DEC_BATCH = 128
DEC_SEQ = 4
PAST_LEN = 16384
D_POOL = 1024
D_SSM = 1024
POOL_WINDOWS = (2, 4, 8, 16)
POOL_GROUP_DIM = 256
POOL_BUF = 15
SSM_GROUP_CH = 16
N_SSM_GROUPS = 64
SSM_STATE = 64
N_MEM = 256
N_XHEADS = 4
XHEAD_DIM = 512
D_FF = 5632
CONV_W = 3
ALPHA = 2.0 ** 0.25
LN_EPS = 1e-5

RP = BATCH * SEQ
RS = DEC_BATCH * DEC_SEQ
R = RP + RS
TM = 512
NB = R // TM
NBP = RP // TM
BLOCKS_PER_SEQ = SEQ // TM

CHUNK = 16
N_PAIRS = N_SSM_GROUPS // 2
PAIR_W = 2 * CHUNK * SSM_GROUP_CH
CHUNKS_PER_SEQ = SEQ // CHUNK
P_CHUNK_ROWS = BATCH * CHUNKS_PER_SEQ
SCAN_PAD = 64
N_SCAN_STEPS = 7

SUB = 256
TF = 512
NF = D_FF // TF

VMEM_LIMIT = 56 * 1024 * 1024
VMEM_LIMIT_FFN = 60 * 1024 * 1024


def _cparams(sem, vmem_limit=VMEM_LIMIT):
    return pltpu.CompilerParams(dimension_semantics=sem, vmem_limit_bytes=vmem_limit)


def _const_spec(shape):
    n = len(shape)
    return pl.BlockSpec(shape, lambda *_: (0,) * n, pipeline_mode=pl.Buffered(1))


def _layer_norm(x, g, b):
    mu = jnp.mean(x, axis=-1, keepdims=True)
    xc = x - mu
    var = jnp.mean(xc * xc, axis=-1, keepdims=True)
    return xc * lax.rsqrt(var + LN_EPS) * g + b


def _dot(a, b):
    return jnp.dot(a, b, preferred_element_type=F32)


def _dot_nt(a, b):
    return lax.dot_general(a, b, (((1,), (1,)), ((), ())), preferred_element_type=F32)


def _cast_spec(a, n_steps):
    return pl.BlockSpec((a.shape[0] // n_steps, a.shape[1]), lambda i: (jnp.minimum(i, n_steps - 1), 0))


def _cast_specs(arrays, n_steps):
    specs = [_cast_spec(a, n_steps) for a in arrays]
    return specs, [jax.ShapeDtypeStruct(a.shape, BF16) for a in arrays]


def _cast_slices(in_refs, out_refs):
    for src_ref, dst_ref in zip(in_refs, out_refs):
        dst_ref[...] = src_ref[...].astype(BF16)


N_PREP_IN, N_PREP_OUT = 7, 10
PREP_PAIRS = N_PAIRS // NBP


def _pool_mix(g, pooled, w_ref, scale_ref):
    sl = slice(g * POOL_GROUP_DIM, (g + 1) * POOL_GROUP_DIM)
    return (_dot(pooled.astype(BF16), w_ref[g]) * scale_ref[:, sl]).astype(BF16)


N_FRONT_OUT = 5


def _front_kernel(xp_ref, xs_ref, w_ref, st_ref, wp_ref, scale_ref, *refs):
    prep_in, refs = refs[:N_PREP_IN], refs[N_PREP_IN:]
    n_cast = (len(refs) - N_FRONT_OUT - N_PREP_OUT - 1) // 2
    cast_in, refs = refs[:n_cast], refs[n_cast:]
    ussm_ref, a_ref, utail_ref, us_ref, xsm_ref = refs[:N_FRONT_OUT]
    prep_out, cast_out, ext_ref = (refs[N_FRONT_OUT:N_FRONT_OUT + N_PREP_OUT],
                                   refs[N_FRONT_OUT + N_PREP_OUT:-1], refs[-1])
    i = pl.program_id(0)

    @pl.when(i == 0)
    def _():
        for j in range(DEC_SEQ):
            xsm_ref[j * DEC_BATCH:(j + 1) * DEC_BATCH, :] = xs_ref[:, j, :]

    xb = jnp.where(i < NBP, xp_ref[...], xsm_ref[...]).astype(BF16)

    first = (i % BLOCKS_PER_SEQ) == 0
    ext_ref[0:16, :] = jnp.where(first, 0.0, ext_ref[TM:TM + 16, :])
    ext_ref[16:16 + TM, :] = _dot(xb, w_ref[:, 0:D_POOL])
    utail_ref[0] = ext_ref[TM:TM + 16, :]
    pos = (i % BLOCKS_PER_SEQ) * TM + lax.broadcasted_iota(jnp.int32, (TM, 1), 0)
    qc = D_SSM // len(POOL_WINDOWS)
    for g, w in enumerate(POOL_WINDOWS):
        ussm_ref[:, g * qc:(g + 1) * qc] = _dot(xb, w_ref[:, D_POOL + g * qc:D_POOL + (g + 1) * qc])
        sl = slice(g * POOL_GROUP_DIM, (g + 1) * POOL_GROUP_DIM)
        acc = ext_ref[16:16 + TM, sl]
        for k in range(1, w):
            acc = acc + ext_ref[16 - k:16 - k + TM, sl]
        cnt = jnp.minimum(pos + 1, w).astype(F32)
        a_ref[:, sl] = _pool_mix(g, acc / cnt - ext_ref[16:16 + TM, sl], wp_ref, scale_ref)

    for pp in range(PREP_PAIRS):
        _ssm_prep_pair(pp, *prep_in, *prep_out)
    _cast_slices(cast_in, cast_out)

    @pl.when(i == NBP)
    def _():
        us_ref[...] = ext_ref[16:16 + TM, :]
        tok = lambda t, sl: ext_ref[16 + t * DEC_BATCH:16 + (t + 1) * DEC_BATCH, sl]
        for j in range(DEC_SEQ):
            rows = slice(j * DEC_BATCH, (j + 1) * DEC_BATCH)
            for g, w in enumerate(POOL_WINDOWS):
                sl = slice(g * POOL_GROUP_DIM, (g + 1) * POOL_GROUP_DIM)
                acc = tok(j, sl)
                for k in range(1, w):
                    e = POOL_BUF + j - k
                    acc = acc + (tok(e - POOL_BUF, sl) if e >= POOL_BUF else st_ref[e, :, sl])
                cnt = float(min(PAST_LEN + j + 1, w))
                a_ref[rows, sl] = _pool_mix(g, acc / cnt - tok(j, sl), wp_ref, scale_ref)


def _front(xp, xs, w, state_t, w_pool, pool_scale, prep_args, to_cast):
    G, N, C, L = N_SSM_GROUPS, SSM_STATE, SSM_GROUP_CH, CHUNK
    wd = L * C
    step = lambda i: jnp.minimum(i, NBP - 1)
    g3 = lambda a, b: pl.BlockSpec((2 * PREP_PAIRS, a, b), lambda i: (step(i), 0, 0))
    p3 = lambda a, b: pl.BlockSpec((PREP_PAIRS, a, b), lambda i: (step(i), 0, 0))
    sds = jax.ShapeDtypeStruct
    cast_specs, cast_shapes = _cast_specs(to_cast, NBP)
    row = lambda wdt: pl.BlockSpec((TM, wdt), lambda i: (i, 0))
    outs = pl.pallas_call(
        _front_kernel,
        out_shape=(
            sds((R, D_SSM), F32), sds((R, D_POOL), BF16), sds((NB, 16, D_POOL), F32), sds((RS, D_POOL), F32),
            sds((RS, D_MODEL), F32),
            sds((N_PAIRS, 2, wd, wd), BF16),
            sds((N_PAIRS, PAIR_W, 128), BF16), sds((N_PAIRS, PAIR_W, 128), BF16),
            sds((N_PAIRS, PAIR_W, 128), BF16), sds((N_PAIRS, PAIR_W, 128), BF16),
            sds((N_PAIRS, PAIR_W, 128), BF16), sds((N_PAIRS, PAIR_W, 128), BF16),
            sds((N_PAIRS, 8, 128), F32), sds((N_PAIRS, 8, 128), F32), sds((N_PAIRS, 8, 128), F32),
            *cast_shapes,
        ),
        grid=(NB,),
        in_specs=[
            pl.BlockSpec((TM, D_MODEL), lambda i: (step(i), 0)),
            _const_spec((DEC_BATCH, DEC_SEQ, D_MODEL)),
            _const_spec((D_MODEL, D_MODEL)),
            _const_spec((POOL_BUF, DEC_BATCH, D_POOL)),
            _const_spec((4, POOL_GROUP_DIM, POOL_GROUP_DIM)),
            _const_spec((1, D_POOL)),
            g3(1, N), g3(1, N), g3(1, 1), g3(C, N), g3(C, N), g3(C, N), g3(C, N),
            *cast_specs,
        ],
        out_specs=(
            row(D_SSM), row(D_POOL), pl.BlockSpec((1, 16, D_POOL), lambda i: (i, 0, 0)),
            pl.BlockSpec((RS, D_POOL), lambda i: (0, 0)),
            pl.BlockSpec((RS, D_MODEL), lambda i: (0, 0)),
            pl.BlockSpec((PREP_PAIRS, 2, wd, wd), lambda i: (step(i), 0, 0, 0)),
            p3(PAIR_W, 128), p3(PAIR_W, 128), p3(PAIR_W, 128), p3(PAIR_W, 128),
            p3(PAIR_W, 128), p3(PAIR_W, 128),
            p3(8, 128), p3(8, 128), p3(8, 128),
            *cast_specs,
        ),
        scratch_shapes=[pltpu.VMEM((TM + 16, D_POOL), F32)],
        compiler_params=_cparams(("arbitrary",)),
        name="front",
    )(xp, xs, w, state_t, w_pool, pool_scale, *prep_args, *to_cast)
    names = ("t", "pre", "pim", "p4re", "p4im", "qre", "qim", "apow_re", "apow_im", "lam4")
    return (outs[:N_FRONT_OUT], dict(zip(names, outs[N_FRONT_OUT:N_FRONT_OUT + N_PREP_OUT])),
            outs[N_FRONT_OUT + N_PREP_OUT:])


SLAB_GROUPS = 128 // SSM_GROUP_CH
SLAB_PAIRS = SLAB_GROUPS // 2
N_SLABS = N_SSM_GROUPS // SLAB_GROUPS


def _ssm_pair(q, u, t_ref, pre_ref, pim_ref, p4re_ref, p4im_ref, qre_ref, qim_ref,
              apow_re_ref, apow_im_ref, lam4_ref, dskip_ref, h0re_ref, h0im_ref,
              hp_re_ref, hp_im_ref, hs_re_ref, hs_im_ref, hre_scr, him_scr):
    ub = u.astype(BF16)
    half = PAIR_W // 2
    y = jnp.concatenate([_dot(ub[:, :half], t_ref[q, 0]), _dot(ub[:, half:], t_ref[q, 1])], axis=1)

    ubp = ub[:P_CHUNK_ROWS]
    hre_scr[SCAN_PAD:SCAN_PAD + P_CHUNK_ROWS, :] = _dot(ubp, pre_ref[q])
    him_scr[SCAN_PAD:SCAN_PAD + P_CHUNK_ROWS, :] = _dot(ubp, pim_ref[q])
    kk = lax.broadcasted_iota(jnp.int32, (P_CHUNK_ROWS, 1), 0) % CHUNKS_PER_SEQ
    for s in range(N_SCAN_STEPS):
        d = 1 << s
        ar = apow_re_ref[q, s:s + 1, :]
        ai = apow_im_ref[q, s:s + 1, :]
        hr = hre_scr[SCAN_PAD:SCAN_PAD + P_CHUNK_ROWS, :]
        hi = him_scr[SCAN_PAD:SCAN_PAD + P_CHUNK_ROWS, :]
        pr = hre_scr[SCAN_PAD - d:SCAN_PAD - d + P_CHUNK_ROWS, :]
        pi = him_scr[SCAN_PAD - d:SCAN_PAD - d + P_CHUNK_ROWS, :]
        keep = kk >= d
        hre_scr[SCAN_PAD:SCAN_PAD + P_CHUNK_ROWS, :] = hr + jnp.where(keep, ar * pr - ai * pi, 0.0)
        him_scr[SCAN_PAD:SCAN_PAD + P_CHUNK_ROWS, :] = hi + jnp.where(keep, ar * pi + ai * pr, 0.0)
    hp_re_ref[q] = jnp.zeros((8, 128), F32)
    hp_im_ref[q] = jnp.zeros((8, 128), F32)
    for b in range(BATCH):
        last = SCAN_PAD + (b + 1) * CHUNKS_PER_SEQ - 1
        hp_re_ref[q, b:b + 1, :] = hre_scr[last:last + 1, :]
        hp_im_ref[q, b:b + 1, :] = him_scr[last:last + 1, :]
    prev_ok = kk >= 1
    hprev_re = jnp.where(prev_ok, hre_scr[SCAN_PAD - 1:SCAN_PAD - 1 + P_CHUNK_ROWS, :], 0.0)
    hprev_im = jnp.where(prev_ok, him_scr[SCAN_PAD - 1:SCAN_PAD - 1 + P_CHUNK_ROWS, :], 0.0)
    carry_p = _dot_nt(hprev_re.astype(BF16), qre_ref[q]) + _dot_nt(hprev_im.astype(BF16), qim_ref[q])

    ubs = ub[P_CHUNK_ROWS:]
    h0r = h0re_ref[q]
    h0i = h0im_ref[q]
    l4r = lam4_ref[q, 0:1, :]
    l4i = lam4_ref[q, 1:2, :]
    hs_re_ref[q] = l4r * h0r - l4i * h0i + _dot(ubs, p4re_ref[q])
    hs_im_ref[q] = l4r * h0i + l4i * h0r + _dot(ubs, p4im_ref[q])
    carry_s = _dot_nt(h0r.astype(BF16), qre_ref[q]) + _dot_nt(h0i.astype(BF16), qim_ref[q])

    y = y + jnp.concatenate([carry_p, carry_s], axis=0) + dskip_ref[q] * u
    return jax.nn.gelu(y)


N_SSM_IN, N_SSM_OUT = 13, 5


def _ssm_kernel(u_ref, *all_refs):
    n_cast = (len(all_refs) - N_SSM_IN - N_SSM_OUT - 2) // 2
    _cast_slices(all_refs[N_SSM_IN:N_SSM_IN + n_cast], all_refs[N_SSM_IN + n_cast + N_SSM_OUT:-2])
    refs = all_refs[:N_SSM_IN] + all_refs[N_SSM_IN + n_cast:N_SSM_IN + n_cast + N_SSM_OUT] + all_refs[-2:]
    z_ref = refs[13]
    hre_scr, him_scr = refs[18], refs[19]
    c = SSM_GROUP_CH
    hre_scr[0:SCAN_PAD, :] = jnp.zeros((SCAN_PAD, 128), F32)
    him_scr[0:SCAN_PAD, :] = jnp.zeros((SCAN_PAD, 128), F32)
    xt = []
    for i in range(CHUNK):
        xp = u_ref[pl.ds(i, P_CHUNK_ROWS, stride=CHUNK), :]
        if i < DEC_SEQ:
            xs = u_ref[RP + i * DEC_BATCH:RP + (i + 1) * DEC_BATCH, :]
        else:
            xs = jnp.zeros((DEC_BATCH, 128), F32)
        xt.append(jnp.concatenate([xp, xs], axis=0).T)
    def pair_input(q):
        halves = []
        for e in range(2):
            g = 2 * q + e
            bt = jnp.concatenate([xt[i][g * c:(g + 1) * c, :] for i in range(CHUNK)], axis=0)
            halves.append(bt.T)
        return jnp.concatenate(halves, axis=1)

    zt = []
    u_pair = pair_input(0)
    for q in range(SLAB_PAIRS):
        u_next = pair_input(q + 1) if q + 1 < SLAB_PAIRS else None
        z = _ssm_pair(q, u_pair, *refs[:13], *refs[14:])
        zt.append(z[:, :PAIR_W // 2].T)
        zt.append(z[:, PAIR_W // 2:].T)
        u_pair = u_next
    for i in range(CHUNK):
        zi = jnp.concatenate([zt[g][i * c:(i + 1) * c, :] for g in range(SLAB_GROUPS)], axis=0).T
        z_ref[pl.ds(i, P_CHUNK_ROWS, stride=CHUNK), :] = zi[:P_CHUNK_ROWS]
        if i < DEC_SEQ:
            z_ref[RP + i * DEC_BATCH:RP + (i + 1) * DEC_BATCH, :] = zi[P_CHUNK_ROWS:]


def _ssm(u_ssm, prep, h0re, h0im, to_cast):
    sp = SLAB_PAIRS
    slab3 = lambda a, b: pl.BlockSpec((sp, a, b), lambda s: (s, 0, 0))
    cast_specs, cast_shapes = _cast_specs(to_cast, N_SLABS)
    outs = pl.pallas_call(
        _ssm_kernel,
        out_shape=(
            jax.ShapeDtypeStruct((R, D_SSM), F32),
            jax.ShapeDtypeStruct((N_PAIRS, 8, 128), F32),
            jax.ShapeDtypeStruct((N_PAIRS, 8, 128), F32),
            jax.ShapeDtypeStruct((N_PAIRS, DEC_BATCH, 128), F32),
            jax.ShapeDtypeStruct((N_PAIRS, DEC_BATCH, 128), F32),
            *cast_shapes,
        ),
        grid=(N_SLABS,),
        in_specs=[
            pl.BlockSpec((R, 128), lambda s: (0, s)),
            pl.BlockSpec((sp, 2, 256, 256), lambda s: (s, 0, 0, 0)),
            slab3(PAIR_W, 128), slab3(PAIR_W, 128), slab3(PAIR_W, 128), slab3(PAIR_W, 128),
            slab3(PAIR_W, 128), slab3(PAIR_W, 128),
            slab3(8, 128), slab3(8, 128), slab3(8, 128),
            slab3(1, PAIR_W),
            slab3(DEC_BATCH, 128), slab3(DEC_BATCH, 128),
            *cast_specs,
        ],
        out_specs=(
            pl.BlockSpec((R, 128), lambda s: (0, s)),
            slab3(8, 128), slab3(8, 128), slab3(DEC_BATCH, 128), slab3(DEC_BATCH, 128),
            *cast_specs,
        ),
        scratch_shapes=[pltpu.VMEM((SCAN_PAD + P_CHUNK_ROWS, 128), F32),
                        pltpu.VMEM((SCAN_PAD + P_CHUNK_ROWS, 128), F32)],
        compiler_params=_cparams(("arbitrary",)),
        name="ssm",
    )(u_ssm, prep["t"], prep["pre"], prep["pim"], prep["p4re"], prep["p4im"],
      prep["qre"], prep["qim"], prep["apow_re"], prep["apow_im"], prep["lam4"],
      prep["dskip"], h0re, h0im, *to_cast)
    return outs[:N_SSM_OUT], outs[N_SSM_OUT:]


def _cmul(ar, ai, br, bi):
    return ar * br - ai * bi, ar * bi + ai * br


def _dot3_nt(a, b):
    ah = a.astype(BF16)
    bh = b.astype(BF16)
    al = (a - ah.astype(F32)).astype(BF16)
    bl = (b - bh.astype(F32)).astype(BF16)
    return _dot_nt(ah, bh) + _dot_nt(ah, bl) + _dot_nt(al, bh)


def _ssm_prep_pair(pp, lam_re_ref, lam_im_ref, lstep_ref, bt_re_ref, bt_im_ref, c_re_ref, c_im_ref,
                   t_ref, pre_ref, pim_ref, p4re_ref, p4im_ref, qre_ref, qim_ref,
                   apr_ref, api_ref, lam4_ref):
    L, C, N = CHUNK, SSM_GROUP_CH, SSM_STATE
    w = L * C
    p_re, p_im, p4_re, p4_im, q_re, q_im, a_re, a_im, l4 = [], [], [], [], [], [], [], [], []
    for e in range(2):
        ge = 2 * pp + e
        dt = jnp.exp(lstep_ref[ge])
        lr, li = lam_re_ref[ge], lam_im_ref[ge]
        mag = jnp.exp(lr * dt)
        ang = li * dt
        zr, zi = mag * jnp.cos(ang), mag * jnp.sin(ang)
        den = lr * lr + li * li
        fr = ((zr - 1.0) * lr + zi * li) / den
        fi = (zi * lr - (zr - 1.0) * li) / den
        bbr, bbi = _cmul(fr, fi, bt_re_ref[ge], bt_im_ref[ge])
        pr, pi = [jnp.ones((1, N), F32)], [jnp.zeros((1, N), F32)]
        for _ in range(L):
            nr, ni = _cmul(pr[-1], pi[-1], zr, zi)
            pr.append(nr)
            pi.append(ni)
        stack = lambda blocks, part: jnp.concatenate([b[part] for b in blocks], axis=0)
        blocks = [_cmul(bbr, bbi, pr[L - 1 - i], pi[L - 1 - i]) for i in range(L)]
        p_re.append(stack(blocks, 0))
        p_im.append(stack(blocks, 1))
        blocks4 = [_cmul(bbr, bbi, pr[DEC_SEQ - 1 - i], pi[DEC_SEQ - 1 - i]) for i in range(DEC_SEQ)]
        pad = jnp.zeros(((L - DEC_SEQ) * C, N), F32)
        p4_re.append(jnp.concatenate([stack(blocks4, 0), pad], axis=0))
        p4_im.append(jnp.concatenate([stack(blocks4, 1), pad], axis=0))
        sr, si = [pr[L]], [pi[L]]
        for _ in range(N_SCAN_STEPS - 1):
            nr, ni = _cmul(sr[-1], si[-1], sr[-1], si[-1])
            sr.append(nr)
            si.append(ni)
        a_re.append(jnp.concatenate(sr + [jnp.zeros((8 - N_SCAN_STEPS, N), F32)], axis=0))
        a_im.append(jnp.concatenate(si + [jnp.zeros((8 - N_SCAN_STEPS, N), F32)], axis=0))
        l4.append(jnp.concatenate([pr[DEC_SEQ], pi[DEC_SEQ], jnp.zeros((6, N), F32)], axis=0))

        cr, ci = c_re_ref[ge], c_im_ref[ge]
        ck = [_cmul(cr, ci, pr[k], pi[k]) for k in range(L + 1)]
        q_re.append(stack(ck[1:], 0))
        q_im.append(-stack(ck[1:], 1))
        v = _dot3_nt(bbr, stack(ck[:L], 0)) - _dot3_nt(bbi, stack(ck[:L], 1))
        lane = lax.broadcasted_iota(jnp.int32, (C, w), 1)
        rows = [v] + [jnp.where(lane >= C * i, pltpu.roll(v, C * i, axis=1), 0.0) for i in range(1, L)]
        t_ref[pp, e] = jnp.concatenate(rows, axis=0).astype(BF16)

    def diag_rows(m):
        z = jnp.zeros_like(m[0])
        return jnp.concatenate([jnp.concatenate([m[0], z], axis=1),
                                jnp.concatenate([z, m[1]], axis=1)], axis=0)

    pre_ref[pp] = diag_rows(p_re).astype(BF16)
    pim_ref[pp] = diag_rows(p_im).astype(BF16)
    p4re_ref[pp] = diag_rows(p4_re).astype(BF16)
    p4im_ref[pp] = diag_rows(p4_im).astype(BF16)
    qre_ref[pp] = diag_rows(q_re).astype(BF16)
    qim_ref[pp] = diag_rows(q_im).astype(BF16)
    apr_ref[pp] = jnp.concatenate(a_re, axis=1)
    api_ref[pp] = jnp.concatenate(a_im, axis=1)
    lam4_ref[pp] = jnp.concatenate(l4, axis=1)


def _ssm_prep_args(lambda_re, lambda_im, log_step, b_re, b_im, c_re, c_im):
    return (lambda_re[:, None, :], lambda_im[:, None, :], log_step[:, None, None],
            b_re.transpose(0, 2, 1), b_im.transpose(0, 2, 1), c_re, c_im)


def _mix_kernel(a_ref, z_ref, xp_ref, xs_ref, wglu_ref, bglu_ref, wout_ref, g_ref, b_ref, *refs):
    n_cast = (len(refs) - 2) // 2
    o_ref, ob_ref = refs[n_cast], refs[n_cast + 1]
    _cast_slices(refs[:n_cast], refs[n_cast + 2:])
    is_prompt = pl.program_id(0) < NBP
    n_sub = TM // SUB

    def glu(r):
        z = z_ref[r * SUB:(r + 1) * SUB, :]
        gate = _dot(z.astype(BF16), wglu_ref[...]) + bglu_ref[...]
        return (z * jax.nn.sigmoid(gate)).astype(BF16)

    bmix = glu(0)
    for r in range(n_sub):
        rows = slice(r * SUB, (r + 1) * SUB)
        bmix_next = glu(r + 1) if r + 1 < n_sub else None
        mix = _dot(a_ref[rows, :], wout_ref[0:D_POOL, :]) + _dot(bmix, wout_ref[D_POOL:, :])
        x = jnp.where(is_prompt, xp_ref[rows, :], xs_ref[rows, :])
        h1 = _layer_norm(ALPHA * x + mix, g_ref[...], b_ref[...])
        o_ref[rows, :] = h1
        ob_ref[rows, :] = h1.astype(BF16)
        bmix = bmix_next


def _mix(a, z, xp, xs, w_glu, b_glu, w_out, g, b, to_cast):
    row = lambda w: pl.BlockSpec((TM, w), lambda i: (i, 0))
    cast_specs, cast_shapes = _cast_specs(to_cast, NBP)
    outs = pl.pallas_call(
        _mix_kernel,
        out_shape=(jax.ShapeDtypeStruct((R, D_MODEL), F32), jax.ShapeDtypeStruct((R, D_MODEL), BF16), *cast_shapes),
        grid=(NB,),
        in_specs=[
            row(D_POOL), row(D_SSM),
            pl.BlockSpec((TM, D_MODEL), lambda i: (jnp.minimum(i, NBP - 1), 0)),
            _const_spec((RS, D_MODEL)),
            _const_spec((D_SSM, D_SSM)), _const_spec((1, D_SSM)),
            _const_spec((D_MODEL, D_MODEL)), _const_spec((1, D_MODEL)), _const_spec((1, D_MODEL)),
            *cast_specs,
        ],
        out_specs=(row(D_MODEL), row(D_MODEL), *cast_specs),
        compiler_params=_cparams(("arbitrary",)),
        name="mix_ln1",
    )(a, z, xp, xs, w_glu, b_glu, w_out, g, b, *to_cast)
    return outs[0], outs[1], outs[2:]


def _qproj_sample_kernel(x_ref, w_ref, o_ref):
    q = _dot(x_ref[...], w_ref[...])
    for j in range(DEC_SEQ):
        o_ref[:, j, :] = q[j * DEC_BATCH:(j + 1) * DEC_BATCH, :]
    o_ref[:, DEC_SEQ:, :] = jnp.zeros((DEC_BATCH, Q_PAD - DEC_SEQ, D_MODEL), F32)


def _qproj_sample(h, w):
    return pl.pallas_call(
        _qproj_sample_kernel,
        out_shape=jax.ShapeDtypeStruct((DEC_BATCH, Q_PAD, D_MODEL), F32),
        grid=(1,),
        in_specs=[pl.BlockSpec((RS, D_MODEL), lambda i: (RP // RS, 0)),
                  pl.BlockSpec((D_MODEL, D_MODEL), lambda i: (0, 0))],
        out_specs=pl.BlockSpec((DEC_BATCH, Q_PAD, D_MODEL), lambda i: (0, 0, 0)),
        compiler_params=_cparams(("arbitrary",)),
        name="qproj_sample",
    )(h, w)


def _memproj_kernel(m_ref, w_ref, o_ref, ob_ref, wb_scr):
    @pl.when(pl.program_id(0) == 0)
    def _():
        wb_scr[...] = w_ref[...].astype(BF16)

    k = _dot(m_ref[...].astype(BF16), wb_scr[...])
    ob_ref[...] = k.astype(BF16)
    for h in range(N_XHEADS):
        o_ref[0, 0, :, h, :] = k[:, h * XHEAD_DIM:(h + 1) * XHEAD_DIM]


def _memproj(mem, w, name):
    rows = BATCH * N_MEM
    seq = pl.BlockSpec((N_MEM, D_MODEL), lambda b: (b, 0))
    return pl.pallas_call(
        _memproj_kernel,
        out_shape=(jax.ShapeDtypeStruct((1, BATCH, N_MEM, N_XHEADS, XHEAD_DIM), F32),
                   jax.ShapeDtypeStruct((rows, D_MODEL), BF16)),
        grid=(BATCH,),
        in_specs=[seq, _const_spec((D_MODEL, D_MODEL))],
        out_specs=(pl.BlockSpec((1, 1, N_MEM, N_XHEADS, XHEAD_DIM), lambda b: (0, b, 0, 0, 0)), seq),
        scratch_shapes=[pltpu.VMEM((D_MODEL, D_MODEL), BF16)],
        compiler_params=_cparams(("arbitrary",)),
        name=name,
    )(mem, w)


def _attend(q, k, v, between=None):
    def scores(h):
        sl = slice(h * XHEAD_DIM, (h + 1) * XHEAD_DIM)
        return lax.dot_general(q[:, sl], k[:, sl], (((1,), (1,)), ((), ())),
                               preferred_element_type=F32) * (XHEAD_DIM ** -0.5)

    def head_out(h, s):
        s = s - jnp.max(s, axis=-1, keepdims=True)
        e = jnp.exp(s)
        p = e / jnp.sum(e, axis=-1, keepdims=True)
        return _dot(p.astype(BF16), v[:, h * XHEAD_DIM:(h + 1) * XHEAD_DIM])

    outs = []
    s = scores(0)
    for h in range(N_XHEADS):
        if between is not None:
            between(h)
        s_next = scores(h + 1) if h + 1 < N_XHEADS else None
        outs.append(head_out(h, s))
        s = s_next
    return jnp.concatenate(outs, axis=1)


def _attn_block_kernel(h_ref, q_ref, k_ref, v_ref, os_ref, wo_ref, g_ref, b_ref, *refs):
    n_cast = (len(refs) - 1) // 2
    o_ref = refs[n_cast]
    _cast_slices(refs[:n_cast], refs[n_cast + 1:])
    i = pl.program_id(0)
    n_sub = TM // SUB

    def normed(rows, y):
        o_ref[rows, :] = _layer_norm(ALPHA * h_ref[rows, :] + y, g_ref[...], b_ref[...])

    @pl.when(i < NBP)
    def _():
        prev_rows, prev_o = None, None
        for r in range(n_sub):
            rows = slice(r * SUB, (r + 1) * SUB)
            ys = []
            between = None
            if prev_o is not None:
                between = lambda hh, po=prev_o: ys.append(
                    _dot(po, wo_ref[:, hh * XHEAD_DIM:(hh + 1) * XHEAD_DIM]))
            o = _attend(q_ref[rows, :], k_ref[...], v_ref[...], between).astype(BF16)
            if prev_o is not None:
                normed(prev_rows, jnp.concatenate(ys, axis=1))
            prev_rows, prev_o = rows, o
        normed(prev_rows, _dot(prev_o, wo_ref[...]))

    @pl.when(i == NBP)
    def _():
        for r in range(n_sub):
            rows = slice(r * SUB, (r + 1) * SUB)
            steps = range(r * SUB // DEC_BATCH, (r + 1) * SUB // DEC_BATCH)
            o = jnp.concatenate([os_ref[:, j, :] for j in steps], axis=0).astype(BF16)
            normed(rows, _dot(o, wo_ref[...]))


def _attn_block(h, q_prompt, kb, vb, o_sample, wo, g, b, to_cast):
    kv = pl.BlockSpec((N_MEM, D_MODEL), lambda i: (jnp.minimum(i // BLOCKS_PER_SEQ, BATCH - 1), 0))
    row = pl.BlockSpec((TM, D_MODEL), lambda i: (i, 0))
    qrow = pl.BlockSpec((TM, D_MODEL), lambda i: (jnp.minimum(i, NBP - 1), 0))
    cast_specs, cast_shapes = _cast_specs(to_cast, NBP)
    outs = pl.pallas_call(
        _attn_block_kernel,
        out_shape=(jax.ShapeDtypeStruct((R, D_MODEL), F32), *cast_shapes),
        grid=(NB,),
        in_specs=[row, qrow, kv, kv, _const_spec((DEC_BATCH, Q_PAD, D_MODEL)), _const_spec((D_MODEL, D_MODEL)),
                  _const_spec((1, D_MODEL)), _const_spec((1, D_MODEL)), *cast_specs],
        out_specs=(row, *cast_specs),
        compiler_params=_cparams(("arbitrary",)),
        name="attn_block",
    )(h, q_prompt, kb, vb, o_sample, wo, g, b, *to_cast)
    return outs[0], outs[1:]


ATT_BB = 4
Q_PAD = 8
KV_ROWS = N_MEM * N_XHEADS


QP_ROWS = RP // (DEC_BATCH // ATT_BB)


def _attn_sample_kernel(q_ref, k_ref, v_ref, h_ref, wq_ref, o_ref, qp_ref):
    hb = h_ref[...]
    qcols = D_MODEL // ATT_BB
    shape = (N_XHEADS * Q_PAD, KV_ROWS)
    same_head = (lax.broadcasted_iota(jnp.int32, shape, 0) // Q_PAD
                 == lax.broadcasted_iota(jnp.int32, shape, 1) % N_XHEADS)
    def scores(b):
        q = q_ref[b]
        qs = jnp.concatenate([q[:, h * XHEAD_DIM:(h + 1) * XHEAD_DIM] for h in range(N_XHEADS)], axis=0)
        k = k_ref[0, b].reshape(KV_ROWS, XHEAD_DIM).astype(BF16)
        return lax.dot_general(qs.astype(BF16), k, (((1,), (1,)), ((), ())),
                               preferred_element_type=F32) * (XHEAD_DIM ** -0.5)

    def attend(b, s):
        s = jnp.where(same_head, s, -1e30)
        s = s - jnp.max(s, axis=-1, keepdims=True)
        e = jnp.exp(s)
        p = e / jnp.sum(e, axis=-1, keepdims=True)
        v = v_ref[0, b].reshape(KV_ROWS, XHEAD_DIM).astype(BF16)
        o = _dot(p.astype(BF16), v)
        for h in range(N_XHEADS):
            o_ref[b, :, h * XHEAD_DIM:(h + 1) * XHEAD_DIM] = o[h * Q_PAD:(h + 1) * Q_PAD]

    s_prev = scores(0)
    for b in range(ATT_BB):
        cols = slice(b * qcols, (b + 1) * qcols)
        qp_ref[:, cols] = _dot(hb, wq_ref[:, cols]).astype(BF16)
        s_next = scores(b + 1) if b + 1 < ATT_BB else None
        attend(b, s_prev)
        s_prev = s_next


def _attn_sample(q_pad, k, v, h, wq):
    kv = pl.BlockSpec((1, ATT_BB, N_MEM, N_XHEADS, XHEAD_DIM), lambda i: (0, i, 0, 0, 0))
    qo = pl.BlockSpec((ATT_BB, Q_PAD, D_MODEL), lambda i: (i, 0, 0))
    hq = pl.BlockSpec((QP_ROWS, D_MODEL), lambda i: (i, 0))
    return pl.pallas_call(
        _attn_sample_kernel,
        out_shape=(jax.ShapeDtypeStruct((DEC_BATCH, Q_PAD, D_MODEL), F32),
                   jax.ShapeDtypeStruct((RP, D_MODEL), BF16)),
        grid=(DEC_BATCH // ATT_BB,),
        in_specs=[qo, kv, kv, hq, _const_spec((D_MODEL, D_MODEL))],
        out_specs=(qo, hq),
        compiler_params=_cparams(("arbitrary",)),
        name="attn_sample",
    )(q_pad, k, v, h, wq)


FTM = 1024
FNB = RP // FTM
FBLOCKS_PER_SEQ = SEQ // FTM
FSUB = 512
FSUB_LAST = 128
FSUB_SAMPLE = 256


def _ffn_rows(n_rows, sub, conv_taps, h_ref, hb_scr, g_scr, g_base, wg, wu, wd, cw_ref, cb_ref,
              lng_ref, lnb_ref, y_ref, first, last):
    def gate_up(lo):
        if first:
            h = h_ref[lo:lo + sub, :]
            hb = h.astype(BF16)
            hb_scr[lo:lo + sub, :] = hb
        else:
            h, hb = None, hb_scr[lo:lo + sub, :]
        g = _dot(hb, wg)
        up = _dot(hb, wu)
        g_scr[g_base + lo:g_base + lo + sub, :] = g
        tap0, tap1 = conv_taps(lo, sub)
        gc = cb_ref[...] + cw_ref[0:1, :] * tap0 + cw_ref[1:2, :] * tap1 + cw_ref[2:3, :] * g
        return h, (jax.nn.silu(gc) * up).astype(BF16)

    def down(lo, h, act):
        y = _dot(act, wd) + (ALPHA * h if first else y_ref[lo:lo + sub, :])
        y_ref[lo:lo + sub, :] = _layer_norm(y, lng_ref[...], lnb_ref[...]) if last else y

    n_sub = n_rows // sub
    pending = gate_up(0)
    for r in range(n_sub):
        nxt = gate_up((r + 1) * sub) if r + 1 < n_sub else None
        down(r * sub, *pending)
        pending = nxt


def _ffn_chunk(f, run):
    @pl.when(f == 0)
    def _():
        run(True, False)

    @pl.when(jnp.logical_and(f > 0, f < NF - 1))
    def _():
        run(False, False)

    @pl.when(f == NF - 1)
    def _():
        run(False, True)


def _ffn_prompt_kernel(h_ref, wg_ref, wu_ref, wd_ref, cw_ref, cb_ref, lng_ref, lnb_ref,
                       y_ref, gtail_ref, hb_scr, g_scr, carry_scr):
    i = pl.program_id(0)
    f = pl.program_id(1)
    first_block = (i % FBLOCKS_PER_SEQ) == 0

    @pl.when(first_block)
    def _():
        g_scr[0:8, :] = jnp.zeros((8, TF), F32)

    @pl.when(jnp.logical_not(first_block))
    def _():
        g_scr[0:8, :] = carry_scr[f]

    taps = lambda lo, sub: (g_scr[6 + lo:6 + lo + sub, :], g_scr[7 + lo:7 + lo + sub, :])
    _ffn_chunk(f, lambda first, last: _ffn_rows(
        FTM, FSUB_LAST if last else FSUB, taps, h_ref, hb_scr, g_scr, 8, wg_ref[...], wu_ref[...], wd_ref[...], cw_ref, cb_ref,
        lng_ref, lnb_ref, y_ref, first, last))
    tail = g_scr[FTM:FTM + 8, :]
    carry_scr[f] = tail
    gtail_ref[0] = tail


def _ffn_sample_kernel(h_ref, wg_ref, wu_ref, wd_ref, cw_ref, cb_ref, lng_ref, lnb_ref, st_ref,
                       yseq_ref, gnew_ref, wdb_ref, hb_scr, g_scr, y_ref):
    f = pl.program_id(1)
    n_st = 2 * DEC_BATCH
    wdb_ref[...] = wd_ref[...].astype(BF16)
    g_scr[0:n_st, :] = st_ref[...]
    taps = lambda lo, sub: (g_scr[lo:lo + sub, :], g_scr[DEC_BATCH + lo:DEC_BATCH + lo + sub, :])
    _ffn_chunk(f, lambda first, last: _ffn_rows(
        RS, FSUB_SAMPLE, taps, h_ref, hb_scr, g_scr, n_st, wg_ref[...], wu_ref[...], wdb_ref[...], cw_ref, cb_ref,
        lng_ref, lnb_ref, y_ref, first, last))
    for t in range(CONV_W - 1):
        gnew_ref[:, t, :] = g_scr[RS + t * DEC_BATCH:RS + (t + 1) * DEC_BATCH, :]

    @pl.when(f == NF - 1)
    def _():
        for j in range(DEC_SEQ):
            yseq_ref[:, j, :] = y_ref[j * DEC_BATCH:(j + 1) * DEC_BATCH, :]


def _ffn_specs(tm, row_block_of):
    return [
        pl.BlockSpec((tm, D_MODEL), lambda i, f: (row_block_of(i), 0)),
        pl.BlockSpec((D_MODEL, TF), lambda i, f: (0, f)),
        pl.BlockSpec((D_MODEL, TF), lambda i, f: (0, f)),
        pl.BlockSpec((TF, D_MODEL), lambda i, f: (f, 0)),
        pl.BlockSpec((CONV_W, TF), lambda i, f: (0, f)),
        pl.BlockSpec((1, TF), lambda i, f: (0, f)),
        pl.BlockSpec((1, D_MODEL), lambda i, f: (0, 0)),
        pl.BlockSpec((1, D_MODEL), lambda i, f: (0, 0)),
    ]


def _ffn_prompt(h, wg, wu, wd, cw, cb, lng, lnb):
    return pl.pallas_call(
        _ffn_prompt_kernel,
        out_shape=(jax.ShapeDtypeStruct((RP, D_MODEL), F32),
                   jax.ShapeDtypeStruct((FNB, 8, D_FF), F32)),
        grid=(FNB, NF),
        in_specs=_ffn_specs(FTM, lambda i: i),
        out_specs=(pl.BlockSpec((FTM, D_MODEL), lambda i, f: (i, 0)),
                   pl.BlockSpec((1, 8, TF), lambda i, f: (i, 0, f))),
        scratch_shapes=[pltpu.VMEM((FTM, D_MODEL), BF16),
                        pltpu.VMEM((FTM + 8, TF), F32),
                        pltpu.VMEM((NF, 8, TF), F32)],
        compiler_params=_cparams(("arbitrary", "arbitrary"), VMEM_LIMIT_FFN),
        name="ffn_prompt",
    )(h, wg, wu, wd, cw, cb, lng, lnb)


def _ffn_sample(h, wg, wu, wd, cw, cb, lng, lnb, conv_state):
    n_st = 2 * DEC_BATCH
    specs = _ffn_specs(RS, lambda i: RP // RS)
    return pl.pallas_call(
        _ffn_sample_kernel,
        out_shape=(jax.ShapeDtypeStruct((DEC_BATCH, DEC_SEQ, D_MODEL), F32),
                   jax.ShapeDtypeStruct((DEC_BATCH, CONV_W - 1, D_FF), F32),
                   jax.ShapeDtypeStruct((D_FF, D_MODEL), BF16)),
        grid=(1, NF),
        in_specs=specs + [pl.BlockSpec((n_st, TF), lambda i, f: (0, f))],
        out_specs=(pl.BlockSpec((DEC_BATCH, DEC_SEQ, D_MODEL), lambda i, f: (0, 0, 0)),
                   pl.BlockSpec((DEC_BATCH, CONV_W - 1, TF), lambda i, f: (0, 0, f)),
                   specs[3]),
        scratch_shapes=[pltpu.VMEM((RS, D_MODEL), BF16),
                        pltpu.VMEM((n_st + RS, TF), F32),
                        pltpu.VMEM((RS, D_MODEL), F32)],
        compiler_params=_cparams(("arbitrary", "arbitrary")),
        name="ffn_sample",
    )(h, wg, wu, wd, cw, cb, lng, lnb, conv_state)


def _state_to_pairs(s):
    return s.reshape(s.shape[0], N_PAIRS, 128).transpose(1, 0, 2)


def _state_from_pairs(s):
    return s.transpose(1, 0, 2).reshape(1, s.shape[1], N_SSM_GROUPS, SSM_STATE)


def kernel(x_prompt, x_sample, mem_prompt, state_pool, state_ssm_re, state_ssm_im, state_conv, cache_mem_k, cache_mem_v, w_in, w_pool, pool_scale, lambda_re, lambda_im, log_step, b_re, b_im, c_re, c_im, d_skip, w_glu, b_glu, w_out, ln1_g, ln1_b, w_q, w_k, w_v, w_o, ln2_g, ln2_b, w_gate, w_up, conv_w, conv_b, w_down, ln3_g, ln3_b):
    bf = lambda w: w[0].astype(BF16)
    row = lambda v: v[0].reshape(1, -1)
    xp = x_prompt.reshape(RP, D_MODEL)

    prep_args = _ssm_prep_args(lambda_re[0], lambda_im[0], log_step[0], b_re[0], b_im[0], c_re[0], c_im[0])
    (u_ssm, a, utail, us_pool, xs), prep, (w_glu_b, w_out_b) = _front(
        xp, x_sample, bf(w_in), state_pool[0].transpose(1, 0, 2), bf(w_pool), row(pool_scale), prep_args,
        (w_glu[0], w_out[0]))
    prep["dskip"] = jnp.broadcast_to(d_skip[0].reshape(N_PAIRS, 2, 1, SSM_GROUP_CH),
                                     (N_PAIRS, 2, CHUNK, SSM_GROUP_CH)).reshape(N_PAIRS, 1, PAIR_W)
    (z, hp_re, hp_im, hs_re, hs_im), (w_q_b, w_o_b) = _ssm(
        u_ssm, prep, _state_to_pairs(state_ssm_re[0]), _state_to_pairs(state_ssm_im[0]), (w_q[0], w_o[0]))
    h1, h1b, (wg,) = _mix(a, z, xp, xs, w_glu_b, row(b_glu), w_out_b, row(ln1_g), row(ln1_b), (w_gate[0],))

    mem = mem_prompt.reshape(BATCH * N_MEM, D_MODEL)
    mk, mkb = _memproj(mem, w_k[0], "mem_k")
    mv, mvb = _memproj(mem, w_v[0], "mem_v")
    o_s, q_p = _attn_sample(_qproj_sample(h1b, w_q_b), cache_mem_k, cache_mem_v, h1b, w_q_b)
    h2, (wu,) = _attn_block(h1, q_p, mkb, mvb, o_s, w_o_b, row(ln2_g), row(ln2_b), (w_up[0],))

    cw, cb = conv_w[0], row(conv_b)
    conv_st = state_conv[0].transpose(1, 0, 2).reshape(2 * DEC_BATCH, D_FF)
    y_sample, g_new, wd = _ffn_sample(h2, wg, wu, w_down[0], cw, cb, row(ln3_g), row(ln3_b), conv_st)
    y_p, gtail = _ffn_prompt(h2, wg, wu, wd, cw, cb, row(ln3_g), row(ln3_b))

    y_prompt = y_p.reshape(BATCH, SEQ, D_MODEL)
    p_pool = utail[BLOCKS_PER_SEQ - 1:NBP:BLOCKS_PER_SEQ, 16 - POOL_BUF:][None]
    s_ext = jnp.concatenate([state_pool[0], us_pool.reshape(DEC_SEQ, DEC_BATCH, D_POOL).transpose(1, 0, 2)], axis=1)
    s_pool = s_ext[None, :, DEC_SEQ:]
    p_conv = gtail[FBLOCKS_PER_SEQ - 1::FBLOCKS_PER_SEQ, 6:8][None]
    s_conv = g_new[None]
    return (y_prompt, y_sample,
            p_pool, _state_from_pairs(hp_re[:, :BATCH]), _state_from_pairs(hp_im[:, :BATCH]), p_conv,
            mk, mv,
            s_pool, _state_from_pairs(hs_re), _state_from_pairs(hs_im), s_conv)
```

```python
import jax
import jax.numpy as jnp
from jax import lax
from jax.experimental import pallas as pl
from jax.experimental.pallas import tpu as pltpu

F32 = jnp.float32
BF16 = jnp.bfloat16

D_MODEL = 2048
BATCH = 4
SEQ = 2048
DEC_BATCH = 128
DEC_SEQ = 4
PAST_LEN = 16384
D_POOL = 1024
D_SSM = 1024
POOL_WINDOWS = (2, 4, 8, 16)
POOL_GROUP_DIM = 256
POOL_BUF = 15
SSM_GROUP_CH = 16
N_SSM_GROUPS = 64
SSM_STATE = 64
N_MEM = 256
N_XHEADS = 4
XHEAD_DIM = 512
D_FF = 5632
CONV_W = 3
ALPHA = 2.0 ** 0.25
LN_EPS = 1e-5

RP = BATCH * SEQ
RS = DEC_BATCH * DEC_SEQ
R = RP + RS
TM = 512
NB = R // TM
NBP = RP // TM
BLOCKS_PER_SEQ = SEQ // TM

CHUNK = 16
N_PAIRS = N_SSM_GROUPS // 2
PAIR_W = 2 * CHUNK * SSM_GROUP_CH
CHUNKS_PER_SEQ = SEQ // CHUNK
P_CHUNK_ROWS = BATCH * CHUNKS_PER_SEQ
SCAN_PAD = 64
N_SCAN_STEPS = 7

SUB = 256
TF = 512
NF = D_FF // TF

VMEM_LIMIT = 56 * 1024 * 1024
VMEM_LIMIT_FFN = 60 * 1024 * 1024


def _cparams(sem, vmem_limit=VMEM_LIMIT):
    return pltpu.CompilerParams(dimension_semantics=sem, vmem_limit_bytes=vmem_limit)


def _const_spec(shape):
    n = len(shape)
    return pl.BlockSpec(shape, lambda *_: (0,) * n, pipeline_mode=pl.Buffered(1))


def _layer_norm(x, g, b):
    mu = jnp.mean(x, axis=-1, keepdims=True)
    xc = x - mu
    var = jnp.mean(xc * xc, axis=-1, keepdims=True)
    return xc * lax.rsqrt(var + LN_EPS) * g + b


def _dot(a, b):
    return jnp.dot(a, b, preferred_element_type=F32)


def _dot_nt(a, b):
    return lax.dot_general(a, b, (((1,), (1,)), ((), ())), preferred_element_type=F32)


def _cast_spec(a, n_steps):
    return pl.BlockSpec((a.shape[0] // n_steps, a.shape[1]), lambda i: (jnp.minimum(i, n_steps - 1), 0))


def _cast_specs(arrays, n_steps):
    specs = [_cast_spec(a, n_steps) for a in arrays]
    return specs, [jax.ShapeDtypeStruct(a.shape, BF16) for a in arrays]


def _cast_slices(in_refs, out_refs):
    for src_ref, dst_ref in zip(in_refs, out_refs):
        dst_ref[...] = src_ref[...].astype(BF16)


N_PREP_IN, N_PREP_OUT = 7, 10
PREP_PAIRS = N_PAIRS // NBP


def _pool_mix(g, pooled, w_ref, scale_ref):
    sl = slice(g * POOL_GROUP_DIM, (g + 1) * POOL_GROUP_DIM)
    return (_dot(pooled.astype(BF16), w_ref[g]) * scale_ref[:, sl]).astype(BF16)


N_FRONT_OUT = 5


def _front_kernel(xp_ref, xs_ref, w_ref, st_ref, wp_ref, scale_ref, *refs):
    prep_in, refs = refs[:N_PREP_IN], refs[N_PREP_IN:]
    n_cast = (len(refs) - N_FRONT_OUT - N_PREP_OUT - 1) // 2
    cast_in, refs = refs[:n_cast], refs[n_cast:]
    ussm_ref, a_ref, utail_ref, us_ref, xsm_ref = refs[:N_FRONT_OUT]
    prep_out, cast_out, ext_ref = (refs[N_FRONT_OUT:N_FRONT_OUT + N_PREP_OUT],
                                   refs[N_FRONT_OUT + N_PREP_OUT:-1], refs[-1])
    i = pl.program_id(0)

    @pl.when(i == 0)
    def _():
        for j in range(DEC_SEQ):
            xsm_ref[j * DEC_BATCH:(j + 1) * DEC_BATCH, :] = xs_ref[:, j, :]

    xb = jnp.where(i < NBP, xp_ref[...], xsm_ref[...]).astype(BF16)

    first = (i % BLOCKS_PER_SEQ) == 0
    ext_ref[0:16, :] = jnp.where(first, 0.0, ext_ref[TM:TM + 16, :])
    ext_ref[16:16 + TM, :] = _dot(xb, w_ref[:, 0:D_POOL])
    utail_ref[0] = ext_ref[TM:TM + 16, :]
    pos = (i % BLOCKS_PER_SEQ) * TM + lax.broadcasted_iota(jnp.int32, (TM, 1), 0)
    qc = D_SSM // len(POOL_WINDOWS)
    for g, w in enumerate(POOL_WINDOWS):
        ussm_ref[:, g * qc:(g + 1) * qc] = _dot(xb, w_ref[:, D_POOL + g * qc:D_POOL + (g + 1) * qc])
        sl = slice(g * POOL_GROUP_DIM, (g + 1) * POOL_GROUP_DIM)
        acc = ext_ref[16:16 + TM, sl]
        for k in range(1, w):
            acc = acc + ext_ref[16 - k:16 - k + TM, sl]
        cnt = jnp.minimum(pos + 1, w).astype(F32)
        a_ref[:, sl] = _pool_mix(g, acc / cnt - ext_ref[16:16 + TM, sl], wp_ref, scale_ref)

    for pp in range(PREP_PAIRS):
        _ssm_prep_pair(pp, *prep_in, *prep_out)
    _cast_slices(cast_in, cast_out)

    @pl.when(i == NBP)
    def _():
        us_ref[...] = ext_ref[16:16 + TM, :]
        tok = lambda t, sl: ext_ref[16 + t * DEC_BATCH:16 + (t + 1) * DEC_BATCH, sl]
        for j in range(DEC_SEQ):
            rows = slice(j * DEC_BATCH, (j + 1) * DEC_BATCH)
            for g, w in enumerate(POOL_WINDOWS):
                sl = slice(g * POOL_GROUP_DIM, (g + 1) * POOL_GROUP_DIM)
                acc = tok(j, sl)
                for k in range(1, w):
                    e = POOL_BUF + j - k
                    acc = acc + (tok(e - POOL_BUF, sl) if e >= POOL_BUF else st_ref[:, e, sl])
                cnt = float(min(PAST_LEN + j + 1, w))
                a_ref[rows, sl] = _pool_mix(g, acc / cnt - tok(j, sl), wp_ref, scale_ref)


def _front(xp, xs, w, state_t, w_pool, pool_scale, prep_args, to_cast):
    G, N, C, L = N_SSM_GROUPS, SSM_STATE, SSM_GROUP_CH, CHUNK
    wd = L * C
    step = lambda i: jnp.minimum(i, NBP - 1)
    g3 = lambda a, b: pl.BlockSpec((2 * PREP_PAIRS, a, b), lambda i: (step(i), 0, 0))
    p3 = lambda a, b: pl.BlockSpec((PREP_PAIRS, a, b), lambda i: (step(i), 0, 0))
    sds = jax.ShapeDtypeStruct
    cast_specs, cast_shapes = _cast_specs(to_cast, NBP)
    row = lambda wdt: pl.BlockSpec((TM, wdt), lambda i: (i, 0))
    outs = pl.pallas_call(
        _front_kernel,
        out_shape=(
            sds((R, D_SSM), F32), sds((R, D_POOL), BF16), sds((NB, 16, D_POOL), F32), sds((RS, D_POOL), F32),
            sds((RS, D_MODEL), F32),
            sds((N_PAIRS, 2, wd, wd), BF16),
            sds((N_PAIRS, PAIR_W, 128), BF16), sds((N_PAIRS, PAIR_W, 128), BF16),
            sds((N_PAIRS, PAIR_W, 128), BF16), sds((N_PAIRS, PAIR_W, 128), BF16),
            sds((N_PAIRS, PAIR_W, 128), BF16), sds((N_PAIRS, PAIR_W, 128), BF16),
            sds((N_PAIRS, 8, 128), F32), sds((N_PAIRS, 8, 128), F32), sds((N_PAIRS, 8, 128), F32),
            *cast_shapes,
        ),
        grid=(NB,),
        in_specs=[
            pl.BlockSpec((TM, D_MODEL), lambda i: (step(i), 0)),
            _const_spec((DEC_BATCH, DEC_SEQ, D_MODEL)),
            _const_spec((D_MODEL, D_MODEL)),
            _const_spec((DEC_BATCH, POOL_BUF, D_POOL)),
            _const_spec((4, POOL_GROUP_DIM, POOL_GROUP_DIM)),
            _const_spec((1, D_POOL)),
            g3(1, N), g3(1, N), g3(1, 1), g3(C, N), g3(C, N), g3(C, N), g3(C, N),
            *cast_specs,
        ],
        out_specs=(
            row(D_SSM), row(D_POOL), pl.BlockSpec((1, 16, D_POOL), lambda i: (i, 0, 0)),
            pl.BlockSpec((RS, D_POOL), lambda i: (0, 0)),
            pl.BlockSpec((RS, D_MODEL), lambda i: (0, 0)),
            pl.BlockSpec((PREP_PAIRS, 2, wd, wd), lambda i: (step(i), 0, 0, 0)),
            p3(PAIR_W, 128), p3(PAIR_W, 128), p3(PAIR_W, 128), p3(PAIR_W, 128),
            p3(PAIR_W, 128), p3(PAIR_W, 128),
            p3(8, 128), p3(8, 128), p3(8, 128),
            *cast_specs,
        ),
        scratch_shapes=[pltpu.VMEM((TM + 16, D_POOL), F32)],
        compiler_params=_cparams(("arbitrary",)),
        name="front",
    )(xp, xs, w, state_t, w_pool, pool_scale, *prep_args, *to_cast)
    names = ("t", "pre", "pim", "p4re", "p4im", "qre", "qim", "apow_re", "apow_im", "lam4")
    return (outs[:N_FRONT_OUT], dict(zip(names, outs[N_FRONT_OUT:N_FRONT_OUT + N_PREP_OUT])),
            outs[N_FRONT_OUT + N_PREP_OUT:])


SLAB_GROUPS = 128 // SSM_GROUP_CH
SLAB_PAIRS = SLAB_GROUPS // 2
N_SLABS = N_SSM_GROUPS // SLAB_GROUPS


def _ssm_pair(q, u, t_ref, pre_ref, pim_ref, p4re_ref, p4im_ref, qre_ref, qim_ref,
              apow_re_ref, apow_im_ref, lam4_ref, dskip_ref, h0re_ref, h0im_ref,
              hp_re_ref, hp_im_ref, hs_re_ref, hs_im_ref, hre_scr, him_scr):
    ub = u.astype(BF16)
    half = PAIR_W // 2
    y = jnp.concatenate([_dot(ub[:, :half], t_ref[q, 0]), _dot(ub[:, half:], t_ref[q, 1])], axis=1)

    ubp = ub[:P_CHUNK_ROWS]
    hre_scr[SCAN_PAD:SCAN_PAD + P_CHUNK_ROWS, :] = _dot(ubp, pre_ref[q])
    him_scr[SCAN_PAD:SCAN_PAD + P_CHUNK_ROWS, :] = _dot(ubp, pim_ref[q])
    kk = lax.broadcasted_iota(jnp.int32, (P_CHUNK_ROWS, 1), 0) % CHUNKS_PER_SEQ
    for s in range(N_SCAN_STEPS):
        d = 1 << s
        ar = apow_re_ref[q, s:s + 1, :]
        ai = apow_im_ref[q, s:s + 1, :]
        hr = hre_scr[SCAN_PAD:SCAN_PAD + P_CHUNK_ROWS, :]
        hi = him_scr[SCAN_PAD:SCAN_PAD + P_CHUNK_ROWS, :]
        pr = hre_scr[SCAN_PAD - d:SCAN_PAD - d + P_CHUNK_ROWS, :]
        pi = him_scr[SCAN_PAD - d:SCAN_PAD - d + P_CHUNK_ROWS, :]
        keep = kk >= d
        hre_scr[SCAN_PAD:SCAN_PAD + P_CHUNK_ROWS, :] = hr + jnp.where(keep, ar * pr - ai * pi, 0.0)
        him_scr[SCAN_PAD:SCAN_PAD + P_CHUNK_ROWS, :] = hi + jnp.where(keep, ar * pi + ai * pr, 0.0)
    hp_re_ref[q] = jnp.zeros((8, 128), F32)
    hp_im_ref[q] = jnp.zeros((8, 128), F32)
    for b in range(BATCH):
        last = SCAN_PAD + (b + 1) * CHUNKS_PER_SEQ - 1
        hp_re_ref[q, b:b + 1, :] = hre_scr[last:last + 1, :]
        hp_im_ref[q, b:b + 1, :] = him_scr[last:last + 1, :]
    prev_ok = kk >= 1
    hprev_re = jnp.where(prev_ok, hre_scr[SCAN_PAD - 1:SCAN_PAD - 1 + P_CHUNK_ROWS, :], 0.0)
    hprev_im = jnp.where(prev_ok, him_scr[SCAN_PAD - 1:SCAN_PAD - 1 + P_CHUNK_ROWS, :], 0.0)
    carry_p = _dot_nt(hprev_re.astype(BF16), qre_ref[q]) + _dot_nt(hprev_im.astype(BF16), qim_ref[q])

    ubs = ub[P_CHUNK_ROWS:]
    h0r = h0re_ref[q]
    h0i = h0im_ref[q]
    l4r = lam4_ref[q, 0:1, :]
    l4i = lam4_ref[q, 1:2, :]
    hs_re_ref[q] = l4r * h0r - l4i * h0i + _dot(ubs, p4re_ref[q])
    hs_im_ref[q] = l4r * h0i + l4i * h0r + _dot(ubs, p4im_ref[q])
    carry_s = _dot_nt(h0r.astype(BF16), qre_ref[q]) + _dot_nt(h0i.astype(BF16), qim_ref[q])

    y = y + jnp.concatenate([carry_p, carry_s], axis=0) + dskip_ref[q] * u
    return jax.nn.gelu(y)


N_SSM_IN, N_SSM_OUT = 13, 5


def _ssm_kernel(u_ref, *all_refs):
    n_cast = (len(all_refs) - N_SSM_IN - N_SSM_OUT - 2) // 2
    _cast_slices(all_refs[N_SSM_IN:N_SSM_IN + n_cast], all_refs[N_SSM_IN + n_cast + N_SSM_OUT:-2])
    refs = all_refs[:N_SSM_IN] + all_refs[N_SSM_IN + n_cast:N_SSM_IN + n_cast + N_SSM_OUT] + all_refs[-2:]
    z_ref = refs[13]
    hre_scr, him_scr = refs[18], refs[19]
    c = SSM_GROUP_CH
    hre_scr[0:SCAN_PAD, :] = jnp.zeros((SCAN_PAD, 128), F32)
    him_scr[0:SCAN_PAD, :] = jnp.zeros((SCAN_PAD, 128), F32)
    xt = []
    for i in range(CHUNK):
        xp = u_ref[pl.ds(i, P_CHUNK_ROWS, stride=CHUNK), :]
        if i < DEC_SEQ:
            xs = u_ref[RP + i * DEC_BATCH:RP + (i + 1) * DEC_BATCH, :]
        else:
            xs = jnp.zeros((DEC_BATCH, 128), F32)
        xt.append(jnp.concatenate([xp, xs], axis=0).T)
    def pair_input(q):
        halves = []
        for e in range(2):
            g = 2 * q + e
            bt = jnp.concatenate([xt[i][g * c:(g + 1) * c, :] for i in range(CHUNK)], axis=0)
            halves.append(bt.T)
        return jnp.concatenate(halves, axis=1)

    zt = []
    u_pair = pair_input(0)
    for q in range(SLAB_PAIRS):
        u_next = pair_input(q + 1) if q + 1 < SLAB_PAIRS else None
        z = _ssm_pair(q, u_pair, *refs[:13], *refs[14:])
        zt.append(z[:, :PAIR_W // 2].T)
        zt.append(z[:, PAIR_W // 2:].T)
        u_pair = u_next
    for i in range(CHUNK):
        zi = jnp.concatenate([zt[g][i * c:(i + 1) * c, :] for g in range(SLAB_GROUPS)], axis=0).T
        z_ref[pl.ds(i, P_CHUNK_ROWS, stride=CHUNK), :] = zi[:P_CHUNK_ROWS]
        if i < DEC_SEQ:
            z_ref[RP + i * DEC_BATCH:RP + (i + 1) * DEC_BATCH, :] = zi[P_CHUNK_ROWS:]


def _ssm(u_ssm, prep, h0re, h0im, to_cast):
    sp = SLAB_PAIRS
    slab3 = lambda a, b: pl.BlockSpec((sp, a, b), lambda s: (s, 0, 0))
    cast_specs, cast_shapes = _cast_specs(to_cast, N_SLABS)
    outs = pl.pallas_call(
        _ssm_kernel,
        out_shape=(
            jax.ShapeDtypeStruct((R, D_SSM), F32),
            jax.ShapeDtypeStruct((N_PAIRS, 8, 128), F32),
            jax.ShapeDtypeStruct((N_PAIRS, 8, 128), F32),
            jax.ShapeDtypeStruct((N_PAIRS, DEC_BATCH, 128), F32),
            jax.ShapeDtypeStruct((N_PAIRS, DEC_BATCH, 128), F32),
            *cast_shapes,
        ),
        grid=(N_SLABS,),
        in_specs=[
            pl.BlockSpec((R, 128), lambda s: (0, s)),
            pl.BlockSpec((sp, 2, 256, 256), lambda s: (s, 0, 0, 0)),
            slab3(PAIR_W, 128), slab3(PAIR_W, 128), slab3(PAIR_W, 128), slab3(PAIR_W, 128),
            slab3(PAIR_W, 128), slab3(PAIR_W, 128),
            slab3(8, 128), slab3(8, 128), slab3(8, 128),
            slab3(1, PAIR_W),
            slab3(DEC_BATCH, 128), slab3(DEC_BATCH, 128),
            *cast_specs,
        ],
        out_specs=(
            pl.BlockSpec((R, 128), lambda s: (0, s)),
            slab3(8, 128), slab3(8, 128), slab3(DEC_BATCH, 128), slab3(DEC_BATCH, 128),
            *cast_specs,
        ),
        scratch_shapes=[pltpu.VMEM((SCAN_PAD + P_CHUNK_ROWS, 128), F32),
                        pltpu.VMEM((SCAN_PAD + P_CHUNK_ROWS, 128), F32)],
        compiler_params=_cparams(("arbitrary",)),
        name="ssm",
    )(u_ssm, prep["t"], prep["pre"], prep["pim"], prep["p4re"], prep["p4im"],
      prep["qre"], prep["qim"], prep["apow_re"], prep["apow_im"], prep["lam4"],
      prep["dskip"], h0re, h0im, *to_cast)
    return outs[:N_SSM_OUT], outs[N_SSM_OUT:]


def _cmul(ar, ai, br, bi):
    return ar * br - ai * bi, ar * bi + ai * br


def _dot3_nt(a, b):
    ah = a.astype(BF16)
    bh = b.astype(BF16)
    al = (a - ah.astype(F32)).astype(BF16)
    bl = (b - bh.astype(F32)).astype(BF16)
    return _dot_nt(ah, bh) + _dot_nt(ah, bl) + _dot_nt(al, bh)


def _ssm_prep_pair(pp, lam_re_ref, lam_im_ref, lstep_ref, bt_re_ref, bt_im_ref, c_re_ref, c_im_ref,
                   t_ref, pre_ref, pim_ref, p4re_ref, p4im_ref, qre_ref, qim_ref,
                   apr_ref, api_ref, lam4_ref):
    L, C, N = CHUNK, SSM_GROUP_CH, SSM_STATE
    w = L * C
    p_re, p_im, p4_re, p4_im, q_re, q_im, a_re, a_im, l4 = [], [], [], [], [], [], [], [], []
    for e in range(2):
        ge = 2 * pp + e
        dt = jnp.exp(lstep_ref[ge])
        lr, li = lam_re_ref[ge], lam_im_ref[ge]
        mag = jnp.exp(lr * dt)
        ang = li * dt
        zr, zi = mag * jnp.cos(ang), mag * jnp.sin(ang)
        den = lr * lr + li * li
        fr = ((zr - 1.0) * lr + zi * li) / den
        fi = (zi * lr - (zr - 1.0) * li) / den
        bbr, bbi = _cmul(fr, fi, bt_re_ref[ge], bt_im_ref[ge])
        pr, pi = [jnp.ones((1, N), F32)], [jnp.zeros((1, N), F32)]
        for _ in range(L):
            nr, ni = _cmul(pr[-1], pi[-1], zr, zi)
            pr.append(nr)
            pi.append(ni)
        stack = lambda blocks, part: jnp.concatenate([b[part] for b in blocks], axis=0)
        blocks = [_cmul(bbr, bbi, pr[L - 1 - i], pi[L - 1 - i]) for i in range(L)]
        p_re.append(stack(blocks, 0))
        p_im.append(stack(blocks, 1))
        blocks4 = [_cmul(bbr, bbi, pr[DEC_SEQ - 1 - i], pi[DEC_SEQ - 1 - i]) for i in range(DEC_SEQ)]
        pad = jnp.zeros(((L - DEC_SEQ) * C, N), F32)
        p4_re.append(jnp.concatenate([stack(blocks4, 0), pad], axis=0))
        p4_im.append(jnp.concatenate([stack(blocks4, 1), pad], axis=0))
        sr, si = [pr[L]], [pi[L]]
        for _ in range(N_SCAN_STEPS - 1):
            nr, ni = _cmul(sr[-1], si[-1], sr[-1], si[-1])
            sr.append(nr)
            si.append(ni)
        a_re.append(jnp.concatenate(sr + [jnp.zeros((8 - N_SCAN_STEPS, N), F32)], axis=0))
        a_im.append(jnp.concatenate(si + [jnp.zeros((8 - N_SCAN_STEPS, N), F32)], axis=0))
        l4.append(jnp.concatenate([pr[DEC_SEQ], pi[DEC_SEQ], jnp.zeros((6, N), F32)], axis=0))

        cr, ci = c_re_ref[ge], c_im_ref[ge]
        ck = [_cmul(cr, ci, pr[k], pi[k]) for k in range(L + 1)]
        q_re.append(stack(ck[1:], 0))
        q_im.append(-stack(ck[1:], 1))
        v = _dot3_nt(bbr, stack(ck[:L], 0)) - _dot3_nt(bbi, stack(ck[:L], 1))
        lane = lax.broadcasted_iota(jnp.int32, (C, w), 1)
        rows = [v] + [jnp.where(lane >= C * i, pltpu.roll(v, C * i, axis=1), 0.0) for i in range(1, L)]
        t_ref[pp, e] = jnp.concatenate(rows, axis=0).astype(BF16)

    def diag_rows(m):
        z = jnp.zeros_like(m[0])
        return jnp.concatenate([jnp.concatenate([m[0], z], axis=1),
                                jnp.concatenate([z, m[1]], axis=1)], axis=0)

    pre_ref[pp] = diag_rows(p_re).astype(BF16)
    pim_ref[pp] = diag_rows(p_im).astype(BF16)
    p4re_ref[pp] = diag_rows(p4_re).astype(BF16)
    p4im_ref[pp] = diag_rows(p4_im).astype(BF16)
    qre_ref[pp] = diag_rows(q_re).astype(BF16)
    qim_ref[pp] = diag_rows(q_im).astype(BF16)
    apr_ref[pp] = jnp.concatenate(a_re, axis=1)
    api_ref[pp] = jnp.concatenate(a_im, axis=1)
    lam4_ref[pp] = jnp.concatenate(l4, axis=1)


def _ssm_prep_args(lambda_re, lambda_im, log_step, b_re, b_im, c_re, c_im):
    return (lambda_re[:, None, :], lambda_im[:, None, :], log_step[:, None, None],
            b_re.transpose(0, 2, 1), b_im.transpose(0, 2, 1), c_re, c_im)


def _mix_kernel(a_ref, z_ref, xp_ref, xs_ref, wglu_ref, bglu_ref, wout_ref, g_ref, b_ref, *refs):
    n_cast = (len(refs) - 2) // 2
    o_ref, ob_ref = refs[n_cast], refs[n_cast + 1]
    _cast_slices(refs[:n_cast], refs[n_cast + 2:])
    is_prompt = pl.program_id(0) < NBP
    n_sub = TM // SUB

    def glu(r):
        z = z_ref[r * SUB:(r + 1) * SUB, :]
        gate = _dot(z.astype(BF16), wglu_ref[...]) + bglu_ref[...]
        return (z * jax.nn.sigmoid(gate)).astype(BF16)

    bmix = glu(0)
    for r in range(n_sub):
        rows = slice(r * SUB, (r + 1) * SUB)
        bmix_next = glu(r + 1) if r + 1 < n_sub else None
        mix = _dot(a_ref[rows, :], wout_ref[0:D_POOL, :]) + _dot(bmix, wout_ref[D_POOL:, :])
        x = jnp.where(is_prompt, xp_ref[rows, :], xs_ref[rows, :])
        h1 = _layer_norm(ALPHA * x + mix, g_ref[...], b_ref[...])
        o_ref[rows, :] = h1
        ob_ref[rows, :] = h1.astype(BF16)
        bmix = bmix_next


def _mix(a, z, xp, xs, w_glu, b_glu, w_out, g, b, to_cast):
    row = lambda w: pl.BlockSpec((TM, w), lambda i: (i, 0))
    cast_specs, cast_shapes = _cast_specs(to_cast, NBP)
    outs = pl.pallas_call(
        _mix_kernel,
        out_shape=(jax.ShapeDtypeStruct((R, D_MODEL), F32), jax.ShapeDtypeStruct((R, D_MODEL), BF16), *cast_shapes),
        grid=(NB,),
        in_specs=[
            row(D_POOL), row(D_SSM),
            pl.BlockSpec((TM, D_MODEL), lambda i: (jnp.minimum(i, NBP - 1), 0)),
            _const_spec((RS, D_MODEL)),
            _const_spec((D_SSM, D_SSM)), _const_spec((1, D_SSM)),
            _const_spec((D_MODEL, D_MODEL)), _const_spec((1, D_MODEL)), _const_spec((1, D_MODEL)),
            *cast_specs,
        ],
        out_specs=(row(D_MODEL), row(D_MODEL), *cast_specs),
        compiler_params=_cparams(("arbitrary",)),
        name="mix_ln1",
    )(a, z, xp, xs, w_glu, b_glu, w_out, g, b, *to_cast)
    return outs[0], outs[1], outs[2:]


def _qproj_sample_kernel(x_ref, w_ref, o_ref):
    q = _dot(x_ref[...], w_ref[...])
    for j in range(DEC_SEQ):
        o_ref[:, j, :] = q[j * DEC_BATCH:(j + 1) * DEC_BATCH, :]
    o_ref[:, DEC_SEQ:, :] = jnp.zeros((DEC_BATCH, Q_PAD - DEC_SEQ, D_MODEL), F32)


def _qproj_sample(h, w):
    return pl.pallas_call(
        _qproj_sample_kernel,
        out_shape=jax.ShapeDtypeStruct((DEC_BATCH, Q_PAD, D_MODEL), F32),
        grid=(1,),
        in_specs=[pl.BlockSpec((RS, D_MODEL), lambda i: (RP // RS, 0)),
                  pl.BlockSpec((D_MODEL, D_MODEL), lambda i: (0, 0))],
        out_specs=pl.BlockSpec((DEC_BATCH, Q_PAD, D_MODEL), lambda i: (0, 0, 0)),
        compiler_params=_cparams(("arbitrary",)),
        name="qproj_sample",
    )(h, w)


def _memproj_kernel(m_ref, w_ref, o_ref, ob_ref, wb_scr):
    @pl.when(pl.program_id(0) == 0)
    def _():
        wb_scr[...] = w_ref[...].astype(BF16)

    k = _dot(m_ref[...].astype(BF16), wb_scr[...])
    ob_ref[...] = k.astype(BF16)
    for h in range(N_XHEADS):
        o_ref[0, 0, :, h, :] = k[:, h * XHEAD_DIM:(h + 1) * XHEAD_DIM]


def _memproj(mem, w, name):
    rows = BATCH * N_MEM
    seq = pl.BlockSpec((N_MEM, D_MODEL), lambda b: (b, 0))
    return pl.pallas_call(
        _memproj_kernel,
        out_shape=(jax.ShapeDtypeStruct((1, BATCH, N_MEM, N_XHEADS, XHEAD_DIM), F32),
                   jax.ShapeDtypeStruct((rows, D_MODEL), BF16)),
        grid=(BATCH,),
        in_specs=[seq, _const_spec((D_MODEL, D_MODEL))],
        out_specs=(pl.BlockSpec((1, 1, N_MEM, N_XHEADS, XHEAD_DIM), lambda b: (0, b, 0, 0, 0)), seq),
        scratch_shapes=[pltpu.VMEM((D_MODEL, D_MODEL), BF16)],
        compiler_params=_cparams(("arbitrary",)),
        name=name,
    )(mem, w)


def _attend(q, k, v, between=None):
    def scores(h):
        sl = slice(h * XHEAD_DIM, (h + 1) * XHEAD_DIM)
        return lax.dot_general(q[:, sl], k[:, sl], (((1,), (1,)), ((), ())),
                               preferred_element_type=F32) * (XHEAD_DIM ** -0.5)

    def head_out(h, s):
        s = s - jnp.max(s, axis=-1, keepdims=True)
        e = jnp.exp(s)
        p = e / jnp.sum(e, axis=-1, keepdims=True)
        return _dot(p.astype(BF16), v[:, h * XHEAD_DIM:(h + 1) * XHEAD_DIM])

    outs = []
    s = scores(0)
    for h in range(N_XHEADS):
        if between is not None:
            between(h)
        s_next = scores(h + 1) if h + 1 < N_XHEADS else None
        outs.append(head_out(h, s))
        s = s_next
    return jnp.concatenate(outs, axis=1)


def _attn_block_kernel(h_ref, q_ref, k_ref, v_ref, os_ref, wo_ref, g_ref, b_ref, *refs):
    n_cast = (len(refs) - 1) // 2
    o_ref = refs[n_cast]
    _cast_slices(refs[:n_cast], refs[n_cast + 1:])
    i = pl.program_id(0)
    n_sub = TM // SUB

    def normed(rows, y):
        o_ref[rows, :] = _layer_norm(ALPHA * h_ref[rows, :] + y, g_ref[...], b_ref[...])

    @pl.when(i < NBP)
    def _():
        prev_rows, prev_o = None, None
        for r in range(n_sub):
            rows = slice(r * SUB, (r + 1) * SUB)
            ys = []
            between = None
            if prev_o is not None:
                between = lambda hh, po=prev_o: ys.append(
                    _dot(po, wo_ref[:, hh * XHEAD_DIM:(hh + 1) * XHEAD_DIM]))
            o = _attend(q_ref[rows, :], k_ref[...], v_ref[...], between).astype(BF16)
            if prev_o is not None:
                normed(prev_rows, jnp.concatenate(ys, axis=1))
            prev_rows, prev_o = rows, o
        normed(prev_rows, _dot(prev_o, wo_ref[...]))

    @pl.when(i == NBP)
    def _():
        for r in range(n_sub):
            rows = slice(r * SUB, (r + 1) * SUB)
            steps = range(r * SUB // DEC_BATCH, (r + 1) * SUB // DEC_BATCH)
            o = jnp.concatenate([os_ref[:, j, :] for j in steps], axis=0).astype(BF16)
            normed(rows, _dot(o, wo_ref[...]))


def _attn_block(h, q_prompt, kb, vb, o_sample, wo, g, b, to_cast):
    kv = pl.BlockSpec((N_MEM, D_MODEL), lambda i: (jnp.minimum(i // BLOCKS_PER_SEQ, BATCH - 1), 0))
    row = pl.BlockSpec((TM, D_MODEL), lambda i: (i, 0))
    qrow = pl.BlockSpec((TM, D_MODEL), lambda i: (jnp.minimum(i, NBP - 1), 0))
    cast_specs, cast_shapes = _cast_specs(to_cast, NBP)
    outs = pl.pallas_call(
        _attn_block_kernel,
        out_shape=(jax.ShapeDtypeStruct((R, D_MODEL), F32), *cast_shapes),
        grid=(NB,),
        in_specs=[row, qrow, kv, kv, _const_spec((DEC_BATCH, Q_PAD, D_MODEL)), _const_spec((D_MODEL, D_MODEL)),
                  _const_spec((1, D_MODEL)), _const_spec((1, D_MODEL)), *cast_specs],
        out_specs=(row, *cast_specs),
        compiler_params=_cparams(("arbitrary",)),
        name="attn_block",
    )(h, q_prompt, kb, vb, o_sample, wo, g, b, *to_cast)
    return outs[0], outs[1:]


ATT_BB = 4
Q_PAD = 8
KV_ROWS = N_MEM * N_XHEADS


QP_ROWS = RP // (DEC_BATCH // ATT_BB)


def _attn_sample_kernel(q_ref, k_ref, v_ref, h_ref, wq_ref, o_ref, qp_ref):
    hb = h_ref[...]
    qcols = D_MODEL // ATT_BB
    shape = (N_XHEADS * Q_PAD, KV_ROWS)
    same_head = (lax.broadcasted_iota(jnp.int32, shape, 0) // Q_PAD
                 == lax.broadcasted_iota(jnp.int32, shape, 1) % N_XHEADS)
    def scores(b):
        q = q_ref[b]
        qs = jnp.concatenate([q[:, h * XHEAD_DIM:(h + 1) * XHEAD_DIM] for h in range(N_XHEADS)], axis=0)
        k = k_ref[0, b].reshape(KV_ROWS, XHEAD_DIM).astype(BF16)
        return lax.dot_general(qs.astype(BF16), k, (((1,), (1,)), ((), ())),
                               preferred_element_type=F32) * (XHEAD_DIM ** -0.5)

    def attend(b, s):
        s = jnp.where(same_head, s, -1e30)
        s = s - jnp.max(s, axis=-1, keepdims=True)
        e = jnp.exp(s)
        p = e / jnp.sum(e, axis=-1, keepdims=True)
        v = v_ref[0, b].reshape(KV_ROWS, XHEAD_DIM).astype(BF16)
        o = _dot(p.astype(BF16), v)
        for h in range(N_XHEADS):
            o_ref[b, :, h * XHEAD_DIM:(h + 1) * XHEAD_DIM] = o[h * Q_PAD:(h + 1) * Q_PAD]

    s_prev = scores(0)
    for b in range(ATT_BB):
        cols = slice(b * qcols, (b + 1) * qcols)
        qp_ref[:, cols] = _dot(hb, wq_ref[:, cols]).astype(BF16)
        s_next = scores(b + 1) if b + 1 < ATT_BB else None
        attend(b, s_prev)
        s_prev = s_next


def _attn_sample(q_pad, k, v, h, wq):
    kv = pl.BlockSpec((1, ATT_BB, N_MEM, N_XHEADS, XHEAD_DIM), lambda i: (0, i, 0, 0, 0))
    qo = pl.BlockSpec((ATT_BB, Q_PAD, D_MODEL), lambda i: (i, 0, 0))
    hq = pl.BlockSpec((QP_ROWS, D_MODEL), lambda i: (i, 0))
    return pl.pallas_call(
        _attn_sample_kernel,
        out_shape=(jax.ShapeDtypeStruct((DEC_BATCH, Q_PAD, D_MODEL), F32),
                   jax.ShapeDtypeStruct((RP, D_MODEL), BF16)),
        grid=(DEC_BATCH // ATT_BB,),
        in_specs=[qo, kv, kv, hq, _const_spec((D_MODEL, D_MODEL))],
        out_specs=(qo, hq),
        compiler_params=_cparams(("arbitrary",)),
        name="attn_sample",
    )(q_pad, k, v, h, wq)


FTM = 1024
FNB = RP // FTM
FBLOCKS_PER_SEQ = SEQ // FTM
FSUB = 512
FSUB_LAST = 128
FSUB_SAMPLE = 256


def _ffn_rows(n_rows, sub, conv_taps, h_ref, hb_scr, g_scr, g_base, wg, wu, wd, cw_ref, cb_ref,
              lng_ref, lnb_ref, y_ref, first, last):
    def gate_up(lo):
        if first:
            h = h_ref[lo:lo + sub, :]
            hb = h.astype(BF16)
            hb_scr[lo:lo + sub, :] = hb
        else:
            h, hb = None, hb_scr[lo:lo + sub, :]
        g = _dot(hb, wg)
        up = _dot(hb, wu)
        g_scr[g_base + lo:g_base + lo + sub, :] = g
        tap0, tap1 = conv_taps(lo, sub)
        gc = cb_ref[...] + cw_ref[0:1, :] * tap0 + cw_ref[1:2, :] * tap1 + cw_ref[2:3, :] * g
        return h, (jax.nn.silu(gc) * up).astype(BF16)

    def down(lo, h, act):
        y = _dot(act, wd) + (ALPHA * h if first else y_ref[lo:lo + sub, :])
        y_ref[lo:lo + sub, :] = _layer_norm(y, lng_ref[...], lnb_ref[...]) if last else y

    n_sub = n_rows // sub
    pending = gate_up(0)
    for r in range(n_sub):
        nxt = gate_up((r + 1) * sub) if r + 1 < n_sub else None
        down(r * sub, *pending)
        pending = nxt


def _ffn_chunk(f, run):
    @pl.when(f == 0)
    def _():
        run(True, False)

    @pl.when(jnp.logical_and(f > 0, f < NF - 1))
    def _():
        run(False, False)

    @pl.when(f == NF - 1)
    def _():
        run(False, True)


def _ffn_prompt_kernel(h_ref, wg_ref, wu_ref, wd_ref, cw_ref, cb_ref, lng_ref, lnb_ref,
                       y_ref, gtail_ref, hb_scr, g_scr, carry_scr):
    i = pl.program_id(0)
    f = pl.program_id(1)
    first_block = (i % FBLOCKS_PER_SEQ) == 0

    @pl.when(first_block)
    def _():
        g_scr[0:8, :] = jnp.zeros((8, TF), F32)

    @pl.when(jnp.logical_not(first_block))
    def _():
        g_scr[0:8, :] = carry_scr[f]

    taps = lambda lo, sub: (g_scr[6 + lo:6 + lo + sub, :], g_scr[7 + lo:7 + lo + sub, :])
    _ffn_chunk(f, lambda first, last: _ffn_rows(
        FTM, FSUB_LAST if last else FSUB, taps, h_ref, hb_scr, g_scr, 8, wg_ref[...], wu_ref[...], wd_ref[...], cw_ref, cb_ref,
        lng_ref, lnb_ref, y_ref, first, last))
    tail = g_scr[FTM:FTM + 8, :]
    carry_scr[f] = tail
    gtail_ref[0] = tail


def _ffn_sample_kernel(h_ref, wg_ref, wu_ref, wd_ref, cw_ref, cb_ref, lng_ref, lnb_ref, st_ref,
                       yseq_ref, gnew_ref, wdb_ref, hb_scr, g_scr, y_ref):
    f = pl.program_id(1)
    n_st = 2 * DEC_BATCH
    wdb_ref[...] = wd_ref[...].astype(BF16)
    for t in range(CONV_W - 1):
        g_scr[t * DEC_BATCH:(t + 1) * DEC_BATCH, :] = st_ref[:, t, :]
    taps = lambda lo, sub: (g_scr[lo:lo + sub, :], g_scr[DEC_BATCH + lo:DEC_BATCH + lo + sub, :])
    _ffn_chunk(f, lambda first, last: _ffn_rows(
        RS, FSUB_SAMPLE, taps, h_ref, hb_scr, g_scr, n_st, wg_ref[...], wu_ref[...], wdb_ref[...], cw_ref, cb_ref,
        lng_ref, lnb_ref, y_ref, first, last))
    for t in range(CONV_W - 1):
        gnew_ref[:, t, :] = g_scr[RS + t * DEC_BATCH:RS + (t + 1) * DEC_BATCH, :]

    @pl.when(f == NF - 1)
    def _():
        for j in range(DEC_SEQ):
            yseq_ref[:, j, :] = y_ref[j * DEC_BATCH:(j + 1) * DEC_BATCH, :]


def _ffn_specs(tm, row_block_of):
    return [
        pl.BlockSpec((tm, D_MODEL), lambda i, f: (row_block_of(i), 0)),
        pl.BlockSpec((D_MODEL, TF), lambda i, f: (0, f)),
        pl.BlockSpec((D_MODEL, TF), lambda i, f: (0, f)),
        pl.BlockSpec((TF, D_MODEL), lambda i, f: (f, 0)),
        pl.BlockSpec((CONV_W, TF), lambda i, f: (0, f)),
        pl.BlockSpec((1, TF), lambda i, f: (0, f)),
        pl.BlockSpec((1, D_MODEL), lambda i, f: (0, 0)),
        pl.BlockSpec((1, D_MODEL), lambda i, f: (0, 0)),
    ]


def _ffn_prompt(h, wg, wu, wd, cw, cb, lng, lnb):
    return pl.pallas_call(
        _ffn_prompt_kernel,
        out_shape=(jax.ShapeDtypeStruct((RP, D_MODEL), F32),
                   jax.ShapeDtypeStruct((FNB, 8, D_FF), F32)),
        grid=(FNB, NF),
        in_specs=_ffn_specs(FTM, lambda i: i),
        out_specs=(pl.BlockSpec((FTM, D_MODEL), lambda i, f: (i, 0)),
                   pl.BlockSpec((1, 8, TF), lambda i, f: (i, 0, f))),
        scratch_shapes=[pltpu.VMEM((FTM, D_MODEL), BF16),
                        pltpu.VMEM((FTM + 8, TF), F32),
                        pltpu.VMEM((NF, 8, TF), F32)],
        compiler_params=_cparams(("arbitrary", "arbitrary"), VMEM_LIMIT_FFN),
        name="ffn_prompt",
    )(h, wg, wu, wd, cw, cb, lng, lnb)


def _ffn_sample(h, wg, wu, wd, cw, cb, lng, lnb, conv_state):
    n_st = 2 * DEC_BATCH
    specs = _ffn_specs(RS, lambda i: RP // RS)
    return pl.pallas_call(
        _ffn_sample_kernel,
        out_shape=(jax.ShapeDtypeStruct((DEC_BATCH, DEC_SEQ, D_MODEL), F32),
                   jax.ShapeDtypeStruct((DEC_BATCH, CONV_W - 1, D_FF), F32),
                   jax.ShapeDtypeStruct((D_FF, D_MODEL), BF16)),
        grid=(1, NF),
        in_specs=specs + [pl.BlockSpec((DEC_BATCH, CONV_W - 1, TF), lambda i, f: (0, 0, f))],
        out_specs=(pl.BlockSpec((DEC_BATCH, DEC_SEQ, D_MODEL), lambda i, f: (0, 0, 0)),
                   pl.BlockSpec((DEC_BATCH, CONV_W - 1, TF), lambda i, f: (0, 0, f)),
                   specs[3]),
        scratch_shapes=[pltpu.VMEM((RS, D_MODEL), BF16),
                        pltpu.VMEM((n_st + RS, TF), F32),
                        pltpu.VMEM((RS, D_MODEL), F32)],
        compiler_params=_cparams(("arbitrary", "arbitrary")),
        name="ffn_sample",
    )(h, wg, wu, wd, cw, cb, lng, lnb, conv_state)


def _state_to_pairs(s):
    return s.reshape(s.shape[0], N_PAIRS, 128).transpose(1, 0, 2)


def _state_from_pairs(s):
    return s.transpose(1, 0, 2).reshape(1, s.shape[1], N_SSM_GROUPS, SSM_STATE)


def kernel(x_prompt, x_sample, mem_prompt, state_pool, state_ssm_re, state_ssm_im, state_conv, cache_mem_k, cache_mem_v, w_in, w_pool, pool_scale, lambda_re, lambda_im, log_step, b_re, b_im, c_re, c_im, d_skip, w_glu, b_glu, w_out, ln1_g, ln1_b, w_q, w_k, w_v, w_o, ln2_g, ln2_b, w_gate, w_up, conv_w, conv_b, w_down, ln3_g, ln3_b):
    bf = lambda w: w[0].astype(BF16)
    row = lambda v: v[0].reshape(1, -1)
    xp = x_prompt.reshape(RP, D_MODEL)

    prep_args = _ssm_prep_args(lambda_re[0], lambda_im[0], log_step[0], b_re[0], b_im[0], c_re[0], c_im[0])
    (u_ssm, a, utail, us_pool, xs), prep, (w_glu_b, w_out_b) = _front(
        xp, x_sample, bf(w_in), state_pool[0], bf(w_pool), row(pool_scale), prep_args,
        (w_glu[0], w_out[0]))
    prep["dskip"] = jnp.broadcast_to(d_skip[0].reshape(N_PAIRS, 2, 1, SSM_GROUP_CH),
                                     (N_PAIRS, 2, CHUNK, SSM_GROUP_CH)).reshape(N_PAIRS, 1, PAIR_W)
    (z, hp_re, hp_im, hs_re, hs_im), (w_q_b, w_o_b) = _ssm(
        u_ssm, prep, _state_to_pairs(state_ssm_re[0]), _state_to_pairs(state_ssm_im[0]), (w_q[0], w_o[0]))
    h1, h1b, (wg,) = _mix(a, z, xp, xs, w_glu_b, row(b_glu), w_out_b, row(ln1_g), row(ln1_b), (w_gate[0],))

    mem = mem_prompt.reshape(BATCH * N_MEM, D_MODEL)
    mk, mkb = _memproj(mem, w_k[0], "mem_k")
    mv, mvb = _memproj(mem, w_v[0], "mem_v")
    o_s, q_p = _attn_sample(_qproj_sample(h1b, w_q_b), cache_mem_k, cache_mem_v, h1b, w_q_b)
    h2, (wu,) = _attn_block(h1, q_p, mkb, mvb, o_s, w_o_b, row(ln2_g), row(ln2_b), (w_up[0],))

    cw, cb = conv_w[0], row(conv_b)
    y_sample, g_new, wd = _ffn_sample(h2, wg, wu, w_down[0], cw, cb, row(ln3_g), row(ln3_b), state_conv[0])
    y_p, gtail = _ffn_prompt(h2, wg, wu, wd, cw, cb, row(ln3_g), row(ln3_b))

    y_prompt = y_p.reshape(BATCH, SEQ, D_MODEL)
    p_pool = utail[BLOCKS_PER_SEQ - 1:NBP:BLOCKS_PER_SEQ, 16 - POOL_BUF:][None]
    s_ext = jnp.concatenate([state_pool[0], us_pool.reshape(DEC_SEQ, DEC_BATCH, D_POOL).transpose(1, 0, 2)], axis=1)
    s_pool = s_ext[None, :, DEC_SEQ:]
    p_conv = gtail[FBLOCKS_PER_SEQ - 1::FBLOCKS_PER_SEQ, 6:8][None]
    s_conv = g_new[None]
    return (y_prompt, y_sample,
            p_pool, _state_from_pairs(hp_re[:, :BATCH]), _state_from_pairs(hp_im[:, :BATCH]), p_conv,
            mk, mv,
            s_pool, _state_from_pairs(hs_re), _state_from_pairs(hs_im), s_conv)
```
